```python
import jax
import jax.numpy as jnp
from jax import lax
import numpy as np

D_MODEL = 2048
BATCH = 8
SEQ = 4096
DEPTH = 4
DEC_BATCH = 1
DEC_SEQ = 16384
PAST_LEN = 128

HEAD_DIM = 128
LRU_WIDTH = 512
LRU_BLOCKS = 4
CONV_WIDTH = 4
CONV_LEFT = 2
LRU_C = 8.0
DIL_HEADS = 6
DIL_PATTERNS = ((128, 1), (512, 4), (2048, 16))
SWA_HEADS = 6
SWA_KV_HEADS = 2
SWA_WINDOW = 128
DIL_WIDTH = DIL_HEADS * HEAD_DIM
SWA_WIDTH = SWA_HEADS * HEAD_DIM
SWA_KV_WIDTH = SWA_KV_HEADS * HEAD_DIM
MIX_WIDTH = LRU_WIDTH + DIL_WIDTH + SWA_WIDTH
IN_SPLITS = tuple(int(c) for c in np.cumsum([LRU_WIDTH, LRU_WIDTH, DIL_WIDTH, DIL_WIDTH, DIL_WIDTH, SWA_WIDTH, SWA_KV_WIDTH]))
IN_WIDTH = IN_SPLITS[-1] + SWA_KV_WIDTH
GROUP_SPLITS = (LRU_WIDTH, LRU_WIDTH + DIL_WIDTH)
N_MEM = 256
MEM_HEADS = 4
MEM_WIDTH = MEM_HEADS * HEAD_DIM
D_FF = 4 * D_MODEL
EPS = 1e-6

kernel_name = 'hybrid_bidir_encoder_parallel_groups'


def rms_norm(x, g):
    xf = x.astype(jnp.float32)
    y = xf * lax.rsqrt(jnp.mean(xf * xf, axis=-1, keepdims=True) + EPS)
    return (y * g.astype(jnp.float32)).astype(x.dtype)


def alibi_slopes(n):
    return jnp.asarray([2.0 ** (-8.0 * (i + 1) / n) for i in range(n)], dtype=jnp.float32)


def banded_attention(q, k, v, window, dist_scale, slopes):
    b, L, hq, dh = q.shape
    hkv = k.shape[2]
    rep = hq // hkv
    w = window
    nb = -(-L // w)
    lp = nb * w
    qb = jnp.pad(q, ((0, 0), (0, lp - L), (0, 0), (0, 0))).reshape(b, nb, w, hkv, rep, dh)
    pad_kv = ((0, 0), (w, lp - L + w), (0, 0), (0, 0))

    def key_blocks(t):
        tp = jnp.pad(t, pad_kv)
        return jnp.concatenate([tp[:, o * w:o * w + lp].reshape(b, nb, w, hkv, dh) for o in range(3)], axis=2)

    kb = key_blocks(k)
    vb = key_blocks(v)
    scores = jnp.einsum('bnqgrd,bnkgd->bngrqk', qb, kb, preferred_element_type=jnp.float32) * (dh ** -0.5)
    qi = jnp.arange(w)[:, None]
    kj = jnp.arange(3 * w)[None, :]
    rel = kj - w - qi
    k_pos = jnp.arange(nb)[:, None, None] * w + kj[None] - w
    valid = (jnp.abs(rel)[None] <= w) & (k_pos >= 0) & (k_pos < L)
    dist = (dist_scale * jnp.abs(rel)).astype(jnp.float32)
    bias = -slopes.reshape(hkv, rep)[:, :, None, None] * dist[None, None]
    logits = jnp.where(valid[None, :, None, None], scores + bias[None, None], -jnp.inf)
    m = jnp.max(logits, axis=-1)
    p = jnp.exp(logits - m[..., None])
    s = jnp.sum(p, axis=-1)
    o = jnp.einsum('bngrqk,bnkgd->bnqgrd', p.astype(v.dtype), vb, preferred_element_type=jnp.float32)
    o = o / s.transpose(0, 1, 4, 2, 3)[..., None]
    o = o.reshape(b, lp, hq, dh)[:, :L].astype(q.dtype)
    m = m.transpose(0, 1, 4, 2, 3).reshape(b, lp, hq)[:, :L]
    s = s.transpose(0, 1, 4, 2, 3).reshape(b, lp, hq)[:, :L]
    return o, m, s


def _linear_combine(c1, c2):
    a1, b1 = c1
    a2, b2 = c2
    return a1 * a2, a2 * b1 + b2


def rg_lru_mixer(xa, gate, conv_w, conv_b, wa, ba, wx, bx, lam):
    b, s, c = xa.shape
    xp = jnp.pad(xa, ((0, 0), (CONV_LEFT, CONV_WIDTH - 1 - CONV_LEFT), (0, 0)))
    xc = conv_b + sum(conv_w[j] * xp[:, j:j + s] for j in range(CONV_WIDTH))
    xc = xc.astype(jnp.float32)
    xblk = xc.reshape(b, s, LRU_BLOCKS, c // LRU_BLOCKS)
    h_total = jnp.zeros_like(xc)
    for direction in range(2):
        r = jax.nn.sigmoid(jnp.einsum('bsnd,nde->bsne', xblk, wa[direction]).reshape(b, s, c) + ba[direction])
        i = jax.nn.sigmoid(jnp.einsum('bsnd,nde->bsne', xblk, wx[direction]).reshape(b, s, c) + bx[direction])
        log_a = -LRU_C * r * jax.nn.softplus(-lam[direction].astype(jnp.float32))
        a = jnp.exp(log_a)
        u = jnp.sqrt(-jnp.expm1(2.0 * log_a)) * (i * xc)
        _, h = lax.associative_scan(_linear_combine, (a, u), axis=1, reverse=(direction == 1))
        h_total = h_total + h
    return (h_total * jax.nn.gelu(gate.astype(jnp.float32))).astype(xa.dtype)


def dilated_attention(q, k, v):
    b, s, nh, dh = q.shape
    slopes = alibi_slopes(nh)
    outs, ms, dens = [], [], []
    for window, d in DIL_PATTERNS:
        n = s // d

        def strided(t):
            return t.reshape(b, n, d, nh, dh).transpose(0, 2, 1, 3, 4).reshape(b * d, n, nh, dh)

        o, m, den = banded_attention(strided(q), strided(k), strided(v), window // (2 * d), d, slopes)
        outs.append(o.reshape(b, d, n, nh, dh).transpose(0, 2, 1, 3, 4).reshape(b, s, nh, dh))
        ms.append(m.reshape(b, d, n, nh).transpose(0, 2, 1, 3).reshape(b, s, nh))
        dens.append(den.reshape(b, d, n, nh).transpose(0, 2, 1, 3).reshape(b, s, nh))
    o = jnp.stack(outs).astype(jnp.float32)
    m = jnp.stack(ms)
    den = jnp.stack(dens)
    wgt = den * jnp.exp(m - jnp.max(m, axis=0))
    return (jnp.sum(wgt[..., None] * o, axis=0) / jnp.sum(wgt, axis=0)[..., None]).astype(q.dtype)


def windowed_gqa_sink(q, k, v, sink):
    o, m, den = banded_attention(q, k, v, SWA_WINDOW, 1, alibi_slopes(q.shape[2]))
    factor = jax.nn.sigmoid(m + jnp.log(den) - sink.astype(jnp.float32))
    return (o.astype(jnp.float32) * factor[..., None]).astype(q.dtype)


def parallel_mixer(h, w_in, conv_w, conv_b, lru_wa, lru_ba, lru_wx, lru_bx, lru_lam, swa_sink, group_norm, w_out):
    b, s, _ = h.shape
    proj = jnp.einsum('bsd,de->bse', h, w_in)
    xa, gate, qb, kb, vb, qc, kc, vc = jnp.split(proj, IN_SPLITS, axis=-1)
    ya = rg_lru_mixer(xa, gate, conv_w, conv_b, lru_wa, lru_ba, lru_wx, lru_bx, lru_lam)
    yb = dilated_attention(qb.reshape(b, s, DIL_HEADS, HEAD_DIM), kb.reshape(b, s, DIL_HEADS, HEAD_DIM),
                           vb.reshape(b, s, DIL_HEADS, HEAD_DIM)).reshape(b, s, DIL_WIDTH)
    yc = windowed_gqa_sink(qc.reshape(b, s, SWA_HEADS, HEAD_DIM), kc.reshape(b, s, SWA_KV_HEADS, HEAD_DIM),
                           vc.reshape(b, s, SWA_KV_HEADS, HEAD_DIM), swa_sink).reshape(b, s, SWA_WIDTH)
    g_a, g_b, g_c = jnp.split(group_norm, GROUP_SPLITS)
    y = jnp.concatenate([rms_norm(ya, g_a), rms_norm(yb, g_b), rms_norm(yc, g_c)], axis=-1)
    return jnp.einsum('bse,ed->bsd', y, w_out)


def memory_cross_attention(h, mem_n, w_mq, w_mk, w_mv, w_mo):
    b, s, _ = h.shape
    q = jnp.einsum('bsd,de->bse', h, w_mq).reshape(b, s, MEM_HEADS, HEAD_DIM)
    k = jnp.einsum('bmd,de->bme', mem_n, w_mk).reshape(b, -1, MEM_HEADS, HEAD_DIM)
    v = jnp.einsum('bmd,de->bme', mem_n, w_mv).reshape(b, -1, MEM_HEADS, HEAD_DIM)
    scores = jnp.einsum('bshd,bmhd->bhsm', q, k, preferred_element_type=jnp.float32) * (HEAD_DIM ** -0.5)
    p = jax.nn.softmax(scores, axis=-1)
    o = jnp.einsum('bhsm,bmhd->bshd', p.astype(v.dtype), v).reshape(b, s, MEM_WIDTH)
    return jnp.einsum('bse,ed->bsd', o, w_mo)


def squared_relu_mlp(h, w_ff1, w_ff2):
    u = jnp.square(jax.nn.relu(jnp.einsum('bsd,df->bsf', h, w_ff1)))
    return jnp.einsum('bsf,fd->bsd', u, w_ff2)


def trunk(x, mem, weights):
    (mix_norm_pre, mix_norm_post, w_in, conv_w, conv_b, lru_wa, lru_ba, lru_wx, lru_bx, lru_lam,
     swa_sink, group_norm, w_out, mem_norm_pre, mem_norm_post, mem_kv_norm, w_mq, w_mk, w_mv, w_mo,
     ffn_norm_pre, ffn_norm_post, w_ff1, w_ff2) = weights
    for l in range(DEPTH):
        y = parallel_mixer(rms_norm(x, mix_norm_pre[l]), w_in[l], conv_w[l], conv_b[l], lru_wa[l], lru_ba[l],
                           lru_wx[l], lru_bx[l], lru_lam[l], swa_sink[l], group_norm[l], w_out[l])
        x = x + rms_norm(y, mix_norm_post[l])
        y = memory_cross_attention(rms_norm(x, mem_norm_pre[l]), rms_norm(mem, mem_kv_norm[l]),
                                   w_mq[l], w_mk[l], w_mv[l], w_mo[l])
        x = x + rms_norm(y, mem_norm_post[l])
        y = squared_relu_mlp(rms_norm(x, ffn_norm_pre[l]), w_ff1[l], w_ff2[l])
        x = x + rms_norm(y, ffn_norm_post[l])
    return x


def setup_inputs(seed: int = 0) -> dict:
    key = jax.random.key(seed)
    ks = iter(jax.random.split(key, 40))

    def nrm(shape, fan_in):
        return jax.random.normal(next(ks), shape, jnp.float32) * (fan_in ** -0.5)

    def gain(shape):
        return 1.0 + 0.05 * jax.random.normal(next(ks), shape, jnp.float32)

    def small(shape, scale=0.01):
        return scale * jax.random.normal(next(ks), shape, jnp.float32)

    blk = LRU_WIDTH // LRU_BLOCKS
    x_prompt = jax.random.normal(next(ks), (BATCH, SEQ, D_MODEL), jnp.float32)
    x_sample = jax.random.normal(next(ks), (DEC_BATCH, DEC_SEQ, D_MODEL), jnp.float32)
    mem_prompt = jax.random.normal(next(ks), (BATCH, N_MEM, D_MODEL), jnp.float32)
    mem_sample = jax.random.normal(next(ks), (DEC_BATCH, N_MEM, D_MODEL), jnp.float32)
    a0 = jax.random.uniform(next(ks), (DEPTH, 2, LRU_WIDTH), jnp.float32, 0.9, 0.999)
    lru_lam = jnp.log(a0) - jnp.log1p(-a0)
    return {
        'x_prompt': x_prompt,
        'x_sample': x_sample,
        'mem_prompt': mem_prompt,
        'mem_sample': mem_sample,
        'mix_norm_pre': gain((DEPTH, D_MODEL)),
        'mix_norm_post': gain((DEPTH, D_MODEL)),
        'w_in': nrm((DEPTH, D_MODEL, IN_WIDTH), D_MODEL),
        'conv_w': nrm((DEPTH, CONV_WIDTH, LRU_WIDTH), CONV_WIDTH),
        'conv_b': small((DEPTH, LRU_WIDTH)),
        'lru_wa': nrm((DEPTH, 2, LRU_BLOCKS, blk, blk), blk),
        'lru_ba': small((DEPTH, 2, LRU_WIDTH)),
        'lru_wx': nrm((DEPTH, 2, LRU_BLOCKS, blk, blk), blk),
        'lru_bx': small((DEPTH, 2, LRU_WIDTH)),
        'lru_lam': lru_lam,
        'swa_sink': jax.random.normal(next(ks), (DEPTH, SWA_HEADS), jnp.float32),
        'group_norm': gain((DEPTH, MIX_WIDTH)),
        'w_out': nrm((DEPTH, MIX_WIDTH, D_MODEL), MIX_WIDTH),
        'mem_norm_pre': gain((DEPTH, D_MODEL)),
        'mem_norm_post': gain((DEPTH, D_MODEL)),
        'mem_kv_norm': gain((DEPTH, D_MODEL)),
        'w_mq': nrm((DEPTH, D_MODEL, MEM_WIDTH), D_MODEL),
        'w_mk': nrm((DEPTH, D_MODEL, MEM_WIDTH), D_MODEL),
        'w_mv': nrm((DEPTH, D_MODEL, MEM_WIDTH), D_MODEL),
        'w_mo': nrm((DEPTH, MEM_WIDTH, D_MODEL), MEM_WIDTH),
        'ffn_norm_pre': gain((DEPTH, D_MODEL)),
        'ffn_norm_post': gain((DEPTH, D_MODEL)),
        'w_ff1': nrm((DEPTH, D_MODEL, D_FF), D_MODEL),
        'w_ff2': nrm((DEPTH, D_FF, D_MODEL), D_FF),
    }


def reference(x_prompt, x_sample, mem_prompt, mem_sample, mix_norm_pre, mix_norm_post, w_in, conv_w, conv_b,
              lru_wa, lru_ba, lru_wx, lru_bx, lru_lam, swa_sink, group_norm, w_out, mem_norm_pre, mem_norm_post,
              mem_kv_norm, w_mq, w_mk, w_mv, w_mo, ffn_norm_pre, ffn_norm_post, w_ff1, w_ff2):
    weights = (mix_norm_pre, mix_norm_post, w_in, conv_w, conv_b, lru_wa, lru_ba, lru_wx, lru_bx, lru_lam,
               swa_sink, group_norm, w_out, mem_norm_pre, mem_norm_post, mem_kv_norm, w_mq, w_mk, w_mv, w_mo,
               ffn_norm_pre, ffn_norm_post, w_ff1, w_ff2)
    y_prompt = trunk(x_prompt, mem_prompt, weights)
    y_sample = trunk(x_sample, mem_sample, weights)
    return (y_prompt, y_sample)
```

```python
import functools
from typing import NamedTuple

import numpy as np
import jax
import jax.numpy as jnp
from jax import lax
from jax.experimental import pallas as pl
from jax.experimental.pallas import tpu as pltpu

F32 = jnp.float32
BF16 = jnp.bfloat16

D_MODEL = 2048
BATCH = 8
SEQ = 4096
DEPTH = 4
DEC_BATCH = 1
DEC_SEQ = 16384
HEAD_DIM = 128
LRU_WIDTH = 512
LRU_BLOCKS = 4
LRU_BLOCK_WIDTH = LRU_WIDTH // LRU_BLOCKS
CONV_WIDTH = 4
CONV_LEFT = 2
LRU_C = 8.0
DIL_HEADS = 6
DIL_PATTERNS = ((128, 1), (512, 4), (2048, 16))
SWA_HEADS = 6
SWA_KV_HEADS = 2
SWA_WINDOW = 128
DIL_WIDTH = DIL_HEADS * HEAD_DIM
SWA_WIDTH = SWA_HEADS * HEAD_DIM
SWA_KV_WIDTH = SWA_KV_HEADS * HEAD_DIM
MIX_WIDTH = LRU_WIDTH + DIL_WIDTH + SWA_WIDTH
IN_WIDTH = 2 * LRU_WIDTH + 3 * DIL_WIDTH + SWA_WIDTH + 2 * SWA_KV_WIDTH
N_MEM = 256
MEM_HEADS = 4
MEM_WIDTH = MEM_HEADS * HEAD_DIM
D_FF = 4 * D_MODEL
EPS = 1e-6

_NEG = -1e30
_QK_SCALE = HEAD_DIM ** -0.5
_STAT_M = 0
_STAT_L = 8
_HALO_ROWS = 8
_V7X_VMEM_BYTES = 64 * 1024 * 1024
_VMEM_LIMIT = _V7X_VMEM_BYTES - 8 * 1024 * 1024


class _Cfg(NamedTuple):
    d_model: int
    d_ff: int
    depth: int
    groups: tuple
    n_mem: int
    tm: int
    tm_ffn: int
    tf: int
    lru_chunk: int
    lru_rows: int
    attn_rows: int


_CFG = _Cfg(d_model=D_MODEL, d_ff=D_FF, depth=DEPTH, groups=((BATCH, SEQ), (DEC_BATCH, DEC_SEQ)),
            n_mem=N_MEM, tm=512, tm_ffn=768, tf=512, lru_chunk=1024, lru_rows=256, attn_rows=512)


def _sequences(cfg):
    out, start = [], 0
    for n, length in cfg.groups:
        for _ in range(n):
            out.append((start, length))
            start += length
    return out, start


def _chunk_flags(cfg, rows, dilation=1):
    seqs, total = _sequences(cfg)
    span = rows * dilation
    starts = {s for s, _ in seqs}
    ends = {s + l for s, l in seqs}
    for s, l in seqs:
        assert l % span == 0, (l, span)
    n = total // span
    flags = np.zeros((n,), np.int32)
    for c in range(n):
        flags[c] = (1 if c * span in starts else 0) | (2 if (c + 1) * span in ends else 0)
    return jnp.asarray(flags)


def _params(semantics):
    return pltpu.CompilerParams(dimension_semantics=semantics, vmem_limit_bytes=_VMEM_LIMIT)


def _rms(x, g):
    ms = jnp.mean(x * x, axis=-1, keepdims=True)
    return x * lax.rsqrt(ms + EPS) * g


def _resident(shape):
    return pl.BlockSpec(shape, lambda *_: (0,) * len(shape), pipeline_mode=pl.Buffered(1))


def _inproj_plan():
    lru_w = 2 * LRU_WIDTH
    dil_w = 3 * DIL_WIDTH
    segs = [
        (0, lru_w, 0, None),
        (lru_w, lru_w + DIL_WIDTH, 1, _QK_SCALE),
        (lru_w + DIL_WIDTH, lru_w + dil_w, 1, None),
        (lru_w + dil_w, lru_w + dil_w + SWA_WIDTH, 2, _QK_SCALE),
        (lru_w + dil_w + SWA_WIDTH, IN_WIDTH, 2, None),
    ]
    base = {0: 0, 1: lru_w, 2: lru_w + dil_w}
    plan = []
    for c0, c1, oi, scale in segs:
        c = c0
        while c < c1:
            n = min(512, c1 - c)
            plan.append((c, c + n, oi, c - base[oi], scale))
            c += n
    return tuple(plan)


def _inproj_kernel(x_ref, g_ref, w_ref, lru_ref, qkvb_ref, qkvc_ref, xn_scr, *, plan):
    xn_scr[...] = _rms(x_ref[...], g_ref[...]).astype(BF16)
    outs = (lru_ref, qkvb_ref, qkvc_ref)
    for c0, c1, oi, o0, scale in plan:
        acc = jnp.dot(xn_scr[...], w_ref[:, c0:c1], preferred_element_type=F32)
        if scale is not None:
            acc = acc * scale
        outs[oi][:, o0:o0 + (c1 - c0)] = acc.astype(outs[oi].dtype)


def _inproj(cfg, x, g, w):
    t, d = x.shape
    tm = cfg.tm
    widths = (2 * LRU_WIDTH, 3 * DIL_WIDTH, SWA_WIDTH + 2 * SWA_KV_WIDTH)
    return pl.pallas_call(
        functools.partial(_inproj_kernel, plan=_inproj_plan()),
        grid=(t // tm,),
        in_specs=[
            pl.BlockSpec((tm, d), lambda i: (i, 0)),
            _resident((1, d)),
            _resident((d, IN_WIDTH)),
        ],
        out_specs=[pl.BlockSpec((tm, wd), lambda i: (i, 0)) for wd in widths],
        out_shape=[
            jax.ShapeDtypeStruct((t, widths[0]), F32),
            jax.ShapeDtypeStruct((t, widths[1]), BF16),
            jax.ShapeDtypeStruct((t, widths[2]), BF16),
        ],
        scratch_shapes=[pltpu.VMEM((tm, d), BF16)],
        compiler_params=_params(("parallel",)),
        name="mixer_inproj",
    )(x, g, w)


def _lru_fill_halo(first, last, xa_ref, xp_ref, xn_ref, xext, lc):
    xext[_HALO_ROWS:_HALO_ROWS + lc, :] = xa_ref[...]

    @pl.when(first)
    def _():
        xext[0:_HALO_ROWS, :] = jnp.zeros((_HALO_ROWS, LRU_WIDTH), F32)

    @pl.when(jnp.logical_not(first))
    def _():
        xext[0:_HALO_ROWS, :] = xp_ref[...]

    @pl.when(last)
    def _():
        xext[_HALO_ROWS + lc:, :] = jnp.zeros((_HALO_ROWS, LRU_WIDTH), F32)

    @pl.when(jnp.logical_not(last))
    def _():
        xext[_HALO_ROWS + lc:, :] = xn_ref[...]


def _lru_gates(xext, cw_ref, cb_ref, wg_ref, ba_ref, bx_ref, lam_ref, a_scr, u_scr, lc, rb):
    lam = lam_ref[...]
    neg = -lam
    softplus = jnp.maximum(neg, 0.0) + jnp.log1p(jnp.exp(-jnp.abs(neg)))
    cb = cb_ref[...]
    taps = [cw_ref[j:j + 1, :] for j in range(CONV_WIDTH)]
    for blk in range(lc // rb):
        r0 = blk * rb
        xc = cb + sum(taps[j] * xext[r0 + _HALO_ROWS - CONV_LEFT + j:r0 + _HALO_ROWS - CONV_LEFT + j + rb, :]
                      for j in range(CONV_WIDTH))
        xcb = xc.astype(BF16)
        for n in range(LRU_BLOCKS):
            cs = slice(n * LRU_BLOCK_WIDTH, (n + 1) * LRU_BLOCK_WIDTH)
            g = jnp.dot(xcb[:, cs], wg_ref[n], preferred_element_type=F32)
            r = jax.nn.sigmoid(g[:, :LRU_BLOCK_WIDTH] + ba_ref[:, cs])
            ig = jax.nn.sigmoid(g[:, LRU_BLOCK_WIDTH:] + bx_ref[:, cs])
            log_a = (-LRU_C * r) * softplus[:, cs]
            a = jnp.exp(log_a)
            one_minus_a2 = -jnp.tanh(log_a) * (1.0 + a * a)
            a_scr[r0:r0 + rb, cs] = a
            u_scr[r0:r0 + rb, cs] = jnp.sqrt(one_minus_a2) * (ig * xc[:, cs])


def _lru_scan(reset, a_scr, u_scr, h_dst, carry, lc, reverse):
    @pl.when(reset)
    def _():
        carry[...] = jnp.zeros((1, LRU_WIDTH), F32)

    def step(i, h):
        t = lc - 1 - i if reverse else i
        h = a_scr[pl.ds(t, 1), :] * h + u_scr[pl.ds(t, 1), :]
        h_dst[pl.ds(t, 1), :] = h
        return h

    carry[...] = lax.fori_loop(0, lc, step, carry[...], unroll=8)


def _lru_fwd_kernel(flags_ref, xa_ref, xp_ref, xn_ref, cw_ref, cb_ref, wg_ref, ba_ref, bx_ref, lam_ref,
                    hf_ref, xext, a_scr, u_scr, carry, *, lc, rb):
    fl = flags_ref[pl.program_id(0)]
    first = (fl & 1) != 0
    last = (fl & 2) != 0
    _lru_fill_halo(first, last, xa_ref, xp_ref, xn_ref, xext, lc)
    _lru_gates(xext, cw_ref, cb_ref, wg_ref, ba_ref, bx_ref, lam_ref, a_scr, u_scr, lc, rb)
    _lru_scan(first, a_scr, u_scr, hf_ref, carry, lc, reverse=False)


def _lru_bwd_kernel(flags_ref, xa_ref, xp_ref, xn_ref, gate_ref, hf_ref, cw_ref, cb_ref, wg_ref, ba_ref,
                    bx_ref, lam_ref, gn_ref, y_ref, xext, a_scr, u_scr, h_scr, carry, *, lc, rb, nchunks):
    fl = flags_ref[nchunks - 1 - pl.program_id(0)]
    first = (fl & 1) != 0
    last = (fl & 2) != 0
    _lru_fill_halo(first, last, xa_ref, xp_ref, xn_ref, xext, lc)
    _lru_gates(xext, cw_ref, cb_ref, wg_ref, ba_ref, bx_ref, lam_ref, a_scr, u_scr, lc, rb)
    _lru_scan(last, a_scr, u_scr, h_scr, carry, lc, reverse=True)
    for blk in range(lc // rb):
        rows = slice(blk * rb, (blk + 1) * rb)
        h = hf_ref[rows, :] + h_scr[rows, :]
        y = h * jax.nn.gelu(gate_ref[rows, :])
        y_ref[rows, :] = _rms(y, gn_ref[...]).astype(BF16)


def _lru(cfg, lru_in, cw, cb, wg, ba, bx, lam, gn):
    t = lru_in.shape[0]
    lc, rb = cfg.lru_chunk, cfg.lru_rows
    nchunks = t // lc
    hb = lc // _HALO_ROWS
    nhalo = t // _HALO_ROWS
    flags = _chunk_flags(cfg, lc)
    row = lambda: _resident((1, LRU_WIDTH))

    def specs(chunk_of):
        return [
            pl.BlockSpec((lc, LRU_WIDTH), lambda i, f: (chunk_of(i), 0)),
            pl.BlockSpec((_HALO_ROWS, LRU_WIDTH), lambda i, f: (jnp.maximum(chunk_of(i) * hb - 1, 0), 0)),
            pl.BlockSpec((_HALO_ROWS, LRU_WIDTH), lambda i, f: (jnp.minimum((chunk_of(i) + 1) * hb, nhalo - 1), 0)),
        ]

    def weights(d):
        return [
            _resident((CONV_WIDTH, LRU_WIDTH)), row(),
            pl.BlockSpec((None, LRU_BLOCKS, LRU_BLOCK_WIDTH, 2 * LRU_BLOCK_WIDTH), lambda i, f: (d, 0, 0, 0)),
            pl.BlockSpec((None, 1, LRU_WIDTH), lambda i, f: (d, 0, 0)),
            pl.BlockSpec((None, 1, LRU_WIDTH), lambda i, f: (d, 0, 0)),
            pl.BlockSpec((None, 1, LRU_WIDTH), lambda i, f: (d, 0, 0)),
        ]

    scratch = [pltpu.VMEM((lc + 2 * _HALO_ROWS, LRU_WIDTH), F32), pltpu.VMEM((lc, LRU_WIDTH), F32),
               pltpu.VMEM((lc, LRU_WIDTH), F32)]
    carry = [pltpu.VMEM((1, LRU_WIDTH), F32)]

    fwd = lambda i: i
    hf = pl.pallas_call(
        functools.partial(_lru_fwd_kernel, lc=lc, rb=rb),
        grid_spec=pltpu.PrefetchScalarGridSpec(
            num_scalar_prefetch=1, grid=(nchunks,),
            in_specs=specs(fwd) + weights(0),
            out_specs=pl.BlockSpec((lc, LRU_WIDTH), lambda i, f: (i, 0)),
            scratch_shapes=scratch + carry),
        out_shape=jax.ShapeDtypeStruct((t, LRU_WIDTH), F32),
        compiler_params=_params(("arbitrary",)),
        name="lru_forward",
    )(flags, lru_in, lru_in, lru_in, cw, cb, wg, ba, bx, lam)

    bwd = lambda i: nchunks - 1 - i
    return pl.pallas_call(
        functools.partial(_lru_bwd_kernel, lc=lc, rb=rb, nchunks=nchunks),
        grid_spec=pltpu.PrefetchScalarGridSpec(
            num_scalar_prefetch=1, grid=(nchunks,),
            in_specs=specs(bwd) + [
                pl.BlockSpec((lc, LRU_WIDTH), lambda i, f: (bwd(i), 1)),
                pl.BlockSpec((lc, LRU_WIDTH), lambda i, f: (bwd(i), 0)),
            ] + weights(1) + [row()],
            out_specs=pl.BlockSpec((lc, LRU_WIDTH), lambda i, f: (bwd(i), 0)),
            scratch_shapes=scratch + [pltpu.VMEM((lc, LRU_WIDTH), F32)] + carry),
        out_shape=jax.ShapeDtypeStruct((t, LRU_WIDTH), BF16),
        compiler_params=_params(("arbitrary",)),
        name="lru_backward",
    )(flags, lru_in, lru_in, lru_in, lru_in, hf, cw, cb, wg, ba, bx, lam, gn)


def _alibi_slopes(n):
    return [2.0 ** (-8.0 * (i + 1) / n) for i in range(n)]


def _band_bias(w, dist_scale, hq, hkv):
    rep = hq // hkv
    slopes = _alibi_slopes(hq)
    qi = np.arange(w)[:, None]
    kj = np.arange(3 * w)[None, :]
    rel = np.abs(kj - w - qi)
    out = np.empty((hkv, rep * w, 3 * w), np.float32)
    for g in range(hkv):
        for r in range(rep):
            out[g, r * w:(r + 1) * w] = np.where(rel <= w, -slopes[g * rep + r] * dist_scale * rel, _NEG)
    return jnp.asarray(out)


def _attn_kernel(flags_ref, q_ref, kc_ref, kp_ref, kn_ref, vc_ref, vp_ref, vn_ref, bias_ref, *rest,
                 w, nsub, hq, hkv, mode):
    rep = hq // hkv
    rows_total = nsub * w
    if mode == "init":
        acc_out, st_out, kbuf, vbuf = rest
    elif mode == "mid":
        acc_in, st_in, acc_out, st_out, kbuf, vbuf = rest
    elif mode == "final":
        acc_in, st_in, gn_ref, y_out, kbuf, vbuf = rest
    else:
        sink_ref, gn_ref, y_out, kbuf, vbuf = rest

    fl = flags_ref[pl.program_id(1)]
    pen_first = jnp.where((fl & 1) != 0, _NEG, 0.0)
    pen_last = jnp.where((fl & 2) != 0, _NEG, 0.0)
    col = lax.broadcasted_iota(jnp.int32, (1, 3 * w), 1)
    pen_first = jnp.where(col < w, pen_first, 0.0)
    pen_last = jnp.where(col >= 2 * w, pen_last, 0.0)

    kbuf[0:w, :] = kp_ref[...]
    kbuf[w:w + rows_total, :] = kc_ref[...]
    kbuf[w + rows_total:, :] = kn_ref[...]
    vbuf[0:w, :] = vp_ref[...]
    vbuf[w:w + rows_total, :] = vc_ref[...]
    vbuf[w + rows_total:, :] = vn_ref[...]

    lane = lax.broadcasted_iota(jnp.int32, (w, 128), 1)
    for j in range(nsub):
        rows = slice(j * w, (j + 1) * w)
        if mode in ("mid", "final"):
            st_prev = st_in[rows, :]
        stats = jnp.zeros((w, 128), F32)
        heads_out = []
        for g in range(hkv):
            gs = slice(g * HEAD_DIM, (g + 1) * HEAD_DIM)
            q_parts = [q_ref[rows, (g * rep + r) * HEAD_DIM:(g * rep + r + 1) * HEAD_DIM] for r in range(rep)]
            q = q_parts[0] if rep == 1 else jnp.concatenate(q_parts, axis=0)
            kw = kbuf[j * w:(j + 3) * w, gs]
            vw = vbuf[j * w:(j + 3) * w, gs]
            s = lax.dot_general(q, kw, (((1,), (1,)), ((), ())), preferred_element_type=F32)
            s = s + bias_ref[g]
            if j == 0:
                s = s + pen_first
            if j == nsub - 1:
                s = s + pen_last
            m_all = jnp.max(s, axis=-1, keepdims=True)
            p = jnp.exp(s - m_all)
            l_all = jnp.sum(p, axis=-1, keepdims=True)
            acc_all = jnp.dot(p.astype(BF16), vw, preferred_element_type=F32)
            for r in range(rep):
                h = g * rep + r
                hs = slice(h * HEAD_DIM, (h + 1) * HEAD_DIM)
                m = m_all[r * w:(r + 1) * w]
                l = l_all[r * w:(r + 1) * w]
                acc = acc_all[r * w:(r + 1) * w]
                if mode in ("mid", "final"):
                    m_p = st_prev[:, _STAT_M + h:_STAT_M + h + 1]
                    l_p = st_prev[:, _STAT_L + h:_STAT_L + h + 1]
                    m_n = jnp.maximum(m_p, m)
                    alpha = jnp.exp(m_p - m_n)
                    beta = jnp.exp(m - m_n)
                    l = alpha * l_p + beta * l
                    acc = alpha * acc_in[rows, hs] + beta * acc
                    m = m_n
                if mode in ("init", "mid"):
                    acc_out[rows, hs] = acc
                    stats = jnp.where(lane == _STAT_M + h, m, stats)
                    stats = jnp.where(lane == _STAT_L + h, l, stats)
                elif mode == "final":
                    heads_out.append(acc / l)
                else:
                    factor = jax.nn.sigmoid(m + jnp.log(l) - sink_ref[:, h:h + 1])
                    heads_out.append((acc / l) * factor)
        if mode in ("init", "mid"):
            st_out[rows, :] = stats
        else:
            y = jnp.concatenate(heads_out, axis=1)
            y_out[rows, :] = _rms(y, gn_ref[...]).astype(BF16)


def _attn_call(cfg, qkv, col0, w, dilation, hq, hkv, mode, extra_in=(), gn=None, sink=None):
    t, c = qkv.shape
    d = dilation
    tn = t // d
    seqs, _ = _sequences(cfg)
    rows = min([cfg.attn_rows] + [l // d for _, l in seqs])
    nsub = rows // w
    nchunks = tn // rows
    nblk = tn // w
    flags = _chunk_flags(cfg, rows, d)
    qw, kvw = hq * HEAD_DIM, hkv * HEAD_DIM
    assert col0 % kvw == 0 and col0 % qw == 0 and c % kvw == 0
    qcol, kcol, vcol = col0 // qw, (col0 + qw) // kvw, (col0 + qw + kvw) // kvw
    qper, kvper = c // qw if c % qw == 0 else None, c // kvw
    assert qper is not None or d == 1
    qper = qper or 0
    view = qkv.reshape(tn, d * c)

    def cur(colblk, per):
        return lambda r, i, f: (i, r * per + colblk)

    def prev(colblk):
        return lambda r, i, f: (jnp.maximum(i * nsub - 1, 0), r * kvper + colblk)

    def nxt(colblk):
        return lambda r, i, f: (jnp.minimum((i + 1) * nsub, nblk - 1), r * kvper + colblk)

    in_specs = [
        pl.BlockSpec((rows, qw), cur(qcol, qper)),
        pl.BlockSpec((rows, kvw), cur(kcol, kvper)),
        pl.BlockSpec((w, kvw), prev(kcol)),
        pl.BlockSpec((w, kvw), nxt(kcol)),
        pl.BlockSpec((rows, kvw), cur(vcol, kvper)),
        pl.BlockSpec((w, kvw), prev(vcol)),
        pl.BlockSpec((w, kvw), nxt(vcol)),
        _resident((hkv, (hq // hkv) * w, 3 * w)),
    ]
    args = [view, view, view, view, view, view, view, _band_bias(w, d, hq, hkv)]
    acc_spec = pl.BlockSpec((rows, qw), lambda r, i, f: (i, r))
    st_spec = pl.BlockSpec((rows, 128), lambda r, i, f: (i, r))
    acc_shape = jax.ShapeDtypeStruct((tn, d * qw), F32)
    st_shape = jax.ShapeDtypeStruct((tn, d * 128), F32)
    y_shape = jax.ShapeDtypeStruct((tn, d * qw), BF16)
    if mode in ("mid", "final"):
        acc_prev, st_prev = extra_in
        in_specs += [acc_spec, st_spec]
        args += [acc_prev.reshape(tn, d * qw), st_prev.reshape(tn, d * 128)]
    if mode == "swa":
        in_specs += [_resident((1, 128))]
        args += [sink]
    if mode in ("final", "swa"):
        in_specs += [_resident((1, qw))]
        args += [gn]
        out_specs, out_shape = acc_spec, y_shape
    else:
        out_specs, out_shape = [acc_spec, st_spec], [acc_shape, st_shape]

    out = pl.pallas_call(
        functools.partial(_attn_kernel, w=w, nsub=nsub, hq=hq, hkv=hkv, mode=mode),
        grid_spec=pltpu.PrefetchScalarGridSpec(
            num_scalar_prefetch=1, grid=(d, nchunks),
            in_specs=in_specs, out_specs=out_specs,
            scratch_shapes=[pltpu.VMEM((rows + 2 * w, kvw), BF16), pltpu.VMEM((rows + 2 * w, kvw), BF16)]),
        out_shape=out_shape,
        compiler_params=_params(("parallel", "parallel")),
        name=f"banded_attention_{mode}_d{d}",
    )(flags, *args)
    if mode in ("final", "swa"):
        return out.reshape(t, qw)
    return out[0].reshape(t, qw), out[1].reshape(t, 128)


def _dilated(cfg, qkvb, gn):
    state = ()
    out = None
    for idx, (window, d) in enumerate(DIL_PATTERNS):
        mode = "init" if idx == 0 else ("final" if idx == len(DIL_PATTERNS) - 1 else "mid")
        out = _attn_call(cfg, qkvb, 0, window // (2 * d), d, DIL_HEADS, DIL_HEADS, mode, extra_in=state,
                         gn=gn if mode == "final" else None)
        state = out
    return out


def _swa(cfg, qkvc, gn, sink):
    return _attn_call(cfg, qkvc, 0, SWA_WINDOW, 1, SWA_HEADS, SWA_KV_HEADS, "swa", gn=gn, sink=sink)


def _outproj_kernel(ya_ref, yb_ref, yc_ref, x_ref, w_ref, g_ref, o_ref, y_scr, *, nchunk):
    d = o_ref.shape[1]
    b0, b1 = LRU_WIDTH, LRU_WIDTH + DIL_WIDTH
    for c in range(0, d, nchunk):
        cs = slice(c, c + nchunk)
        acc = jnp.dot(ya_ref[...], w_ref[0:b0, cs], preferred_element_type=F32)
        acc += jnp.dot(yb_ref[...], w_ref[b0:b1, cs], preferred_element_type=F32)
        acc += jnp.dot(yc_ref[...], w_ref[b1:, cs], preferred_element_type=F32)
        y_scr[:, cs] = acc
    o_ref[...] = x_ref[...] + _rms(y_scr[...], g_ref[...])


def _outproj(cfg, ya, yb, yc, x, w, g):
    t, d = x.shape
    tm = cfg.tm
    return pl.pallas_call(
        functools.partial(_outproj_kernel, nchunk=min(512, d)),
        grid=(t // tm,),
        in_specs=[
            pl.BlockSpec((tm, LRU_WIDTH), lambda i: (i, 0)),
            pl.BlockSpec((tm, DIL_WIDTH), lambda i: (i, 0)),
            pl.BlockSpec((tm, SWA_WIDTH), lambda i: (i, 0)),
            pl.BlockSpec((tm, d), lambda i: (i, 0)),
            _resident((MIX_WIDTH, d)),
            _resident((1, d)),
        ],
        out_specs=pl.BlockSpec((tm, d), lambda i: (i, 0)),
        out_shape=jax.ShapeDtypeStruct((t, d), F32),
        scratch_shapes=[pltpu.VMEM((tm, d), F32)],
        compiler_params=_params(("parallel",)),
        name="mixer_outproj",
    )(ya, yb, yc, x, w, g)


def _memkv_kernel(mem_ref, g_ref, w_ref, k_ref, v_ref):
    mn = _rms(mem_ref[...], g_ref[...]).astype(BF16)
    kv = jnp.dot(mn, w_ref[...], preferred_element_type=F32)
    k_ref[...] = kv[:, :MEM_WIDTH].astype(BF16)
    v_ref[...] = kv[:, MEM_WIDTH:].astype(BF16)


def _memkv(cfg, mem, g, wkv):
    rows, d = mem.shape
    nm = cfg.n_mem
    shape = jax.ShapeDtypeStruct((rows, MEM_WIDTH), BF16)
    return pl.pallas_call(
        _memkv_kernel,
        grid=(rows // nm,),
        in_specs=[pl.BlockSpec((nm, d), lambda i: (i, 0)), _resident((1, d)), _resident((d, 2 * MEM_WIDTH))],
        out_specs=[pl.BlockSpec((nm, MEM_WIDTH), lambda i: (i, 0))] * 2,
        out_shape=[shape, shape],
        compiler_params=_params(("parallel",)),
        name="memory_kv",
    )(mem, g, wkv)


def _cross_kernel(seq_ref, x_ref, gpre_ref, wq_ref, k_ref, v_ref, wo_ref, gpost_ref, o_ref, xn_scr, o_scr):
    del seq_ref
    x = x_ref[...]
    xn_scr[...] = _rms(x, gpre_ref[...]).astype(BF16)
    q = (jnp.dot(xn_scr[...], wq_ref[...], preferred_element_type=F32) * _QK_SCALE).astype(BF16)
    for h in range(MEM_HEADS):
        hs = slice(h * HEAD_DIM, (h + 1) * HEAD_DIM)
        s = lax.dot_general(q[:, hs], k_ref[:, hs], (((1,), (1,)), ((), ())), preferred_element_type=F32)
        m = jnp.max(s, axis=-1, keepdims=True)
        p = jnp.exp(s - m)
        l = jnp.sum(p, axis=-1, keepdims=True)
        o = jnp.dot(p.astype(BF16), v_ref[:, hs], preferred_element_type=F32)
        o_scr[:, hs] = (o / l).astype(BF16)
    y = jnp.dot(o_scr[...], wo_ref[...], preferred_element_type=F32)
    o_ref[...] = x + _rms(y, gpost_ref[...])


def _cross(cfg, x, gpre, wq, kmem, vmem, wo, gpost):
    t, d = x.shape
    tm, nm = cfg.tm, cfg.n_mem
    seqs, _ = _sequences(cfg)
    seq_of_tile = []
    for si, (_, length) in enumerate(seqs):
        assert length % tm == 0
        seq_of_tile += [si] * (length // tm)
    seq_of_tile = jnp.asarray(np.asarray(seq_of_tile, np.int32))
    return pl.pallas_call(
        _cross_kernel,
        grid_spec=pltpu.PrefetchScalarGridSpec(
            num_scalar_prefetch=1, grid=(t // tm,),
            in_specs=[
                pl.BlockSpec((tm, d), lambda i, s: (i, 0)),
                _resident((1, d)),
                _resident((d, MEM_WIDTH)),
                pl.BlockSpec((nm, MEM_WIDTH), lambda i, s: (s[i], 0)),
                pl.BlockSpec((nm, MEM_WIDTH), lambda i, s: (s[i], 0)),
                _resident((MEM_WIDTH, d)),
                _resident((1, d)),
            ],
            out_specs=pl.BlockSpec((tm, d), lambda i, s: (i, 0)),
            scratch_shapes=[pltpu.VMEM((tm, d), BF16), pltpu.VMEM((tm, MEM_WIDTH), BF16)]),
        out_shape=jax.ShapeDtypeStruct((t, d), F32),
        compiler_params=_params(("parallel",)),
        name="memory_cross_attention",
    )(seq_of_tile, x, gpre, wq, kmem, vmem, wo, gpost)


def _ffn_kernel(x_ref, gpre_ref, w1_ref, w2_ref, gpost_ref, o_ref, xn_scr, acc_scr):
    f = pl.program_id(1)

    @pl.when(f == 0)
    def _():
        xn_scr[...] = _rms(x_ref[...], gpre_ref[...]).astype(BF16)

    h = jnp.dot(xn_scr[...], w1_ref[...], preferred_element_type=F32)
    h = jnp.square(jnp.maximum(h, 0.0)).astype(BF16)
    part = jnp.dot(h, w2_ref[...], preferred_element_type=F32)

    @pl.when(f == 0)
    def _():
        acc_scr[...] = part

    @pl.when(f > 0)
    def _():
        acc_scr[...] += part

    @pl.when(f == pl.num_programs(1) - 1)
    def _():
        o_ref[...] = x_ref[...] + _rms(acc_scr[...], gpost_ref[...])


def _ffn(cfg, x, gpre, w1, w2, gpost):
    t, d = x.shape
    tm, tf = cfg.tm_ffn, cfg.tf
    dff = w1.shape[1]
    return pl.pallas_call(
        _ffn_kernel,
        grid=(t // tm, dff // tf),
        in_specs=[
            pl.BlockSpec((tm, d), lambda i, f: (i, 0)),
            _resident((1, d)),
            pl.BlockSpec((d, tf), lambda i, f: (0, f)),
            pl.BlockSpec((tf, d), lambda i, f: (f, 0)),
            _resident((1, d)),
        ],
        out_specs=pl.BlockSpec((tm, d), lambda i, f: (i, 0)),
        out_shape=jax.ShapeDtypeStruct((t, d), F32),
        scratch_shapes=[pltpu.VMEM((tm, d), BF16), pltpu.VMEM((tm, d), F32)],
        compiler_params=_params(("parallel", "arbitrary")),
        name="squared_relu_mlp",
    )(x, gpre, w1, w2, gpost)


def _forward(cfg, x, mem, p):
    d = cfg.d_model
    row = lambda a: a.reshape(1, -1).astype(F32)
    for l in range(cfg.depth):
        lru_in, qkvb, qkvc = _inproj(cfg, x, row(p["mix_norm_pre"][l]), p["w_in"][l])
        gn = p["group_norm"][l]
        wg = jnp.concatenate([p["lru_wa"][l], p["lru_wx"][l]], axis=-1).astype(BF16)
        ya = _lru(cfg, lru_in, p["conv_w"][l], row(p["conv_b"][l]), wg,
                  p["lru_ba"][l][:, None, :], p["lru_bx"][l][:, None, :], p["lru_lam"][l][:, None, :],
                  row(gn[:LRU_WIDTH]))
        yb = _dilated(cfg, qkvb, row(gn[LRU_WIDTH:LRU_WIDTH + DIL_WIDTH]))
        sink = jnp.zeros((1, 128), F32).at[0, :SWA_HEADS].set(p["swa_sink"][l].astype(F32))
        yc = _swa(cfg, qkvc, row(gn[LRU_WIDTH + DIL_WIDTH:]), sink)
        x = _outproj(cfg, ya, yb, yc, x, p["w_out"][l], row(p["mix_norm_post"][l]))
        kmem, vmem = _memkv(cfg, mem, row(p["mem_kv_norm"][l]), p["w_mkv"][l])
        x = _cross(cfg, x, row(p["mem_norm_pre"][l]), p["w_mq"][l], kmem, vmem, p["w_mo"][l],
                   row(p["mem_norm_post"][l]))
        x = _ffn(cfg, x, row(p["ffn_norm_pre"][l]), p["w_ff1"][l], p["w_ff2"][l], row(p["ffn_norm_post"][l]))
    del d
    return x


def _run(cfg, x_prompt, x_sample, mem_prompt, mem_sample, mix_norm_pre, mix_norm_post, w_in, conv_w, conv_b,
         lru_wa, lru_ba, lru_wx, lru_bx, lru_lam, swa_sink, group_norm, w_out, mem_norm_pre, mem_norm_post,
         mem_kv_norm, w_mq, w_mk, w_mv, w_mo, ffn_norm_pre, ffn_norm_post, w_ff1, w_ff2):
    d = cfg.d_model
    x = jnp.concatenate([x_prompt.reshape(-1, d), x_sample.reshape(-1, d)], axis=0)
    mem = jnp.concatenate([mem_prompt.reshape(-1, d), mem_sample.reshape(-1, d)], axis=0)
    p = dict(
        mix_norm_pre=mix_norm_pre, mix_norm_post=mix_norm_post, w_in=w_in.astype(BF16), conv_w=conv_w,
        conv_b=conv_b, lru_wa=lru_wa, lru_ba=lru_ba, lru_wx=lru_wx, lru_bx=lru_bx, lru_lam=lru_lam,
        swa_sink=swa_sink, group_norm=group_norm, w_out=w_out.astype(BF16), mem_norm_pre=mem_norm_pre,
        mem_norm_post=mem_norm_post, mem_kv_norm=mem_kv_norm, w_mq=w_mq.astype(BF16),
        w_mkv=jnp.concatenate([w_mk, w_mv], axis=-1).astype(BF16), w_mo=w_mo.astype(BF16),
        ffn_norm_pre=ffn_norm_pre, ffn_norm_post=ffn_norm_post, w_ff1=w_ff1.astype(BF16),
        w_ff2=w_ff2.astype(BF16))
    y = _forward(cfg, x, mem, p)
    n_prompt = x_prompt.shape[0] * x_prompt.shape[1]
    return y[:n_prompt].reshape(x_prompt.shape), y[n_prompt:].reshape(x_sample.shape)


def kernel(x_prompt, x_sample, mem_prompt, mem_sample, mix_norm_pre, mix_norm_post, w_in, conv_w, conv_b, lru_wa,
           lru_ba, lru_wx, lru_bx, lru_lam, swa_sink, group_norm, w_out, mem_norm_pre, mem_norm_post, mem_kv_norm,
           w_mq, w_mk, w_mv, w_mo, ffn_norm_pre, ffn_norm_post, w_ff1, w_ff2):
    return _run(_CFG, x_prompt, x_sample, mem_prompt, mem_sample, mix_norm_pre, mix_norm_post, w_in, conv_w,
                conv_b, lru_wa, lru_ba, lru_wx, lru_bx, lru_lam, swa_sink, group_norm, w_out, mem_norm_pre,
                mem_norm_post, mem_kv_norm, w_mq, w_mk, w_mv, w_mo, ffn_norm_pre, ffn_norm_post, w_ff1, w_ff2)
```

```python
import functools
from typing import NamedTuple

import numpy as np
import jax
import jax.numpy as jnp
from jax import lax
from jax.experimental import pallas as pl
from jax.experimental.pallas import tpu as pltpu

F32 = jnp.float32
BF16 = jnp.bfloat16

D_MODEL = 2048
BATCH = 8
SEQ = 4096
DEPTH = 4
DEC_BATCH = 1
DEC_SEQ = 16384
HEAD_DIM = 128
LRU_WIDTH = 512
LRU_BLOCKS = 4
LRU_BLOCK_WIDTH = LRU_WIDTH // LRU_BLOCKS
CONV_WIDTH = 4
CONV_LEFT = 2
LRU_C = 8.0
DIL_HEADS = 6
DIL_PATTERNS = ((128, 1), (512, 4), (2048, 16))
SWA_HEADS = 6
SWA_KV_HEADS = 2
SWA_WINDOW = 128
DIL_WIDTH = DIL_HEADS * HEAD_DIM
SWA_WIDTH = SWA_HEADS * HEAD_DIM
SWA_KV_WIDTH = SWA_KV_HEADS * HEAD_DIM
MIX_WIDTH = LRU_WIDTH + DIL_WIDTH + SWA_WIDTH
IN_WIDTH = 2 * LRU_WIDTH + 3 * DIL_WIDTH + SWA_WIDTH + 2 * SWA_KV_WIDTH
N_MEM = 256
MEM_HEADS = 4
MEM_WIDTH = MEM_HEADS * HEAD_DIM
D_FF = 4 * D_MODEL
EPS = 1e-6

_NEG = -1e30
_QK_SCALE = HEAD_DIM ** -0.5
_HALO_ROWS = 8
_DIL_W = DIL_PATTERNS[0][0] // (2 * DIL_PATTERNS[0][1])
assert all(wn // (2 * d) == _DIL_W for wn, d in DIL_PATTERNS)
_PERM_ROWS = 256
_V7X_VMEM_BYTES = 64 * 1024 * 1024
_VMEM_LIMIT = _V7X_VMEM_BYTES - 8 * 1024 * 1024
_NT = (((1,), (1,)), ((), ()))


class _Cfg(NamedTuple):
    d_model: int
    d_ff: int
    depth: int
    groups: tuple
    n_mem: int
    tm: int
    tm_ffn: int
    tf: int
    lru_chunk: int
    lru_rows: int
    swa_rows: int
    dil_rows: int
    swa_batch: int
    dil_batch: int


_CFG = _Cfg(d_model=D_MODEL, d_ff=D_FF, depth=DEPTH, groups=((BATCH, SEQ), (DEC_BATCH, DEC_SEQ)),
            n_mem=N_MEM, tm=512, tm_ffn=768, tf=512, lru_chunk=1024, lru_rows=256, swa_rows=512,
            dil_rows=1024, swa_batch=2, dil_batch=4)


def _sequences(cfg):
    out, start = [], 0
    for n, length in cfg.groups:
        for _ in range(n):
            out.append((start, length))
            start += length
    return out, start


def _chunk_flags(cfg, rows):
    seqs, total = _sequences(cfg)
    starts = {s for s, _ in seqs}
    ends = {s + l for s, l in seqs}
    for s, l in seqs:
        assert l % rows == 0, (l, rows)
    n = total // rows
    flags = np.zeros((n,), np.int32)
    for c in range(n):
        flags[c] = (1 if c * rows in starts else 0) | (2 if (c + 1) * rows in ends else 0)
    return jnp.asarray(flags)


def _params(semantics):
    return pltpu.CompilerParams(dimension_semantics=semantics, vmem_limit_bytes=_VMEM_LIMIT)


def _rms(x, g):
    ms = jnp.mean(x * x, axis=-1, keepdims=True)
    return x * lax.rsqrt(ms + EPS) * g


def _resident(shape):
    return pl.BlockSpec(shape, lambda *_: (0,) * len(shape), pipeline_mode=pl.Buffered(1))


def _inproj_plan():
    lru_w = 2 * LRU_WIDTH
    dil_w = 3 * DIL_WIDTH
    segs = [
        (0, lru_w, 0, None),
        (lru_w, lru_w + DIL_WIDTH, 1, _QK_SCALE),
        (lru_w + DIL_WIDTH, lru_w + dil_w, 1, None),
        (lru_w + dil_w, lru_w + dil_w + SWA_WIDTH, 2, _QK_SCALE),
        (lru_w + dil_w + SWA_WIDTH, IN_WIDTH, 2, None),
    ]
    base = {0: 0, 1: lru_w, 2: lru_w + dil_w}
    plan = []
    for c0, c1, oi, scale in segs:
        c = c0
        while c < c1:
            n = min(512, c1 - c)
            plan.append((c, c + n, oi, c - base[oi], scale))
            c += n
    return tuple(plan)


def _inproj_kernel(x_ref, g_ref, w_ref, lru_ref, qkvb_ref, qkvc_ref, xn_scr, *, plan):
    xn_scr[...] = _rms(x_ref[...], g_ref[...]).astype(BF16)
    outs = (lru_ref, qkvb_ref, qkvc_ref)
    for c0, c1, oi, o0, scale in plan:
        acc = jnp.dot(xn_scr[...], w_ref[:, c0:c1], preferred_element_type=F32)
        if scale is not None:
            acc = acc * scale
        outs[oi][:, o0:o0 + (c1 - c0)] = acc.astype(outs[oi].dtype)


def _inproj(cfg, x, g, w):
    t, d = x.shape
    tm = cfg.tm
    widths = (2 * LRU_WIDTH, 3 * DIL_WIDTH, SWA_WIDTH + 2 * SWA_KV_WIDTH)
    return pl.pallas_call(
        functools.partial(_inproj_kernel, plan=_inproj_plan()),
        grid=(t // tm,),
        in_specs=[
            pl.BlockSpec((tm, d), lambda i: (i, 0)),
            _resident((1, d)),
            _resident((d, IN_WIDTH)),
        ],
        out_specs=[pl.BlockSpec((tm, wd), lambda i: (i, 0)) for wd in widths],
        out_shape=[
            jax.ShapeDtypeStruct((t, widths[0]), F32),
            jax.ShapeDtypeStruct((t, widths[1]), BF16),
            jax.ShapeDtypeStruct((t, widths[2]), BF16),
        ],
        scratch_shapes=[pltpu.VMEM((tm, d), BF16)],
        compiler_params=_params(("parallel",)),
        name="mixer_inproj",
    )(x, g, w)


def _lru_fill_halo(first, last, xa_ref, xp_ref, xn_ref, xext, lc):
    xext[_HALO_ROWS:_HALO_ROWS + lc, :] = xa_ref[...]

    @pl.when(first)
    def _():
        xext[0:_HALO_ROWS, :] = jnp.zeros((_HALO_ROWS, LRU_WIDTH), F32)

    @pl.when(jnp.logical_not(first))
    def _():
        xext[0:_HALO_ROWS, :] = xp_ref[...]

    @pl.when(last)
    def _():
        xext[_HALO_ROWS + lc:, :] = jnp.zeros((_HALO_ROWS, LRU_WIDTH), F32)

    @pl.when(jnp.logical_not(last))
    def _():
        xext[_HALO_ROWS + lc:, :] = xn_ref[...]


def _lru_gates(xext, cw_ref, cb_ref, wg_ref, ba_ref, bx_ref, lam_ref, a_scr, u_scr, lc, rb):
    lam = lam_ref[...]
    neg = -lam
    softplus = jnp.maximum(neg, 0.0) + jnp.log1p(jnp.exp(-jnp.abs(neg)))
    cb = cb_ref[...]
    taps = [cw_ref[j:j + 1, :] for j in range(CONV_WIDTH)]
    for blk in range(lc // rb):
        r0 = blk * rb
        xc = cb + sum(taps[j] * xext[r0 + _HALO_ROWS - CONV_LEFT + j:r0 + _HALO_ROWS - CONV_LEFT + j + rb, :]
                      for j in range(CONV_WIDTH))
        xcb = xc.astype(BF16)
        for n in range(LRU_BLOCKS):
            cs = slice(n * LRU_BLOCK_WIDTH, (n + 1) * LRU_BLOCK_WIDTH)
            g = jnp.dot(xcb[:, cs], wg_ref[n], preferred_element_type=F32)
            r = jax.nn.sigmoid(g[:, :LRU_BLOCK_WIDTH] + ba_ref[:, cs])
            ig = jax.nn.sigmoid(g[:, LRU_BLOCK_WIDTH:] + bx_ref[:, cs])
            log_a = (-LRU_C * r) * softplus[:, cs]
            a = jnp.exp(log_a)
            one_minus_a2 = -jnp.tanh(log_a) * (1.0 + a * a)
            a_scr[r0:r0 + rb, cs] = a
            u_scr[r0:r0 + rb, cs] = jnp.sqrt(one_minus_a2) * (ig * xc[:, cs])


def _lru_scan(reset, a_scr, u_scr, h_dst, carry, lc, reverse):
    @pl.when(reset)
    def _():
        carry[...] = jnp.zeros((1, LRU_WIDTH), F32)

    def step(i, h):
        t = lc - 1 - i if reverse else i
        h = a_scr[pl.ds(t, 1), :] * h + u_scr[pl.ds(t, 1), :]
        h_dst[pl.ds(t, 1), :] = h
        return h

    carry[...] = lax.fori_loop(0, lc, step, carry[...], unroll=8)


def _lru_fwd_kernel(flags_ref, xa_ref, xp_ref, xn_ref, cw_ref, cb_ref, wg_ref, ba_ref, bx_ref, lam_ref,
                    hf_ref, xext, a_scr, u_scr, carry, *, lc, rb):
    fl = flags_ref[pl.program_id(0)]
    first = (fl & 1) != 0
    last = (fl & 2) != 0
    _lru_fill_halo(first, last, xa_ref, xp_ref, xn_ref, xext, lc)
    _lru_gates(xext, cw_ref, cb_ref, wg_ref, ba_ref, bx_ref, lam_ref, a_scr, u_scr, lc, rb)
    _lru_scan(first, a_scr, u_scr, hf_ref, carry, lc, reverse=False)


def _lru_bwd_kernel(flags_ref, xa_ref, xp_ref, xn_ref, gate_ref, hf_ref, cw_ref, cb_ref, wg_ref, ba_ref,
                    bx_ref, lam_ref, gn_ref, y_ref, xext, a_scr, u_scr, h_scr, carry, *, lc, rb, nchunks):
    fl = flags_ref[nchunks - 1 - pl.program_id(0)]
    first = (fl & 1) != 0
    last = (fl & 2) != 0
    _lru_fill_halo(first, last, xa_ref, xp_ref, xn_ref, xext, lc)
    _lru_gates(xext, cw_ref, cb_ref, wg_ref, ba_ref, bx_ref, lam_ref, a_scr, u_scr, lc, rb)
    _lru_scan(last, a_scr, u_scr, h_scr, carry, lc, reverse=True)
    for blk in range(lc // rb):
        rows = slice(blk * rb, (blk + 1) * rb)
        h = hf_ref[rows, :] + h_scr[rows, :]
        y = h * jax.nn.gelu(gate_ref[rows, :])
        y_ref[rows, :] = _rms(y, gn_ref[...]).astype(BF16)


def _lru(cfg, lru_in, cw, cb, wg, ba, bx, lam, gn):
    t = lru_in.shape[0]
    lc, rb = cfg.lru_chunk, cfg.lru_rows
    nchunks = t // lc
    hb = lc // _HALO_ROWS
    nhalo = t // _HALO_ROWS
    flags = _chunk_flags(cfg, lc)
    row = lambda: _resident((1, LRU_WIDTH))

    def specs(chunk_of):
        return [
            pl.BlockSpec((lc, LRU_WIDTH), lambda i, f: (chunk_of(i), 0)),
            pl.BlockSpec((_HALO_ROWS, LRU_WIDTH), lambda i, f: (jnp.maximum(chunk_of(i) * hb - 1, 0), 0)),
            pl.BlockSpec((_HALO_ROWS, LRU_WIDTH), lambda i, f: (jnp.minimum((chunk_of(i) + 1) * hb, nhalo - 1), 0)),
        ]

    def weights(d):
        return [
            _resident((CONV_WIDTH, LRU_WIDTH)), row(),
            pl.BlockSpec((None, LRU_BLOCKS, LRU_BLOCK_WIDTH, 2 * LRU_BLOCK_WIDTH), lambda i, f: (d, 0, 0, 0)),
            pl.BlockSpec((None, 1, LRU_WIDTH), lambda i, f: (d, 0, 0)),
            pl.BlockSpec((None, 1, LRU_WIDTH), lambda i, f: (d, 0, 0)),
            pl.BlockSpec((None, 1, LRU_WIDTH), lambda i, f: (d, 0, 0)),
        ]

    scratch = [pltpu.VMEM((lc + 2 * _HALO_ROWS, LRU_WIDTH), F32), pltpu.VMEM((lc, LRU_WIDTH), F32),
               pltpu.VMEM((lc, LRU_WIDTH), F32)]
    carry = [pltpu.VMEM((1, LRU_WIDTH), F32)]

    fwd = lambda i: i
    hf = pl.pallas_call(
        functools.partial(_lru_fwd_kernel, lc=lc, rb=rb),
        grid_spec=pltpu.PrefetchScalarGridSpec(
            num_scalar_prefetch=1, grid=(nchunks,),
            in_specs=specs(fwd) + weights(0),
            out_specs=pl.BlockSpec((lc, LRU_WIDTH), lambda i, f: (i, 0)),
            scratch_shapes=scratch + carry),
        out_shape=jax.ShapeDtypeStruct((t, LRU_WIDTH), F32),
        compiler_params=_params(("arbitrary",)),
        name="lru_forward",
    )(flags, lru_in, lru_in, lru_in, cw, cb, wg, ba, bx, lam)

    bwd = lambda i: nchunks - 1 - i
    return pl.pallas_call(
        functools.partial(_lru_bwd_kernel, lc=lc, rb=rb, nchunks=nchunks),
        grid_spec=pltpu.PrefetchScalarGridSpec(
            num_scalar_prefetch=1, grid=(nchunks,),
            in_specs=specs(bwd) + [
                pl.BlockSpec((lc, LRU_WIDTH), lambda i, f: (bwd(i), 1)),
                pl.BlockSpec((lc, LRU_WIDTH), lambda i, f: (bwd(i), 0)),
            ] + weights(1) + [row()],
            out_specs=pl.BlockSpec((lc, LRU_WIDTH), lambda i, f: (bwd(i), 0)),
            scratch_shapes=scratch + [pltpu.VMEM((lc, LRU_WIDTH), F32)] + carry),
        out_shape=jax.ShapeDtypeStruct((t, LRU_WIDTH), BF16),
        compiler_params=_params(("arbitrary",)),
        name="lru_backward",
    )(flags, lru_in, lru_in, lru_in, lru_in, hf, cw, cb, wg, ba, bx, lam, gn)


def _alibi_slopes(n):
    return [2.0 ** (-8.0 * (i + 1) / n) for i in range(n)]


def _band_bias(w, dist_scale, hq, hkv):
    rep = hq // hkv
    slopes = _alibi_slopes(hq)
    qi = np.arange(w)[:, None]
    kj = np.arange(3 * w)[None, :]
    rel = np.abs(kj - w - qi)
    out = np.empty((hkv, rep * w, 3 * w), np.float32)
    for g in range(hkv):
        for r in range(rep):
            out[g, r * w:(r + 1) * w] = np.where(rel <= w, -slopes[g * rep + r] * dist_scale * rel, _NEG)
    return out


def _edge_penalties(fl, w):
    col = lax.broadcasted_iota(jnp.int32, (1, 3 * w), 1)
    pen_first = jnp.where(col < w, jnp.where((fl & 1) != 0, _NEG, 0.0), 0.0)
    pen_last = jnp.where(col >= 2 * w, jnp.where((fl & 2) != 0, _NEG, 0.0), 0.0)
    return pen_first, pen_last


def _attend(qs, ks, vs, biases):
    scores = [lax.dot_general(q, k, _NT, preferred_element_type=F32) for q, k in zip(qs, ks)]
    ms, ls, ps = [], [], []
    for s, b in zip(scores, biases):
        s = s + b
        m = jnp.max(s, axis=-1, keepdims=True)
        p = jnp.exp(s - m)
        ms.append(m)
        ls.append(jnp.sum(p, axis=-1, keepdims=True))
        ps.append(p.astype(BF16))
    accs = [jnp.dot(p, v, preferred_element_type=F32) for p, v in zip(ps, vs)]
    return ms, ls, accs


def _swa_kernel(flags_ref, q_ref, kc_ref, kp_ref, kn_ref, vc_ref, vp_ref, vn_ref, bias_ref, sink_ref, gn_ref,
                y_out, kbuf, vbuf, *, w, nsub, nb):
    rep = SWA_HEADS // SWA_KV_HEADS
    rows_total = nsub * w
    pen_first, pen_last = _edge_penalties(flags_ref[pl.program_id(0)], w)
    kbuf[0:w, :] = kp_ref[...]
    kbuf[w:w + rows_total, :] = kc_ref[...]
    kbuf[w + rows_total:, :] = kn_ref[...]
    vbuf[0:w, :] = vp_ref[...]
    vbuf[w:w + rows_total, :] = vc_ref[...]
    vbuf[w + rows_total:, :] = vn_ref[...]

    for j0 in range(0, nsub, nb):
        blocks = list(range(j0, min(j0 + nb, nsub)))
        qs, ks, vs, bs = [], [], [], []
        for j in blocks:
            rows = slice(j * w, (j + 1) * w)
            for g in range(SWA_KV_HEADS):
                gs = slice(g * HEAD_DIM, (g + 1) * HEAD_DIM)
                qs.append(jnp.concatenate(
                    [q_ref[rows, (g * rep + r) * HEAD_DIM:(g * rep + r + 1) * HEAD_DIM] for r in range(rep)], axis=0))
                ks.append(kbuf[j * w:(j + 3) * w, gs])
                vs.append(vbuf[j * w:(j + 3) * w, gs])
                b = bias_ref[g]
                if j == 0:
                    b = b + pen_first
                if j == nsub - 1:
                    b = b + pen_last
                bs.append(b)
        ms, ls, accs = _attend(qs, ks, vs, bs)
        for bi, j in enumerate(blocks):
            heads_out = []
            for g in range(SWA_KV_HEADS):
                idx = bi * SWA_KV_HEADS + g
                for r in range(rep):
                    h = g * rep + r
                    part = slice(r * w, (r + 1) * w)
                    m, l, acc = ms[idx][part], ls[idx][part], accs[idx][part]
                    factor = jax.nn.sigmoid(m + jnp.log(l) - sink_ref[:, h:h + 1])
                    heads_out.append((acc / l) * factor)
            y = jnp.concatenate(heads_out, axis=1)
            y_out[j * w:(j + 1) * w, :] = _rms(y, gn_ref[...]).astype(BF16)


def _swa(cfg, qkvc, gn, sink):
    t, c = qkvc.shape
    w = SWA_WINDOW
    rows = cfg.swa_rows
    nsub = rows // w
    nchunks = t // rows
    nblk = t // w
    qw, kvw = SWA_WIDTH, SWA_KV_WIDTH
    rep = SWA_HEADS // SWA_KV_HEADS
    assert qw % kvw == 0 and c == qw + 2 * kvw
    kcol, vcol = qw // kvw, qw // kvw + 1
    prev = lambda col: (lambda i, f: (jnp.maximum(i * nsub - 1, 0), col))
    nxt = lambda col: (lambda i, f: (jnp.minimum((i + 1) * nsub, nblk - 1), col))
    return pl.pallas_call(
        functools.partial(_swa_kernel, w=w, nsub=nsub, nb=cfg.swa_batch),
        grid_spec=pltpu.PrefetchScalarGridSpec(
            num_scalar_prefetch=1, grid=(nchunks,),
            in_specs=[
                pl.BlockSpec((rows, qw), lambda i, f: (i, 0)),
                pl.BlockSpec((rows, kvw), lambda i, f: (i, kcol)),
                pl.BlockSpec((w, kvw), prev(kcol)),
                pl.BlockSpec((w, kvw), nxt(kcol)),
                pl.BlockSpec((rows, kvw), lambda i, f: (i, vcol)),
                pl.BlockSpec((w, kvw), prev(vcol)),
                pl.BlockSpec((w, kvw), nxt(vcol)),
                _resident((SWA_KV_HEADS, rep * w, 3 * w)),
                _resident((1, 128)),
                _resident((1, qw)),
            ],
            out_specs=pl.BlockSpec((rows, qw), lambda i, f: (i, 0)),
            scratch_shapes=[pltpu.VMEM((rows + 2 * w, kvw), BF16), pltpu.VMEM((rows + 2 * w, kvw), BF16)]),
        out_shape=jax.ShapeDtypeStruct((t, qw), BF16),
        compiler_params=_params(("parallel",)),
        name="windowed_gqa_sink",
    )(_chunk_flags(cfg, rows), qkvc, qkvc, qkvc, qkvc, qkvc, qkvc, qkvc,
      jnp.asarray(_band_bias(w, 1, SWA_HEADS, SWA_KV_HEADS)), sink, gn)


def _perm_matrix(d):
    n = _PERM_ROWS
    per = n // d
    p = np.zeros((n, n), np.float32)
    for r in range(d):
        for m in range(per):
            p[r * per + m, d * m + r] = 1.0
    return p


def _dilated_kernel(flags_ref, q_ref, kp_ref, kc_ref, kn_ref, vp_ref, vc_ref, vn_ref, bias_ref, perm_ref, gn_ref,
                    y_ref, qd, kd, vd, biasv, acc_nat, m_nat, l_nat, *, c, nb):
    w = _DIL_W
    nh = DIL_HEADS
    nsb = c // w
    pen_first, pen_last = _edge_penalties(flags_ref[pl.program_id(0)], w)
    for p in range(len(DIL_PATTERNS)):
        for h in range(nh):
            b = bias_ref[p, h]
            biasv[p, 0, h] = b
            biasv[p, 1, h] = b + pen_first
            biasv[p, 2, h] = b + pen_last
            biasv[p, 3, h] = b + pen_first + pen_last

    heads = [slice(h * HEAD_DIM, (h + 1) * HEAD_DIM) for h in range(nh)]

    def run_pattern(p, d, qsrc, ksrc, vsrc, qstride, kstride):
        nblk = c // (d * w)
        first_pattern = p == 0
        last_pattern = p == len(DIL_PATTERNS) - 1

        def body(it, carry):
            qs, ks, vs, bs, where = [], [], [], [], []
            for b in range(nb):
                sb = it * nb + b
                r = sb // nblk
                s = sb % nblk
                var = (s == 0).astype(jnp.int32) + 2 * (s == nblk - 1).astype(jnp.int32)
                q0 = pl.multiple_of(r * qstride + s * w, w)
                k0 = pl.multiple_of(r * kstride + s * w, w)
                where.append(d * w * s + r)
                for h in range(nh):
                    qs.append(qsrc[pl.ds(q0, w), heads[h]])
                    ks.append(ksrc[pl.ds(k0, 3 * w), heads[h]])
                    vs.append(vsrc[pl.ds(k0, 3 * w), heads[h]])
                    bs.append(biasv[p, var, h])
            ms, ls, accs = _attend(qs, ks, vs, bs)
            for b in range(nb):
                nat = pl.ds(where[b], w, stride=d) if d > 1 else pl.ds(pl.multiple_of(where[b], w), w)
                for h in range(nh):
                    m, l, acc = ms[b * nh + h], ls[b * nh + h], accs[b * nh + h]
                    if not first_pattern:
                        m_p = m_nat[h, nat, :]
                        m_n = jnp.maximum(m_p, m)
                        alpha = jnp.exp(m_p - m_n)
                        beta = jnp.exp(m - m_n)
                        l = alpha * l_nat[h, nat, :] + beta * l
                        acc = alpha * acc_nat[h, nat, :] + beta * acc
                        m = m_n
                    if last_pattern:
                        acc_nat[h, nat, :] = acc / l
                    else:
                        acc_nat[h, nat, :] = acc
                        m_nat[h, nat, :] = jnp.broadcast_to(m, (w, HEAD_DIM))
                        l_nat[h, nat, :] = jnp.broadcast_to(l, (w, HEAD_DIM))
            return carry

        lax.fori_loop(0, nsb // nb, body, 0)

    def deinterleave(p, d):
        per = _PERM_ROWS // d
        halo_groups = (d * w) // _PERM_ROWS
        chunk_groups = c // _PERM_ROWS
        kstride = (chunk_groups + 2 * halo_groups) * per
        perm = perm_ref[p - 1]

        def move(src, row0, dst, g, stride):
            res = jnp.dot(perm, src[row0:row0 + _PERM_ROWS, :], preferred_element_type=F32).astype(BF16)
            for r in range(d):
                dst[r * stride + g * per:r * stride + (g + 1) * per, :] = res[r * per:(r + 1) * per]

        for g in range(chunk_groups):
            move(q_ref, g * _PERM_ROWS, qd, g, c // d)
        for prev_ref, cur_ref, next_ref, dst in ((kp_ref, kc_ref, kn_ref, kd), (vp_ref, vc_ref, vn_ref, vd)):
            srcs = ([(prev_ref, c - (halo_groups - g) * _PERM_ROWS) for g in range(halo_groups)]
                    + [(cur_ref, g * _PERM_ROWS) for g in range(chunk_groups)]
                    + [(next_ref, g * _PERM_ROWS) for g in range(halo_groups)])
            for g, (src, row0) in enumerate(srcs):
                move(src, row0, dst, g, kstride)
        return c // d, kstride

    for prev_ref, cur_ref, next_ref, dst in ((kp_ref, kc_ref, kn_ref, kd), (vp_ref, vc_ref, vn_ref, vd)):
        dst[0:w, :] = prev_ref[c - w:c, :]
        dst[w:w + c, :] = cur_ref[...]
        dst[w + c:2 * w + c, :] = next_ref[0:w, :]
    for p, (_, d) in enumerate(DIL_PATTERNS):
        if d == 1:
            run_pattern(p, d, q_ref, kd, vd, 0, 0)
        else:
            qstride, kstride = deinterleave(p, d)
            run_pattern(p, d, qd, kd, vd, qstride, kstride)

    rb = _PERM_ROWS
    for blk in range(c // rb):
        rows = slice(blk * rb, (blk + 1) * rb)
        y = jnp.concatenate([acc_nat[h, rows, :] for h in range(nh)], axis=1)
        y_ref[rows, :] = _rms(y, gn_ref[...]).astype(BF16)


def _dilated(cfg, qkvb, gn):
    t = qkvb.shape[0]
    c = cfg.dil_rows
    w = _DIL_W
    n = t // c
    dmax = max(d for _, d in DIL_PATTERNS)
    assert DIL_PATTERNS[0][1] == 1 and c % (dmax * w) == 0 and c >= dmax * w and c % _PERM_ROWS == 0
    assert all((d * w) % _PERM_ROWS == 0 for _, d in DIL_PATTERNS[1:])
    assert (c // w) % cfg.dil_batch == 0
    bias = np.stack([_band_bias(w, d, DIL_HEADS, DIL_HEADS) for _, d in DIL_PATTERNS])
    perm = np.stack([_perm_matrix(d) for _, d in DIL_PATTERNS[1:]])
    blk = (c, DIL_WIDTH)
    prev = lambda col: (lambda i, f: (jnp.maximum(i - 1, 0), col))
    cur = lambda col: (lambda i, f: (i, col))
    nxt = lambda col: (lambda i, f: (jnp.minimum(i + 1, n - 1), col))
    return pl.pallas_call(
        functools.partial(_dilated_kernel, c=c, nb=cfg.dil_batch),
        grid_spec=pltpu.PrefetchScalarGridSpec(
            num_scalar_prefetch=1, grid=(n,),
            in_specs=[
                pl.BlockSpec(blk, cur(0)),
                pl.BlockSpec(blk, prev(1)), pl.BlockSpec(blk, cur(1)), pl.BlockSpec(blk, nxt(1)),
                pl.BlockSpec(blk, prev(2)), pl.BlockSpec(blk, cur(2)), pl.BlockSpec(blk, nxt(2)),
                _resident(bias.shape), _resident(perm.shape), _resident((1, DIL_WIDTH)),
            ],
            out_specs=pl.BlockSpec(blk, cur(0)),
            scratch_shapes=[
                pltpu.VMEM((c, DIL_WIDTH), BF16),
                pltpu.VMEM((3 * c, DIL_WIDTH), BF16), pltpu.VMEM((3 * c, DIL_WIDTH), BF16),
                pltpu.VMEM((len(DIL_PATTERNS), 4, DIL_HEADS, w, 3 * w), F32),
                pltpu.VMEM((DIL_HEADS, c, HEAD_DIM), F32),
                pltpu.VMEM((DIL_HEADS, c, HEAD_DIM), F32),
                pltpu.VMEM((DIL_HEADS, c, HEAD_DIM), F32),
            ]),
        out_shape=jax.ShapeDtypeStruct((t, DIL_WIDTH), BF16),
        compiler_params=_params(("parallel",)),
        name="dilated_attention",
    )(_chunk_flags(cfg, c), qkvb, qkvb, qkvb, qkvb, qkvb, qkvb, qkvb,
      jnp.asarray(bias), jnp.asarray(perm, dtype=BF16), gn)


def _outproj_kernel(ya_ref, yb_ref, yc_ref, x_ref, w_ref, g_ref, o_ref, y_scr, *, nchunk):
    d = o_ref.shape[1]
    b0, b1 = LRU_WIDTH, LRU_WIDTH + DIL_WIDTH
    for c in range(0, d, nchunk):
        cs = slice(c, c + nchunk)
        acc = jnp.dot(ya_ref[...], w_ref[0:b0, cs], preferred_element_type=F32)
        acc += jnp.dot(yb_ref[...], w_ref[b0:b1, cs], preferred_element_type=F32)
        acc += jnp.dot(yc_ref[...], w_ref[b1:, cs], preferred_element_type=F32)
        y_scr[:, cs] = acc
    o_ref[...] = x_ref[...] + _rms(y_scr[...], g_ref[...])


def _outproj(cfg, ya, yb, yc, x, w, g):
    t, d = x.shape
    tm = cfg.tm
    return pl.pallas_call(
        functools.partial(_outproj_kernel, nchunk=min(512, d)),
        grid=(t // tm,),
        in_specs=[
            pl.BlockSpec((tm, LRU_WIDTH), lambda i: (i, 0)),
            pl.BlockSpec((tm, DIL_WIDTH), lambda i: (i, 0)),
            pl.BlockSpec((tm, SWA_WIDTH), lambda i: (i, 0)),
            pl.BlockSpec((tm, d), lambda i: (i, 0)),
            _resident((MIX_WIDTH, d)),
            _resident((1, d)),
        ],
        out_specs=pl.BlockSpec((tm, d), lambda i: (i, 0)),
        out_shape=jax.ShapeDtypeStruct((t, d), F32),
        scratch_shapes=[pltpu.VMEM((tm, d), F32)],
        compiler_params=_params(("parallel",)),
        name="mixer_outproj",
    )(ya, yb, yc, x, w, g)


def _memkv_kernel(mem_ref, g_ref, w_ref, k_ref, v_ref):
    mn = _rms(mem_ref[...], g_ref[...]).astype(BF16)
    kv = jnp.dot(mn, w_ref[...], preferred_element_type=F32)
    k_ref[...] = kv[:, :MEM_WIDTH].astype(BF16)
    v_ref[...] = kv[:, MEM_WIDTH:].astype(BF16)


def _memkv(cfg, mem, g, wkv):
    rows, d = mem.shape
    nm = cfg.n_mem
    shape = jax.ShapeDtypeStruct((rows, MEM_WIDTH), BF16)
    return pl.pallas_call(
        _memkv_kernel,
        grid=(rows // nm,),
        in_specs=[pl.BlockSpec((nm, d), lambda i: (i, 0)), _resident((1, d)), _resident((d, 2 * MEM_WIDTH))],
        out_specs=[pl.BlockSpec((nm, MEM_WIDTH), lambda i: (i, 0))] * 2,
        out_shape=[shape, shape],
        compiler_params=_params(("parallel",)),
        name="memory_kv",
    )(mem, g, wkv)


def _cross_kernel(seq_ref, x_ref, gpre_ref, wq_ref, k_ref, v_ref, wo_ref, gpost_ref, o_ref, xn_scr, o_scr):
    del seq_ref
    x = x_ref[...]
    xn_scr[...] = _rms(x, gpre_ref[...]).astype(BF16)
    q = (jnp.dot(xn_scr[...], wq_ref[...], preferred_element_type=F32) * _QK_SCALE).astype(BF16)
    for h in range(MEM_HEADS):
        hs = slice(h * HEAD_DIM, (h + 1) * HEAD_DIM)
        s = lax.dot_general(q[:, hs], k_ref[:, hs], _NT, preferred_element_type=F32)
        m = jnp.max(s, axis=-1, keepdims=True)
        p = jnp.exp(s - m)
        l = jnp.sum(p, axis=-1, keepdims=True)
        o = jnp.dot(p.astype(BF16), v_ref[:, hs], preferred_element_type=F32)
        o_scr[:, hs] = (o / l).astype(BF16)
    y = jnp.dot(o_scr[...], wo_ref[...], preferred_element_type=F32)
    o_ref[...] = x + _rms(y, gpost_ref[...])


def _cross(cfg, x, gpre, wq, kmem, vmem, wo, gpost):
    t, d = x.shape
    tm, nm = cfg.tm, cfg.n_mem
    seqs, _ = _sequences(cfg)
    seq_of_tile = []
    for si, (_, length) in enumerate(seqs):
        assert length % tm == 0
        seq_of_tile += [si] * (length // tm)
    seq_of_tile = jnp.asarray(np.asarray(seq_of_tile, np.int32))
    return pl.pallas_call(
        _cross_kernel,
        grid_spec=pltpu.PrefetchScalarGridSpec(
            num_scalar_prefetch=1, grid=(t // tm,),
            in_specs=[
                pl.BlockSpec((tm, d), lambda i, s: (i, 0)),
                _resident((1, d)),
                _resident((d, MEM_WIDTH)),
                pl.BlockSpec((nm, MEM_WIDTH), lambda i, s: (s[i], 0)),
                pl.BlockSpec((nm, MEM_WIDTH), lambda i, s: (s[i], 0)),
                _resident((MEM_WIDTH, d)),
                _resident((1, d)),
            ],
            out_specs=pl.BlockSpec((tm, d), lambda i, s: (i, 0)),
            scratch_shapes=[pltpu.VMEM((tm, d), BF16), pltpu.VMEM((tm, MEM_WIDTH), BF16)]),
        out_shape=jax.ShapeDtypeStruct((t, d), F32),
        compiler_params=_params(("parallel",)),
        name="memory_cross_attention",
    )(seq_of_tile, x, gpre, wq, kmem, vmem, wo, gpost)


def _ffn_kernel(x_ref, gpre_ref, w1_ref, w2_ref, gpost_ref, o_ref, xn_scr, acc_scr):
    f = pl.program_id(1)

    @pl.when(f == 0)
    def _():
        xn_scr[...] = _rms(x_ref[...], gpre_ref[...]).astype(BF16)

    h = jnp.dot(xn_scr[...], w1_ref[...], preferred_element_type=F32)
    h = jnp.square(jnp.maximum(h, 0.0)).astype(BF16)
    part = jnp.dot(h, w2_ref[...], preferred_element_type=F32)

    @pl.when(f == 0)
    def _():
        acc_scr[...] = part

    @pl.when(f > 0)
    def _():
        acc_scr[...] += part

    @pl.when(f == pl.num_programs(1) - 1)
    def _():
        o_ref[...] = x_ref[...] + _rms(acc_scr[...], gpost_ref[...])


def _ffn(cfg, x, gpre, w1, w2, gpost):
    t, d = x.shape
    tm, tf = cfg.tm_ffn, cfg.tf
    dff = w1.shape[1]
    return pl.pallas_call(
        _ffn_kernel,
        grid=(t // tm, dff // tf),
        in_specs=[
            pl.BlockSpec((tm, d), lambda i, f: (i, 0)),
            _resident((1, d)),
            pl.BlockSpec((d, tf), lambda i, f: (0, f)),
            pl.BlockSpec((tf, d), lambda i, f: (f, 0)),
            _resident((1, d)),
        ],
        out_specs=pl.BlockSpec((tm, d), lambda i, f: (i, 0)),
        out_shape=jax.ShapeDtypeStruct((t, d), F32),
        scratch_shapes=[pltpu.VMEM((tm, d), BF16), pltpu.VMEM((tm, d), F32)],
        compiler_params=_params(("parallel", "arbitrary")),
        name="squared_relu_mlp",
    )(x, gpre, w1, w2, gpost)


def _forward(cfg, x, mem, p):
    row = lambda a: a.reshape(1, -1).astype(F32)
    for l in range(cfg.depth):
        lru_in, qkvb, qkvc = _inproj(cfg, x, row(p["mix_norm_pre"][l]), p["w_in"][l])
        gn = p["group_norm"][l]
        wg = jnp.concatenate([p["lru_wa"][l], p["lru_wx"][l]], axis=-1).astype(BF16)
        ya = _lru(cfg, lru_in, p["conv_w"][l], row(p["conv_b"][l]), wg,
                  p["lru_ba"][l][:, None, :], p["lru_bx"][l][:, None, :], p["lru_lam"][l][:, None, :],
                  row(gn[:LRU_WIDTH]))
        yb = _dilated(cfg, qkvb, row(gn[LRU_WIDTH:LRU_WIDTH + DIL_WIDTH]))
        sink = jnp.zeros((1, 128), F32).at[0, :SWA_HEADS].set(p["swa_sink"][l].astype(F32))
        yc = _swa(cfg, qkvc, row(gn[LRU_WIDTH + DIL_WIDTH:]), sink)
        x = _outproj(cfg, ya, yb, yc, x, p["w_out"][l], row(p["mix_norm_post"][l]))
        kmem, vmem = _memkv(cfg, mem, row(p["mem_kv_norm"][l]), p["w_mkv"][l])
        x = _cross(cfg, x, row(p["mem_norm_pre"][l]), p["w_mq"][l], kmem, vmem, p["w_mo"][l],
                   row(p["mem_norm_post"][l]))
        x = _ffn(cfg, x, row(p["ffn_norm_pre"][l]), p["w_ff1"][l], p["w_ff2"][l], row(p["ffn_norm_post"][l]))
    return x


def _run(cfg, x_prompt, x_sample, mem_prompt, mem_sample, mix_norm_pre, mix_norm_post, w_in, conv_w, conv_b,
         lru_wa, lru_ba, lru_wx, lru_bx, lru_lam, swa_sink, group_norm, w_out, mem_norm_pre, mem_norm_post,
         mem_kv_norm, w_mq, w_mk, w_mv, w_mo, ffn_norm_pre, ffn_norm_post, w_ff1, w_ff2):
    d = cfg.d_model
    x = jnp.concatenate([x_prompt.reshape(-1, d), x_sample.reshape(-1, d)], axis=0)
    mem = jnp.concatenate([mem_prompt.reshape(-1, d), mem_sample.reshape(-1, d)], axis=0)
    p = dict(
        mix_norm_pre=mix_norm_pre, mix_norm_post=mix_norm_post, w_in=w_in.astype(BF16), conv_w=conv_w,
        conv_b=conv_b, lru_wa=lru_wa, lru_ba=lru_ba, lru_wx=lru_wx, lru_bx=lru_bx, lru_lam=lru_lam,
        swa_sink=swa_sink, group_norm=group_norm, w_out=w_out.astype(BF16), mem_norm_pre=mem_norm_pre,
        mem_norm_post=mem_norm_post, mem_kv_norm=mem_kv_norm, w_mq=w_mq.astype(BF16),
        w_mkv=jnp.concatenate([w_mk, w_mv], axis=-1).astype(BF16), w_mo=w_mo.astype(BF16),
        ffn_norm_pre=ffn_norm_pre, ffn_norm_post=ffn_norm_post, w_ff1=w_ff1.astype(BF16),
        w_ff2=w_ff2.astype(BF16))
    y = _forward(cfg, x, mem, p)
    n_prompt = x_prompt.shape[0] * x_prompt.shape[1]
    return y[:n_prompt].reshape(x_prompt.shape), y[n_prompt:].reshape(x_sample.shape)


def kernel(x_prompt, x_sample, mem_prompt, mem_sample, mix_norm_pre, mix_norm_post, w_in, conv_w, conv_b, lru_wa,
           lru_ba, lru_wx, lru_bx, lru_lam, swa_sink, group_norm, w_out, mem_norm_pre, mem_norm_post, mem_kv_norm,
           w_mq, w_mk, w_mv, w_mo, ffn_norm_pre, ffn_norm_post, w_ff1, w_ff2):
    return _run(_CFG, x_prompt, x_sample, mem_prompt, mem_sample, mix_norm_pre, mix_norm_post, w_in, conv_w,
                conv_b, lru_wa, lru_ba, lru_wx, lru_bx, lru_lam, swa_sink, group_norm, w_out, mem_norm_pre,
                mem_norm_post, mem_kv_norm, w_mq, w_mk, w_mv, w_mo, ffn_norm_pre, ffn_norm_post, w_ff1, w_ff2)
```

```python
import functools
from typing import NamedTuple

import numpy as np
import jax
import jax.numpy as jnp
from jax import lax
from jax.experimental import pallas as pl
from jax.experimental.pallas import tpu as pltpu

F32 = jnp.float32
BF16 = jnp.bfloat16

D_MODEL = 2048
BATCH = 8
SEQ = 4096
DEPTH = 4
DEC_BATCH = 1
DEC_SEQ = 16384
HEAD_DIM = 128
LRU_WIDTH = 512
LRU_BLOCKS = 4
LRU_BLOCK_WIDTH = LRU_WIDTH // LRU_BLOCKS
CONV_WIDTH = 4
CONV_LEFT = 2
LRU_C = 8.0
DIL_HEADS = 6
DIL_PATTERNS = ((128, 1), (512, 4), (2048, 16))
SWA_HEADS = 6
SWA_KV_HEADS = 2
SWA_WINDOW = 128
DIL_WIDTH = DIL_HEADS * HEAD_DIM
SWA_WIDTH = SWA_HEADS * HEAD_DIM
SWA_KV_WIDTH = SWA_KV_HEADS * HEAD_DIM
MIX_WIDTH = LRU_WIDTH + DIL_WIDTH + SWA_WIDTH
IN_WIDTH = 2 * LRU_WIDTH + 3 * DIL_WIDTH + SWA_WIDTH + 2 * SWA_KV_WIDTH
N_MEM = 256
MEM_HEADS = 4
MEM_WIDTH = MEM_HEADS * HEAD_DIM
D_FF = 4 * D_MODEL
EPS = 1e-6

_NEG = -1e30
_QK_SCALE = HEAD_DIM ** -0.5
_HALO_ROWS = 8
_DIL_W = DIL_PATTERNS[0][0] // (2 * DIL_PATTERNS[0][1])
assert all(wn // (2 * d) == _DIL_W for wn, d in DIL_PATTERNS)
_PERM_ROWS = 256
_V7X_VMEM_BYTES = 64 * 1024 * 1024
_VMEM_LIMIT = _V7X_VMEM_BYTES - 8 * 1024 * 1024
_NT = (((1,), (1,)), ((), ()))


class _Cfg(NamedTuple):
    d_model: int
    d_ff: int
    depth: int
    groups: tuple
    n_mem: int
    tm: int
    tm_ffn: int
    tf: int
    lru_chunk: int
    lru_rows: int
    swa_rows: int
    dil_rows: int
    swa_batch: int
    dil_batch: int


_CFG = _Cfg(d_model=D_MODEL, d_ff=D_FF, depth=DEPTH, groups=((BATCH, SEQ), (DEC_BATCH, DEC_SEQ)),
            n_mem=N_MEM, tm=512, tm_ffn=1024, tf=512, lru_chunk=1024, lru_rows=256, swa_rows=512,
            dil_rows=1024, swa_batch=2, dil_batch=4)


def _sequences(cfg):
    out, start = [], 0
    for n, length in cfg.groups:
        for _ in range(n):
            out.append((start, length))
            start += length
    return out, start


def _chunk_flags(cfg, rows):
    seqs, total = _sequences(cfg)
    starts = {s for s, _ in seqs}
    ends = {s + l for s, l in seqs}
    for s, l in seqs:
        assert l % rows == 0, (l, rows)
    n = total // rows
    flags = np.zeros((n,), np.int32)
    for c in range(n):
        flags[c] = (1 if c * rows in starts else 0) | (2 if (c + 1) * rows in ends else 0)
    return jnp.asarray(flags)


def _params(semantics):
    return pltpu.CompilerParams(dimension_semantics=semantics, vmem_limit_bytes=_VMEM_LIMIT)


def _rms(x, g):
    ms = jnp.mean(x * x, axis=-1, keepdims=True)
    return x * lax.rsqrt(ms + EPS) * g


def _resident(shape):
    return pl.BlockSpec(shape, lambda *_: (0,) * len(shape), pipeline_mode=pl.Buffered(1))


def _inproj_plan():
    lru_w = 2 * LRU_WIDTH
    dil_w = 3 * DIL_WIDTH
    segs = [
        (0, lru_w, 0, None),
        (lru_w, lru_w + DIL_WIDTH, 1, _QK_SCALE),
        (lru_w + DIL_WIDTH, lru_w + dil_w, 1, None),
        (lru_w + dil_w, lru_w + dil_w + SWA_WIDTH, 2, _QK_SCALE),
        (lru_w + dil_w + SWA_WIDTH, IN_WIDTH, 2, None),
    ]
    base = {0: 0, 1: lru_w, 2: lru_w + dil_w}
    plan = []
    for c0, c1, oi, scale in segs:
        c = c0
        while c < c1:
            n = min(512, c1 - c)
            plan.append((c, c + n, oi, c - base[oi], scale))
            c += n
    return tuple(plan)


def _inproj_kernel(x_ref, g_ref, w_ref, lru_ref, qkvb_ref, qkvc_ref, xn_scr, *, plan):
    xn_scr[...] = _rms(x_ref[...], g_ref[...]).astype(BF16)
    outs = (lru_ref, qkvb_ref, qkvc_ref)
    for c0, c1, oi, o0, scale in plan:
        acc = jnp.dot(xn_scr[...], w_ref[:, c0:c1], preferred_element_type=F32)
        if scale is not None:
            acc = acc * scale
        outs[oi][:, o0:o0 + (c1 - c0)] = acc.astype(outs[oi].dtype)


def _inproj(cfg, x, g, w):
    t, d = x.shape
    tm = cfg.tm
    widths = (2 * LRU_WIDTH, 3 * DIL_WIDTH, SWA_WIDTH + 2 * SWA_KV_WIDTH)
    return pl.pallas_call(
        functools.partial(_inproj_kernel, plan=_inproj_plan()),
        grid=(t // tm,),
        in_specs=[
            pl.BlockSpec((tm, d), lambda i: (i, 0)),
            _resident((1, d)),
            _resident((d, IN_WIDTH)),
        ],
        out_specs=[pl.BlockSpec((tm, wd), lambda i: (i, 0)) for wd in widths],
        out_shape=[
            jax.ShapeDtypeStruct((t, widths[0]), F32),
            jax.ShapeDtypeStruct((t, widths[1]), BF16),
            jax.ShapeDtypeStruct((t, widths[2]), BF16),
        ],
        scratch_shapes=[pltpu.VMEM((tm, d), BF16)],
        compiler_params=_params(("parallel",)),
        name="mixer_inproj",
    )(x, g, w)


def _lru_fill_halo(first, last, xa_ref, xp_ref, xn_ref, xext, lc):
    xext[_HALO_ROWS:_HALO_ROWS + lc, :] = xa_ref[...]

    @pl.when(first)
    def _():
        xext[0:_HALO_ROWS, :] = jnp.zeros((_HALO_ROWS, LRU_WIDTH), F32)

    @pl.when(jnp.logical_not(first))
    def _():
        xext[0:_HALO_ROWS, :] = xp_ref[...]

    @pl.when(last)
    def _():
        xext[_HALO_ROWS + lc:, :] = jnp.zeros((_HALO_ROWS, LRU_WIDTH), F32)

    @pl.when(jnp.logical_not(last))
    def _():
        xext[_HALO_ROWS + lc:, :] = xn_ref[...]


def _lru_gates(xext, cw_ref, cb_ref, wg_ref, ba_ref, bx_ref, lam_ref, a_scr, u_scr, lc, rb):
    lam = lam_ref[...]
    neg = -lam
    softplus = jnp.maximum(neg, 0.0) + jnp.log1p(jnp.exp(-jnp.abs(neg)))
    cb = cb_ref[...]
    taps = [cw_ref[j:j + 1, :] for j in range(CONV_WIDTH)]
    for blk in range(lc // rb):
        r0 = blk * rb
        xc = cb + sum(taps[j] * xext[r0 + _HALO_ROWS - CONV_LEFT + j:r0 + _HALO_ROWS - CONV_LEFT + j + rb, :]
                      for j in range(CONV_WIDTH))
        xcb = xc.astype(BF16)
        for n in range(LRU_BLOCKS):
            cs = slice(n * LRU_BLOCK_WIDTH, (n + 1) * LRU_BLOCK_WIDTH)
            g = jnp.dot(xcb[:, cs], wg_ref[n], preferred_element_type=F32)
            r = jax.nn.sigmoid(g[:, :LRU_BLOCK_WIDTH] + ba_ref[:, cs])
            ig = jax.nn.sigmoid(g[:, LRU_BLOCK_WIDTH:] + bx_ref[:, cs])
            log_a = (-LRU_C * r) * softplus[:, cs]
            a = jnp.exp(log_a)
            one_minus_a2 = -jnp.tanh(log_a) * (1.0 + a * a)
            a_scr[r0:r0 + rb, cs] = a
            u_scr[r0:r0 + rb, cs] = jnp.sqrt(one_minus_a2) * (ig * xc[:, cs])


def _lru_scan(reset, a_scr, u_scr, h_dst, carry, lc, reverse):
    @pl.when(reset)
    def _():
        carry[...] = jnp.zeros((1, LRU_WIDTH), F32)

    def step(i, h):
        t = lc - 1 - i if reverse else i
        h = a_scr[pl.ds(t, 1), :] * h + u_scr[pl.ds(t, 1), :]
        h_dst[pl.ds(t, 1), :] = h
        return h

    carry[...] = lax.fori_loop(0, lc, step, carry[...], unroll=8)


def _lru_fwd_kernel(flags_ref, xa_ref, xp_ref, xn_ref, cw_ref, cb_ref, wg_ref, ba_ref, bx_ref, lam_ref,
                    hf_ref, xext, a_scr, u_scr, carry, *, lc, rb):
    fl = flags_ref[pl.program_id(0)]
    first = (fl & 1) != 0
    last = (fl & 2) != 0
    _lru_fill_halo(first, last, xa_ref, xp_ref, xn_ref, xext, lc)
    _lru_gates(xext, cw_ref, cb_ref, wg_ref, ba_ref, bx_ref, lam_ref, a_scr, u_scr, lc, rb)
    _lru_scan(first, a_scr, u_scr, hf_ref, carry, lc, reverse=False)


def _lru_bwd_kernel(flags_ref, xa_ref, xp_ref, xn_ref, gate_ref, hf_ref, cw_ref, cb_ref, wg_ref, ba_ref,
                    bx_ref, lam_ref, gn_ref, y_ref, xext, a_scr, u_scr, h_scr, carry, *, lc, rb, nchunks):
    fl = flags_ref[nchunks - 1 - pl.program_id(0)]
    first = (fl & 1) != 0
    last = (fl & 2) != 0
    _lru_fill_halo(first, last, xa_ref, xp_ref, xn_ref, xext, lc)
    _lru_gates(xext, cw_ref, cb_ref, wg_ref, ba_ref, bx_ref, lam_ref, a_scr, u_scr, lc, rb)
    _lru_scan(last, a_scr, u_scr, h_scr, carry, lc, reverse=True)
    for blk in range(lc // rb):
        rows = slice(blk * rb, (blk + 1) * rb)
        h = hf_ref[rows, :] + h_scr[rows, :]
        y = h * jax.nn.gelu(gate_ref[rows, :])
        y_ref[rows, :] = _rms(y, gn_ref[...]).astype(BF16)


def _lru(cfg, lru_in, cw, cb, wg, ba, bx, lam, gn):
    t = lru_in.shape[0]
    lc, rb = cfg.lru_chunk, cfg.lru_rows
    nchunks = t // lc
    hb = lc // _HALO_ROWS
    nhalo = t // _HALO_ROWS
    flags = _chunk_flags(cfg, lc)
    row = lambda: _resident((1, LRU_WIDTH))

    def specs(chunk_of):
        return [
            pl.BlockSpec((lc, LRU_WIDTH), lambda i, f: (chunk_of(i), 0)),
            pl.BlockSpec((_HALO_ROWS, LRU_WIDTH), lambda i, f: (jnp.maximum(chunk_of(i) * hb - 1, 0), 0)),
            pl.BlockSpec((_HALO_ROWS, LRU_WIDTH), lambda i, f: (jnp.minimum((chunk_of(i) + 1) * hb, nhalo - 1), 0)),
        ]

    def weights(d):
        return [
            _resident((CONV_WIDTH, LRU_WIDTH)), row(),
            pl.BlockSpec((None, LRU_BLOCKS, LRU_BLOCK_WIDTH, 2 * LRU_BLOCK_WIDTH), lambda i, f: (d, 0, 0, 0)),
            pl.BlockSpec((None, 1, LRU_WIDTH), lambda i, f: (d, 0, 0)),
            pl.BlockSpec((None, 1, LRU_WIDTH), lambda i, f: (d, 0, 0)),
            pl.BlockSpec((None, 1, LRU_WIDTH), lambda i, f: (d, 0, 0)),
        ]

    scratch = [pltpu.VMEM((lc + 2 * _HALO_ROWS, LRU_WIDTH), F32), pltpu.VMEM((lc, LRU_WIDTH), F32),
               pltpu.VMEM((lc, LRU_WIDTH), F32)]
    carry = [pltpu.VMEM((1, LRU_WIDTH), F32)]

    fwd = lambda i: i
    hf = pl.pallas_call(
        functools.partial(_lru_fwd_kernel, lc=lc, rb=rb),
        grid_spec=pltpu.PrefetchScalarGridSpec(
            num_scalar_prefetch=1, grid=(nchunks,),
            in_specs=specs(fwd) + weights(0),
            out_specs=pl.BlockSpec((lc, LRU_WIDTH), lambda i, f: (i, 0)),
            scratch_shapes=scratch + carry),
        out_shape=jax.ShapeDtypeStruct((t, LRU_WIDTH), F32),
        compiler_params=_params(("arbitrary",)),
        name="lru_forward",
    )(flags, lru_in, lru_in, lru_in, cw, cb, wg, ba, bx, lam)

    bwd = lambda i: nchunks - 1 - i
    return pl.pallas_call(
        functools.partial(_lru_bwd_kernel, lc=lc, rb=rb, nchunks=nchunks),
        grid_spec=pltpu.PrefetchScalarGridSpec(
            num_scalar_prefetch=1, grid=(nchunks,),
            in_specs=specs(bwd) + [
                pl.BlockSpec((lc, LRU_WIDTH), lambda i, f: (bwd(i), 1)),
                pl.BlockSpec((lc, LRU_WIDTH), lambda i, f: (bwd(i), 0)),
            ] + weights(1) + [row()],
            out_specs=pl.BlockSpec((lc, LRU_WIDTH), lambda i, f: (bwd(i), 0)),
            scratch_shapes=scratch + [pltpu.VMEM((lc, LRU_WIDTH), F32)] + carry),
        out_shape=jax.ShapeDtypeStruct((t, LRU_WIDTH), BF16),
        compiler_params=_params(("arbitrary",)),
        name="lru_backward",
    )(flags, lru_in, lru_in, lru_in, lru_in, hf, cw, cb, wg, ba, bx, lam, gn)


def _alibi_slopes(n):
    return [2.0 ** (-8.0 * (i + 1) / n) for i in range(n)]


def _band_bias(w, dist_scale, hq, hkv):
    rep = hq // hkv
    slopes = _alibi_slopes(hq)
    qi = np.arange(w)[:, None]
    kj = np.arange(3 * w)[None, :]
    rel = np.abs(kj - w - qi)
    out = np.empty((hkv, rep * w, 3 * w), np.float32)
    for g in range(hkv):
        for r in range(rep):
            out[g, r * w:(r + 1) * w] = np.where(rel <= w, -slopes[g * rep + r] * dist_scale * rel, _NEG)
    return out


def _edge_penalties(fl, w):
    col = lax.broadcasted_iota(jnp.int32, (1, 3 * w), 1)
    pen_first = jnp.where(col < w, jnp.where((fl & 1) != 0, _NEG, 0.0), 0.0)
    pen_last = jnp.where(col >= 2 * w, jnp.where((fl & 2) != 0, _NEG, 0.0), 0.0)
    return pen_first, pen_last


def _attend(qs, ks, vs, biases):
    scores = [lax.dot_general(q, k, _NT, preferred_element_type=F32) for q, k in zip(qs, ks)]
    ms, ls, ps = [], [], []
    for s, b in zip(scores, biases):
        s = s + b
        m = jnp.max(s, axis=-1, keepdims=True)
        p = jnp.exp(s - m)
        ms.append(m)
        ls.append(jnp.sum(p, axis=-1, keepdims=True))
        ps.append(p.astype(BF16))
    accs = [jnp.dot(p, v, preferred_element_type=F32) for p, v in zip(ps, vs)]
    return ms, ls, accs


def _swa_kernel(flags_ref, q_ref, kc_ref, kp_ref, kn_ref, vc_ref, vp_ref, vn_ref, bias_ref, sink_ref, gn_ref,
                y_out, kbuf, vbuf, *, w, nsub, nb):
    rep = SWA_HEADS // SWA_KV_HEADS
    rows_total = nsub * w
    pen_first, pen_last = _edge_penalties(flags_ref[pl.program_id(0)], w)
    kbuf[0:w, :] = kp_ref[...]
    kbuf[w:w + rows_total, :] = kc_ref[...]
    kbuf[w + rows_total:, :] = kn_ref[...]
    vbuf[0:w, :] = vp_ref[...]
    vbuf[w:w + rows_total, :] = vc_ref[...]
    vbuf[w + rows_total:, :] = vn_ref[...]

    for j0 in range(0, nsub, nb):
        blocks = list(range(j0, min(j0 + nb, nsub)))
        qs, ks, vs, bs = [], [], [], []
        for j in blocks:
            rows = slice(j * w, (j + 1) * w)
            for g in range(SWA_KV_HEADS):
                gs = slice(g * HEAD_DIM, (g + 1) * HEAD_DIM)
                qs.append(jnp.concatenate(
                    [q_ref[rows, (g * rep + r) * HEAD_DIM:(g * rep + r + 1) * HEAD_DIM] for r in range(rep)], axis=0))
                ks.append(kbuf[j * w:(j + 3) * w, gs])
                vs.append(vbuf[j * w:(j + 3) * w, gs])
                b = bias_ref[g]
                if j == 0:
                    b = b + pen_first
                if j == nsub - 1:
                    b = b + pen_last
                bs.append(b)
        ms, ls, accs = _attend(qs, ks, vs, bs)
        for bi, j in enumerate(blocks):
            heads_out = []
            for g in range(SWA_KV_HEADS):
                idx = bi * SWA_KV_HEADS + g
                for r in range(rep):
                    h = g * rep + r
                    part = slice(r * w, (r + 1) * w)
                    m, l, acc = ms[idx][part], ls[idx][part], accs[idx][part]
                    factor = jax.nn.sigmoid(m + jnp.log(l) - sink_ref[:, h:h + 1])
                    heads_out.append((acc / l) * factor)
            y = jnp.concatenate(heads_out, axis=1)
            y_out[j * w:(j + 1) * w, :] = _rms(y, gn_ref[...]).astype(BF16)


def _swa(cfg, qkvc, gn, sink):
    t, c = qkvc.shape
    w = SWA_WINDOW
    rows = cfg.swa_rows
    nsub = rows // w
    nchunks = t // rows
    nblk = t // w
    qw, kvw = SWA_WIDTH, SWA_KV_WIDTH
    rep = SWA_HEADS // SWA_KV_HEADS
    assert qw % kvw == 0 and c == qw + 2 * kvw
    kcol, vcol = qw // kvw, qw // kvw + 1
    prev = lambda col: (lambda i, f: (jnp.maximum(i * nsub - 1, 0), col))
    nxt = lambda col: (lambda i, f: (jnp.minimum((i + 1) * nsub, nblk - 1), col))
    return pl.pallas_call(
        functools.partial(_swa_kernel, w=w, nsub=nsub, nb=cfg.swa_batch),
        grid_spec=pltpu.PrefetchScalarGridSpec(
            num_scalar_prefetch=1, grid=(nchunks,),
            in_specs=[
                pl.BlockSpec((rows, qw), lambda i, f: (i, 0)),
                pl.BlockSpec((rows, kvw), lambda i, f: (i, kcol)),
                pl.BlockSpec((w, kvw), prev(kcol)),
                pl.BlockSpec((w, kvw), nxt(kcol)),
                pl.BlockSpec((rows, kvw), lambda i, f: (i, vcol)),
                pl.BlockSpec((w, kvw), prev(vcol)),
                pl.BlockSpec((w, kvw), nxt(vcol)),
                _resident((SWA_KV_HEADS, rep * w, 3 * w)),
                _resident((1, 128)),
                _resident((1, qw)),
            ],
            out_specs=pl.BlockSpec((rows, qw), lambda i, f: (i, 0)),
            scratch_shapes=[pltpu.VMEM((rows + 2 * w, kvw), BF16), pltpu.VMEM((rows + 2 * w, kvw), BF16)]),
        out_shape=jax.ShapeDtypeStruct((t, qw), BF16),
        compiler_params=_params(("parallel",)),
        name="windowed_gqa_sink",
    )(_chunk_flags(cfg, rows), qkvc, qkvc, qkvc, qkvc, qkvc, qkvc, qkvc,
      jnp.asarray(_band_bias(w, 1, SWA_HEADS, SWA_KV_HEADS)), sink, gn)


def _perm_matrix(d):
    n = _PERM_ROWS
    per = n // d
    p = np.zeros((n, n), np.float32)
    for r in range(d):
        for m in range(per):
            p[r * per + m, d * m + r] = 1.0
    return p


def _dilated_kernel(flags_ref, q_ref, kp_ref, kc_ref, kn_ref, vp_ref, vc_ref, vn_ref, bias_ref, perm_ref, gn_ref,
                    y_ref, qd, kd, vd, biasv, acc_nat, m_nat, l_nat, *, c, nb):
    w = _DIL_W
    nh = DIL_HEADS
    nsb = c // w
    pen_first, pen_last = _edge_penalties(flags_ref[pl.program_id(0)], w)
    for p in range(len(DIL_PATTERNS)):
        for h in range(nh):
            b = bias_ref[p, h]
            biasv[p, 0, h] = b
            biasv[p, 1, h] = b + pen_first
            biasv[p, 2, h] = b + pen_last
            biasv[p, 3, h] = b + pen_first + pen_last

    heads = [slice(h * HEAD_DIM, (h + 1) * HEAD_DIM) for h in range(nh)]

    def run_pattern(p, d, qsrc, ksrc, vsrc, qstride, kstride, first_pattern, last_pattern):
        nblk = c // (d * w)
        assert d == 1 or not last_pattern

        def body(it, carry):
            qs, ks, vs, bs, where = [], [], [], [], []
            for b in range(nb):
                sb = it * nb + b
                r = sb // nblk
                s = sb % nblk
                var = jnp.where(s == 0, 1, 0) + jnp.where(s == nblk - 1, 2, 0)
                q0 = pl.multiple_of(r * qstride + s * w, w)
                k0 = pl.multiple_of(r * kstride + s * w, w)
                where.append(d * w * s + r)
                for h in range(nh):
                    qs.append(qsrc[pl.ds(q0, w), heads[h]])
                    ks.append(ksrc[pl.ds(k0, 3 * w), heads[h]])
                    vs.append(vsrc[pl.ds(k0, 3 * w), heads[h]])
                    bs.append(biasv[p, var, h])
            ms, ls, accs = _attend(qs, ks, vs, bs)
            for b in range(nb):
                nat = pl.ds(where[b], w, stride=d) if d > 1 else pl.ds(pl.multiple_of(where[b], w), w)
                outs = []
                for h in range(nh):
                    m, l, acc = ms[b * nh + h], ls[b * nh + h], accs[b * nh + h]
                    if not first_pattern:
                        m_p = m_nat[h, nat, :]
                        m_n = jnp.maximum(m_p, m)
                        alpha = jnp.exp(m_p - m_n)
                        beta = jnp.exp(m - m_n)
                        l = alpha * l_nat[h, nat, :] + beta * l
                        acc = alpha * acc_nat[h, nat, :] + beta * acc
                        m = m_n
                    if last_pattern:
                        outs.append(acc / l)
                    else:
                        acc_nat[h, nat, :] = acc
                        m_nat[h, nat, :] = jnp.broadcast_to(m, (w, HEAD_DIM))
                        l_nat[h, nat, :] = jnp.broadcast_to(l, (w, HEAD_DIM))
                if last_pattern:
                    y_ref[nat, :] = _rms(jnp.concatenate(outs, axis=1), gn_ref[...]).astype(BF16)
            return carry

        lax.fori_loop(0, nsb // nb, body, 0)

    def deinterleave(p, d):
        per = _PERM_ROWS // d
        halo_groups = (d * w) // _PERM_ROWS
        chunk_groups = c // _PERM_ROWS
        kstride = (chunk_groups + 2 * halo_groups) * per
        perm = perm_ref[p - 1]

        def move(src, row0, dst, g, stride):
            res = jnp.dot(perm, src[row0:row0 + _PERM_ROWS, :], preferred_element_type=F32).astype(BF16)
            for r in range(d):
                dst[r * stride + g * per:r * stride + (g + 1) * per, :] = res[r * per:(r + 1) * per]

        for g in range(chunk_groups):
            move(q_ref, g * _PERM_ROWS, qd, g, c // d)
        for prev_ref, cur_ref, next_ref, dst in ((kp_ref, kc_ref, kn_ref, kd), (vp_ref, vc_ref, vn_ref, vd)):
            srcs = ([(prev_ref, c - (halo_groups - g) * _PERM_ROWS) for g in range(halo_groups)]
                    + [(cur_ref, g * _PERM_ROWS) for g in range(chunk_groups)]
                    + [(next_ref, g * _PERM_ROWS) for g in range(halo_groups)])
            for g, (src, row0) in enumerate(srcs):
                move(src, row0, dst, g, kstride)
        return c // d, kstride

    order = sorted(range(len(DIL_PATTERNS)), key=lambda p: -DIL_PATTERNS[p][1])
    for idx, p in enumerate(order):
        d = DIL_PATTERNS[p][1]
        first, last = idx == 0, idx == len(order) - 1
        if d == 1:
            for prev_ref, cur_ref, next_ref, dst in ((kp_ref, kc_ref, kn_ref, kd), (vp_ref, vc_ref, vn_ref, vd)):
                dst[0:w, :] = prev_ref[c - w:c, :]
                dst[w:w + c, :] = cur_ref[...]
                dst[w + c:2 * w + c, :] = next_ref[0:w, :]
            run_pattern(p, d, q_ref, kd, vd, 0, 0, first, last)
        else:
            qstride, kstride = deinterleave(p, d)
            run_pattern(p, d, qd, kd, vd, qstride, kstride, first, last)


def _dilated(cfg, qkvb, gn):
    t = qkvb.shape[0]
    c = cfg.dil_rows
    w = _DIL_W
    n = t // c
    dmax = max(d for _, d in DIL_PATTERNS)
    assert DIL_PATTERNS[0][1] == 1 and c % (dmax * w) == 0 and c >= dmax * w and c % _PERM_ROWS == 0
    assert all((d * w) % _PERM_ROWS == 0 for _, d in DIL_PATTERNS[1:])
    assert (c // w) % cfg.dil_batch == 0
    bias = np.stack([_band_bias(w, d, DIL_HEADS, DIL_HEADS) for _, d in DIL_PATTERNS])
    perm = np.stack([_perm_matrix(d) for _, d in DIL_PATTERNS[1:]])
    blk = (c, DIL_WIDTH)
    prev = lambda col: (lambda i, f: (jnp.maximum(i - 1, 0), col))
    cur = lambda col: (lambda i, f: (i, col))
    nxt = lambda col: (lambda i, f: (jnp.minimum(i + 1, n - 1), col))
    return pl.pallas_call(
        functools.partial(_dilated_kernel, c=c, nb=cfg.dil_batch),
        grid_spec=pltpu.PrefetchScalarGridSpec(
            num_scalar_prefetch=1, grid=(n,),
            in_specs=[
                pl.BlockSpec(blk, cur(0)),
                pl.BlockSpec(blk, prev(1)), pl.BlockSpec(blk, cur(1)), pl.BlockSpec(blk, nxt(1)),
                pl.BlockSpec(blk, prev(2)), pl.BlockSpec(blk, cur(2)), pl.BlockSpec(blk, nxt(2)),
                _resident(bias.shape), _resident(perm.shape), _resident((1, DIL_WIDTH)),
            ],
            out_specs=pl.BlockSpec(blk, cur(0)),
            scratch_shapes=[
                pltpu.VMEM((c, DIL_WIDTH), BF16),
                pltpu.VMEM((3 * c, DIL_WIDTH), BF16), pltpu.VMEM((3 * c, DIL_WIDTH), BF16),
                pltpu.VMEM((len(DIL_PATTERNS), 4, DIL_HEADS, w, 3 * w), F32),
                pltpu.VMEM((DIL_HEADS, c, HEAD_DIM), F32),
                pltpu.VMEM((DIL_HEADS, c, HEAD_DIM), F32),
                pltpu.VMEM((DIL_HEADS, c, HEAD_DIM), F32),
            ]),
        out_shape=jax.ShapeDtypeStruct((t, DIL_WIDTH), BF16),
        compiler_params=_params(("parallel",)),
        name="dilated_attention",
    )(_chunk_flags(cfg, c), qkvb, qkvb, qkvb, qkvb, qkvb, qkvb, qkvb,
      jnp.asarray(bias), jnp.asarray(perm, dtype=BF16), gn)


def _outproj_kernel(ya_ref, yb_ref, yc_ref, x_ref, w_ref, g_ref, o_ref, y_scr, *, nchunk):
    d = o_ref.shape[1]
    b0, b1 = LRU_WIDTH, LRU_WIDTH + DIL_WIDTH
    for c in range(0, d, nchunk):
        cs = slice(c, c + nchunk)
        acc = jnp.dot(ya_ref[...], w_ref[0:b0, cs], preferred_element_type=F32)
        acc += jnp.dot(yb_ref[...], w_ref[b0:b1, cs], preferred_element_type=F32)
        acc += jnp.dot(yc_ref[...], w_ref[b1:, cs], preferred_element_type=F32)
        y_scr[:, cs] = acc
    o_ref[...] = x_ref[...] + _rms(y_scr[...], g_ref[...])


def _outproj(cfg, ya, yb, yc, x, w, g):
    t, d = x.shape
    tm = cfg.tm
    return pl.pallas_call(
        functools.partial(_outproj_kernel, nchunk=min(512, d)),
        grid=(t // tm,),
        in_specs=[
            pl.BlockSpec((tm, LRU_WIDTH), lambda i: (i, 0)),
            pl.BlockSpec((tm, DIL_WIDTH), lambda i: (i, 0)),
            pl.BlockSpec((tm, SWA_WIDTH), lambda i: (i, 0)),
            pl.BlockSpec((tm, d), lambda i: (i, 0)),
            _resident((MIX_WIDTH, d)),
            _resident((1, d)),
        ],
        out_specs=pl.BlockSpec((tm, d), lambda i: (i, 0)),
        out_shape=jax.ShapeDtypeStruct((t, d), F32),
        scratch_shapes=[pltpu.VMEM((tm, d), F32)],
        compiler_params=_params(("parallel",)),
        name="mixer_outproj",
    )(ya, yb, yc, x, w, g)


def _memkv_kernel(mem_ref, g_ref, w_ref, k_ref, v_ref):
    mn = _rms(mem_ref[...], g_ref[...]).astype(BF16)
    kv = jnp.dot(mn, w_ref[...], preferred_element_type=F32)
    k_ref[...] = kv[:, :MEM_WIDTH].astype(BF16)
    v_ref[...] = kv[:, MEM_WIDTH:].astype(BF16)


def _memkv(cfg, mem, g, wkv):
    rows, d = mem.shape
    nm = cfg.n_mem
    shape = jax.ShapeDtypeStruct((rows, MEM_WIDTH), BF16)
    return pl.pallas_call(
        _memkv_kernel,
        grid=(rows // nm,),
        in_specs=[pl.BlockSpec((nm, d), lambda i: (i, 0)), _resident((1, d)), _resident((d, 2 * MEM_WIDTH))],
        out_specs=[pl.BlockSpec((nm, MEM_WIDTH), lambda i: (i, 0))] * 2,
        out_shape=[shape, shape],
        compiler_params=_params(("parallel",)),
        name="memory_kv",
    )(mem, g, wkv)


def _cross_kernel(seq_ref, x_ref, gpre_ref, wq_ref, k_ref, v_ref, wo_ref, gpost_ref, o_ref, xn_scr, o_scr):
    del seq_ref
    x = x_ref[...]
    xn_scr[...] = _rms(x, gpre_ref[...]).astype(BF16)
    q = (jnp.dot(xn_scr[...], wq_ref[...], preferred_element_type=F32) * _QK_SCALE).astype(BF16)
    for h in range(MEM_HEADS):
        hs = slice(h * HEAD_DIM, (h + 1) * HEAD_DIM)
        s = lax.dot_general(q[:, hs], k_ref[:, hs], _NT, preferred_element_type=F32)
        m = jnp.max(s, axis=-1, keepdims=True)
        p = jnp.exp(s - m)
        l = jnp.sum(p, axis=-1, keepdims=True)
        o = jnp.dot(p.astype(BF16), v_ref[:, hs], preferred_element_type=F32)
        o_scr[:, hs] = (o / l).astype(BF16)
    y = jnp.dot(o_scr[...], wo_ref[...], preferred_element_type=F32)
    o_ref[...] = x + _rms(y, gpost_ref[...])


def _cross(cfg, x, gpre, wq, kmem, vmem, wo, gpost):
    t, d = x.shape
    tm, nm = cfg.tm, cfg.n_mem
    seqs, _ = _sequences(cfg)
    seq_of_tile = []
    for si, (_, length) in enumerate(seqs):
        assert length % tm == 0
        seq_of_tile += [si] * (length // tm)
    seq_of_tile = jnp.asarray(np.asarray(seq_of_tile, np.int32))
    return pl.pallas_call(
        _cross_kernel,
        grid_spec=pltpu.PrefetchScalarGridSpec(
            num_scalar_prefetch=1, grid=(t // tm,),
            in_specs=[
                pl.BlockSpec((tm, d), lambda i, s: (i, 0)),
                _resident((1, d)),
                _resident((d, MEM_WIDTH)),
                pl.BlockSpec((nm, MEM_WIDTH), lambda i, s: (s[i], 0)),
                pl.BlockSpec((nm, MEM_WIDTH), lambda i, s: (s[i], 0)),
                _resident((MEM_WIDTH, d)),
                _resident((1, d)),
            ],
            out_specs=pl.BlockSpec((tm, d), lambda i, s: (i, 0)),
            scratch_shapes=[pltpu.VMEM((tm, d), BF16), pltpu.VMEM((tm, MEM_WIDTH), BF16)]),
        out_shape=jax.ShapeDtypeStruct((t, d), F32),
        compiler_params=_params(("parallel",)),
        name="memory_cross_attention",
    )(seq_of_tile, x, gpre, wq, kmem, vmem, wo, gpost)


def _ffn_kernel(x_ref, gpre_ref, w1_ref, w2_ref, gpost_ref, o_ref, xn_scr, h_scr, *, nchunk):
    f = pl.program_id(1)
    tf = w1_ref.shape[1]
    d = w2_ref.shape[1]

    @pl.when(f == 0)
    def _():
        xn_scr[...] = _rms(x_ref[...], gpre_ref[...]).astype(BF16)
        o_ref[...] = jnp.zeros_like(o_ref)

    for c in range(0, tf, nchunk):
        cs = slice(c, min(c + nchunk, tf))
        h = jnp.dot(xn_scr[...], w1_ref[:, cs], preferred_element_type=F32)
        h_scr[:, cs] = jnp.square(jnp.maximum(h, 0.0)).astype(BF16)
    for c in range(0, d, nchunk):
        cs = slice(c, min(c + nchunk, d))
        o_ref[:, cs] += jnp.dot(h_scr[...], w2_ref[:, cs], preferred_element_type=F32)

    @pl.when(f == pl.num_programs(1) - 1)
    def _():
        o_ref[...] = x_ref[...] + _rms(o_ref[...], gpost_ref[...])


def _ffn(cfg, x, gpre, w1, w2, gpost):
    t, d = x.shape
    tm, tf = cfg.tm_ffn, cfg.tf
    dff = w1.shape[1]
    return pl.pallas_call(
        functools.partial(_ffn_kernel, nchunk=512),
        grid=(t // tm, dff // tf),
        in_specs=[
            pl.BlockSpec((tm, d), lambda i, f: (i, 0)),
            _resident((1, d)),
            pl.BlockSpec((d, tf), lambda i, f: (0, f)),
            pl.BlockSpec((tf, d), lambda i, f: (f, 0)),
            _resident((1, d)),
        ],
        out_specs=pl.BlockSpec((tm, d), lambda i, f: (i, 0)),
        out_shape=jax.ShapeDtypeStruct((t, d), F32),
        scratch_shapes=[pltpu.VMEM((tm, d), BF16), pltpu.VMEM((tm, tf), BF16)],
        compiler_params=_params(("parallel", "arbitrary")),
        name="squared_relu_mlp",
    )(x, gpre, w1, w2, gpost)


def _forward(cfg, x, mem, p):
    row = lambda a: a.reshape(1, -1).astype(F32)
    for l in range(cfg.depth):
        lru_in, qkvb, qkvc = _inproj(cfg, x, row(p["mix_norm_pre"][l]), p["w_in"][l])
        gn = p["group_norm"][l]
        wg = jnp.concatenate([p["lru_wa"][l], p["lru_wx"][l]], axis=-1).astype(BF16)
        ya = _lru(cfg, lru_in, p["conv_w"][l], row(p["conv_b"][l]), wg,
                  p["lru_ba"][l][:, None, :], p["lru_bx"][l][:, None, :], p["lru_lam"][l][:, None, :],
                  row(gn[:LRU_WIDTH]))
        yb = _dilated(cfg, qkvb, row(gn[LRU_WIDTH:LRU_WIDTH + DIL_WIDTH]))
        sink = jnp.zeros((1, 128), F32).at[0, :SWA_HEADS].set(p["swa_sink"][l].astype(F32))
        yc = _swa(cfg, qkvc, row(gn[LRU_WIDTH + DIL_WIDTH:]), sink)
        x = _outproj(cfg, ya, yb, yc, x, p["w_out"][l], row(p["mix_norm_post"][l]))
        kmem, vmem = _memkv(cfg, mem, row(p["mem_kv_norm"][l]), p["w_mkv"][l])
        x = _cross(cfg, x, row(p["mem_norm_pre"][l]), p["w_mq"][l], kmem, vmem, p["w_mo"][l],
                   row(p["mem_norm_post"][l]))
        x = _ffn(cfg, x, row(p["ffn_norm_pre"][l]), p["w_ff1"][l], p["w_ff2"][l], row(p["ffn_norm_post"][l]))
    return x


def _run(cfg, x_prompt, x_sample, mem_prompt, mem_sample, mix_norm_pre, mix_norm_post, w_in, conv_w, conv_b,
         lru_wa, lru_ba, lru_wx, lru_bx, lru_lam, swa_sink, group_norm, w_out, mem_norm_pre, mem_norm_post,
         mem_kv_norm, w_mq, w_mk, w_mv, w_mo, ffn_norm_pre, ffn_norm_post, w_ff1, w_ff2):
    d = cfg.d_model
    p = dict(
        mix_norm_pre=mix_norm_pre, mix_norm_post=mix_norm_post, w_in=w_in.astype(BF16), conv_w=conv_w,
        conv_b=conv_b, lru_wa=lru_wa, lru_ba=lru_ba, lru_wx=lru_wx, lru_bx=lru_bx, lru_lam=lru_lam,
        swa_sink=swa_sink, group_norm=group_norm, w_out=w_out.astype(BF16), mem_norm_pre=mem_norm_pre,
        mem_norm_post=mem_norm_post, mem_kv_norm=mem_kv_norm, w_mq=w_mq.astype(BF16),
        w_mkv=jnp.concatenate([w_mk, w_mv], axis=-1).astype(BF16), w_mo=w_mo.astype(BF16),
        ffn_norm_pre=ffn_norm_pre, ffn_norm_post=ffn_norm_post, w_ff1=w_ff1.astype(BF16),
        w_ff2=w_ff2.astype(BF16))
    outs = []
    for group, x, mem in zip(cfg.groups, (x_prompt, x_sample), (mem_prompt, mem_sample)):
        sub = cfg._replace(groups=(group,))
        outs.append(_forward(sub, x.reshape(-1, d), mem.reshape(-1, d), p).reshape(x.shape))
    return tuple(outs)


def kernel(x_prompt, x_sample, mem_prompt, mem_sample, mix_norm_pre, mix_norm_post, w_in, conv_w, conv_b, lru_wa,
           lru_ba, lru_wx, lru_bx, lru_lam, swa_sink, group_norm, w_out, mem_norm_pre, mem_norm_post, mem_kv_norm,
           w_mq, w_mk, w_mv, w_mo, ffn_norm_pre, ffn_norm_post, w_ff1, w_ff2):
    return _run(_CFG, x_prompt, x_sample, mem_prompt, mem_sample, mix_norm_pre, mix_norm_post, w_in, conv_w,
                conv_b, lru_wa, lru_ba, lru_wx, lru_bx, lru_lam, swa_sink, group_norm, w_out, mem_norm_pre,
                mem_norm_post, mem_kv_norm, w_mq, w_mk, w_mv, w_mo, ffn_norm_pre, ffn_norm_post, w_ff1, w_ff2)
```

```python
import functools
from typing import NamedTuple

import numpy as np
import jax
import jax.numpy as jnp
from jax import lax
from jax.experimental import pallas as pl
from jax.experimental.pallas import tpu as pltpu

F32 = jnp.float32
BF16 = jnp.bfloat16

D_MODEL = 2048
BATCH = 8
SEQ = 4096
DEPTH = 4
DEC_BATCH = 1
DEC_SEQ = 16384
HEAD_DIM = 128
LRU_WIDTH = 512
LRU_BLOCKS = 4
LRU_BLOCK_WIDTH = LRU_WIDTH // LRU_BLOCKS
CONV_WIDTH = 4
CONV_LEFT = 2
LRU_C = 8.0
DIL_HEADS = 6
DIL_PATTERNS = ((128, 1), (512, 4), (2048, 16))
SWA_HEADS = 6
SWA_KV_HEADS = 2
SWA_WINDOW = 128
DIL_WIDTH = DIL_HEADS * HEAD_DIM
SWA_WIDTH = SWA_HEADS * HEAD_DIM
SWA_KV_WIDTH = SWA_KV_HEADS * HEAD_DIM
MIX_WIDTH = LRU_WIDTH + DIL_WIDTH + SWA_WIDTH
IN_WIDTH = 2 * LRU_WIDTH + 3 * DIL_WIDTH + SWA_WIDTH + 2 * SWA_KV_WIDTH
N_MEM = 256
MEM_HEADS = 4
MEM_WIDTH = MEM_HEADS * HEAD_DIM
D_FF = 4 * D_MODEL
EPS = 1e-6

_NEG = -1e30
_QK_SCALE = HEAD_DIM ** -0.5
_HALO_ROWS = 8
_SCAN_GROUP = 8
_DIL_W = DIL_PATTERNS[0][0] // (2 * DIL_PATTERNS[0][1])
assert all(wn // (2 * d) == _DIL_W for wn, d in DIL_PATTERNS)
_PERM_ROWS = 256
_V7X_VMEM_BYTES = 64 * 1024 * 1024
_VMEM_LIMIT = _V7X_VMEM_BYTES - 8 * 1024 * 1024
_NT = (((1,), (1,)), ((), ()))


class _Cfg(NamedTuple):
    d_model: int
    d_ff: int
    depth: int
    groups: tuple
    n_mem: int
    tm: int
    tm_ffn: int
    tf: int
    lru_chunk: int
    lru_rows: int
    swa_rows: int
    dil_rows: int
    swa_batch: int
    dil_batch: int


_CFG = _Cfg(d_model=D_MODEL, d_ff=D_FF, depth=DEPTH, groups=((BATCH, SEQ), (DEC_BATCH, DEC_SEQ)),
            n_mem=N_MEM, tm=512, tm_ffn=1024, tf=512, lru_chunk=1024, lru_rows=256, swa_rows=512,
            dil_rows=1024, swa_batch=2, dil_batch=4)


def _sequences(cfg):
    out, start = [], 0
    for n, length in cfg.groups:
        for _ in range(n):
            out.append((start, length))
            start += length
    return out, start


def _chunk_flags(cfg, rows):
    seqs, total = _sequences(cfg)
    starts = {s for s, _ in seqs}
    ends = {s + l for s, l in seqs}
    for s, l in seqs:
        assert l % rows == 0, (l, rows)
    n = total // rows
    flags = np.zeros((n,), np.int32)
    for c in range(n):
        flags[c] = (1 if c * rows in starts else 0) | (2 if (c + 1) * rows in ends else 0)
    return jnp.asarray(flags)


def _params(semantics):
    return pltpu.CompilerParams(dimension_semantics=semantics, vmem_limit_bytes=_VMEM_LIMIT)


def _rms(x, g):
    ms = jnp.mean(x * x, axis=-1, keepdims=True)
    return x * lax.rsqrt(ms + EPS) * g


def _resident(shape, layer=None):
    if layer is None:
        return pl.BlockSpec(shape, lambda *_: (0,) * len(shape), pipeline_mode=pl.Buffered(1))
    return pl.BlockSpec((None,) + tuple(shape), lambda *_: (layer,) + (0,) * len(shape),
                        pipeline_mode=pl.Buffered(1))


def _inproj_plan():
    lru_w = 2 * LRU_WIDTH
    dil_w = 3 * DIL_WIDTH
    segs = [
        (0, lru_w, 0, None),
        (lru_w, lru_w + DIL_WIDTH, 1, _QK_SCALE),
        (lru_w + DIL_WIDTH, lru_w + dil_w, 1, None),
        (lru_w + dil_w, lru_w + dil_w + SWA_WIDTH, 2, _QK_SCALE),
        (lru_w + dil_w + SWA_WIDTH, IN_WIDTH, 2, None),
    ]
    base = {0: 0, 1: lru_w, 2: lru_w + dil_w}
    plan = []
    for c0, c1, oi, scale in segs:
        c = c0
        while c < c1:
            n = min(512, c1 - c)
            plan.append((c, c + n, oi, c - base[oi], scale))
            c += n
    return tuple(plan)


def _inproj_kernel(x_ref, g_ref, w_ref, lru_ref, qkvb_ref, qkvc_ref, xn_scr, *, plan):
    xn_scr[...] = _rms(x_ref[...], g_ref[...]).astype(BF16)
    outs = (lru_ref, qkvb_ref, qkvc_ref)
    for c0, c1, oi, o0, scale in plan:
        acc = jnp.dot(xn_scr[...], w_ref[:, c0:c1], preferred_element_type=F32)
        if scale is not None:
            acc = acc * scale
        outs[oi][:, o0:o0 + (c1 - c0)] = acc.astype(outs[oi].dtype)


def _inproj(cfg, x, g, w, layer):
    t, d = x.shape
    tm = cfg.tm
    widths = (2 * LRU_WIDTH, 3 * DIL_WIDTH, SWA_WIDTH + 2 * SWA_KV_WIDTH)
    return pl.pallas_call(
        functools.partial(_inproj_kernel, plan=_inproj_plan()),
        grid=(t // tm,),
        in_specs=[
            pl.BlockSpec((tm, d), lambda i: (i, 0)),
            _resident((1, d)),
            _resident((d, IN_WIDTH), layer),
        ],
        out_specs=[pl.BlockSpec((tm, wd), lambda i: (i, 0)) for wd in widths],
        out_shape=[
            jax.ShapeDtypeStruct((t, widths[0]), F32),
            jax.ShapeDtypeStruct((t, widths[1]), BF16),
            jax.ShapeDtypeStruct((t, widths[2]), BF16),
        ],
        scratch_shapes=[pltpu.VMEM((tm, d), BF16)],
        compiler_params=_params(("parallel",)),
        name="mixer_inproj",
    )(x, g, w)


def _lru_fill_halo(first, last, xa_ref, xp_ref, xn_ref, xext, lc):
    xext[_HALO_ROWS:_HALO_ROWS + lc, :] = xa_ref[...]

    @pl.when(first)
    def _():
        xext[0:_HALO_ROWS, :] = jnp.zeros((_HALO_ROWS, LRU_WIDTH), F32)

    @pl.when(jnp.logical_not(first))
    def _():
        xext[0:_HALO_ROWS, :] = xp_ref[...]

    @pl.when(last)
    def _():
        xext[_HALO_ROWS + lc:, :] = jnp.zeros((_HALO_ROWS, LRU_WIDTH), F32)

    @pl.when(jnp.logical_not(last))
    def _():
        xext[_HALO_ROWS + lc:, :] = xn_ref[...]


def _lru_gates(xext, cw_ref, cb_ref, wg_ref, ba_ref, bx_ref, lam_ref, a_scr, u_scr, lc, rb):
    lam = lam_ref[...]
    neg = -lam
    softplus = jnp.maximum(neg, 0.0) + jnp.log1p(jnp.exp(-jnp.abs(neg)))
    cb = cb_ref[...]
    taps = [cw_ref[j:j + 1, :] for j in range(CONV_WIDTH)]
    for blk in range(lc // rb):
        r0 = blk * rb
        xc = cb + sum(taps[j] * xext[r0 + _HALO_ROWS - CONV_LEFT + j:r0 + _HALO_ROWS - CONV_LEFT + j + rb, :]
                      for j in range(CONV_WIDTH))
        xcb = xc.astype(BF16)
        for n in range(LRU_BLOCKS):
            cs = slice(n * LRU_BLOCK_WIDTH, (n + 1) * LRU_BLOCK_WIDTH)
            g = jnp.dot(xcb[:, cs], wg_ref[n], preferred_element_type=F32)
            r = jax.nn.sigmoid(g[:, :LRU_BLOCK_WIDTH] + ba_ref[:, cs])
            ig = jax.nn.sigmoid(g[:, LRU_BLOCK_WIDTH:] + bx_ref[:, cs])
            log_a = (-LRU_C * r) * softplus[:, cs]
            a = jnp.exp(log_a)
            one_minus_a2 = -jnp.tanh(log_a) * (1.0 + a * a)
            u = jnp.sqrt(one_minus_a2) * (ig * xc[:, cs])
            groups = slice(r0 // _SCAN_GROUP, (r0 + rb) // _SCAN_GROUP)
            a_scr[groups, :, cs] = a.reshape(rb // _SCAN_GROUP, _SCAN_GROUP, LRU_BLOCK_WIDTH)
            u_scr[groups, :, cs] = u.reshape(rb // _SCAN_GROUP, _SCAN_GROUP, LRU_BLOCK_WIDTH)


def _lru_scan(reset, a_scr, u_scr, h_dst, carry, lc, reverse):
    @pl.when(reset)
    def _():
        carry[...] = jnp.zeros((1, LRU_WIDTH), F32)

    def group(i, h):
        g = lc // _SCAN_GROUP - 1 - i if reverse else i
        order = [_SCAN_GROUP - 1 - j if reverse else j for j in range(_SCAN_GROUP)]
        p = a_scr[g, order[0]:order[0] + 1, :]
        q = u_scr[g, order[0]:order[0] + 1, :]
        outs = [p * h + q]
        for j in order[1:]:
            a = a_scr[g, j:j + 1, :]
            q = a * q + u_scr[g, j:j + 1, :]
            p = a * p
            outs.append(p * h + q)
        for j, o in zip(order, outs):
            h_dst[g, j:j + 1, :] = o
        return outs[-1]

    carry[...] = lax.fori_loop(0, lc // _SCAN_GROUP, group, carry[...], unroll=2)


def _lru_fwd_kernel(flags_ref, xa_ref, xp_ref, xn_ref, cw_ref, cb_ref, wg_ref, ba_ref, bx_ref, lam_ref,
                    hf_ref, xext, a_scr, u_scr, carry, *, lc, rb):
    fl = flags_ref[pl.program_id(0)]
    first = (fl & 1) != 0
    last = (fl & 2) != 0
    _lru_fill_halo(first, last, xa_ref, xp_ref, xn_ref, xext, lc)
    _lru_gates(xext, cw_ref, cb_ref, wg_ref, ba_ref, bx_ref, lam_ref, a_scr, u_scr, lc, rb)
    _lru_scan(first, a_scr, u_scr, hf_ref, carry, lc, reverse=False)


def _lru_bwd_kernel(flags_ref, xa_ref, xp_ref, xn_ref, gate_ref, hf_ref, cw_ref, cb_ref, wg_ref, ba_ref,
                    bx_ref, lam_ref, gn_ref, y_ref, xext, a_scr, u_scr, h_scr, carry, *, lc, rb, nchunks):
    fl = flags_ref[nchunks - 1 - pl.program_id(0)]
    first = (fl & 1) != 0
    last = (fl & 2) != 0
    _lru_fill_halo(first, last, xa_ref, xp_ref, xn_ref, xext, lc)
    _lru_gates(xext, cw_ref, cb_ref, wg_ref, ba_ref, bx_ref, lam_ref, a_scr, u_scr, lc, rb)
    _lru_scan(last, a_scr, u_scr, h_scr, carry, lc, reverse=True)
    for blk in range(lc // rb):
        rows = slice(blk * rb, (blk + 1) * rb)
        groups = slice(blk * rb // _SCAN_GROUP, (blk + 1) * rb // _SCAN_GROUP)
        h = (hf_ref[groups] + h_scr[groups]).reshape(rb, LRU_WIDTH)
        y = h * jax.nn.gelu(gate_ref[rows, :])
        y_ref[rows, :] = _rms(y, gn_ref[...]).astype(BF16)


def _lru(cfg, lru_in, cw, cb, wg, ba, bx, lam, gn):
    t = lru_in.shape[0]
    lc, rb = cfg.lru_chunk, cfg.lru_rows
    nchunks = t // lc
    hb = lc // _HALO_ROWS
    nhalo = t // _HALO_ROWS
    flags = _chunk_flags(cfg, lc)
    row = lambda: _resident((1, LRU_WIDTH))

    def specs(chunk_of):
        return [
            pl.BlockSpec((lc, LRU_WIDTH), lambda i, f: (chunk_of(i), 0)),
            pl.BlockSpec((_HALO_ROWS, LRU_WIDTH), lambda i, f: (jnp.maximum(chunk_of(i) * hb - 1, 0), 0)),
            pl.BlockSpec((_HALO_ROWS, LRU_WIDTH), lambda i, f: (jnp.minimum((chunk_of(i) + 1) * hb, nhalo - 1), 0)),
        ]

    def weights(d):
        return [
            _resident((CONV_WIDTH, LRU_WIDTH)), row(),
            pl.BlockSpec((None, LRU_BLOCKS, LRU_BLOCK_WIDTH, 2 * LRU_BLOCK_WIDTH), lambda i, f: (d, 0, 0, 0)),
            pl.BlockSpec((None, 1, LRU_WIDTH), lambda i, f: (d, 0, 0)),
            pl.BlockSpec((None, 1, LRU_WIDTH), lambda i, f: (d, 0, 0)),
            pl.BlockSpec((None, 1, LRU_WIDTH), lambda i, f: (d, 0, 0)),
        ]

    grouped = (lc // _SCAN_GROUP, _SCAN_GROUP, LRU_WIDTH)
    scratch = [pltpu.VMEM((lc + 2 * _HALO_ROWS, LRU_WIDTH), F32), pltpu.VMEM(grouped, F32),
               pltpu.VMEM(grouped, F32)]
    carry = [pltpu.VMEM((1, LRU_WIDTH), F32)]

    fwd = lambda i: i
    hf = pl.pallas_call(
        functools.partial(_lru_fwd_kernel, lc=lc, rb=rb),
        grid_spec=pltpu.PrefetchScalarGridSpec(
            num_scalar_prefetch=1, grid=(nchunks,),
            in_specs=specs(fwd) + weights(0),
            out_specs=pl.BlockSpec(grouped, lambda i, f: (i, 0, 0)),
            scratch_shapes=scratch + carry),
        out_shape=jax.ShapeDtypeStruct((t // _SCAN_GROUP, _SCAN_GROUP, LRU_WIDTH), F32),
        compiler_params=_params(("arbitrary",)),
        name="lru_forward",
    )(flags, lru_in, lru_in, lru_in, cw, cb, wg, ba, bx, lam)

    bwd = lambda i: nchunks - 1 - i
    return pl.pallas_call(
        functools.partial(_lru_bwd_kernel, lc=lc, rb=rb, nchunks=nchunks),
        grid_spec=pltpu.PrefetchScalarGridSpec(
            num_scalar_prefetch=1, grid=(nchunks,),
            in_specs=specs(bwd) + [
                pl.BlockSpec((lc, LRU_WIDTH), lambda i, f: (bwd(i), 1)),
                pl.BlockSpec(grouped, lambda i, f: (bwd(i), 0, 0)),
            ] + weights(1) + [row()],
            out_specs=pl.BlockSpec((lc, LRU_WIDTH), lambda i, f: (bwd(i), 0)),
            scratch_shapes=scratch + [pltpu.VMEM(grouped, F32)] + carry),
        out_shape=jax.ShapeDtypeStruct((t, LRU_WIDTH), BF16),
        compiler_params=_params(("arbitrary",)),
        name="lru_backward",
    )(flags, lru_in, lru_in, lru_in, lru_in, hf, cw, cb, wg, ba, bx, lam, gn)


def _alibi_slopes(n):
    return [2.0 ** (-8.0 * (i + 1) / n) for i in range(n)]


def _band_bias(w, dist_scale, hq, hkv):
    rep = hq // hkv
    slopes = _alibi_slopes(hq)
    qi = np.arange(w)[:, None]
    kj = np.arange(3 * w)[None, :]
    rel = np.abs(kj - w - qi)
    out = np.empty((hkv, rep * w, 3 * w), np.float32)
    for g in range(hkv):
        for r in range(rep):
            out[g, r * w:(r + 1) * w] = np.where(rel <= w, -slopes[g * rep + r] * dist_scale * rel, _NEG)
    return out


def _edge_penalties(fl, w):
    col = lax.broadcasted_iota(jnp.int32, (1, 3 * w), 1)
    pen_first = jnp.where(col < w, jnp.where((fl & 1) != 0, _NEG, 0.0), 0.0)
    pen_last = jnp.where(col >= 2 * w, jnp.where((fl & 2) != 0, _NEG, 0.0), 0.0)
    return pen_first, pen_last


def _attend(qs, ks, vs, biases):
    scores = [lax.dot_general(q, k, _NT, preferred_element_type=F32) for q, k in zip(qs, ks)]
    ms, ps = [], []
    for s, b in zip(scores, biases):
        if b is not None:
            s = s + b
        m = jnp.max(s, axis=-1, keepdims=True)
        ms.append(m)
        ps.append(jnp.exp(s - m).astype(BF16))
    ls, accs = [], []
    for p, v in zip(ps, vs):
        v1 = jnp.concatenate([v, jnp.ones(v.shape, v.dtype)], axis=1)
        out = jnp.dot(p, v1, preferred_element_type=F32)
        accs.append(out[:, :HEAD_DIM])
        ls.append(out[:, HEAD_DIM:])
    return ms, ls, accs


def _swa_kernel(flags_ref, q_ref, kc_ref, kp_ref, kn_ref, vc_ref, vp_ref, vn_ref, bias_ref, sink_ref, gn_ref,
                y_out, kbuf, vbuf, *, w, nsub, nb):
    rep = SWA_HEADS // SWA_KV_HEADS
    rows_total = nsub * w
    pen_first, pen_last = _edge_penalties(flags_ref[pl.program_id(0)], w)
    kbuf[0:w, :] = kp_ref[...]
    kbuf[w:w + rows_total, :] = kc_ref[...]
    kbuf[w + rows_total:, :] = kn_ref[...]
    vbuf[0:w, :] = vp_ref[...]
    vbuf[w:w + rows_total, :] = vc_ref[...]
    vbuf[w + rows_total:, :] = vn_ref[...]

    for j0 in range(0, nsub, nb):
        blocks = list(range(j0, min(j0 + nb, nsub)))
        qs, ks, vs, bs = [], [], [], []
        for j in blocks:
            rows = slice(j * w, (j + 1) * w)
            for g in range(SWA_KV_HEADS):
                gs = slice(g * HEAD_DIM, (g + 1) * HEAD_DIM)
                qs.append(jnp.concatenate(
                    [q_ref[rows, (g * rep + r) * HEAD_DIM:(g * rep + r + 1) * HEAD_DIM] for r in range(rep)], axis=0))
                ks.append(kbuf[j * w:(j + 3) * w, gs])
                vs.append(vbuf[j * w:(j + 3) * w, gs])
                b = bias_ref[g]
                if j == 0:
                    b = b + pen_first
                if j == nsub - 1:
                    b = b + pen_last
                bs.append(b)
        ms, ls, accs = _attend(qs, ks, vs, bs)
        for bi, j in enumerate(blocks):
            heads_out = []
            for g in range(SWA_KV_HEADS):
                idx = bi * SWA_KV_HEADS + g
                for r in range(rep):
                    h = g * rep + r
                    part = slice(r * w, (r + 1) * w)
                    m, l, acc = ms[idx][part], ls[idx][part], accs[idx][part]
                    factor = jax.nn.sigmoid(m + jnp.log(l) - sink_ref[:, h:h + 1])
                    heads_out.append((acc / l) * factor)
            y = jnp.concatenate(heads_out, axis=1)
            y_out[j * w:(j + 1) * w, :] = _rms(y, gn_ref[...]).astype(BF16)


def _swa(cfg, qkvc, gn, sink):
    t, c = qkvc.shape
    w = SWA_WINDOW
    rows = cfg.swa_rows
    nsub = rows // w
    nchunks = t // rows
    nblk = t // w
    qw, kvw = SWA_WIDTH, SWA_KV_WIDTH
    rep = SWA_HEADS // SWA_KV_HEADS
    assert qw % kvw == 0 and c == qw + 2 * kvw
    kcol, vcol = qw // kvw, qw // kvw + 1
    prev = lambda col: (lambda i, f: (jnp.maximum(i * nsub - 1, 0), col))
    nxt = lambda col: (lambda i, f: (jnp.minimum((i + 1) * nsub, nblk - 1), col))
    return pl.pallas_call(
        functools.partial(_swa_kernel, w=w, nsub=nsub, nb=cfg.swa_batch),
        grid_spec=pltpu.PrefetchScalarGridSpec(
            num_scalar_prefetch=1, grid=(nchunks,),
            in_specs=[
                pl.BlockSpec((rows, qw), lambda i, f: (i, 0)),
                pl.BlockSpec((rows, kvw), lambda i, f: (i, kcol)),
                pl.BlockSpec((w, kvw), prev(kcol)),
                pl.BlockSpec((w, kvw), nxt(kcol)),
                pl.BlockSpec((rows, kvw), lambda i, f: (i, vcol)),
                pl.BlockSpec((w, kvw), prev(vcol)),
                pl.BlockSpec((w, kvw), nxt(vcol)),
                _resident((SWA_KV_HEADS, rep * w, 3 * w)),
                _resident((1, 128)),
                _resident((1, qw)),
            ],
            out_specs=pl.BlockSpec((rows, qw), lambda i, f: (i, 0)),
            scratch_shapes=[pltpu.VMEM((rows + 2 * w, kvw), BF16), pltpu.VMEM((rows + 2 * w, kvw), BF16)]),
        out_shape=jax.ShapeDtypeStruct((t, qw), BF16),
        compiler_params=_params(("parallel",)),
        name="windowed_gqa_sink",
    )(_chunk_flags(cfg, rows), qkvc, qkvc, qkvc, qkvc, qkvc, qkvc, qkvc,
      jnp.asarray(_band_bias(w, 1, SWA_HEADS, SWA_KV_HEADS)), sink, gn)


def _perm_matrix(d):
    n = _PERM_ROWS
    per = n // d
    p = np.zeros((n, n), np.float32)
    for r in range(d):
        for m in range(per):
            p[r * per + m, d * m + r] = 1.0
    return p


def _dilated_kernel(flags_ref, q_ref, kp_ref, kc_ref, kn_ref, vp_ref, vc_ref, vn_ref, bias_ref, perm_ref, gn_ref,
                    y_ref, qd, kd, vd, biasv, acc_nat, m_nat, l_nat, *, c, nb):
    w = _DIL_W
    nh = DIL_HEADS
    nsb = c // w
    pen_first, pen_last = _edge_penalties(flags_ref[pl.program_id(0)], w)
    for p in range(len(DIL_PATTERNS)):
        for h in range(nh):
            b = bias_ref[p, h]
            biasv[p, 0, h] = b
            biasv[p, 1, h] = b + pen_first
            biasv[p, 2, h] = b + pen_last
            biasv[p, 3, h] = b + pen_first + pen_last

    heads = [slice(h * HEAD_DIM, (h + 1) * HEAD_DIM) for h in range(nh)]

    def run_pattern(p, d, qsrc, ksrc, vsrc, qstride, kstride, first_pattern, last_pattern):
        nblk = c // (d * w)
        assert d == 1 or not last_pattern

        def body(it, carry):
            qs, ks, vs, bs, where = [], [], [], [], []
            for b in range(nb):
                sb = it * nb + b
                r = sb // nblk
                s = sb % nblk
                var = jnp.where(s == 0, 1, 0) + jnp.where(s == nblk - 1, 2, 0)
                q0 = pl.multiple_of(r * qstride + s * w, w)
                k0 = pl.multiple_of(r * kstride + s * w, w)
                where.append(d * w * s + r)
                for h in range(nh):
                    qs.append(qsrc[pl.ds(q0, w), heads[h]])
                    ks.append(ksrc[pl.ds(k0, 3 * w), heads[h]])
                    vs.append(vsrc[pl.ds(k0, 3 * w), heads[h]])
                    bs.append(biasv[p, var, h])
            ms, ls, accs = _attend(qs, ks, vs, bs)
            for b in range(nb):
                nat = pl.ds(where[b], w, stride=d) if d > 1 else pl.ds(pl.multiple_of(where[b], w), w)
                outs = []
                for h in range(nh):
                    m, l, acc = ms[b * nh + h], ls[b * nh + h], accs[b * nh + h]
                    if not first_pattern:
                        m_p = m_nat[h, nat, :]
                        m_n = jnp.maximum(m_p, m)
                        alpha = jnp.exp(m_p - m_n)
                        beta = jnp.exp(m - m_n)
                        l = alpha * l_nat[h, nat, :] + beta * l
                        acc = alpha * acc_nat[h, nat, :] + beta * acc
                        m = m_n
                    if last_pattern:
                        outs.append(acc / l)
                    else:
                        acc_nat[h, nat, :] = acc
                        m_nat[h, nat, :] = jnp.broadcast_to(m, (w, HEAD_DIM))
                        l_nat[h, nat, :] = jnp.broadcast_to(l, (w, HEAD_DIM))
                if last_pattern:
                    y_ref[nat, :] = _rms(jnp.concatenate(outs, axis=1), gn_ref[...]).astype(BF16)
            return carry

        lax.fori_loop(0, nsb // nb, body, 0)

    def deinterleave(p, d):
        per = _PERM_ROWS // d
        halo_groups = (d * w) // _PERM_ROWS
        chunk_groups = c // _PERM_ROWS
        kstride = (chunk_groups + 2 * halo_groups) * per
        perm = perm_ref[p - 1]

        def move(src, row0, dst, g, stride):
            res = jnp.dot(perm, src[row0:row0 + _PERM_ROWS, :], preferred_element_type=F32).astype(BF16)
            for r in range(d):
                dst[r * stride + g * per:r * stride + (g + 1) * per, :] = res[r * per:(r + 1) * per]

        for g in range(chunk_groups):
            move(q_ref, g * _PERM_ROWS, qd, g, c // d)
        for prev_ref, cur_ref, next_ref, dst in ((kp_ref, kc_ref, kn_ref, kd), (vp_ref, vc_ref, vn_ref, vd)):
            srcs = ([(prev_ref, c - (halo_groups - g) * _PERM_ROWS) for g in range(halo_groups)]
                    + [(cur_ref, g * _PERM_ROWS) for g in range(chunk_groups)]
                    + [(next_ref, g * _PERM_ROWS) for g in range(halo_groups)])
            for g, (src, row0) in enumerate(srcs):
                move(src, row0, dst, g, kstride)
        return c // d, kstride

    order = sorted(range(len(DIL_PATTERNS)), key=lambda p: -DIL_PATTERNS[p][1])
    for idx, p in enumerate(order):
        d = DIL_PATTERNS[p][1]
        first, last = idx == 0, idx == len(order) - 1
        if d == 1:
            for prev_ref, cur_ref, next_ref, dst in ((kp_ref, kc_ref, kn_ref, kd), (vp_ref, vc_ref, vn_ref, vd)):
                dst[0:w, :] = prev_ref[c - w:c, :]
                dst[w:w + c, :] = cur_ref[...]
                dst[w + c:2 * w + c, :] = next_ref[0:w, :]
            run_pattern(p, d, q_ref, kd, vd, 0, 0, first, last)
        else:
            qstride, kstride = deinterleave(p, d)
            run_pattern(p, d, qd, kd, vd, qstride, kstride, first, last)


def _dilated(cfg, qkvb, gn):
    t = qkvb.shape[0]
    c = cfg.dil_rows
    w = _DIL_W
    n = t // c
    dmax = max(d for _, d in DIL_PATTERNS)
    assert DIL_PATTERNS[0][1] == 1 and c % (dmax * w) == 0 and c >= dmax * w and c % _PERM_ROWS == 0
    assert all((d * w) % _PERM_ROWS == 0 for _, d in DIL_PATTERNS[1:])
    assert (c // w) % cfg.dil_batch == 0
    bias = np.stack([_band_bias(w, d, DIL_HEADS, DIL_HEADS) for _, d in DIL_PATTERNS])
    perm = np.stack([_perm_matrix(d) for _, d in DIL_PATTERNS[1:]])
    blk = (c, DIL_WIDTH)
    prev = lambda col: (lambda i, f: (jnp.maximum(i - 1, 0), col))
    cur = lambda col: (lambda i, f: (i, col))
    nxt = lambda col: (lambda i, f: (jnp.minimum(i + 1, n - 1), col))
    return pl.pallas_call(
        functools.partial(_dilated_kernel, c=c, nb=cfg.dil_batch),
        grid_spec=pltpu.PrefetchScalarGridSpec(
            num_scalar_prefetch=1, grid=(n,),
            in_specs=[
                pl.BlockSpec(blk, cur(0)),
                pl.BlockSpec(blk, prev(1)), pl.BlockSpec(blk, cur(1)), pl.BlockSpec(blk, nxt(1)),
                pl.BlockSpec(blk, prev(2)), pl.BlockSpec(blk, cur(2)), pl.BlockSpec(blk, nxt(2)),
                _resident(bias.shape), _resident(perm.shape), _resident((1, DIL_WIDTH)),
            ],
            out_specs=pl.BlockSpec(blk, cur(0)),
            scratch_shapes=[
                pltpu.VMEM((c, DIL_WIDTH), BF16),
                pltpu.VMEM((3 * c, DIL_WIDTH), BF16), pltpu.VMEM((3 * c, DIL_WIDTH), BF16),
                pltpu.VMEM((len(DIL_PATTERNS), 4, DIL_HEADS, w, 3 * w), F32),
                pltpu.VMEM((DIL_HEADS, c, HEAD_DIM), F32),
                pltpu.VMEM((DIL_HEADS, c, HEAD_DIM), F32),
                pltpu.VMEM((DIL_HEADS, c, HEAD_DIM), F32),
            ]),
        out_shape=jax.ShapeDtypeStruct((t, DIL_WIDTH), BF16),
        compiler_params=_params(("parallel",)),
        name="dilated_attention",
    )(_chunk_flags(cfg, c), qkvb, qkvb, qkvb, qkvb, qkvb, qkvb, qkvb,
      jnp.asarray(bias), jnp.asarray(perm, dtype=BF16), gn)


def _outproj_kernel(ya_ref, yb_ref, yc_ref, x_ref, w_ref, g_ref, o_ref, y_scr, *, nchunk):
    d = o_ref.shape[1]
    b0, b1 = LRU_WIDTH, LRU_WIDTH + DIL_WIDTH
    for c in range(0, d, nchunk):
        cs = slice(c, c + nchunk)
        acc = jnp.dot(ya_ref[...], w_ref[0:b0, cs], preferred_element_type=F32)
        acc += jnp.dot(yb_ref[...], w_ref[b0:b1, cs], preferred_element_type=F32)
        acc += jnp.dot(yc_ref[...], w_ref[b1:, cs], preferred_element_type=F32)
        y_scr[:, cs] = acc
    o_ref[...] = x_ref[...] + _rms(y_scr[...], g_ref[...])


def _outproj(cfg, ya, yb, yc, x, w, g, layer):
    t, d = x.shape
    tm = cfg.tm
    return pl.pallas_call(
        functools.partial(_outproj_kernel, nchunk=min(512, d)),
        grid=(t // tm,),
        in_specs=[
            pl.BlockSpec((tm, LRU_WIDTH), lambda i: (i, 0)),
            pl.BlockSpec((tm, DIL_WIDTH), lambda i: (i, 0)),
            pl.BlockSpec((tm, SWA_WIDTH), lambda i: (i, 0)),
            pl.BlockSpec((tm, d), lambda i: (i, 0)),
            _resident((MIX_WIDTH, d), layer),
            _resident((1, d)),
        ],
        out_specs=pl.BlockSpec((tm, d), lambda i: (i, 0)),
        out_shape=jax.ShapeDtypeStruct((t, d), F32),
        scratch_shapes=[pltpu.VMEM((tm, d), F32)],
        compiler_params=_params(("parallel",)),
        name="mixer_outproj",
    )(ya, yb, yc, x, w, g)


def _memkv_kernel(mem_ref, g_ref, w_ref, k_ref, v_ref):
    mn = _rms(mem_ref[...], g_ref[...]).astype(BF16)
    kv = jnp.dot(mn, w_ref[...], preferred_element_type=F32)
    k_ref[...] = kv[:, :MEM_WIDTH].astype(BF16)
    v_ref[...] = kv[:, MEM_WIDTH:].astype(BF16)


def _memkv(cfg, mem, g, wkv, layer):
    rows, d = mem.shape
    nm = cfg.n_mem
    shape = jax.ShapeDtypeStruct((rows, MEM_WIDTH), BF16)
    return pl.pallas_call(
        _memkv_kernel,
        grid=(rows // nm,),
        in_specs=[pl.BlockSpec((nm, d), lambda i: (i, 0)), _resident((1, d)),
                  _resident((d, 2 * MEM_WIDTH), layer)],
        out_specs=[pl.BlockSpec((nm, MEM_WIDTH), lambda i: (i, 0))] * 2,
        out_shape=[shape, shape],
        compiler_params=_params(("parallel",)),
        name="memory_kv",
    )(mem, g, wkv)


def _cross_kernel(seq_ref, x_ref, gpre_ref, wq_ref, k_ref, v_ref, wo_ref, gpost_ref, o_ref, xn_scr, o_scr):
    del seq_ref
    x = x_ref[...]
    xn_scr[...] = _rms(x, gpre_ref[...]).astype(BF16)
    q = (jnp.dot(xn_scr[...], wq_ref[...], preferred_element_type=F32) * _QK_SCALE).astype(BF16)
    heads = [slice(h * HEAD_DIM, (h + 1) * HEAD_DIM) for h in range(MEM_HEADS)]
    _, ls, accs = _attend([q[:, hs] for hs in heads], [k_ref[:, hs] for hs in heads],
                          [v_ref[:, hs] for hs in heads], [None] * MEM_HEADS)
    for hs, l, acc in zip(heads, ls, accs):
        o_scr[:, hs] = (acc / l).astype(BF16)
    y = jnp.dot(o_scr[...], wo_ref[...], preferred_element_type=F32)
    o_ref[...] = x + _rms(y, gpost_ref[...])


def _cross(cfg, x, gpre, wq, kmem, vmem, wo, gpost, layer):
    t, d = x.shape
    tm, nm = cfg.tm, cfg.n_mem
    seqs, _ = _sequences(cfg)
    seq_of_tile = []
    for si, (_, length) in enumerate(seqs):
        assert length % tm == 0
        seq_of_tile += [si] * (length // tm)
    seq_of_tile = jnp.asarray(np.asarray(seq_of_tile, np.int32))
    return pl.pallas_call(
        _cross_kernel,
        grid_spec=pltpu.PrefetchScalarGridSpec(
            num_scalar_prefetch=1, grid=(t // tm,),
            in_specs=[
                pl.BlockSpec((tm, d), lambda i, s: (i, 0)),
                _resident((1, d)),
                _resident((d, MEM_WIDTH), layer),
                pl.BlockSpec((nm, MEM_WIDTH), lambda i, s: (s[i], 0)),
                pl.BlockSpec((nm, MEM_WIDTH), lambda i, s: (s[i], 0)),
                _resident((MEM_WIDTH, d), layer),
                _resident((1, d)),
            ],
            out_specs=pl.BlockSpec((tm, d), lambda i, s: (i, 0)),
            scratch_shapes=[pltpu.VMEM((tm, d), BF16), pltpu.VMEM((tm, MEM_WIDTH), BF16)]),
        out_shape=jax.ShapeDtypeStruct((t, d), F32),
        compiler_params=_params(("parallel",)),
        name="memory_cross_attention",
    )(seq_of_tile, x, gpre, wq, kmem, vmem, wo, gpost)


def _ffn_kernel(x_ref, gpre_ref, w1_ref, w2_ref, gpost_ref, o_ref, xn_scr, h_scr, *, nchunk):
    f = pl.program_id(1)
    tf = w1_ref.shape[1]
    d = w2_ref.shape[1]

    @pl.when(f == 0)
    def _():
        xn_scr[...] = _rms(x_ref[...], gpre_ref[...]).astype(BF16)
        o_ref[...] = jnp.zeros_like(o_ref)

    for c in range(0, tf, nchunk):
        cs = slice(c, min(c + nchunk, tf))
        h = jnp.dot(xn_scr[...], w1_ref[:, cs], preferred_element_type=F32)
        h_scr[:, cs] = jnp.square(jnp.maximum(h, 0.0)).astype(BF16)
    for c in range(0, d, nchunk):
        cs = slice(c, min(c + nchunk, d))
        o_ref[:, cs] += jnp.dot(h_scr[...], w2_ref[:, cs], preferred_element_type=F32)

    @pl.when(f == pl.num_programs(1) - 1)
    def _():
        o_ref[...] = x_ref[...] + _rms(o_ref[...], gpost_ref[...])


def _ffn(cfg, x, gpre, w1, w2, gpost, layer):
    t, d = x.shape
    tm, tf = cfg.tm_ffn, cfg.tf
    dff = w1.shape[2]
    return pl.pallas_call(
        functools.partial(_ffn_kernel, nchunk=512),
        grid=(t // tm, dff // tf),
        in_specs=[
            pl.BlockSpec((tm, d), lambda i, f: (i, 0)),
            _resident((1, d)),
            pl.BlockSpec((None, d, tf), lambda i, f: (layer, 0, f)),
            pl.BlockSpec((None, tf, d), lambda i, f: (layer, f, 0)),
            _resident((1, d)),
        ],
        out_specs=pl.BlockSpec((tm, d), lambda i, f: (i, 0)),
        out_shape=jax.ShapeDtypeStruct((t, d), F32),
        scratch_shapes=[pltpu.VMEM((tm, d), BF16), pltpu.VMEM((tm, tf), BF16)],
        compiler_params=_params(("parallel", "arbitrary")),
        name="squared_relu_mlp",
    )(x, gpre, w1, w2, gpost)


def _forward(cfg, x, mem, p):
    row = lambda a: a.reshape(1, -1).astype(F32)
    for l in range(cfg.depth):
        lru_in, qkvb, qkvc = _inproj(cfg, x, row(p["mix_norm_pre"][l]), p["w_in"], l)
        gn = p["group_norm"][l]
        wg = jnp.concatenate([p["lru_wa"][l], p["lru_wx"][l]], axis=-1).astype(BF16)
        ya = _lru(cfg, lru_in, p["conv_w"][l], row(p["conv_b"][l]), wg,
                  p["lru_ba"][l][:, None, :], p["lru_bx"][l][:, None, :], p["lru_lam"][l][:, None, :],
                  row(gn[:LRU_WIDTH]))
        yb = _dilated(cfg, qkvb, row(gn[LRU_WIDTH:LRU_WIDTH + DIL_WIDTH]))
        sink = jnp.zeros((1, 128), F32).at[0, :SWA_HEADS].set(p["swa_sink"][l].astype(F32))
        yc = _swa(cfg, qkvc, row(gn[LRU_WIDTH + DIL_WIDTH:]), sink)
        x = _outproj(cfg, ya, yb, yc, x, p["w_out"], row(p["mix_norm_post"][l]), l)
        kmem, vmem = _memkv(cfg, mem, row(p["mem_kv_norm"][l]), p["w_mkv"], l)
        x = _cross(cfg, x, row(p["mem_norm_pre"][l]), p["w_mq"], kmem, vmem, p["w_mo"],
                   row(p["mem_norm_post"][l]), l)
        x = _ffn(cfg, x, row(p["ffn_norm_pre"][l]), p["w_ff1"], p["w_ff2"], row(p["ffn_norm_post"][l]), l)
    return x


def _run(cfg, x_prompt, x_sample, mem_prompt, mem_sample, mix_norm_pre, mix_norm_post, w_in, conv_w, conv_b,
         lru_wa, lru_ba, lru_wx, lru_bx, lru_lam, swa_sink, group_norm, w_out, mem_norm_pre, mem_norm_post,
         mem_kv_norm, w_mq, w_mk, w_mv, w_mo, ffn_norm_pre, ffn_norm_post, w_ff1, w_ff2):
    d = cfg.d_model
    p = dict(
        mix_norm_pre=mix_norm_pre, mix_norm_post=mix_norm_post, w_in=w_in.astype(BF16), conv_w=conv_w,
        conv_b=conv_b, lru_wa=lru_wa, lru_ba=lru_ba, lru_wx=lru_wx, lru_bx=lru_bx, lru_lam=lru_lam,
        swa_sink=swa_sink, group_norm=group_norm, w_out=w_out.astype(BF16), mem_norm_pre=mem_norm_pre,
        mem_norm_post=mem_norm_post, mem_kv_norm=mem_kv_norm, w_mq=w_mq.astype(BF16),
        w_mkv=jnp.concatenate([w_mk, w_mv], axis=-1).astype(BF16), w_mo=w_mo.astype(BF16),
        ffn_norm_pre=ffn_norm_pre, ffn_norm_post=ffn_norm_post, w_ff1=w_ff1.astype(BF16),
        w_ff2=w_ff2.astype(BF16))
    outs = []
    for group, x, mem in zip(cfg.groups, (x_prompt, x_sample), (mem_prompt, mem_sample)):
        sub = cfg._replace(groups=(group,))
        outs.append(_forward(sub, x.reshape(-1, d), mem.reshape(-1, d), p).reshape(x.shape))
    return tuple(outs)


def kernel(x_prompt, x_sample, mem_prompt, mem_sample, mix_norm_pre, mix_norm_post, w_in, conv_w, conv_b, lru_wa,
           lru_ba, lru_wx, lru_bx, lru_lam, swa_sink, group_norm, w_out, mem_norm_pre, mem_norm_post, mem_kv_norm,
           w_mq, w_mk, w_mv, w_mo, ffn_norm_pre, ffn_norm_post, w_ff1, w_ff2):
    return _run(_CFG, x_prompt, x_sample, mem_prompt, mem_sample, mix_norm_pre, mix_norm_post, w_in, conv_w,
                conv_b, lru_wa, lru_ba, lru_wx, lru_bx, lru_lam, swa_sink, group_norm, w_out, mem_norm_pre,
                mem_norm_post, mem_kv_norm, w_mq, w_mk, w_mv, w_mo, ffn_norm_pre, ffn_norm_post, w_ff1, w_ff2)
```

```python
import functools
from typing import NamedTuple

import numpy as np
import jax
import jax.numpy as jnp
from jax import lax
from jax.experimental import pallas as pl
from jax.experimental.pallas import tpu as pltpu

F32 = jnp.float32
BF16 = jnp.bfloat16

D_MODEL = 2048
BATCH = 8
SEQ = 4096
DEPTH = 4
DEC_BATCH = 1
DEC_SEQ = 16384
HEAD_DIM = 128
LRU_WIDTH = 512
LRU_BLOCKS = 4
LRU_BLOCK_WIDTH = LRU_WIDTH // LRU_BLOCKS
CONV_WIDTH = 4
CONV_LEFT = 2
LRU_C = 8.0
DIL_HEADS = 6
DIL_PATTERNS = ((128, 1), (512, 4), (2048, 16))
SWA_HEADS = 6
SWA_KV_HEADS = 2
SWA_WINDOW = 128
DIL_WIDTH = DIL_HEADS * HEAD_DIM
SWA_WIDTH = SWA_HEADS * HEAD_DIM
SWA_KV_WIDTH = SWA_KV_HEADS * HEAD_DIM
MIX_WIDTH = LRU_WIDTH + DIL_WIDTH + SWA_WIDTH
IN_WIDTH = 2 * LRU_WIDTH + 3 * DIL_WIDTH + SWA_WIDTH + 2 * SWA_KV_WIDTH
N_MEM = 256
MEM_HEADS = 4
MEM_WIDTH = MEM_HEADS * HEAD_DIM
D_FF = 4 * D_MODEL
EPS = 1e-6

_NEG = -1e30
_QK_SCALE = HEAD_DIM ** -0.5
_HALO_ROWS = 8
_NORM_ROWS = 128
_SCAN_GROUP = 8
_DIL_W = DIL_PATTERNS[0][0] // (2 * DIL_PATTERNS[0][1])
assert all(wn // (2 * d) == _DIL_W for wn, d in DIL_PATTERNS)
_PERM_ROWS = 256
_V7X_VMEM_BYTES = 64 * 1024 * 1024
_VMEM_LIMIT = _V7X_VMEM_BYTES - 3 * 1024 * 1024
_NT = (((1,), (1,)), ((), ()))


class _Cfg(NamedTuple):
    d_model: int
    d_ff: int
    depth: int
    groups: tuple
    n_mem: int
    tm: int
    tm_ffn: int
    tf: int
    lru_chunk: int
    lru_rows: int
    swa_rows: int
    dil_rows: int
    swa_batch: int
    dil_batch: int


_CFG = _Cfg(d_model=D_MODEL, d_ff=D_FF, depth=DEPTH, groups=((BATCH, SEQ), (DEC_BATCH, DEC_SEQ)),
            n_mem=N_MEM, tm=512, tm_ffn=1024, tf=1024, lru_chunk=1024, lru_rows=256, swa_rows=512,
            dil_rows=1024, swa_batch=2, dil_batch=4)


def _sequences(cfg):
    out, start = [], 0
    for n, length in cfg.groups:
        for _ in range(n):
            out.append((start, length))
            start += length
    return out, start


def _chunk_flags(cfg, rows):
    seqs, total = _sequences(cfg)
    starts = {s for s, _ in seqs}
    ends = {s + l for s, l in seqs}
    for s, l in seqs:
        assert l % rows == 0, (l, rows)
    n = total // rows
    flags = np.zeros((n,), np.int32)
    for c in range(n):
        flags[c] = (1 if c * rows in starts else 0) | (2 if (c + 1) * rows in ends else 0)
    return jnp.asarray(flags)


def _params(semantics):
    return pltpu.CompilerParams(dimension_semantics=semantics, vmem_limit_bytes=_VMEM_LIMIT)


def _rms(x, g):
    ms = jnp.mean(x * x, axis=-1, keepdims=True)
    return x * lax.rsqrt(ms + EPS) * g


def _row_blocks(n):
    return [slice(r, min(r + _NORM_ROWS, n)) for r in range(0, n, _NORM_ROWS)]


def _resident(shape, layer=None):
    if layer is None:
        return pl.BlockSpec(shape, lambda *_: (0,) * len(shape), pipeline_mode=pl.Buffered(1))
    return pl.BlockSpec((None,) + tuple(shape), lambda *_: (layer,) + (0,) * len(shape),
                        pipeline_mode=pl.Buffered(1))


def _inproj_plan():
    lru_w = 2 * LRU_WIDTH
    dil_w = 3 * DIL_WIDTH
    segs = [
        (0, lru_w, 0, None),
        (lru_w, lru_w + DIL_WIDTH, 1, _QK_SCALE),
        (lru_w + DIL_WIDTH, lru_w + dil_w, 1, None),
        (lru_w + dil_w, lru_w + dil_w + SWA_WIDTH, 2, _QK_SCALE),
        (lru_w + dil_w + SWA_WIDTH, IN_WIDTH, 2, None),
    ]
    base = {0: 0, 1: lru_w, 2: lru_w + dil_w}
    plan = []
    for c0, c1, oi, scale in segs:
        c = c0
        while c < c1:
            n = min(512, c1 - c)
            plan.append((c, c + n, oi, c - base[oi], scale))
            c += n
    return tuple(plan)


def _inproj_kernel(x_ref, g_ref, w_ref, lru_ref, qkvb_ref, qkvc_ref, xn_scr, *, plan):
    for rows in _row_blocks(x_ref.shape[0]):
        xn_scr[rows, :] = _rms(x_ref[rows, :], g_ref[...]).astype(BF16)
    outs = (lru_ref, qkvb_ref, qkvc_ref)
    for c0, c1, oi, o0, scale in plan:
        acc = jnp.dot(xn_scr[...], w_ref[:, c0:c1], preferred_element_type=F32)
        if scale is not None:
            acc = acc * scale
        outs[oi][:, o0:o0 + (c1 - c0)] = acc.astype(outs[oi].dtype)


def _inproj(cfg, x, g, w, layer):
    t, d = x.shape
    tm = cfg.tm
    widths = (2 * LRU_WIDTH, 3 * DIL_WIDTH, SWA_WIDTH + 2 * SWA_KV_WIDTH)
    return pl.pallas_call(
        functools.partial(_inproj_kernel, plan=_inproj_plan()),
        grid=(t // tm,),
        in_specs=[
            pl.BlockSpec((tm, d), lambda i: (i, 0)),
            _resident((1, d)),
            _resident((d, IN_WIDTH), layer),
        ],
        out_specs=[pl.BlockSpec((tm, wd), lambda i: (i, 0)) for wd in widths],
        out_shape=[
            jax.ShapeDtypeStruct((t, widths[0]), F32),
            jax.ShapeDtypeStruct((t, widths[1]), BF16),
            jax.ShapeDtypeStruct((t, widths[2]), BF16),
        ],
        scratch_shapes=[pltpu.VMEM((tm, d), BF16)],
        compiler_params=_params(("parallel",)),
        name="mixer_inproj",
    )(x, g, w)


def _lru_fill_halo(first, last, xa_ref, xp_ref, xn_ref, xext, lc):
    xext[_HALO_ROWS:_HALO_ROWS + lc, :] = xa_ref[...]

    @pl.when(first)
    def _():
        xext[0:_HALO_ROWS, :] = jnp.zeros((_HALO_ROWS, LRU_WIDTH), F32)

    @pl.when(jnp.logical_not(first))
    def _():
        xext[0:_HALO_ROWS, :] = xp_ref[...]

    @pl.when(last)
    def _():
        xext[_HALO_ROWS + lc:, :] = jnp.zeros((_HALO_ROWS, LRU_WIDTH), F32)

    @pl.when(jnp.logical_not(last))
    def _():
        xext[_HALO_ROWS + lc:, :] = xn_ref[...]


def _lru_gates(xext, cw_ref, cb_ref, wg_ref, ba_ref, bx_ref, lam_ref, a_scr, u_scr, lc, rb):
    lam = lam_ref[...]
    neg = -lam
    softplus = jnp.maximum(neg, 0.0) + jnp.log1p(jnp.exp(-jnp.abs(neg)))
    cb = cb_ref[...]
    taps = [cw_ref[j:j + 1, :] for j in range(CONV_WIDTH)]
    for blk in range(lc // rb):
        r0 = blk * rb
        xc = cb + sum(taps[j] * xext[r0 + _HALO_ROWS - CONV_LEFT + j:r0 + _HALO_ROWS - CONV_LEFT + j + rb, :]
                      for j in range(CONV_WIDTH))
        xcb = xc.astype(BF16)
        for n in range(LRU_BLOCKS):
            cs = slice(n * LRU_BLOCK_WIDTH, (n + 1) * LRU_BLOCK_WIDTH)
            g = jnp.dot(xcb[:, cs], wg_ref[n], preferred_element_type=F32)
            r = jax.nn.sigmoid(g[:, :LRU_BLOCK_WIDTH] + ba_ref[:, cs])
            ig = jax.nn.sigmoid(g[:, LRU_BLOCK_WIDTH:] + bx_ref[:, cs])
            log_a = (-LRU_C * r) * softplus[:, cs]
            a = jnp.exp(log_a)
            one_minus_a2 = -jnp.tanh(log_a) * (1.0 + a * a)
            u = jnp.sqrt(one_minus_a2) * (ig * xc[:, cs])
            groups = slice(r0 // _SCAN_GROUP, (r0 + rb) // _SCAN_GROUP)
            a_scr[groups, :, cs] = a.reshape(rb // _SCAN_GROUP, _SCAN_GROUP, LRU_BLOCK_WIDTH)
            u_scr[groups, :, cs] = u.reshape(rb // _SCAN_GROUP, _SCAN_GROUP, LRU_BLOCK_WIDTH)


def _lru_scan(reset, a_scr, u_scr, h_dst, carry, lc, reverse):
    @pl.when(reset)
    def _():
        carry[...] = jnp.zeros((1, LRU_WIDTH), F32)

    def group(i, h):
        g = lc // _SCAN_GROUP - 1 - i if reverse else i
        order = [_SCAN_GROUP - 1 - j if reverse else j for j in range(_SCAN_GROUP)]
        p = a_scr[g, order[0]:order[0] + 1, :]
        q = u_scr[g, order[0]:order[0] + 1, :]
        outs = [p * h + q]
        for j in order[1:]:
            a = a_scr[g, j:j + 1, :]
            q = a * q + u_scr[g, j:j + 1, :]
            p = a * p
            outs.append(p * h + q)
        for j, o in zip(order, outs):
            h_dst[g, j:j + 1, :] = o
        return outs[-1]

    carry[...] = lax.fori_loop(0, lc // _SCAN_GROUP, group, carry[...], unroll=2)


def _lru_fwd_kernel(flags_ref, xa_ref, xp_ref, xn_ref, cw_ref, cb_ref, wg_ref, ba_ref, bx_ref, lam_ref,
                    hf_ref, xext, a_scr, u_scr, carry, *, lc, rb):
    fl = flags_ref[pl.program_id(0)]
    first = (fl & 1) != 0
    last = (fl & 2) != 0
    _lru_fill_halo(first, last, xa_ref, xp_ref, xn_ref, xext, lc)
    _lru_gates(xext, cw_ref, cb_ref, wg_ref, ba_ref, bx_ref, lam_ref, a_scr, u_scr, lc, rb)
    _lru_scan(first, a_scr, u_scr, hf_ref, carry, lc, reverse=False)


def _lru_bwd_kernel(flags_ref, xa_ref, xp_ref, xn_ref, gate_ref, hf_ref, cw_ref, cb_ref, wg_ref, ba_ref,
                    bx_ref, lam_ref, gn_ref, y_ref, xext, a_scr, u_scr, h_scr, carry, *, lc, rb, nchunks):
    fl = flags_ref[nchunks - 1 - pl.program_id(0)]
    first = (fl & 1) != 0
    last = (fl & 2) != 0
    _lru_fill_halo(first, last, xa_ref, xp_ref, xn_ref, xext, lc)
    _lru_gates(xext, cw_ref, cb_ref, wg_ref, ba_ref, bx_ref, lam_ref, a_scr, u_scr, lc, rb)
    _lru_scan(last, a_scr, u_scr, h_scr, carry, lc, reverse=True)
    for blk in range(lc // rb):
        rows = slice(blk * rb, (blk + 1) * rb)
        groups = slice(blk * rb // _SCAN_GROUP, (blk + 1) * rb // _SCAN_GROUP)
        h = (hf_ref[groups] + h_scr[groups]).reshape(rb, LRU_WIDTH)
        y = h * jax.nn.gelu(gate_ref[rows, :])
        y_ref[rows, :] = _rms(y, gn_ref[...]).astype(BF16)


def _lru(cfg, lru_in, cw, cb, wg, ba, bx, lam, gn):
    t = lru_in.shape[0]
    lc, rb = cfg.lru_chunk, cfg.lru_rows
    nchunks = t // lc
    hb = lc // _HALO_ROWS
    nhalo = t // _HALO_ROWS
    flags = _chunk_flags(cfg, lc)
    row = lambda: _resident((1, LRU_WIDTH))

    def specs(chunk_of):
        return [
            pl.BlockSpec((lc, LRU_WIDTH), lambda i, f: (chunk_of(i), 0)),
            pl.BlockSpec((_HALO_ROWS, LRU_WIDTH), lambda i, f: (jnp.maximum(chunk_of(i) * hb - 1, 0), 0)),
            pl.BlockSpec((_HALO_ROWS, LRU_WIDTH), lambda i, f: (jnp.minimum((chunk_of(i) + 1) * hb, nhalo - 1), 0)),
        ]

    def weights(d):
        return [
            _resident((CONV_WIDTH, LRU_WIDTH)), row(),
            pl.BlockSpec((None, LRU_BLOCKS, LRU_BLOCK_WIDTH, 2 * LRU_BLOCK_WIDTH), lambda i, f: (d, 0, 0, 0)),
            pl.BlockSpec((None, 1, LRU_WIDTH), lambda i, f: (d, 0, 0)),
            pl.BlockSpec((None, 1, LRU_WIDTH), lambda i, f: (d, 0, 0)),
            pl.BlockSpec((None, 1, LRU_WIDTH), lambda i, f: (d, 0, 0)),
        ]

    grouped = (lc // _SCAN_GROUP, _SCAN_GROUP, LRU_WIDTH)
    scratch = [pltpu.VMEM((lc + 2 * _HALO_ROWS, LRU_WIDTH), F32), pltpu.VMEM(grouped, F32),
               pltpu.VMEM(grouped, F32)]
    carry = [pltpu.VMEM((1, LRU_WIDTH), F32)]

    fwd = lambda i: i
    hf = pl.pallas_call(
        functools.partial(_lru_fwd_kernel, lc=lc, rb=rb),
        grid_spec=pltpu.PrefetchScalarGridSpec(
            num_scalar_prefetch=1, grid=(nchunks,),
            in_specs=specs(fwd) + weights(0),
            out_specs=pl.BlockSpec(grouped, lambda i, f: (i, 0, 0)),
            scratch_shapes=scratch + carry),
        out_shape=jax.ShapeDtypeStruct((t // _SCAN_GROUP, _SCAN_GROUP, LRU_WIDTH), F32),
        compiler_params=_params(("arbitrary",)),
        name="lru_forward",
    )(flags, lru_in, lru_in, lru_in, cw, cb, wg, ba, bx, lam)

    bwd = lambda i: nchunks - 1 - i
    return pl.pallas_call(
        functools.partial(_lru_bwd_kernel, lc=lc, rb=rb, nchunks=nchunks),
        grid_spec=pltpu.PrefetchScalarGridSpec(
            num_scalar_prefetch=1, grid=(nchunks,),
            in_specs=specs(bwd) + [
                pl.BlockSpec((lc, LRU_WIDTH), lambda i, f: (bwd(i), 1)),
                pl.BlockSpec(grouped, lambda i, f: (bwd(i), 0, 0)),
            ] + weights(1) + [row()],
            out_specs=pl.BlockSpec((lc, LRU_WIDTH), lambda i, f: (bwd(i), 0)),
            scratch_shapes=scratch + [pltpu.VMEM(grouped, F32)] + carry),
        out_shape=jax.ShapeDtypeStruct((t, LRU_WIDTH), BF16),
        compiler_params=_params(("arbitrary",)),
        name="lru_backward",
    )(flags, lru_in, lru_in, lru_in, lru_in, hf, cw, cb, wg, ba, bx, lam, gn)


def _alibi_slopes(n):
    return [2.0 ** (-8.0 * (i + 1) / n) for i in range(n)]


def _band_bias(w, dist_scale, hq, hkv):
    rep = hq // hkv
    slopes = _alibi_slopes(hq)
    qi = np.arange(w)[:, None]
    kj = np.arange(3 * w)[None, :]
    rel = np.abs(kj - w - qi)
    out = np.empty((hkv, rep * w, 3 * w), np.float32)
    for g in range(hkv):
        for r in range(rep):
            out[g, r * w:(r + 1) * w] = np.where(rel <= w, -slopes[g * rep + r] * dist_scale * rel, _NEG)
    return out


def _edge_penalties(fl, w):
    col = lax.broadcasted_iota(jnp.int32, (1, 3 * w), 1)
    pen_first = jnp.where(col < w, jnp.where((fl & 1) != 0, _NEG, 0.0), 0.0)
    pen_last = jnp.where(col >= 2 * w, jnp.where((fl & 2) != 0, _NEG, 0.0), 0.0)
    return pen_first, pen_last


def _attend(qs, ks, vs, biases):
    scores = [lax.dot_general(q, k, _NT, preferred_element_type=F32) for q, k in zip(qs, ks)]
    ms, ps = [], []
    for s, b in zip(scores, biases):
        if b is not None:
            s = s + b
        m = jnp.max(s, axis=-1, keepdims=True)
        ms.append(m)
        ps.append(jnp.exp(s - m).astype(BF16))
    ls, accs = [], []
    for p, v in zip(ps, vs):
        v1 = jnp.concatenate([v, jnp.ones(v.shape, v.dtype)], axis=1)
        out = jnp.dot(p, v1, preferred_element_type=F32)
        accs.append(out[:, :HEAD_DIM])
        ls.append(out[:, HEAD_DIM:])
    return ms, ls, accs


def _swa_kernel(flags_ref, q_ref, kc_ref, kp_ref, kn_ref, vc_ref, vp_ref, vn_ref, bias_ref, sink_ref, gn_ref,
                y_out, kbuf, vbuf, *, w, nsub, nb):
    rep = SWA_HEADS // SWA_KV_HEADS
    rows_total = nsub * w
    pen_first, pen_last = _edge_penalties(flags_ref[pl.program_id(0)], w)
    kbuf[0:w, :] = kp_ref[...]
    kbuf[w:w + rows_total, :] = kc_ref[...]
    kbuf[w + rows_total:, :] = kn_ref[...]
    vbuf[0:w, :] = vp_ref[...]
    vbuf[w:w + rows_total, :] = vc_ref[...]
    vbuf[w + rows_total:, :] = vn_ref[...]

    for j0 in range(0, nsub, nb):
        blocks = list(range(j0, min(j0 + nb, nsub)))
        qs, ks, vs, bs = [], [], [], []
        for j in blocks:
            rows = slice(j * w, (j + 1) * w)
            for g in range(SWA_KV_HEADS):
                gs = slice(g * HEAD_DIM, (g + 1) * HEAD_DIM)
                qs.append(jnp.concatenate(
                    [q_ref[rows, (g * rep + r) * HEAD_DIM:(g * rep + r + 1) * HEAD_DIM] for r in range(rep)], axis=0))
                ks.append(kbuf[j * w:(j + 3) * w, gs])
                vs.append(vbuf[j * w:(j + 3) * w, gs])
                b = bias_ref[g]
                if j == 0:
                    b = b + pen_first
                if j == nsub - 1:
                    b = b + pen_last
                bs.append(b)
        ms, ls, accs = _attend(qs, ks, vs, bs)
        for bi, j in enumerate(blocks):
            heads_out = []
            for g in range(SWA_KV_HEADS):
                idx = bi * SWA_KV_HEADS + g
                for r in range(rep):
                    h = g * rep + r
                    part = slice(r * w, (r + 1) * w)
                    m, l, acc = ms[idx][part], ls[idx][part], accs[idx][part]
                    factor = jax.nn.sigmoid(m + jnp.log(l) - sink_ref[:, h:h + 1])
                    heads_out.append((acc / l) * factor)
            y = jnp.concatenate(heads_out, axis=1)
            y_out[j * w:(j + 1) * w, :] = _rms(y, gn_ref[...]).astype(BF16)


def _swa(cfg, qkvc, gn, sink):
    t, c = qkvc.shape
    w = SWA_WINDOW
    rows = cfg.swa_rows
    nsub = rows // w
    nchunks = t // rows
    nblk = t // w
    qw, kvw = SWA_WIDTH, SWA_KV_WIDTH
    rep = SWA_HEADS // SWA_KV_HEADS
    assert qw % kvw == 0 and c == qw + 2 * kvw
    kcol, vcol = qw // kvw, qw // kvw + 1
    prev = lambda col: (lambda i, f: (jnp.maximum(i * nsub - 1, 0), col))
    nxt = lambda col: (lambda i, f: (jnp.minimum((i + 1) * nsub, nblk - 1), col))
    return pl.pallas_call(
        functools.partial(_swa_kernel, w=w, nsub=nsub, nb=cfg.swa_batch),
        grid_spec=pltpu.PrefetchScalarGridSpec(
            num_scalar_prefetch=1, grid=(nchunks,),
            in_specs=[
                pl.BlockSpec((rows, qw), lambda i, f: (i, 0)),
                pl.BlockSpec((rows, kvw), lambda i, f: (i, kcol)),
                pl.BlockSpec((w, kvw), prev(kcol)),
                pl.BlockSpec((w, kvw), nxt(kcol)),
                pl.BlockSpec((rows, kvw), lambda i, f: (i, vcol)),
                pl.BlockSpec((w, kvw), prev(vcol)),
                pl.BlockSpec((w, kvw), nxt(vcol)),
                _resident((SWA_KV_HEADS, rep * w, 3 * w)),
                _resident((1, 128)),
                _resident((1, qw)),
            ],
            out_specs=pl.BlockSpec((rows, qw), lambda i, f: (i, 0)),
            scratch_shapes=[pltpu.VMEM((rows + 2 * w, kvw), BF16), pltpu.VMEM((rows + 2 * w, kvw), BF16)]),
        out_shape=jax.ShapeDtypeStruct((t, qw), BF16),
        compiler_params=_params(("parallel",)),
        name="windowed_gqa_sink",
    )(_chunk_flags(cfg, rows), qkvc, qkvc, qkvc, qkvc, qkvc, qkvc, qkvc,
      jnp.asarray(_band_bias(w, 1, SWA_HEADS, SWA_KV_HEADS)), sink, gn)


def _perm_matrix(d):
    n = _PERM_ROWS
    per = n // d
    p = np.zeros((n, n), np.float32)
    for r in range(d):
        for m in range(per):
            p[r * per + m, d * m + r] = 1.0
    return p


def _dilated_kernel(flags_ref, q_ref, kp_ref, kc_ref, kn_ref, vp_ref, vc_ref, vn_ref, bias_ref, perm_ref, gn_ref,
                    y_ref, qd, kd, vd, biasv, acc_nat, m_nat, l_nat, *, c, nb):
    w = _DIL_W
    nh = DIL_HEADS
    nsb = c // w
    pen_first, pen_last = _edge_penalties(flags_ref[pl.program_id(0)], w)
    for p in range(len(DIL_PATTERNS)):
        for h in range(nh):
            b = bias_ref[p, h]
            biasv[p, 0, h] = b
            biasv[p, 1, h] = b + pen_first
            biasv[p, 2, h] = b + pen_last
            biasv[p, 3, h] = b + pen_first + pen_last

    heads = [slice(h * HEAD_DIM, (h + 1) * HEAD_DIM) for h in range(nh)]

    def run_pattern(p, d, qsrc, ksrc, vsrc, qstride, kstride, first_pattern, last_pattern):
        nblk = c // (d * w)
        assert d == 1 or not last_pattern

        def body(it, carry):
            qs, ks, vs, bs, where = [], [], [], [], []
            for b in range(nb):
                sb = it * nb + b
                r = sb // nblk
                s = sb % nblk
                var = jnp.where(s == 0, 1, 0) + jnp.where(s == nblk - 1, 2, 0)
                q0 = pl.multiple_of(r * qstride + s * w, w)
                k0 = pl.multiple_of(r * kstride + s * w, w)
                where.append(d * w * s + r)
                for h in range(nh):
                    qs.append(qsrc[pl.ds(q0, w), heads[h]])
                    ks.append(ksrc[pl.ds(k0, 3 * w), heads[h]])
                    vs.append(vsrc[pl.ds(k0, 3 * w), heads[h]])
                    bs.append(biasv[p, var, h])
            ms, ls, accs = _attend(qs, ks, vs, bs)
            for b in range(nb):
                nat = pl.ds(where[b], w, stride=d) if d > 1 else pl.ds(pl.multiple_of(where[b], w), w)
                outs = []
                for h in range(nh):
                    m, l, acc = ms[b * nh + h], ls[b * nh + h], accs[b * nh + h]
                    if not first_pattern:
                        m_p = m_nat[h, nat, :]
                        m_n = jnp.maximum(m_p, m)
                        alpha = jnp.exp(m_p - m_n)
                        beta = jnp.exp(m - m_n)
                        l = alpha * l_nat[h, nat, :] + beta * l
                        acc = alpha * acc_nat[h, nat, :] + beta * acc
                        m = m_n
                    if last_pattern:
                        outs.append(acc / l)
                    else:
                        acc_nat[h, nat, :] = acc
                        m_nat[h, nat, :] = jnp.broadcast_to(m, (w, HEAD_DIM))
                        l_nat[h, nat, :] = jnp.broadcast_to(l, (w, HEAD_DIM))
                if last_pattern:
                    y_ref[nat, :] = _rms(jnp.concatenate(outs, axis=1), gn_ref[...]).astype(BF16)
            return carry

        lax.fori_loop(0, nsb // nb, body, 0)

    def deinterleave(p, d):
        per = _PERM_ROWS // d
        halo_groups = (d * w) // _PERM_ROWS
        chunk_groups = c // _PERM_ROWS
        kstride = (chunk_groups + 2 * halo_groups) * per
        perm = perm_ref[p - 1]

        def move(src, row0, dst, g, stride):
            res = jnp.dot(perm, src[row0:row0 + _PERM_ROWS, :], preferred_element_type=F32).astype(BF16)
            for r in range(d):
                dst[r * stride + g * per:r * stride + (g + 1) * per, :] = res[r * per:(r + 1) * per]

        for g in range(chunk_groups):
            move(q_ref, g * _PERM_ROWS, qd, g, c // d)
        for prev_ref, cur_ref, next_ref, dst in ((kp_ref, kc_ref, kn_ref, kd), (vp_ref, vc_ref, vn_ref, vd)):
            srcs = ([(prev_ref, c - (halo_groups - g) * _PERM_ROWS) for g in range(halo_groups)]
                    + [(cur_ref, g * _PERM_ROWS) for g in range(chunk_groups)]
                    + [(next_ref, g * _PERM_ROWS) for g in range(halo_groups)])
            for g, (src, row0) in enumerate(srcs):
                move(src, row0, dst, g, kstride)
        return c // d, kstride

    order = sorted(range(len(DIL_PATTERNS)), key=lambda p: -DIL_PATTERNS[p][1])
    for idx, p in enumerate(order):
        d = DIL_PATTERNS[p][1]
        first, last = idx == 0, idx == len(order) - 1
        if d == 1:
            for prev_ref, cur_ref, next_ref, dst in ((kp_ref, kc_ref, kn_ref, kd), (vp_ref, vc_ref, vn_ref, vd)):
                dst[0:w, :] = prev_ref[c - w:c, :]
                dst[w:w + c, :] = cur_ref[...]
                dst[w + c:2 * w + c, :] = next_ref[0:w, :]
            run_pattern(p, d, q_ref, kd, vd, 0, 0, first, last)
        else:
            qstride, kstride = deinterleave(p, d)
            run_pattern(p, d, qd, kd, vd, qstride, kstride, first, last)


def _dilated(cfg, qkvb, gn):
    t = qkvb.shape[0]
    c = cfg.dil_rows
    w = _DIL_W
    n = t // c
    dmax = max(d for _, d in DIL_PATTERNS)
    assert DIL_PATTERNS[0][1] == 1 and c % (dmax * w) == 0 and c >= dmax * w and c % _PERM_ROWS == 0
    assert all((d * w) % _PERM_ROWS == 0 for _, d in DIL_PATTERNS[1:])
    assert (c // w) % cfg.dil_batch == 0
    bias = np.stack([_band_bias(w, d, DIL_HEADS, DIL_HEADS) for _, d in DIL_PATTERNS])
    perm = np.stack([_perm_matrix(d) for _, d in DIL_PATTERNS[1:]])
    blk = (c, DIL_WIDTH)
    prev = lambda col: (lambda i, f: (jnp.maximum(i - 1, 0), col))
    cur = lambda col: (lambda i, f: (i, col))
    nxt = lambda col: (lambda i, f: (jnp.minimum(i + 1, n - 1), col))
    return pl.pallas_call(
        functools.partial(_dilated_kernel, c=c, nb=cfg.dil_batch),
        grid_spec=pltpu.PrefetchScalarGridSpec(
            num_scalar_prefetch=1, grid=(n,),
            in_specs=[
                pl.BlockSpec(blk, cur(0)),
                pl.BlockSpec(blk, prev(1)), pl.BlockSpec(blk, cur(1)), pl.BlockSpec(blk, nxt(1)),
                pl.BlockSpec(blk, prev(2)), pl.BlockSpec(blk, cur(2)), pl.BlockSpec(blk, nxt(2)),
                _resident(bias.shape), _resident(perm.shape), _resident((1, DIL_WIDTH)),
            ],
            out_specs=pl.BlockSpec(blk, cur(0)),
            scratch_shapes=[
                pltpu.VMEM((c, DIL_WIDTH), BF16),
                pltpu.VMEM((3 * c, DIL_WIDTH), BF16), pltpu.VMEM((3 * c, DIL_WIDTH), BF16),
                pltpu.VMEM((len(DIL_PATTERNS), 4, DIL_HEADS, w, 3 * w), F32),
                pltpu.VMEM((DIL_HEADS, c, HEAD_DIM), F32),
                pltpu.VMEM((DIL_HEADS, c, HEAD_DIM), F32),
                pltpu.VMEM((DIL_HEADS, c, HEAD_DIM), F32),
            ]),
        out_shape=jax.ShapeDtypeStruct((t, DIL_WIDTH), BF16),
        compiler_params=_params(("parallel",)),
        name="dilated_attention",
    )(_chunk_flags(cfg, c), qkvb, qkvb, qkvb, qkvb, qkvb, qkvb, qkvb,
      jnp.asarray(bias), jnp.asarray(perm, dtype=BF16), gn)


def _outproj_kernel(ya_ref, yb_ref, yc_ref, x_ref, w_ref, g_ref, o_ref, y_scr, *, nchunk):
    d = o_ref.shape[1]
    b0, b1 = LRU_WIDTH, LRU_WIDTH + DIL_WIDTH
    for c in range(0, d, nchunk):
        cs = slice(c, c + nchunk)
        acc = jnp.dot(ya_ref[...], w_ref[0:b0, cs], preferred_element_type=F32)
        acc += jnp.dot(yb_ref[...], w_ref[b0:b1, cs], preferred_element_type=F32)
        acc += jnp.dot(yc_ref[...], w_ref[b1:, cs], preferred_element_type=F32)
        y_scr[:, cs] = acc
    for rows in _row_blocks(o_ref.shape[0]):
        o_ref[rows, :] = x_ref[rows, :] + _rms(y_scr[rows, :], g_ref[...])


def _outproj(cfg, ya, yb, yc, x, w, g, layer):
    t, d = x.shape
    tm = cfg.tm
    return pl.pallas_call(
        functools.partial(_outproj_kernel, nchunk=min(512, d)),
        grid=(t // tm,),
        in_specs=[
            pl.BlockSpec((tm, LRU_WIDTH), lambda i: (i, 0)),
            pl.BlockSpec((tm, DIL_WIDTH), lambda i: (i, 0)),
            pl.BlockSpec((tm, SWA_WIDTH), lambda i: (i, 0)),
            pl.BlockSpec((tm, d), lambda i: (i, 0)),
            _resident((MIX_WIDTH, d), layer),
            _resident((1, d)),
        ],
        out_specs=pl.BlockSpec((tm, d), lambda i: (i, 0)),
        out_shape=jax.ShapeDtypeStruct((t, d), F32),
        scratch_shapes=[pltpu.VMEM((tm, d), F32)],
        compiler_params=_params(("parallel",)),
        name="mixer_outproj",
    )(ya, yb, yc, x, w, g)


def _memkv_kernel(mem_ref, g_ref, w_ref, k_ref, v_ref):
    mn = _rms(mem_ref[...], g_ref[...]).astype(BF16)
    kv = jnp.dot(mn, w_ref[...], preferred_element_type=F32)
    k_ref[...] = kv[:, :MEM_WIDTH].astype(BF16)
    v_ref[...] = kv[:, MEM_WIDTH:].astype(BF16)


def _memkv(cfg, mem, g, wkv, layer):
    rows, d = mem.shape
    nm = cfg.n_mem
    shape = jax.ShapeDtypeStruct((rows, MEM_WIDTH), BF16)
    return pl.pallas_call(
        _memkv_kernel,
        grid=(rows // nm,),
        in_specs=[pl.BlockSpec((nm, d), lambda i: (i, 0)), _resident((1, d)),
                  _resident((d, 2 * MEM_WIDTH), layer)],
        out_specs=[pl.BlockSpec((nm, MEM_WIDTH), lambda i: (i, 0))] * 2,
        out_shape=[shape, shape],
        compiler_params=_params(("parallel",)),
        name="memory_kv",
    )(mem, g, wkv)


def _cross_kernel(seq_ref, x_ref, gpre_ref, wq_ref, k_ref, v_ref, wo_ref, gpost_ref, o_ref, xn_scr, o_scr, *,
                  nchunk):
    del seq_ref
    tm, d = x_ref.shape
    for rows in _row_blocks(tm):
        xn_scr[rows, :] = _rms(x_ref[rows, :], gpre_ref[...]).astype(BF16)
    q = (jnp.dot(xn_scr[...], wq_ref[...], preferred_element_type=F32) * _QK_SCALE).astype(BF16)
    heads = [slice(h * HEAD_DIM, (h + 1) * HEAD_DIM) for h in range(MEM_HEADS)]
    _, ls, accs = _attend([q[:, hs] for hs in heads], [k_ref[:, hs] for hs in heads],
                          [v_ref[:, hs] for hs in heads], [None] * MEM_HEADS)
    for hs, l, acc in zip(heads, ls, accs):
        o_scr[:, hs] = (acc / l).astype(BF16)
    for c in range(0, d, nchunk):
        cs = slice(c, min(c + nchunk, d))
        o_ref[:, cs] = jnp.dot(o_scr[...], wo_ref[:, cs], preferred_element_type=F32)
    for rows in _row_blocks(tm):
        o_ref[rows, :] = x_ref[rows, :] + _rms(o_ref[rows, :], gpost_ref[...])


def _cross(cfg, x, gpre, wq, kmem, vmem, wo, gpost, layer):
    t, d = x.shape
    tm, nm = cfg.tm, cfg.n_mem
    seqs, _ = _sequences(cfg)
    seq_of_tile = []
    for si, (_, length) in enumerate(seqs):
        assert length % tm == 0
        seq_of_tile += [si] * (length // tm)
    seq_of_tile = jnp.asarray(np.asarray(seq_of_tile, np.int32))
    return pl.pallas_call(
        functools.partial(_cross_kernel, nchunk=512),
        grid_spec=pltpu.PrefetchScalarGridSpec(
            num_scalar_prefetch=1, grid=(t // tm,),
            in_specs=[
                pl.BlockSpec((tm, d), lambda i, s: (i, 0)),
                _resident((1, d)),
                _resident((d, MEM_WIDTH), layer),
                pl.BlockSpec((nm, MEM_WIDTH), lambda i, s: (s[i], 0)),
                pl.BlockSpec((nm, MEM_WIDTH), lambda i, s: (s[i], 0)),
                _resident((MEM_WIDTH, d), layer),
                _resident((1, d)),
            ],
            out_specs=pl.BlockSpec((tm, d), lambda i, s: (i, 0)),
            scratch_shapes=[pltpu.VMEM((tm, d), BF16), pltpu.VMEM((tm, MEM_WIDTH), BF16)]),
        out_shape=jax.ShapeDtypeStruct((t, d), F32),
        compiler_params=_params(("parallel",)),
        name="memory_cross_attention",
    )(seq_of_tile, x, gpre, wq, kmem, vmem, wo, gpost)


def _ffn_kernel(x_ref, gpre_ref, w1_ref, w2_ref, gpost_ref, o_ref, xn_scr, h_scr, *, nchunk):
    f = pl.program_id(1)
    tf = w1_ref.shape[1]
    d = w2_ref.shape[1]

    tm = x_ref.shape[0]
    row_blocks = [slice(r, min(r + _NORM_ROWS, tm)) for r in range(0, tm, _NORM_ROWS)]

    @pl.when(f == 0)
    def _():
        for rows in row_blocks:
            xn_scr[rows, :] = _rms(x_ref[rows, :], gpre_ref[...]).astype(BF16)
            o_ref[rows, :] = jnp.zeros((rows.stop - rows.start, d), F32)

    for c in range(0, tf, nchunk):
        cs = slice(c, min(c + nchunk, tf))
        h = jnp.dot(xn_scr[...], w1_ref[:, cs], preferred_element_type=F32)
        h_scr[:, cs] = jnp.square(jnp.maximum(h, 0.0)).astype(BF16)
    for c in range(0, d, nchunk):
        cs = slice(c, min(c + nchunk, d))
        o_ref[:, cs] += jnp.dot(h_scr[...], w2_ref[:, cs], preferred_element_type=F32)

    @pl.when(f == pl.num_programs(1) - 1)
    def _():
        for rows in row_blocks:
            o_ref[rows, :] = x_ref[rows, :] + _rms(o_ref[rows, :], gpost_ref[...])


def _ffn(cfg, x, gpre, w1, w2, gpost, layer):
    t, d = x.shape
    tm, tf = cfg.tm_ffn, cfg.tf
    dff = w1.shape[2]
    return pl.pallas_call(
        functools.partial(_ffn_kernel, nchunk=512),
        grid=(t // tm, dff // tf),
        in_specs=[
            pl.BlockSpec((tm, d), lambda i, f: (i, 0)),
            _resident((1, d)),
            pl.BlockSpec((None, d, tf), lambda i, f: (layer, 0, f)),
            pl.BlockSpec((None, tf, d), lambda i, f: (layer, f, 0)),
            _resident((1, d)),
        ],
        out_specs=pl.BlockSpec((tm, d), lambda i, f: (i, 0)),
        out_shape=jax.ShapeDtypeStruct((t, d), F32),
        scratch_shapes=[pltpu.VMEM((tm, d), BF16), pltpu.VMEM((tm, tf), BF16)],
        compiler_params=_params(("parallel", "arbitrary")),
        name="squared_relu_mlp",
    )(x, gpre, w1, w2, gpost)


def _forward(cfg, x, mem, p):
    row = lambda a: a.reshape(1, -1).astype(F32)
    for l in range(cfg.depth):
        lru_in, qkvb, qkvc = _inproj(cfg, x, row(p["mix_norm_pre"][l]), p["w_in"], l)
        gn = p["group_norm"][l]
        wg = jnp.concatenate([p["lru_wa"][l], p["lru_wx"][l]], axis=-1).astype(BF16)
        ya = _lru(cfg, lru_in, p["conv_w"][l], row(p["conv_b"][l]), wg,
                  p["lru_ba"][l][:, None, :], p["lru_bx"][l][:, None, :], p["lru_lam"][l][:, None, :],
                  row(gn[:LRU_WIDTH]))
        yb = _dilated(cfg, qkvb, row(gn[LRU_WIDTH:LRU_WIDTH + DIL_WIDTH]))
        sink = jnp.zeros((1, 128), F32).at[0, :SWA_HEADS].set(p["swa_sink"][l].astype(F32))
        yc = _swa(cfg, qkvc, row(gn[LRU_WIDTH + DIL_WIDTH:]), sink)
        x = _outproj(cfg, ya, yb, yc, x, p["w_out"], row(p["mix_norm_post"][l]), l)
        kmem, vmem = _memkv(cfg, mem, row(p["mem_kv_norm"][l]), p["w_mkv"], l)
        x = _cross(cfg, x, row(p["mem_norm_pre"][l]), p["w_mq"], kmem, vmem, p["w_mo"],
                   row(p["mem_norm_post"][l]), l)
        x = _ffn(cfg, x, row(p["ffn_norm_pre"][l]), p["w_ff1"], p["w_ff2"], row(p["ffn_norm_post"][l]), l)
    return x


def _run(cfg, x_prompt, x_sample, mem_prompt, mem_sample, mix_norm_pre, mix_norm_post, w_in, conv_w, conv_b,
         lru_wa, lru_ba, lru_wx, lru_bx, lru_lam, swa_sink, group_norm, w_out, mem_norm_pre, mem_norm_post,
         mem_kv_norm, w_mq, w_mk, w_mv, w_mo, ffn_norm_pre, ffn_norm_post, w_ff1, w_ff2):
    d = cfg.d_model
    p = dict(
        mix_norm_pre=mix_norm_pre, mix_norm_post=mix_norm_post, w_in=w_in.astype(BF16), conv_w=conv_w,
        conv_b=conv_b, lru_wa=lru_wa, lru_ba=lru_ba, lru_wx=lru_wx, lru_bx=lru_bx, lru_lam=lru_lam,
        swa_sink=swa_sink, group_norm=group_norm, w_out=w_out.astype(BF16), mem_norm_pre=mem_norm_pre,
        mem_norm_post=mem_norm_post, mem_kv_norm=mem_kv_norm, w_mq=w_mq.astype(BF16),
        w_mkv=jnp.concatenate([w_mk, w_mv], axis=-1).astype(BF16), w_mo=w_mo.astype(BF16),
        ffn_norm_pre=ffn_norm_pre, ffn_norm_post=ffn_norm_post, w_ff1=w_ff1.astype(BF16),
        w_ff2=w_ff2.astype(BF16))
    outs = []
    for group, x, mem in zip(cfg.groups, (x_prompt, x_sample), (mem_prompt, mem_sample)):
        sub = cfg._replace(groups=(group,))
        outs.append(_forward(sub, x.reshape(-1, d), mem.reshape(-1, d), p).reshape(x.shape))
    return tuple(outs)


def kernel(x_prompt, x_sample, mem_prompt, mem_sample, mix_norm_pre, mix_norm_post, w_in, conv_w, conv_b, lru_wa,
           lru_ba, lru_wx, lru_bx, lru_lam, swa_sink, group_norm, w_out, mem_norm_pre, mem_norm_post, mem_kv_norm,
           w_mq, w_mk, w_mv, w_mo, ffn_norm_pre, ffn_norm_post, w_ff1, w_ff2):
    return _run(_CFG, x_prompt, x_sample, mem_prompt, mem_sample, mix_norm_pre, mix_norm_post, w_in, conv_w,
                conv_b, lru_wa, lru_ba, lru_wx, lru_bx, lru_lam, swa_sink, group_norm, w_out, mem_norm_pre,
                mem_norm_post, mem_kv_norm, w_mq, w_mk, w_mv, w_mo, ffn_norm_pre, ffn_norm_post, w_ff1, w_ff2)
```

```python
import functools
from typing import NamedTuple

import numpy as np
import jax
import jax.numpy as jnp
from jax import lax
from jax.experimental import pallas as pl
from jax.experimental.pallas import tpu as pltpu

F32 = jnp.float32
BF16 = jnp.bfloat16

D_MODEL = 2048
BATCH = 8
SEQ = 4096
DEPTH = 4
DEC_BATCH = 1
DEC_SEQ = 16384
HEAD_DIM = 128
LRU_WIDTH = 512
LRU_BLOCKS = 4
LRU_BLOCK_WIDTH = LRU_WIDTH // LRU_BLOCKS
CONV_WIDTH = 4
CONV_LEFT = 2
LRU_C = 8.0
DIL_HEADS = 6
DIL_PATTERNS = ((128, 1), (512, 4), (2048, 16))
SWA_HEADS = 6
SWA_KV_HEADS = 2
SWA_WINDOW = 128
DIL_WIDTH = DIL_HEADS * HEAD_DIM
SWA_WIDTH = SWA_HEADS * HEAD_DIM
SWA_KV_WIDTH = SWA_KV_HEADS * HEAD_DIM
MIX_WIDTH = LRU_WIDTH + DIL_WIDTH + SWA_WIDTH
IN_WIDTH = 2 * LRU_WIDTH + 3 * DIL_WIDTH + SWA_WIDTH + 2 * SWA_KV_WIDTH
N_MEM = 256
MEM_HEADS = 4
MEM_WIDTH = MEM_HEADS * HEAD_DIM
D_FF = 4 * D_MODEL
EPS = 1e-6

_NEG = -1e30
_QK_SCALE = HEAD_DIM ** -0.5
_HALO_ROWS = 8
_NORM_ROWS = 128
_SCAN_GROUP = 8
_DIL_W = DIL_PATTERNS[0][0] // (2 * DIL_PATTERNS[0][1])
assert all(wn // (2 * d) == _DIL_W for wn, d in DIL_PATTERNS)
_DIL_QROWS_MAX = 128
_PERM_ROWS = 256
_V7X_VMEM_BYTES = 64 * 1024 * 1024
_VMEM_LIMIT = _V7X_VMEM_BYTES - 3 * 1024 * 1024
_NT = (((1,), (1,)), ((), ()))


class _Cfg(NamedTuple):
    d_model: int
    d_ff: int
    depth: int
    groups: tuple
    n_mem: int
    tm: int
    tm_ffn: int
    tf: int
    lru_chunk: int
    lru_rows: int
    swa_rows: int
    dil_rows: int
    swa_batch: int
    dil_batch: int


_CFG = _Cfg(d_model=D_MODEL, d_ff=D_FF, depth=DEPTH, groups=((BATCH, SEQ), (DEC_BATCH, DEC_SEQ)),
            n_mem=N_MEM, tm=512, tm_ffn=1024, tf=1024, lru_chunk=1024, lru_rows=256, swa_rows=512,
            dil_rows=1024, swa_batch=2, dil_batch=4)


def _sequences(cfg):
    out, start = [], 0
    for n, length in cfg.groups:
        for _ in range(n):
            out.append((start, length))
            start += length
    return out, start


def _chunk_flags(cfg, rows):
    seqs, total = _sequences(cfg)
    starts = {s for s, _ in seqs}
    ends = {s + l for s, l in seqs}
    for s, l in seqs:
        assert l % rows == 0, (l, rows)
    n = total // rows
    flags = np.zeros((n,), np.int32)
    for c in range(n):
        flags[c] = (1 if c * rows in starts else 0) | (2 if (c + 1) * rows in ends else 0)
    return jnp.asarray(flags)


def _params(semantics):
    return pltpu.CompilerParams(dimension_semantics=semantics, vmem_limit_bytes=_VMEM_LIMIT)


def _rms(x, g):
    ms = jnp.mean(x * x, axis=-1, keepdims=True)
    return x * lax.rsqrt(ms + EPS) * g


def _row_blocks(n):
    return [slice(r, min(r + _NORM_ROWS, n)) for r in range(0, n, _NORM_ROWS)]


def _resident(shape, layer=None):
    if layer is None:
        return pl.BlockSpec(shape, lambda *_: (0,) * len(shape), pipeline_mode=pl.Buffered(1))
    return pl.BlockSpec((None,) + tuple(shape), lambda *_: (layer,) + (0,) * len(shape),
                        pipeline_mode=pl.Buffered(1))


def _inproj_plan():
    lru_w = 2 * LRU_WIDTH
    dil_w = 3 * DIL_WIDTH
    segs = [
        (0, lru_w, 0, None),
        (lru_w, lru_w + DIL_WIDTH, 1, _QK_SCALE),
        (lru_w + DIL_WIDTH, lru_w + dil_w, 1, None),
        (lru_w + dil_w, lru_w + dil_w + SWA_WIDTH, 2, _QK_SCALE),
        (lru_w + dil_w + SWA_WIDTH, IN_WIDTH, 2, None),
    ]
    base = {0: 0, 1: lru_w, 2: lru_w + dil_w}
    plan = []
    for c0, c1, oi, scale in segs:
        c = c0
        while c < c1:
            n = min(512, c1 - c)
            plan.append((c, c + n, oi, c - base[oi], scale))
            c += n
    return tuple(plan)


def _inproj_kernel(x_ref, g_ref, w_ref, lru_ref, qkvb_ref, qkvc_ref, xn_scr, *, plan):
    for rows in _row_blocks(x_ref.shape[0]):
        xn_scr[rows, :] = _rms(x_ref[rows, :], g_ref[...]).astype(BF16)
    outs = (lru_ref, qkvb_ref, qkvc_ref)
    for c0, c1, oi, o0, scale in plan:
        acc = jnp.dot(xn_scr[...], w_ref[:, c0:c1], preferred_element_type=F32)
        if scale is not None:
            acc = acc * scale
        outs[oi][:, o0:o0 + (c1 - c0)] = acc.astype(outs[oi].dtype)


def _inproj(cfg, x, g, w, layer):
    t, d = x.shape
    tm = cfg.tm
    widths = (2 * LRU_WIDTH, 3 * DIL_WIDTH, SWA_WIDTH + 2 * SWA_KV_WIDTH)
    return pl.pallas_call(
        functools.partial(_inproj_kernel, plan=_inproj_plan()),
        grid=(t // tm,),
        in_specs=[
            pl.BlockSpec((tm, d), lambda i: (i, 0)),
            _resident((1, d)),
            _resident((d, IN_WIDTH), layer),
        ],
        out_specs=[pl.BlockSpec((tm, wd), lambda i: (i, 0)) for wd in widths],
        out_shape=[
            jax.ShapeDtypeStruct((t, widths[0]), F32),
            jax.ShapeDtypeStruct((t, widths[1]), BF16),
            jax.ShapeDtypeStruct((t, widths[2]), BF16),
        ],
        scratch_shapes=[pltpu.VMEM((tm, d), BF16)],
        compiler_params=_params(("parallel",)),
        name="mixer_inproj",
    )(x, g, w)


def _lru_fill_halo(first, last, xa_ref, xp_ref, xn_ref, xext, lc):
    xext[_HALO_ROWS:_HALO_ROWS + lc, :] = xa_ref[...]

    @pl.when(first)
    def _():
        xext[0:_HALO_ROWS, :] = jnp.zeros((_HALO_ROWS, LRU_WIDTH), F32)

    @pl.when(jnp.logical_not(first))
    def _():
        xext[0:_HALO_ROWS, :] = xp_ref[...]

    @pl.when(last)
    def _():
        xext[_HALO_ROWS + lc:, :] = jnp.zeros((_HALO_ROWS, LRU_WIDTH), F32)

    @pl.when(jnp.logical_not(last))
    def _():
        xext[_HALO_ROWS + lc:, :] = xn_ref[...]


def _lru_gates(xext, cw_ref, cb_ref, wg_ref, ba_ref, bx_ref, lam_ref, a_scr, u_scr, lc, rb):
    lam = lam_ref[...]
    neg = -lam
    softplus = jnp.maximum(neg, 0.0) + jnp.log1p(jnp.exp(-jnp.abs(neg)))
    cb = cb_ref[...]
    taps = [cw_ref[j:j + 1, :] for j in range(CONV_WIDTH)]
    for blk in range(lc // rb):
        r0 = blk * rb
        xc = cb + sum(taps[j] * xext[r0 + _HALO_ROWS - CONV_LEFT + j:r0 + _HALO_ROWS - CONV_LEFT + j + rb, :]
                      for j in range(CONV_WIDTH))
        xcb = xc.astype(BF16)
        for n in range(LRU_BLOCKS):
            cs = slice(n * LRU_BLOCK_WIDTH, (n + 1) * LRU_BLOCK_WIDTH)
            g = jnp.dot(xcb[:, cs], wg_ref[n], preferred_element_type=F32)
            r = jax.nn.sigmoid(g[:, :LRU_BLOCK_WIDTH] + ba_ref[:, cs])
            ig = jax.nn.sigmoid(g[:, LRU_BLOCK_WIDTH:] + bx_ref[:, cs])
            log_a = (-LRU_C * r) * softplus[:, cs]
            a = jnp.exp(log_a)
            one_minus_a2 = -jnp.tanh(log_a) * (1.0 + a * a)
            u = jnp.sqrt(one_minus_a2) * (ig * xc[:, cs])
            groups = slice(r0 // _SCAN_GROUP, (r0 + rb) // _SCAN_GROUP)
            a_scr[groups, :, cs] = a.reshape(rb // _SCAN_GROUP, _SCAN_GROUP, LRU_BLOCK_WIDTH)
            u_scr[groups, :, cs] = u.reshape(rb // _SCAN_GROUP, _SCAN_GROUP, LRU_BLOCK_WIDTH)


def _lru_scan(reset, a_scr, u_scr, h_dst, carry, lc, reverse):
    @pl.when(reset)
    def _():
        carry[...] = jnp.zeros((1, LRU_WIDTH), F32)

    def group(i, h):
        g = lc // _SCAN_GROUP - 1 - i if reverse else i
        order = [_SCAN_GROUP - 1 - j if reverse else j for j in range(_SCAN_GROUP)]
        p = a_scr[g, order[0]:order[0] + 1, :]
        q = u_scr[g, order[0]:order[0] + 1, :]
        outs = [p * h + q]
        for j in order[1:]:
            a = a_scr[g, j:j + 1, :]
            q = a * q + u_scr[g, j:j + 1, :]
            p = a * p
            outs.append(p * h + q)
        for j, o in zip(order, outs):
            h_dst[g, j:j + 1, :] = o
        return outs[-1]

    carry[...] = lax.fori_loop(0, lc // _SCAN_GROUP, group, carry[...], unroll=2)


def _lru_fwd_kernel(flags_ref, xa_ref, xp_ref, xn_ref, cw_ref, cb_ref, wg_ref, ba_ref, bx_ref, lam_ref,
                    hf_ref, xext, a_scr, u_scr, carry, *, lc, rb):
    fl = flags_ref[pl.program_id(0)]
    first = (fl & 1) != 0
    last = (fl & 2) != 0
    _lru_fill_halo(first, last, xa_ref, xp_ref, xn_ref, xext, lc)
    _lru_gates(xext, cw_ref, cb_ref, wg_ref, ba_ref, bx_ref, lam_ref, a_scr, u_scr, lc, rb)
    _lru_scan(first, a_scr, u_scr, hf_ref, carry, lc, reverse=False)


def _lru_bwd_kernel(flags_ref, xa_ref, xp_ref, xn_ref, gate_ref, hf_ref, cw_ref, cb_ref, wg_ref, ba_ref,
                    bx_ref, lam_ref, gn_ref, y_ref, xext, a_scr, u_scr, h_scr, carry, *, lc, rb, nchunks):
    fl = flags_ref[nchunks - 1 - pl.program_id(0)]
    first = (fl & 1) != 0
    last = (fl & 2) != 0
    _lru_fill_halo(first, last, xa_ref, xp_ref, xn_ref, xext, lc)
    _lru_gates(xext, cw_ref, cb_ref, wg_ref, ba_ref, bx_ref, lam_ref, a_scr, u_scr, lc, rb)
    _lru_scan(last, a_scr, u_scr, h_scr, carry, lc, reverse=True)
    for blk in range(lc // rb):
        rows = slice(blk * rb, (blk + 1) * rb)
        groups = slice(blk * rb // _SCAN_GROUP, (blk + 1) * rb // _SCAN_GROUP)
        h = (hf_ref[groups] + h_scr[groups]).reshape(rb, LRU_WIDTH)
        y = h * jax.nn.gelu(gate_ref[rows, :])
        y_ref[rows, :] = _rms(y, gn_ref[...]).astype(BF16)


def _lru(cfg, lru_in, cw, cb, wg, ba, bx, lam, gn):
    t = lru_in.shape[0]
    lc, rb = cfg.lru_chunk, cfg.lru_rows
    nchunks = t // lc
    hb = lc // _HALO_ROWS
    nhalo = t // _HALO_ROWS
    flags = _chunk_flags(cfg, lc)
    row = lambda: _resident((1, LRU_WIDTH))

    def specs(chunk_of):
        return [
            pl.BlockSpec((lc, LRU_WIDTH), lambda i, f: (chunk_of(i), 0)),
            pl.BlockSpec((_HALO_ROWS, LRU_WIDTH), lambda i, f: (jnp.maximum(chunk_of(i) * hb - 1, 0), 0)),
            pl.BlockSpec((_HALO_ROWS, LRU_WIDTH), lambda i, f: (jnp.minimum((chunk_of(i) + 1) * hb, nhalo - 1), 0)),
        ]

    def weights(d):
        return [
            _resident((CONV_WIDTH, LRU_WIDTH)), row(),
            pl.BlockSpec((None, LRU_BLOCKS, LRU_BLOCK_WIDTH, 2 * LRU_BLOCK_WIDTH), lambda i, f: (d, 0, 0, 0)),
            pl.BlockSpec((None, 1, LRU_WIDTH), lambda i, f: (d, 0, 0)),
            pl.BlockSpec((None, 1, LRU_WIDTH), lambda i, f: (d, 0, 0)),
            pl.BlockSpec((None, 1, LRU_WIDTH), lambda i, f: (d, 0, 0)),
        ]

    grouped = (lc // _SCAN_GROUP, _SCAN_GROUP, LRU_WIDTH)
    scratch = [pltpu.VMEM((lc + 2 * _HALO_ROWS, LRU_WIDTH), F32), pltpu.VMEM(grouped, F32),
               pltpu.VMEM(grouped, F32)]
    carry = [pltpu.VMEM((1, LRU_WIDTH), F32)]

    fwd = lambda i: i
    hf = pl.pallas_call(
        functools.partial(_lru_fwd_kernel, lc=lc, rb=rb),
        grid_spec=pltpu.PrefetchScalarGridSpec(
            num_scalar_prefetch=1, grid=(nchunks,),
            in_specs=specs(fwd) + weights(0),
            out_specs=pl.BlockSpec(grouped, lambda i, f: (i, 0, 0)),
            scratch_shapes=scratch + carry),
        out_shape=jax.ShapeDtypeStruct((t // _SCAN_GROUP, _SCAN_GROUP, LRU_WIDTH), F32),
        compiler_params=_params(("arbitrary",)),
        name="lru_forward",
    )(flags, lru_in, lru_in, lru_in, cw, cb, wg, ba, bx, lam)

    bwd = lambda i: nchunks - 1 - i
    return pl.pallas_call(
        functools.partial(_lru_bwd_kernel, lc=lc, rb=rb, nchunks=nchunks),
        grid_spec=pltpu.PrefetchScalarGridSpec(
            num_scalar_prefetch=1, grid=(nchunks,),
            in_specs=specs(bwd) + [
                pl.BlockSpec((lc, LRU_WIDTH), lambda i, f: (bwd(i), 1)),
                pl.BlockSpec(grouped, lambda i, f: (bwd(i), 0, 0)),
            ] + weights(1) + [row()],
            out_specs=pl.BlockSpec((lc, LRU_WIDTH), lambda i, f: (bwd(i), 0)),
            scratch_shapes=scratch + [pltpu.VMEM(grouped, F32)] + carry),
        out_shape=jax.ShapeDtypeStruct((t, LRU_WIDTH), BF16),
        compiler_params=_params(("arbitrary",)),
        name="lru_backward",
    )(flags, lru_in, lru_in, lru_in, lru_in, hf, cw, cb, wg, ba, bx, lam, gn)


def _alibi_slopes(n):
    return [2.0 ** (-8.0 * (i + 1) / n) for i in range(n)]


def _band_bias(wq, halo, dist_scale, hq, hkv):
    rep = hq // hkv
    slopes = _alibi_slopes(hq)
    qi = np.arange(wq)[:, None]
    kj = np.arange(wq + 2 * halo)[None, :]
    rel = np.abs(kj - halo - qi)
    out = np.empty((hkv, rep * wq, wq + 2 * halo), np.float32)
    for g in range(hkv):
        for r in range(rep):
            out[g, r * wq:(r + 1) * wq] = np.where(rel <= halo, -slopes[g * rep + r] * dist_scale * rel, _NEG)
    return out


def _edge_penalties(fl, wq, halo):
    col = lax.broadcasted_iota(jnp.int32, (1, wq + 2 * halo), 1)
    pen_first = jnp.where(col < halo, jnp.where((fl & 1) != 0, _NEG, 0.0), 0.0)
    pen_last = jnp.where(col >= wq + halo, jnp.where((fl & 2) != 0, _NEG, 0.0), 0.0)
    return pen_first, pen_last


def _attend(qs, ks, vs, biases):
    scores = [lax.dot_general(q, k, _NT, preferred_element_type=F32) for q, k in zip(qs, ks)]
    ms, ps = [], []
    for s, b in zip(scores, biases):
        if b is not None:
            s = s + b
        m = jnp.max(s, axis=-1, keepdims=True)
        ms.append(m)
        ps.append(jnp.exp(s - m).astype(BF16))
    ls, accs = [], []
    for p, v in zip(ps, vs):
        v1 = jnp.concatenate([v, jnp.ones(v.shape, v.dtype)], axis=1)
        out = jnp.dot(p, v1, preferred_element_type=F32)
        accs.append(out[:, :HEAD_DIM])
        ls.append(out[:, HEAD_DIM:])
    return ms, ls, accs


def _swa_kernel(flags_ref, q_ref, kc_ref, kp_ref, kn_ref, vc_ref, vp_ref, vn_ref, bias_ref, sink_ref, gn_ref,
                y_out, kbuf, vbuf, *, w, nsub, nb):
    rep = SWA_HEADS // SWA_KV_HEADS
    rows_total = nsub * w
    pen_first, pen_last = _edge_penalties(flags_ref[pl.program_id(0)], w, w)
    kbuf[0:w, :] = kp_ref[...]
    kbuf[w:w + rows_total, :] = kc_ref[...]
    kbuf[w + rows_total:, :] = kn_ref[...]
    vbuf[0:w, :] = vp_ref[...]
    vbuf[w:w + rows_total, :] = vc_ref[...]
    vbuf[w + rows_total:, :] = vn_ref[...]

    for j0 in range(0, nsub, nb):
        blocks = list(range(j0, min(j0 + nb, nsub)))
        qs, ks, vs, bs = [], [], [], []
        for j in blocks:
            rows = slice(j * w, (j + 1) * w)
            for g in range(SWA_KV_HEADS):
                gs = slice(g * HEAD_DIM, (g + 1) * HEAD_DIM)
                qs.append(jnp.concatenate(
                    [q_ref[rows, (g * rep + r) * HEAD_DIM:(g * rep + r + 1) * HEAD_DIM] for r in range(rep)], axis=0))
                ks.append(kbuf[j * w:(j + 3) * w, gs])
                vs.append(vbuf[j * w:(j + 3) * w, gs])
                b = bias_ref[g]
                if j == 0:
                    b = b + pen_first
                if j == nsub - 1:
                    b = b + pen_last
                bs.append(b)
        ms, ls, accs = _attend(qs, ks, vs, bs)
        for bi, j in enumerate(blocks):
            heads_out = []
            for g in range(SWA_KV_HEADS):
                idx = bi * SWA_KV_HEADS + g
                for r in range(rep):
                    h = g * rep + r
                    part = slice(r * w, (r + 1) * w)
                    m, l, acc = ms[idx][part], ls[idx][part], accs[idx][part]
                    factor = jax.nn.sigmoid(m + jnp.log(l) - sink_ref[:, h:h + 1])
                    heads_out.append((acc / l) * factor)
            y = jnp.concatenate(heads_out, axis=1)
            y_out[j * w:(j + 1) * w, :] = _rms(y, gn_ref[...]).astype(BF16)


def _swa(cfg, qkvc, gn, sink):
    t, c = qkvc.shape
    w = SWA_WINDOW
    rows = cfg.swa_rows
    nsub = rows // w
    nchunks = t // rows
    nblk = t // w
    qw, kvw = SWA_WIDTH, SWA_KV_WIDTH
    rep = SWA_HEADS // SWA_KV_HEADS
    assert qw % kvw == 0 and c == qw + 2 * kvw
    kcol, vcol = qw // kvw, qw // kvw + 1
    prev = lambda col: (lambda i, f: (jnp.maximum(i * nsub - 1, 0), col))
    nxt = lambda col: (lambda i, f: (jnp.minimum((i + 1) * nsub, nblk - 1), col))
    return pl.pallas_call(
        functools.partial(_swa_kernel, w=w, nsub=nsub, nb=cfg.swa_batch),
        grid_spec=pltpu.PrefetchScalarGridSpec(
            num_scalar_prefetch=1, grid=(nchunks,),
            in_specs=[
                pl.BlockSpec((rows, qw), lambda i, f: (i, 0)),
                pl.BlockSpec((rows, kvw), lambda i, f: (i, kcol)),
                pl.BlockSpec((w, kvw), prev(kcol)),
                pl.BlockSpec((w, kvw), nxt(kcol)),
                pl.BlockSpec((rows, kvw), lambda i, f: (i, vcol)),
                pl.BlockSpec((w, kvw), prev(vcol)),
                pl.BlockSpec((w, kvw), nxt(vcol)),
                _resident((SWA_KV_HEADS, rep * w, 3 * w)),
                _resident((1, 128)),
                _resident((1, qw)),
            ],
            out_specs=pl.BlockSpec((rows, qw), lambda i, f: (i, 0)),
            scratch_shapes=[pltpu.VMEM((rows + 2 * w, kvw), BF16), pltpu.VMEM((rows + 2 * w, kvw), BF16)]),
        out_shape=jax.ShapeDtypeStruct((t, qw), BF16),
        compiler_params=_params(("parallel",)),
        name="windowed_gqa_sink",
    )(_chunk_flags(cfg, rows), qkvc, qkvc, qkvc, qkvc, qkvc, qkvc, qkvc,
      jnp.asarray(_band_bias(w, w, 1, SWA_HEADS, SWA_KV_HEADS)), sink, gn)


def _perm_matrix(d):
    n = _PERM_ROWS
    per = n // d
    p = np.zeros((n, n), np.float32)
    for r in range(d):
        for m in range(per):
            p[r * per + m, d * m + r] = 1.0
    return p


def _dilated_qrows(c, d):
    return min(_DIL_QROWS_MAX, c // d)


def _dilated_kernel(flags_ref, q_ref, kp_ref, kc_ref, kn_ref, vp_ref, vc_ref, vn_ref, bias0, bias1, bias2,
                    perm_ref, gn_ref, y_ref, qd, kd, vd, bv0, bv1, bv2, acc_nat, m_nat, l_nat, *, c, batch_rows):
    w = _DIL_W
    nh = DIL_HEADS
    fl = flags_ref[pl.program_id(0)]
    bias_refs = (bias0, bias1, bias2)
    biasv = (bv0, bv1, bv2)
    for p, (_, d) in enumerate(DIL_PATTERNS):
        wq = _dilated_qrows(c, d)
        pen_first, pen_last = _edge_penalties(fl, wq, w)
        for h in range(nh):
            b = bias_refs[p][h]
            if c // (d * wq) == 1:
                biasv[p][0, h] = b + pen_first + pen_last
            else:
                biasv[p][0, h] = b
                biasv[p][1, h] = b + pen_first
                biasv[p][2, h] = b + pen_last

    heads = [slice(h * HEAD_DIM, (h + 1) * HEAD_DIM) for h in range(nh)]

    def run_pattern(p, d, qsrc, ksrc, vsrc, qstride, kstride, first_pattern, last_pattern):
        wq = _dilated_qrows(c, d)
        win = wq + 2 * w
        nblk = c // (d * wq)
        nb = max(1, batch_rows // wq)
        assert (c // wq) % nb == 0
        assert d == 1 or not last_pattern

        def body(it, carry):
            qs, ks, vs, bs, where = [], [], [], [], []
            for b in range(nb):
                sb = it * nb + b
                r = sb // nblk
                s = sb % nblk
                var = 0 if nblk == 1 else jnp.where(s == 0, 1, 0) + jnp.where(s == nblk - 1, 2, 0)
                q0 = pl.multiple_of(r * qstride + s * wq, w)
                k0 = pl.multiple_of(r * kstride + s * wq, w)
                where.append(d * wq * s + r)
                for h in range(nh):
                    qs.append(qsrc[pl.ds(q0, wq), heads[h]])
                    ks.append(ksrc[pl.ds(k0, win), heads[h]])
                    vs.append(vsrc[pl.ds(k0, win), heads[h]])
                    bs.append(biasv[p][var, h])
            ms, ls, accs = _attend(qs, ks, vs, bs)
            for b in range(nb):
                nat = pl.ds(where[b], wq, stride=d) if d > 1 else pl.ds(pl.multiple_of(where[b], w), wq)
                outs = []
                for h in range(nh):
                    m, l, acc = ms[b * nh + h], ls[b * nh + h], accs[b * nh + h]
                    if not first_pattern:
                        m_p = m_nat[h, nat, :]
                        m_n = jnp.maximum(m_p, m)
                        alpha = jnp.exp(m_p - m_n)
                        beta = jnp.exp(m - m_n)
                        l = alpha * l_nat[h, nat, :] + beta * l
                        acc = alpha * acc_nat[h, nat, :] + beta * acc
                        m = m_n
                    if last_pattern:
                        outs.append(acc / l)
                    else:
                        acc_nat[h, nat, :] = acc
                        m_nat[h, nat, :] = jnp.broadcast_to(m, (wq, HEAD_DIM))
                        l_nat[h, nat, :] = jnp.broadcast_to(l, (wq, HEAD_DIM))
                if last_pattern:
                    y_ref[nat, :] = _rms(jnp.concatenate(outs, axis=1), gn_ref[...]).astype(BF16)
            return carry

        lax.fori_loop(0, c // (wq * nb), body, 0)

    def deinterleave(p, d):
        per = _PERM_ROWS // d
        halo_groups = (d * w) // _PERM_ROWS
        chunk_groups = c // _PERM_ROWS
        kstride = (chunk_groups + 2 * halo_groups) * per
        perm = perm_ref[p - 1]

        def move(src, row0, dst, g, stride):
            res = jnp.dot(perm, src[row0:row0 + _PERM_ROWS, :], preferred_element_type=F32).astype(BF16)
            for r in range(d):
                dst[r * stride + g * per:r * stride + (g + 1) * per, :] = res[r * per:(r + 1) * per]

        for g in range(chunk_groups):
            move(q_ref, g * _PERM_ROWS, qd, g, c // d)
        for prev_ref, cur_ref, next_ref, dst in ((kp_ref, kc_ref, kn_ref, kd), (vp_ref, vc_ref, vn_ref, vd)):
            srcs = ([(prev_ref, c - (halo_groups - g) * _PERM_ROWS) for g in range(halo_groups)]
                    + [(cur_ref, g * _PERM_ROWS) for g in range(chunk_groups)]
                    + [(next_ref, g * _PERM_ROWS) for g in range(halo_groups)])
            for g, (src, row0) in enumerate(srcs):
                move(src, row0, dst, g, kstride)
        return c // d, kstride

    order = sorted(range(len(DIL_PATTERNS)), key=lambda p: -DIL_PATTERNS[p][1])
    for idx, p in enumerate(order):
        d = DIL_PATTERNS[p][1]
        first, last = idx == 0, idx == len(order) - 1
        if d == 1:
            for prev_ref, cur_ref, next_ref, dst in ((kp_ref, kc_ref, kn_ref, kd), (vp_ref, vc_ref, vn_ref, vd)):
                dst[0:w, :] = prev_ref[c - w:c, :]
                dst[w:w + c, :] = cur_ref[...]
                dst[w + c:2 * w + c, :] = next_ref[0:w, :]
            run_pattern(p, d, q_ref, kd, vd, 0, 0, first, last)
        else:
            qstride, kstride = deinterleave(p, d)
            run_pattern(p, d, qd, kd, vd, qstride, kstride, first, last)


def _dilated(cfg, qkvb, gn):
    t = qkvb.shape[0]
    c = cfg.dil_rows
    w = _DIL_W
    n = t // c
    dmax = max(d for _, d in DIL_PATTERNS)
    assert DIL_PATTERNS[0][1] == 1 and c % (dmax * w) == 0 and c >= dmax * w and c % _PERM_ROWS == 0
    assert all((d * w) % _PERM_ROWS == 0 for _, d in DIL_PATTERNS[1:])
    assert len(DIL_PATTERNS) == 3
    biases = [_band_bias(_dilated_qrows(c, d), w, d, DIL_HEADS, DIL_HEADS) for _, d in DIL_PATTERNS]
    variants = [1 if c // (d * _dilated_qrows(c, d)) == 1 else 3 for _, d in DIL_PATTERNS]
    perm = np.stack([_perm_matrix(d) for _, d in DIL_PATTERNS[1:]])
    blk = (c, DIL_WIDTH)
    prev = lambda col: (lambda i, f: (jnp.maximum(i - 1, 0), col))
    cur = lambda col: (lambda i, f: (i, col))
    nxt = lambda col: (lambda i, f: (jnp.minimum(i + 1, n - 1), col))
    return pl.pallas_call(
        functools.partial(_dilated_kernel, c=c, batch_rows=cfg.dil_batch * w),
        grid_spec=pltpu.PrefetchScalarGridSpec(
            num_scalar_prefetch=1, grid=(n,),
            in_specs=[
                pl.BlockSpec(blk, cur(0)),
                pl.BlockSpec(blk, prev(1)), pl.BlockSpec(blk, cur(1)), pl.BlockSpec(blk, nxt(1)),
                pl.BlockSpec(blk, prev(2)), pl.BlockSpec(blk, cur(2)), pl.BlockSpec(blk, nxt(2)),
                _resident(biases[0].shape), _resident(biases[1].shape), _resident(biases[2].shape),
                _resident(perm.shape), _resident((1, DIL_WIDTH)),
            ],
            out_specs=pl.BlockSpec(blk, cur(0)),
            scratch_shapes=[
                pltpu.VMEM((c, DIL_WIDTH), BF16),
                pltpu.VMEM((3 * c, DIL_WIDTH), BF16), pltpu.VMEM((3 * c, DIL_WIDTH), BF16),
                pltpu.VMEM((variants[0],) + biases[0].shape, F32),
                pltpu.VMEM((variants[1],) + biases[1].shape, F32),
                pltpu.VMEM((variants[2],) + biases[2].shape, F32),
                pltpu.VMEM((DIL_HEADS, c, HEAD_DIM), F32),
                pltpu.VMEM((DIL_HEADS, c, HEAD_DIM), F32),
                pltpu.VMEM((DIL_HEADS, c, HEAD_DIM), F32),
            ]),
        out_shape=jax.ShapeDtypeStruct((t, DIL_WIDTH), BF16),
        compiler_params=_params(("parallel",)),
        name="dilated_attention",
    )(_chunk_flags(cfg, c), qkvb, qkvb, qkvb, qkvb, qkvb, qkvb, qkvb,
      jnp.asarray(biases[0]), jnp.asarray(biases[1]), jnp.asarray(biases[2]), jnp.asarray(perm, dtype=BF16), gn)


def _outproj_kernel(ya_ref, yb_ref, yc_ref, x_ref, w_ref, g_ref, o_ref, y_scr, *, nchunk):
    d = o_ref.shape[1]
    b0, b1 = LRU_WIDTH, LRU_WIDTH + DIL_WIDTH
    for c in range(0, d, nchunk):
        cs = slice(c, c + nchunk)
        acc = jnp.dot(ya_ref[...], w_ref[0:b0, cs], preferred_element_type=F32)
        acc += jnp.dot(yb_ref[...], w_ref[b0:b1, cs], preferred_element_type=F32)
        acc += jnp.dot(yc_ref[...], w_ref[b1:, cs], preferred_element_type=F32)
        y_scr[:, cs] = acc
    for rows in _row_blocks(o_ref.shape[0]):
        o_ref[rows, :] = x_ref[rows, :] + _rms(y_scr[rows, :], g_ref[...])


def _outproj(cfg, ya, yb, yc, x, w, g, layer):
    t, d = x.shape
    tm = cfg.tm
    return pl.pallas_call(
        functools.partial(_outproj_kernel, nchunk=min(512, d)),
        grid=(t // tm,),
        in_specs=[
            pl.BlockSpec((tm, LRU_WIDTH), lambda i: (i, 0)),
            pl.BlockSpec((tm, DIL_WIDTH), lambda i: (i, 0)),
            pl.BlockSpec((tm, SWA_WIDTH), lambda i: (i, 0)),
            pl.BlockSpec((tm, d), lambda i: (i, 0)),
            _resident((MIX_WIDTH, d), layer),
            _resident((1, d)),
        ],
        out_specs=pl.BlockSpec((tm, d), lambda i: (i, 0)),
        out_shape=jax.ShapeDtypeStruct((t, d), F32),
        scratch_shapes=[pltpu.VMEM((tm, d), F32)],
        compiler_params=_params(("parallel",)),
        name="mixer_outproj",
    )(ya, yb, yc, x, w, g)


def _memkv_kernel(mem_ref, g_ref, w_ref, k_ref, v_ref):
    mn = _rms(mem_ref[...], g_ref[...]).astype(BF16)
    kv = jnp.dot(mn, w_ref[...], preferred_element_type=F32)
    k_ref[...] = kv[:, :MEM_WIDTH].astype(BF16)
    v_ref[...] = kv[:, MEM_WIDTH:].astype(BF16)


def _memkv(cfg, mem, g, wkv, layer):
    rows, d = mem.shape
    nm = cfg.n_mem
    shape = jax.ShapeDtypeStruct((rows, MEM_WIDTH), BF16)
    return pl.pallas_call(
        _memkv_kernel,
        grid=(rows // nm,),
        in_specs=[pl.BlockSpec((nm, d), lambda i: (i, 0)), _resident((1, d)),
                  _resident((d, 2 * MEM_WIDTH), layer)],
        out_specs=[pl.BlockSpec((nm, MEM_WIDTH), lambda i: (i, 0))] * 2,
        out_shape=[shape, shape],
        compiler_params=_params(("parallel",)),
        name="memory_kv",
    )(mem, g, wkv)


def _cross_kernel(seq_ref, x_ref, gpre_ref, wq_ref, k_ref, v_ref, wo_ref, gpost_ref, o_ref, xn_scr, o_scr, *,
                  nchunk):
    del seq_ref
    tm, d = x_ref.shape
    for rows in _row_blocks(tm):
        xn_scr[rows, :] = _rms(x_ref[rows, :], gpre_ref[...]).astype(BF16)
    q = (jnp.dot(xn_scr[...], wq_ref[...], preferred_element_type=F32) * _QK_SCALE).astype(BF16)
    heads = [slice(h * HEAD_DIM, (h + 1) * HEAD_DIM) for h in range(MEM_HEADS)]
    _, ls, accs = _attend([q[:, hs] for hs in heads], [k_ref[:, hs] for hs in heads],
                          [v_ref[:, hs] for hs in heads], [None] * MEM_HEADS)
    for hs, l, acc in zip(heads, ls, accs):
        o_scr[:, hs] = (acc / l).astype(BF16)
    for c in range(0, d, nchunk):
        cs = slice(c, min(c + nchunk, d))
        o_ref[:, cs] = jnp.dot(o_scr[...], wo_ref[:, cs], preferred_element_type=F32)
    for rows in _row_blocks(tm):
        o_ref[rows, :] = x_ref[rows, :] + _rms(o_ref[rows, :], gpost_ref[...])


def _cross(cfg, x, gpre, wq, kmem, vmem, wo, gpost, layer):
    t, d = x.shape
    tm, nm = cfg.tm, cfg.n_mem
    seqs, _ = _sequences(cfg)
    seq_of_tile = []
    for si, (_, length) in enumerate(seqs):
        assert length % tm == 0
        seq_of_tile += [si] * (length // tm)
    seq_of_tile = jnp.asarray(np.asarray(seq_of_tile, np.int32))
    return pl.pallas_call(
        functools.partial(_cross_kernel, nchunk=512),
        grid_spec=pltpu.PrefetchScalarGridSpec(
            num_scalar_prefetch=1, grid=(t // tm,),
            in_specs=[
                pl.BlockSpec((tm, d), lambda i, s: (i, 0)),
                _resident((1, d)),
                _resident((d, MEM_WIDTH), layer),
                pl.BlockSpec((nm, MEM_WIDTH), lambda i, s: (s[i], 0)),
                pl.BlockSpec((nm, MEM_WIDTH), lambda i, s: (s[i], 0)),
                _resident((MEM_WIDTH, d), layer),
                _resident((1, d)),
            ],
            out_specs=pl.BlockSpec((tm, d), lambda i, s: (i, 0)),
            scratch_shapes=[pltpu.VMEM((tm, d), BF16), pltpu.VMEM((tm, MEM_WIDTH), BF16)]),
        out_shape=jax.ShapeDtypeStruct((t, d), F32),
        compiler_params=_params(("parallel",)),
        name="memory_cross_attention",
    )(seq_of_tile, x, gpre, wq, kmem, vmem, wo, gpost)


def _ffn_kernel(x_ref, gpre_ref, w1_ref, w2_ref, gpost_ref, o_ref, xn_scr, h_scr, *, nchunk):
    f = pl.program_id(1)
    tf = w1_ref.shape[1]
    d = w2_ref.shape[1]

    tm = x_ref.shape[0]
    row_blocks = [slice(r, min(r + _NORM_ROWS, tm)) for r in range(0, tm, _NORM_ROWS)]

    @pl.when(f == 0)
    def _():
        for rows in row_blocks:
            xn_scr[rows, :] = _rms(x_ref[rows, :], gpre_ref[...]).astype(BF16)
            o_ref[rows, :] = jnp.zeros((rows.stop - rows.start, d), F32)

    for c in range(0, tf, nchunk):
        cs = slice(c, min(c + nchunk, tf))
        h = jnp.dot(xn_scr[...], w1_ref[:, cs], preferred_element_type=F32)
        h_scr[:, cs] = jnp.square(jnp.maximum(h, 0.0)).astype(BF16)
    for c in range(0, d, nchunk):
        cs = slice(c, min(c + nchunk, d))
        o_ref[:, cs] += jnp.dot(h_scr[...], w2_ref[:, cs], preferred_element_type=F32)

    @pl.when(f == pl.num_programs(1) - 1)
    def _():
        for rows in row_blocks:
            o_ref[rows, :] = x_ref[rows, :] + _rms(o_ref[rows, :], gpost_ref[...])


def _ffn(cfg, x, gpre, w1, w2, gpost, layer):
    t, d = x.shape
    tm, tf = cfg.tm_ffn, cfg.tf
    dff = w1.shape[2]
    return pl.pallas_call(
        functools.partial(_ffn_kernel, nchunk=512),
        grid=(t // tm, dff // tf),
        in_specs=[
            pl.BlockSpec((tm, d), lambda i, f: (i, 0)),
            _resident((1, d)),
            pl.BlockSpec((None, d, tf), lambda i, f: (layer, 0, f)),
            pl.BlockSpec((None, tf, d), lambda i, f: (layer, f, 0)),
            _resident((1, d)),
        ],
        out_specs=pl.BlockSpec((tm, d), lambda i, f: (i, 0)),
        out_shape=jax.ShapeDtypeStruct((t, d), F32),
        scratch_shapes=[pltpu.VMEM((tm, d), BF16), pltpu.VMEM((tm, tf), BF16)],
        compiler_params=_params(("parallel", "arbitrary")),
        name="squared_relu_mlp",
    )(x, gpre, w1, w2, gpost)


def _forward(cfg, x, mem, p):
    row = lambda a: a.reshape(1, -1).astype(F32)
    for l in range(cfg.depth):
        lru_in, qkvb, qkvc = _inproj(cfg, x, row(p["mix_norm_pre"][l]), p["w_in"], l)
        gn = p["group_norm"][l]
        wg = jnp.concatenate([p["lru_wa"][l], p["lru_wx"][l]], axis=-1).astype(BF16)
        ya = _lru(cfg, lru_in, p["conv_w"][l], row(p["conv_b"][l]), wg,
                  p["lru_ba"][l][:, None, :], p["lru_bx"][l][:, None, :], p["lru_lam"][l][:, None, :],
                  row(gn[:LRU_WIDTH]))
        yb = _dilated(cfg, qkvb, row(gn[LRU_WIDTH:LRU_WIDTH + DIL_WIDTH]))
        sink = jnp.zeros((1, 128), F32).at[0, :SWA_HEADS].set(p["swa_sink"][l].astype(F32))
        yc = _swa(cfg, qkvc, row(gn[LRU_WIDTH + DIL_WIDTH:]), sink)
        x = _outproj(cfg, ya, yb, yc, x, p["w_out"], row(p["mix_norm_post"][l]), l)
        kmem, vmem = _memkv(cfg, mem, row(p["mem_kv_norm"][l]), p["w_mkv"], l)
        x = _cross(cfg, x, row(p["mem_norm_pre"][l]), p["w_mq"], kmem, vmem, p["w_mo"],
                   row(p["mem_norm_post"][l]), l)
        x = _ffn(cfg, x, row(p["ffn_norm_pre"][l]), p["w_ff1"], p["w_ff2"], row(p["ffn_norm_post"][l]), l)
    return x


def _run(cfg, x_prompt, x_sample, mem_prompt, mem_sample, mix_norm_pre, mix_norm_post, w_in, conv_w, conv_b,
         lru_wa, lru_ba, lru_wx, lru_bx, lru_lam, swa_sink, group_norm, w_out, mem_norm_pre, mem_norm_post,
         mem_kv_norm, w_mq, w_mk, w_mv, w_mo, ffn_norm_pre, ffn_norm_post, w_ff1, w_ff2):
    d = cfg.d_model
    p = dict(
        mix_norm_pre=mix_norm_pre, mix_norm_post=mix_norm_post, w_in=w_in.astype(BF16), conv_w=conv_w,
        conv_b=conv_b, lru_wa=lru_wa, lru_ba=lru_ba, lru_wx=lru_wx, lru_bx=lru_bx, lru_lam=lru_lam,
        swa_sink=swa_sink, group_norm=group_norm, w_out=w_out.astype(BF16), mem_norm_pre=mem_norm_pre,
        mem_norm_post=mem_norm_post, mem_kv_norm=mem_kv_norm, w_mq=w_mq.astype(BF16),
        w_mkv=jnp.concatenate([w_mk, w_mv], axis=-1).astype(BF16), w_mo=w_mo.astype(BF16),
        ffn_norm_pre=ffn_norm_pre, ffn_norm_post=ffn_norm_post, w_ff1=w_ff1.astype(BF16),
        w_ff2=w_ff2.astype(BF16))
    outs = []
    for group, x, mem in zip(cfg.groups, (x_prompt, x_sample), (mem_prompt, mem_sample)):
        sub = cfg._replace(groups=(group,))
        outs.append(_forward(sub, x.reshape(-1, d), mem.reshape(-1, d), p).reshape(x.shape))
    return tuple(outs)


def kernel(x_prompt, x_sample, mem_prompt, mem_sample, mix_norm_pre, mix_norm_post, w_in, conv_w, conv_b, lru_wa,
           lru_ba, lru_wx, lru_bx, lru_lam, swa_sink, group_norm, w_out, mem_norm_pre, mem_norm_post, mem_kv_norm,
           w_mq, w_mk, w_mv, w_mo, ffn_norm_pre, ffn_norm_post, w_ff1, w_ff2):
    return _run(_CFG, x_prompt, x_sample, mem_prompt, mem_sample, mix_norm_pre, mix_norm_post, w_in, conv_w,
                conv_b, lru_wa, lru_ba, lru_wx, lru_bx, lru_lam, swa_sink, group_norm, w_out, mem_norm_pre,
                mem_norm_post, mem_kv_norm, w_mq, w_mk, w_mv, w_mo, ffn_norm_pre, ffn_norm_post, w_ff1, w_ff2)
```

```python
import functools
from typing import NamedTuple

import numpy as np
import jax
import jax.numpy as jnp
from jax import lax
from jax.experimental import pallas as pl
from jax.experimental.pallas import tpu as pltpu

F32 = jnp.float32
BF16 = jnp.bfloat16

D_MODEL = 2048
BATCH = 8
SEQ = 4096
DEPTH = 4
DEC_BATCH = 1
DEC_SEQ = 16384
HEAD_DIM = 128
LRU_WIDTH = 512
LRU_BLOCKS = 4
LRU_BLOCK_WIDTH = LRU_WIDTH // LRU_BLOCKS
CONV_WIDTH = 4
CONV_LEFT = 2
LRU_C = 8.0
DIL_HEADS = 6
DIL_PATTERNS = ((128, 1), (512, 4), (2048, 16))
SWA_HEADS = 6
SWA_KV_HEADS = 2
SWA_WINDOW = 128
DIL_WIDTH = DIL_HEADS * HEAD_DIM
SWA_WIDTH = SWA_HEADS * HEAD_DIM
SWA_KV_WIDTH = SWA_KV_HEADS * HEAD_DIM
MIX_WIDTH = LRU_WIDTH + DIL_WIDTH + SWA_WIDTH
IN_WIDTH = 2 * LRU_WIDTH + 3 * DIL_WIDTH + SWA_WIDTH + 2 * SWA_KV_WIDTH
N_MEM = 256
MEM_HEADS = 4
MEM_WIDTH = MEM_HEADS * HEAD_DIM
D_FF = 4 * D_MODEL
EPS = 1e-6

_NEG = -1e30
_QK_SCALE = HEAD_DIM ** -0.5
_HALO_ROWS = 8
_NORM_ROWS = 128
_SCAN_GROUP = 8
_DIL_W = DIL_PATTERNS[0][0] // (2 * DIL_PATTERNS[0][1])
assert all(wn // (2 * d) == _DIL_W for wn, d in DIL_PATTERNS)
_DIL_QROWS_MAX = 128
_PERM_ROWS = 256
_V7X_VMEM_BYTES = 64 * 1024 * 1024
_VMEM_LIMIT = _V7X_VMEM_BYTES - 3 * 1024 * 1024
_NT = (((1,), (1,)), ((), ()))


class _Cfg(NamedTuple):
    d_model: int
    d_ff: int
    depth: int
    groups: tuple
    n_mem: int
    tm: int
    tm_ffn: int
    tf: int
    lru_chunk: int
    lru_rows: int
    swa_rows: int
    dil_rows: int
    swa_batch: int
    dil_batch: int


_CFG = _Cfg(d_model=D_MODEL, d_ff=D_FF, depth=DEPTH, groups=((BATCH, SEQ), (DEC_BATCH, DEC_SEQ)),
            n_mem=N_MEM, tm=512, tm_ffn=1024, tf=1024, lru_chunk=1024, lru_rows=256, swa_rows=512,
            dil_rows=1024, swa_batch=2, dil_batch=4)


def _sequences(cfg):
    out, start = [], 0
    for n, length in cfg.groups:
        for _ in range(n):
            out.append((start, length))
            start += length
    return out, start


def _chunk_flags(cfg, rows):
    seqs, total = _sequences(cfg)
    starts = {s for s, _ in seqs}
    ends = {s + l for s, l in seqs}
    for s, l in seqs:
        assert l % rows == 0, (l, rows)
    n = total // rows
    flags = np.zeros((n,), np.int32)
    for c in range(n):
        flags[c] = (1 if c * rows in starts else 0) | (2 if (c + 1) * rows in ends else 0)
    return jnp.asarray(flags)


def _params(semantics):
    return pltpu.CompilerParams(dimension_semantics=semantics, vmem_limit_bytes=_VMEM_LIMIT)


def _rms(x, g):
    ms = jnp.mean(x * x, axis=-1, keepdims=True)
    return x * lax.rsqrt(ms + EPS) * g


def _row_blocks(n):
    return [slice(r, min(r + _NORM_ROWS, n)) for r in range(0, n, _NORM_ROWS)]


def _resident(shape, layer=None):
    if layer is None:
        return pl.BlockSpec(shape, lambda *_: (0,) * len(shape), pipeline_mode=pl.Buffered(1))
    return pl.BlockSpec((None,) + tuple(shape), lambda *_: (layer,) + (0,) * len(shape),
                        pipeline_mode=pl.Buffered(1))


def _inproj_plan():
    lru_w = 2 * LRU_WIDTH
    dil_w = 3 * DIL_WIDTH
    segs = [
        (0, lru_w, 0, None),
        (lru_w, lru_w + DIL_WIDTH, 1, _QK_SCALE),
        (lru_w + DIL_WIDTH, lru_w + dil_w, 1, None),
        (lru_w + dil_w, lru_w + dil_w + SWA_WIDTH, 2, _QK_SCALE),
        (lru_w + dil_w + SWA_WIDTH, IN_WIDTH, 2, None),
    ]
    base = {0: 0, 1: lru_w, 2: lru_w + dil_w}
    plan = []
    for c0, c1, oi, scale in segs:
        c = c0
        while c < c1:
            n = min(512, c1 - c)
            plan.append((c, c + n, oi, c - base[oi], scale))
            c += n
    return tuple(plan)


def _inproj_kernel(x_ref, g_ref, w_ref, lru_ref, qkvb_ref, qkvc_ref, xn_scr, *, plan):
    for rows in _row_blocks(x_ref.shape[0]):
        xn_scr[rows, :] = _rms(x_ref[rows, :], g_ref[...]).astype(BF16)
    outs = (lru_ref, qkvb_ref, qkvc_ref)
    for c0, c1, oi, o0, scale in plan:
        acc = jnp.dot(xn_scr[...], w_ref[:, c0:c1], preferred_element_type=F32)
        if scale is not None:
            acc = acc * scale
        outs[oi][:, o0:o0 + (c1 - c0)] = acc.astype(outs[oi].dtype)


def _inproj(cfg, x, g, w, layer):
    t, d = x.shape
    tm = cfg.tm
    widths = (2 * LRU_WIDTH, 3 * DIL_WIDTH, SWA_WIDTH + 2 * SWA_KV_WIDTH)
    return pl.pallas_call(
        functools.partial(_inproj_kernel, plan=_inproj_plan()),
        grid=(t // tm,),
        in_specs=[
            pl.BlockSpec((tm, d), lambda i: (i, 0)),
            _resident((1, d)),
            _resident((d, IN_WIDTH), layer),
        ],
        out_specs=[pl.BlockSpec((tm, wd), lambda i: (i, 0)) for wd in widths],
        out_shape=[
            jax.ShapeDtypeStruct((t, widths[0]), F32),
            jax.ShapeDtypeStruct((t, widths[1]), BF16),
            jax.ShapeDtypeStruct((t, widths[2]), BF16),
        ],
        scratch_shapes=[pltpu.VMEM((tm, d), BF16)],
        compiler_params=_params(("parallel",)),
        name="mixer_inproj",
    )(x, g, w)


def _lru_fill_halo(first, last, xa_ref, xp_ref, xn_ref, xext, lc):
    xext[_HALO_ROWS:_HALO_ROWS + lc, :] = xa_ref[...]

    @pl.when(first)
    def _():
        xext[0:_HALO_ROWS, :] = jnp.zeros((_HALO_ROWS, LRU_WIDTH), F32)

    @pl.when(jnp.logical_not(first))
    def _():
        xext[0:_HALO_ROWS, :] = xp_ref[...]

    @pl.when(last)
    def _():
        xext[_HALO_ROWS + lc:, :] = jnp.zeros((_HALO_ROWS, LRU_WIDTH), F32)

    @pl.when(jnp.logical_not(last))
    def _():
        xext[_HALO_ROWS + lc:, :] = xn_ref[...]


def _lru_conv(xext, cw_ref, cb_ref, xc_ref, lc, rb):
    cb = cb_ref[...]
    taps = [cw_ref[j:j + 1, :] for j in range(CONV_WIDTH)]
    for blk in range(lc // rb):
        r0 = blk * rb
        xc_ref[r0:r0 + rb, :] = cb + sum(
            taps[j] * xext[r0 + _HALO_ROWS - CONV_LEFT + j:r0 + _HALO_ROWS - CONV_LEFT + j + rb, :]
            for j in range(CONV_WIDTH))


def _lru_gates(xc_ref, wg_ref, ba_ref, bx_ref, lam_ref, a_scr, u_scr, lc, rb):
    lam = lam_ref[...]
    neg = -lam
    softplus = jnp.maximum(neg, 0.0) + jnp.log1p(jnp.exp(-jnp.abs(neg)))
    for blk in range(lc // rb):
        r0 = blk * rb
        xc = xc_ref[r0:r0 + rb, :]
        xcb = xc.astype(BF16)
        for n in range(LRU_BLOCKS):
            cs = slice(n * LRU_BLOCK_WIDTH, (n + 1) * LRU_BLOCK_WIDTH)
            g = jnp.dot(xcb[:, cs], wg_ref[n], preferred_element_type=F32)
            r = jax.nn.sigmoid(g[:, :LRU_BLOCK_WIDTH] + ba_ref[:, cs])
            ig = jax.nn.sigmoid(g[:, LRU_BLOCK_WIDTH:] + bx_ref[:, cs])
            neg_log_a = (LRU_C * r) * softplus[:, cs]
            a = jnp.exp(-neg_log_a)
            one_minus_a2 = jnp.tanh(neg_log_a) * (1.0 + a * a)
            u = jnp.sqrt(one_minus_a2) * (ig * xc[:, cs])
            groups = slice(r0 // _SCAN_GROUP, (r0 + rb) // _SCAN_GROUP)
            a_scr[groups, :, cs] = a.reshape(rb // _SCAN_GROUP, _SCAN_GROUP, LRU_BLOCK_WIDTH)
            u_scr[groups, :, cs] = u.reshape(rb // _SCAN_GROUP, _SCAN_GROUP, LRU_BLOCK_WIDTH)


def _lru_scan(reset, a_scr, u_scr, h_dst, carry, lc, reverse):
    @pl.when(reset)
    def _():
        carry[...] = jnp.zeros((1, LRU_WIDTH), F32)

    def group(i, h):
        g = lc // _SCAN_GROUP - 1 - i if reverse else i
        order = [_SCAN_GROUP - 1 - j if reverse else j for j in range(_SCAN_GROUP)]
        p = a_scr[g, order[0]:order[0] + 1, :]
        q = u_scr[g, order[0]:order[0] + 1, :]
        outs = [p * h + q]
        for j in order[1:]:
            a = a_scr[g, j:j + 1, :]
            q = a * q + u_scr[g, j:j + 1, :]
            p = a * p
            outs.append(p * h + q)
        for j, o in zip(order, outs):
            h_dst[g, j:j + 1, :] = o
        return outs[-1]

    carry[...] = lax.fori_loop(0, lc // _SCAN_GROUP, group, carry[...], unroll=2)


def _lru_fwd_kernel(flags_ref, xa_ref, xp_ref, xn_ref, cw_ref, cb_ref, wg_ref, ba_ref, bx_ref, lam_ref,
                    hf_ref, xc_ref, xext, a_scr, u_scr, carry, *, lc, rb):
    fl = flags_ref[pl.program_id(0)]
    first = (fl & 1) != 0
    last = (fl & 2) != 0
    _lru_fill_halo(first, last, xa_ref, xp_ref, xn_ref, xext, lc)
    _lru_conv(xext, cw_ref, cb_ref, xc_ref, lc, rb)
    _lru_gates(xc_ref, wg_ref, ba_ref, bx_ref, lam_ref, a_scr, u_scr, lc, rb)
    _lru_scan(first, a_scr, u_scr, hf_ref, carry, lc, reverse=False)


def _lru_bwd_kernel(flags_ref, xc_ref, gate_ref, hf_ref, wg_ref, ba_ref, bx_ref, lam_ref, gn_ref, y_ref,
                    a_scr, u_scr, h_scr, carry, *, lc, rb, nchunks):
    fl = flags_ref[nchunks - 1 - pl.program_id(0)]
    last = (fl & 2) != 0
    _lru_gates(xc_ref, wg_ref, ba_ref, bx_ref, lam_ref, a_scr, u_scr, lc, rb)
    _lru_scan(last, a_scr, u_scr, h_scr, carry, lc, reverse=True)
    for blk in range(lc // rb):
        rows = slice(blk * rb, (blk + 1) * rb)
        groups = slice(blk * rb // _SCAN_GROUP, (blk + 1) * rb // _SCAN_GROUP)
        h = (hf_ref[groups] + h_scr[groups]).reshape(rb, LRU_WIDTH)
        y = h * jax.nn.gelu(gate_ref[rows, :])
        y_ref[rows, :] = _rms(y, gn_ref[...]).astype(BF16)


def _lru(cfg, lru_in, cw, cb, wg, ba, bx, lam, gn):
    t = lru_in.shape[0]
    lc, rb = cfg.lru_chunk, cfg.lru_rows
    nchunks = t // lc
    hb = lc // _HALO_ROWS
    nhalo = t // _HALO_ROWS
    flags = _chunk_flags(cfg, lc)
    row = lambda: _resident((1, LRU_WIDTH))

    def specs(chunk_of):
        return [
            pl.BlockSpec((lc, LRU_WIDTH), lambda i, f: (chunk_of(i), 0)),
            pl.BlockSpec((_HALO_ROWS, LRU_WIDTH), lambda i, f: (jnp.maximum(chunk_of(i) * hb - 1, 0), 0)),
            pl.BlockSpec((_HALO_ROWS, LRU_WIDTH), lambda i, f: (jnp.minimum((chunk_of(i) + 1) * hb, nhalo - 1), 0)),
        ]

    conv_weights = [_resident((CONV_WIDTH, LRU_WIDTH)), row()]

    def weights(d):
        return [
            pl.BlockSpec((None, LRU_BLOCKS, LRU_BLOCK_WIDTH, 2 * LRU_BLOCK_WIDTH), lambda i, f: (d, 0, 0, 0)),
            pl.BlockSpec((None, 1, LRU_WIDTH), lambda i, f: (d, 0, 0)),
            pl.BlockSpec((None, 1, LRU_WIDTH), lambda i, f: (d, 0, 0)),
            pl.BlockSpec((None, 1, LRU_WIDTH), lambda i, f: (d, 0, 0)),
        ]

    grouped = (lc // _SCAN_GROUP, _SCAN_GROUP, LRU_WIDTH)
    scratch = [pltpu.VMEM(grouped, F32), pltpu.VMEM(grouped, F32)]
    carry = [pltpu.VMEM((1, LRU_WIDTH), F32)]

    fwd = lambda i: i
    hf, xc = pl.pallas_call(
        functools.partial(_lru_fwd_kernel, lc=lc, rb=rb),
        grid_spec=pltpu.PrefetchScalarGridSpec(
            num_scalar_prefetch=1, grid=(nchunks,),
            in_specs=specs(fwd) + conv_weights + weights(0),
            out_specs=[pl.BlockSpec(grouped, lambda i, f: (i, 0, 0)),
                       pl.BlockSpec((lc, LRU_WIDTH), lambda i, f: (i, 0))],
            scratch_shapes=[pltpu.VMEM((lc + 2 * _HALO_ROWS, LRU_WIDTH), F32)] + scratch + carry),
        out_shape=[jax.ShapeDtypeStruct((t // _SCAN_GROUP, _SCAN_GROUP, LRU_WIDTH), F32),
                   jax.ShapeDtypeStruct((t, LRU_WIDTH), F32)],
        compiler_params=_params(("arbitrary",)),
        name="lru_forward",
    )(flags, lru_in, lru_in, lru_in, cw, cb, wg, ba, bx, lam)

    bwd = lambda i: nchunks - 1 - i
    return pl.pallas_call(
        functools.partial(_lru_bwd_kernel, lc=lc, rb=rb, nchunks=nchunks),
        grid_spec=pltpu.PrefetchScalarGridSpec(
            num_scalar_prefetch=1, grid=(nchunks,),
            in_specs=[
                pl.BlockSpec((lc, LRU_WIDTH), lambda i, f: (bwd(i), 0)),
                pl.BlockSpec((lc, LRU_WIDTH), lambda i, f: (bwd(i), 1)),
                pl.BlockSpec(grouped, lambda i, f: (bwd(i), 0, 0)),
            ] + weights(1) + [row()],
            out_specs=pl.BlockSpec((lc, LRU_WIDTH), lambda i, f: (bwd(i), 0)),
            scratch_shapes=scratch + [pltpu.VMEM(grouped, F32)] + carry),
        out_shape=jax.ShapeDtypeStruct((t, LRU_WIDTH), BF16),
        compiler_params=_params(("arbitrary",)),
        name="lru_backward",
    )(flags, xc, lru_in, hf, wg, ba, bx, lam, gn)


def _alibi_slopes(n):
    return [2.0 ** (-8.0 * (i + 1) / n) for i in range(n)]


def _band_bias(wq, halo, dist_scale, hq, hkv):
    rep = hq // hkv
    slopes = _alibi_slopes(hq)
    qi = np.arange(wq)[:, None]
    kj = np.arange(wq + 2 * halo)[None, :]
    rel = np.abs(kj - halo - qi)
    out = np.empty((hkv, rep * wq, wq + 2 * halo), np.float32)
    for g in range(hkv):
        for r in range(rep):
            out[g, r * wq:(r + 1) * wq] = np.where(rel <= halo, -slopes[g * rep + r] * dist_scale * rel, _NEG)
    return out


def _edge_penalties(fl, wq, halo):
    col = lax.broadcasted_iota(jnp.int32, (1, wq + 2 * halo), 1)
    pen_first = jnp.where(col < halo, jnp.where((fl & 1) != 0, _NEG, 0.0), 0.0)
    pen_last = jnp.where(col >= wq + halo, jnp.where((fl & 2) != 0, _NEG, 0.0), 0.0)
    return pen_first, pen_last


def _attend(qs, ks, vs, biases):
    scores = [lax.dot_general(q, k, _NT, preferred_element_type=F32) for q, k in zip(qs, ks)]
    ms, ps = [], []
    for s, b in zip(scores, biases):
        if b is not None:
            s = s + b
        m = jnp.max(s, axis=-1, keepdims=True)
        ms.append(m)
        ps.append(jnp.exp(s - m).astype(BF16))
    ls, accs = [], []
    for p, v in zip(ps, vs):
        v1 = jnp.concatenate([v, jnp.ones(v.shape, v.dtype)], axis=1)
        out = jnp.dot(p, v1, preferred_element_type=F32)
        accs.append(out[:, :HEAD_DIM])
        ls.append(out[:, HEAD_DIM:])
    return ms, ls, accs


def _swa_kernel(flags_ref, q_ref, kc_ref, kp_ref, kn_ref, vc_ref, vp_ref, vn_ref, bias_ref, sink_ref, gn_ref,
                y_out, kbuf, vbuf, *, w, nsub, nb):
    rep = SWA_HEADS // SWA_KV_HEADS
    rows_total = nsub * w
    pen_first, pen_last = _edge_penalties(flags_ref[pl.program_id(0)], w, w)
    kbuf[0:w, :] = kp_ref[...]
    kbuf[w:w + rows_total, :] = kc_ref[...]
    kbuf[w + rows_total:, :] = kn_ref[...]
    vbuf[0:w, :] = vp_ref[...]
    vbuf[w:w + rows_total, :] = vc_ref[...]
    vbuf[w + rows_total:, :] = vn_ref[...]

    for j0 in range(0, nsub, nb):
        blocks = list(range(j0, min(j0 + nb, nsub)))
        qs, ks, vs, bs = [], [], [], []
        for j in blocks:
            rows = slice(j * w, (j + 1) * w)
            for g in range(SWA_KV_HEADS):
                gs = slice(g * HEAD_DIM, (g + 1) * HEAD_DIM)
                qs.append(jnp.concatenate(
                    [q_ref[rows, (g * rep + r) * HEAD_DIM:(g * rep + r + 1) * HEAD_DIM] for r in range(rep)], axis=0))
                ks.append(kbuf[j * w:(j + 3) * w, gs])
                vs.append(vbuf[j * w:(j + 3) * w, gs])
                b = bias_ref[g]
                if j == 0:
                    b = b + pen_first
                if j == nsub - 1:
                    b = b + pen_last
                bs.append(b)
        ms, ls, accs = _attend(qs, ks, vs, bs)
        for bi, j in enumerate(blocks):
            heads_out = []
            for g in range(SWA_KV_HEADS):
                idx = bi * SWA_KV_HEADS + g
                for r in range(rep):
                    h = g * rep + r
                    part = slice(r * w, (r + 1) * w)
                    m, l, acc = ms[idx][part], ls[idx][part], accs[idx][part]
                    factor = jax.nn.sigmoid(m + jnp.log(l) - sink_ref[:, h:h + 1])
                    heads_out.append((acc / l) * factor)
            y = jnp.concatenate(heads_out, axis=1)
            y_out[j * w:(j + 1) * w, :] = _rms(y, gn_ref[...]).astype(BF16)


def _swa(cfg, qkvc, gn, sink):
    t, c = qkvc.shape
    w = SWA_WINDOW
    rows = cfg.swa_rows
    nsub = rows // w
    nchunks = t // rows
    nblk = t // w
    qw, kvw = SWA_WIDTH, SWA_KV_WIDTH
    rep = SWA_HEADS // SWA_KV_HEADS
    assert qw % kvw == 0 and c == qw + 2 * kvw
    kcol, vcol = qw // kvw, qw // kvw + 1
    prev = lambda col: (lambda i, f: (jnp.maximum(i * nsub - 1, 0), col))
    nxt = lambda col: (lambda i, f: (jnp.minimum((i + 1) * nsub, nblk - 1), col))
    return pl.pallas_call(
        functools.partial(_swa_kernel, w=w, nsub=nsub, nb=cfg.swa_batch),
        grid_spec=pltpu.PrefetchScalarGridSpec(
            num_scalar_prefetch=1, grid=(nchunks,),
            in_specs=[
                pl.BlockSpec((rows, qw), lambda i, f: (i, 0)),
                pl.BlockSpec((rows, kvw), lambda i, f: (i, kcol)),
                pl.BlockSpec((w, kvw), prev(kcol)),
                pl.BlockSpec((w, kvw), nxt(kcol)),
                pl.BlockSpec((rows, kvw), lambda i, f: (i, vcol)),
                pl.BlockSpec((w, kvw), prev(vcol)),
                pl.BlockSpec((w, kvw), nxt(vcol)),
                _resident((SWA_KV_HEADS, rep * w, 3 * w)),
                _resident((1, 128)),
                _resident((1, qw)),
            ],
            out_specs=pl.BlockSpec((rows, qw), lambda i, f: (i, 0)),
            scratch_shapes=[pltpu.VMEM((rows + 2 * w, kvw), BF16), pltpu.VMEM((rows + 2 * w, kvw), BF16)]),
        out_shape=jax.ShapeDtypeStruct((t, qw), BF16),
        compiler_params=_params(("parallel",)),
        name="windowed_gqa_sink",
    )(_chunk_flags(cfg, rows), qkvc, qkvc, qkvc, qkvc, qkvc, qkvc, qkvc,
      jnp.asarray(_band_bias(w, w, 1, SWA_HEADS, SWA_KV_HEADS)), sink, gn)


def _perm_matrix(d):
    n = _PERM_ROWS
    per = n // d
    p = np.zeros((n, n), np.float32)
    for r in range(d):
        for m in range(per):
            p[r * per + m, d * m + r] = 1.0
    return p


def _dilated_qrows(c, d):
    return min(_DIL_QROWS_MAX, c // d)


def _dilated_kernel(flags_ref, q_ref, kp_ref, kc_ref, kn_ref, vp_ref, vc_ref, vn_ref, bias0, bias1, bias2,
                    perm_ref, gn_ref, y_ref, qd, kd, vd, bv0, bv1, bv2, acc_nat, m_nat, l_nat, *, c, batch_rows):
    w = _DIL_W
    nh = DIL_HEADS
    fl = flags_ref[pl.program_id(0)]
    bias_refs = (bias0, bias1, bias2)
    biasv = (bv0, bv1, bv2)
    for p, (_, d) in enumerate(DIL_PATTERNS):
        wq = _dilated_qrows(c, d)
        pen_first, pen_last = _edge_penalties(fl, wq, w)
        for h in range(nh):
            b = bias_refs[p][h]
            if c // (d * wq) == 1:
                biasv[p][0, h] = b + pen_first + pen_last
            else:
                biasv[p][0, h] = b
                biasv[p][1, h] = b + pen_first
                biasv[p][2, h] = b + pen_last

    heads = [slice(h * HEAD_DIM, (h + 1) * HEAD_DIM) for h in range(nh)]

    def run_pattern(p, d, qsrc, ksrc, vsrc, qstride, kstride, first_pattern, last_pattern):
        wq = _dilated_qrows(c, d)
        win = wq + 2 * w
        nblk = c // (d * wq)
        nb = max(1, batch_rows // wq)
        assert (c // wq) % nb == 0
        assert d == 1 or not last_pattern

        def body(it, carry):
            qs, ks, vs, bs, where = [], [], [], [], []
            for b in range(nb):
                sb = it * nb + b
                r = sb // nblk
                s = sb % nblk
                var = 0 if nblk == 1 else jnp.where(s == 0, 1, 0) + jnp.where(s == nblk - 1, 2, 0)
                q0 = pl.multiple_of(r * qstride + s * wq, w)
                k0 = pl.multiple_of(r * kstride + s * wq, w)
                where.append(d * wq * s + r)
                for h in range(nh):
                    qs.append(qsrc[pl.ds(q0, wq), heads[h]])
                    ks.append(ksrc[pl.ds(k0, win), heads[h]])
                    vs.append(vsrc[pl.ds(k0, win), heads[h]])
                    bs.append(biasv[p][var, h])
            ms, ls, accs = _attend(qs, ks, vs, bs)
            for b in range(nb):
                nat = pl.ds(where[b], wq, stride=d) if d > 1 else pl.ds(pl.multiple_of(where[b], w), wq)
                outs = []
                for h in range(nh):
                    m, l, acc = ms[b * nh + h], ls[b * nh + h], accs[b * nh + h]
                    if not first_pattern:
                        m_p = m_nat[h, nat, :]
                        m_n = jnp.maximum(m_p, m)
                        alpha = jnp.exp(m_p - m_n)
                        beta = jnp.exp(m - m_n)
                        l = alpha * l_nat[h, nat, :] + beta * l
                        acc = alpha * acc_nat[h, nat, :] + beta * acc
                        m = m_n
                    if last_pattern:
                        outs.append(acc / l)
                    else:
                        acc_nat[h, nat, :] = acc
                        m_nat[h, nat, :] = jnp.broadcast_to(m, (wq, HEAD_DIM))
                        l_nat[h, nat, :] = jnp.broadcast_to(l, (wq, HEAD_DIM))
                if last_pattern:
                    y_ref[nat, :] = _rms(jnp.concatenate(outs, axis=1), gn_ref[...]).astype(BF16)
            return carry

        lax.fori_loop(0, c // (wq * nb), body, 0)

    def deinterleave(p, d):
        per = _PERM_ROWS // d
        halo_groups = (d * w) // _PERM_ROWS
        chunk_groups = c // _PERM_ROWS
        kstride = (chunk_groups + 2 * halo_groups) * per
        perm = perm_ref[p - 1]

        def move(src, row0, dst, g, stride):
            res = jnp.dot(perm, src[row0:row0 + _PERM_ROWS, :], preferred_element_type=F32).astype(BF16)
            for r in range(d):
                dst[r * stride + g * per:r * stride + (g + 1) * per, :] = res[r * per:(r + 1) * per]

        for g in range(chunk_groups):
            move(q_ref, g * _PERM_ROWS, qd, g, c // d)
        for prev_ref, cur_ref, next_ref, dst in ((kp_ref, kc_ref, kn_ref, kd), (vp_ref, vc_ref, vn_ref, vd)):
            srcs = ([(prev_ref, c - (halo_groups - g) * _PERM_ROWS) for g in range(halo_groups)]
                    + [(cur_ref, g * _PERM_ROWS) for g in range(chunk_groups)]
                    + [(next_ref, g * _PERM_ROWS) for g in range(halo_groups)])
            for g, (src, row0) in enumerate(srcs):
                move(src, row0, dst, g, kstride)
        return c // d, kstride

    order = sorted(range(len(DIL_PATTERNS)), key=lambda p: -DIL_PATTERNS[p][1])
    for idx, p in enumerate(order):
        d = DIL_PATTERNS[p][1]
        first, last = idx == 0, idx == len(order) - 1
        if d == 1:
            for prev_ref, cur_ref, next_ref, dst in ((kp_ref, kc_ref, kn_ref, kd), (vp_ref, vc_ref, vn_ref, vd)):
                dst[0:w, :] = prev_ref[c - w:c, :]
                dst[w:w + c, :] = cur_ref[...]
                dst[w + c:2 * w + c, :] = next_ref[0:w, :]
            run_pattern(p, d, q_ref, kd, vd, 0, 0, first, last)
        else:
            qstride, kstride = deinterleave(p, d)
            run_pattern(p, d, qd, kd, vd, qstride, kstride, first, last)


def _dilated(cfg, qkvb, gn):
    t = qkvb.shape[0]
    c = cfg.dil_rows
    w = _DIL_W
    n = t // c
    dmax = max(d for _, d in DIL_PATTERNS)
    assert DIL_PATTERNS[0][1] == 1 and c % (dmax * w) == 0 and c >= dmax * w and c % _PERM_ROWS == 0
    assert all((d * w) % _PERM_ROWS == 0 for _, d in DIL_PATTERNS[1:])
    assert len(DIL_PATTERNS) == 3
    biases = [_band_bias(_dilated_qrows(c, d), w, d, DIL_HEADS, DIL_HEADS) for _, d in DIL_PATTERNS]
    variants = [1 if c // (d * _dilated_qrows(c, d)) == 1 else 3 for _, d in DIL_PATTERNS]
    perm = np.stack([_perm_matrix(d) for _, d in DIL_PATTERNS[1:]])
    blk = (c, DIL_WIDTH)
    prev = lambda col: (lambda i, f: (jnp.maximum(i - 1, 0), col))
    cur = lambda col: (lambda i, f: (i, col))
    nxt = lambda col: (lambda i, f: (jnp.minimum(i + 1, n - 1), col))
    return pl.pallas_call(
        functools.partial(_dilated_kernel, c=c, batch_rows=cfg.dil_batch * w),
        grid_spec=pltpu.PrefetchScalarGridSpec(
            num_scalar_prefetch=1, grid=(n,),
            in_specs=[
                pl.BlockSpec(blk, cur(0)),
                pl.BlockSpec(blk, prev(1)), pl.BlockSpec(blk, cur(1)), pl.BlockSpec(blk, nxt(1)),
                pl.BlockSpec(blk, prev(2)), pl.BlockSpec(blk, cur(2)), pl.BlockSpec(blk, nxt(2)),
                _resident(biases[0].shape), _resident(biases[1].shape), _resident(biases[2].shape),
                _resident(perm.shape), _resident((1, DIL_WIDTH)),
            ],
            out_specs=pl.BlockSpec(blk, cur(0)),
            scratch_shapes=[
                pltpu.VMEM((c, DIL_WIDTH), BF16),
                pltpu.VMEM((3 * c, DIL_WIDTH), BF16), pltpu.VMEM((3 * c, DIL_WIDTH), BF16),
                pltpu.VMEM((variants[0],) + biases[0].shape, F32),
                pltpu.VMEM((variants[1],) + biases[1].shape, F32),
                pltpu.VMEM((variants[2],) + biases[2].shape, F32),
                pltpu.VMEM((DIL_HEADS, c, HEAD_DIM), F32),
                pltpu.VMEM((DIL_HEADS, c, HEAD_DIM), F32),
                pltpu.VMEM((DIL_HEADS, c, HEAD_DIM), F32),
            ]),
        out_shape=jax.ShapeDtypeStruct((t, DIL_WIDTH), BF16),
        compiler_params=_params(("parallel",)),
        name="dilated_attention",
    )(_chunk_flags(cfg, c), qkvb, qkvb, qkvb, qkvb, qkvb, qkvb, qkvb,
      jnp.asarray(biases[0]), jnp.asarray(biases[1]), jnp.asarray(biases[2]), jnp.asarray(perm, dtype=BF16), gn)


def _outproj_kernel(ya_ref, yb_ref, yc_ref, x_ref, w_ref, g_ref, o_ref, y_scr, *, nchunk):
    d = o_ref.shape[1]
    b0, b1 = LRU_WIDTH, LRU_WIDTH + DIL_WIDTH
    for c in range(0, d, nchunk):
        cs = slice(c, c + nchunk)
        acc = jnp.dot(ya_ref[...], w_ref[0:b0, cs], preferred_element_type=F32)
        acc += jnp.dot(yb_ref[...], w_ref[b0:b1, cs], preferred_element_type=F32)
        acc += jnp.dot(yc_ref[...], w_ref[b1:, cs], preferred_element_type=F32)
        y_scr[:, cs] = acc
    for rows in _row_blocks(o_ref.shape[0]):
        o_ref[rows, :] = x_ref[rows, :] + _rms(y_scr[rows, :], g_ref[...])


def _outproj(cfg, ya, yb, yc, x, w, g, layer):
    t, d = x.shape
    tm = cfg.tm
    return pl.pallas_call(
        functools.partial(_outproj_kernel, nchunk=min(512, d)),
        grid=(t // tm,),
        in_specs=[
            pl.BlockSpec((tm, LRU_WIDTH), lambda i: (i, 0)),
            pl.BlockSpec((tm, DIL_WIDTH), lambda i: (i, 0)),
            pl.BlockSpec((tm, SWA_WIDTH), lambda i: (i, 0)),
            pl.BlockSpec((tm, d), lambda i: (i, 0)),
            _resident((MIX_WIDTH, d), layer),
            _resident((1, d)),
        ],
        out_specs=pl.BlockSpec((tm, d), lambda i: (i, 0)),
        out_shape=jax.ShapeDtypeStruct((t, d), F32),
        scratch_shapes=[pltpu.VMEM((tm, d), F32)],
        compiler_params=_params(("parallel",)),
        name="mixer_outproj",
    )(ya, yb, yc, x, w, g)


def _memkv_kernel(mem_ref, g_ref, w_ref, k_ref, v_ref):
    mn = _rms(mem_ref[...], g_ref[...]).astype(BF16)
    kv = jnp.dot(mn, w_ref[...], preferred_element_type=F32)
    k_ref[...] = kv[:, :MEM_WIDTH].astype(BF16)
    v_ref[...] = kv[:, MEM_WIDTH:].astype(BF16)


def _memkv(cfg, mem, g, wkv, layer):
    rows, d = mem.shape
    nm = cfg.n_mem
    shape = jax.ShapeDtypeStruct((rows, MEM_WIDTH), BF16)
    return pl.pallas_call(
        _memkv_kernel,
        grid=(rows // nm,),
        in_specs=[pl.BlockSpec((nm, d), lambda i: (i, 0)), _resident((1, d)),
                  _resident((d, 2 * MEM_WIDTH), layer)],
        out_specs=[pl.BlockSpec((nm, MEM_WIDTH), lambda i: (i, 0))] * 2,
        out_shape=[shape, shape],
        compiler_params=_params(("parallel",)),
        name="memory_kv",
    )(mem, g, wkv)


def _cross_kernel(seq_ref, x_ref, gpre_ref, wq_ref, k_ref, v_ref, wo_ref, gpost_ref, o_ref, xn_scr, o_scr, *,
                  nchunk):
    del seq_ref
    tm, d = x_ref.shape
    for rows in _row_blocks(tm):
        xn_scr[rows, :] = _rms(x_ref[rows, :], gpre_ref[...]).astype(BF16)
    q = (jnp.dot(xn_scr[...], wq_ref[...], preferred_element_type=F32) * _QK_SCALE).astype(BF16)
    heads = [slice(h * HEAD_DIM, (h + 1) * HEAD_DIM) for h in range(MEM_HEADS)]
    _, ls, accs = _attend([q[:, hs] for hs in heads], [k_ref[:, hs] for hs in heads],
                          [v_ref[:, hs] for hs in heads], [None] * MEM_HEADS)
    for hs, l, acc in zip(heads, ls, accs):
        o_scr[:, hs] = (acc / l).astype(BF16)
    for c in range(0, d, nchunk):
        cs = slice(c, min(c + nchunk, d))
        o_ref[:, cs] = jnp.dot(o_scr[...], wo_ref[:, cs], preferred_element_type=F32)
    for rows in _row_blocks(tm):
        o_ref[rows, :] = x_ref[rows, :] + _rms(o_ref[rows, :], gpost_ref[...])


def _cross(cfg, x, gpre, wq, kmem, vmem, wo, gpost, layer):
    t, d = x.shape
    tm, nm = cfg.tm, cfg.n_mem
    seqs, _ = _sequences(cfg)
    seq_of_tile = []
    for si, (_, length) in enumerate(seqs):
        assert length % tm == 0
        seq_of_tile += [si] * (length // tm)
    seq_of_tile = jnp.asarray(np.asarray(seq_of_tile, np.int32))
    return pl.pallas_call(
        functools.partial(_cross_kernel, nchunk=512),
        grid_spec=pltpu.PrefetchScalarGridSpec(
            num_scalar_prefetch=1, grid=(t // tm,),
            in_specs=[
                pl.BlockSpec((tm, d), lambda i, s: (i, 0)),
                _resident((1, d)),
                _resident((d, MEM_WIDTH), layer),
                pl.BlockSpec((nm, MEM_WIDTH), lambda i, s: (s[i], 0)),
                pl.BlockSpec((nm, MEM_WIDTH), lambda i, s: (s[i], 0)),
                _resident((MEM_WIDTH, d), layer),
                _resident((1, d)),
            ],
            out_specs=pl.BlockSpec((tm, d), lambda i, s: (i, 0)),
            scratch_shapes=[pltpu.VMEM((tm, d), BF16), pltpu.VMEM((tm, MEM_WIDTH), BF16)]),
        out_shape=jax.ShapeDtypeStruct((t, d), F32),
        compiler_params=_params(("parallel",)),
        name="memory_cross_attention",
    )(seq_of_tile, x, gpre, wq, kmem, vmem, wo, gpost)


def _ffn_kernel(x_ref, gpre_ref, w1_ref, w2_ref, gpost_ref, o_ref, xn_scr, h_scr, *, nchunk):
    f = pl.program_id(1)
    tf = w1_ref.shape[1]
    d = w2_ref.shape[1]

    tm = x_ref.shape[0]
    row_blocks = [slice(r, min(r + _NORM_ROWS, tm)) for r in range(0, tm, _NORM_ROWS)]

    @pl.when(f == 0)
    def _():
        for rows in row_blocks:
            xn_scr[rows, :] = _rms(x_ref[rows, :], gpre_ref[...]).astype(BF16)
            o_ref[rows, :] = jnp.zeros((rows.stop - rows.start, d), F32)

    for c in range(0, tf, nchunk):
        cs = slice(c, min(c + nchunk, tf))
        h = jnp.dot(xn_scr[...], w1_ref[:, cs], preferred_element_type=F32)
        h_scr[:, cs] = jnp.square(jnp.maximum(h, 0.0)).astype(BF16)
    for c in range(0, d, nchunk):
        cs = slice(c, min(c + nchunk, d))
        o_ref[:, cs] += jnp.dot(h_scr[...], w2_ref[:, cs], preferred_element_type=F32)

    @pl.when(f == pl.num_programs(1) - 1)
    def _():
        for rows in row_blocks:
            o_ref[rows, :] = x_ref[rows, :] + _rms(o_ref[rows, :], gpost_ref[...])


def _ffn(cfg, x, gpre, w1, w2, gpost, layer):
    t, d = x.shape
    tm, tf = cfg.tm_ffn, cfg.tf
    dff = w1.shape[2]
    return pl.pallas_call(
        functools.partial(_ffn_kernel, nchunk=512),
        grid=(t // tm, dff // tf),
        in_specs=[
            pl.BlockSpec((tm, d), lambda i, f: (i, 0)),
            _resident((1, d)),
            pl.BlockSpec((None, d, tf), lambda i, f: (layer, 0, f)),
            pl.BlockSpec((None, tf, d), lambda i, f: (layer, f, 0)),
            _resident((1, d)),
        ],
        out_specs=pl.BlockSpec((tm, d), lambda i, f: (i, 0)),
        out_shape=jax.ShapeDtypeStruct((t, d), F32),
        scratch_shapes=[pltpu.VMEM((tm, d), BF16), pltpu.VMEM((tm, tf), BF16)],
        compiler_params=_params(("parallel", "arbitrary")),
        name="squared_relu_mlp",
    )(x, gpre, w1, w2, gpost)


def _forward(cfg, x, mem, p):
    row = lambda a: a.reshape(1, -1).astype(F32)
    for l in range(cfg.depth):
        lru_in, qkvb, qkvc = _inproj(cfg, x, row(p["mix_norm_pre"][l]), p["w_in"], l)
        gn = p["group_norm"][l]
        wg = jnp.concatenate([p["lru_wa"][l], p["lru_wx"][l]], axis=-1).astype(BF16)
        ya = _lru(cfg, lru_in, p["conv_w"][l], row(p["conv_b"][l]), wg,
                  p["lru_ba"][l][:, None, :], p["lru_bx"][l][:, None, :], p["lru_lam"][l][:, None, :],
                  row(gn[:LRU_WIDTH]))
        yb = _dilated(cfg, qkvb, row(gn[LRU_WIDTH:LRU_WIDTH + DIL_WIDTH]))
        sink = jnp.zeros((1, 128), F32).at[0, :SWA_HEADS].set(p["swa_sink"][l].astype(F32))
        yc = _swa(cfg, qkvc, row(gn[LRU_WIDTH + DIL_WIDTH:]), sink)
        x = _outproj(cfg, ya, yb, yc, x, p["w_out"], row(p["mix_norm_post"][l]), l)
        kmem, vmem = _memkv(cfg, mem, row(p["mem_kv_norm"][l]), p["w_mkv"], l)
        x = _cross(cfg, x, row(p["mem_norm_pre"][l]), p["w_mq"], kmem, vmem, p["w_mo"],
                   row(p["mem_norm_post"][l]), l)
        x = _ffn(cfg, x, row(p["ffn_norm_pre"][l]), p["w_ff1"], p["w_ff2"], row(p["ffn_norm_post"][l]), l)
    return x


def _run(cfg, x_prompt, x_sample, mem_prompt, mem_sample, mix_norm_pre, mix_norm_post, w_in, conv_w, conv_b,
         lru_wa, lru_ba, lru_wx, lru_bx, lru_lam, swa_sink, group_norm, w_out, mem_norm_pre, mem_norm_post,
         mem_kv_norm, w_mq, w_mk, w_mv, w_mo, ffn_norm_pre, ffn_norm_post, w_ff1, w_ff2):
    d = cfg.d_model
    p = dict(
        mix_norm_pre=mix_norm_pre, mix_norm_post=mix_norm_post, w_in=w_in.astype(BF16), conv_w=conv_w,
        conv_b=conv_b, lru_wa=lru_wa, lru_ba=lru_ba, lru_wx=lru_wx, lru_bx=lru_bx, lru_lam=lru_lam,
        swa_sink=swa_sink, group_norm=group_norm, w_out=w_out.astype(BF16), mem_norm_pre=mem_norm_pre,
        mem_norm_post=mem_norm_post, mem_kv_norm=mem_kv_norm, w_mq=w_mq.astype(BF16),
        w_mkv=jnp.concatenate([w_mk, w_mv], axis=-1).astype(BF16), w_mo=w_mo.astype(BF16),
        ffn_norm_pre=ffn_norm_pre, ffn_norm_post=ffn_norm_post, w_ff1=w_ff1.astype(BF16),
        w_ff2=w_ff2.astype(BF16))
    outs = []
    for group, x, mem in zip(cfg.groups, (x_prompt, x_sample), (mem_prompt, mem_sample)):
        sub = cfg._replace(groups=(group,))
        outs.append(_forward(sub, x.reshape(-1, d), mem.reshape(-1, d), p).reshape(x.shape))
    return tuple(outs)


def kernel(x_prompt, x_sample, mem_prompt, mem_sample, mix_norm_pre, mix_norm_post, w_in, conv_w, conv_b, lru_wa,
           lru_ba, lru_wx, lru_bx, lru_lam, swa_sink, group_norm, w_out, mem_norm_pre, mem_norm_post, mem_kv_norm,
           w_mq, w_mk, w_mv, w_mo, ffn_norm_pre, ffn_norm_post, w_ff1, w_ff2):
    return _run(_CFG, x_prompt, x_sample, mem_prompt, mem_sample, mix_norm_pre, mix_norm_post, w_in, conv_w,
                conv_b, lru_wa, lru_ba, lru_wx, lru_bx, lru_lam, swa_sink, group_norm, w_out, mem_norm_pre,
                mem_norm_post, mem_kv_norm, w_mq, w_mk, w_mv, w_mo, ffn_norm_pre, ffn_norm_post, w_ff1, w_ff2)
```

```python
import functools
from typing import NamedTuple

import numpy as np
import jax
import jax.numpy as jnp
from jax import lax
from jax.experimental import pallas as pl
from jax.experimental.pallas import tpu as pltpu

F32 = jnp.float32
BF16 = jnp.bfloat16

D_MODEL = 2048
BATCH = 8
SEQ = 4096
DEPTH = 4
DEC_BATCH = 1
DEC_SEQ = 16384
HEAD_DIM = 128
LRU_WIDTH = 512
LRU_BLOCKS = 4
LRU_BLOCK_WIDTH = LRU_WIDTH // LRU_BLOCKS
CONV_WIDTH = 4
CONV_LEFT = 2
LRU_C = 8.0
DIL_HEADS = 6
DIL_PATTERNS = ((128, 1), (512, 4), (2048, 16))
SWA_HEADS = 6
SWA_KV_HEADS = 2
SWA_WINDOW = 128
DIL_WIDTH = DIL_HEADS * HEAD_DIM
SWA_WIDTH = SWA_HEADS * HEAD_DIM
SWA_KV_WIDTH = SWA_KV_HEADS * HEAD_DIM
MIX_WIDTH = LRU_WIDTH + DIL_WIDTH + SWA_WIDTH
IN_WIDTH = 2 * LRU_WIDTH + 3 * DIL_WIDTH + SWA_WIDTH + 2 * SWA_KV_WIDTH
N_MEM = 256
MEM_HEADS = 4
MEM_WIDTH = MEM_HEADS * HEAD_DIM
D_FF = 4 * D_MODEL
EPS = 1e-6

_NEG = -1e30
_QK_SCALE = HEAD_DIM ** -0.5
_HALO_ROWS = 8
_NORM_ROWS = 128
_SCAN_GROUP = 8
_DIL_W = DIL_PATTERNS[0][0] // (2 * DIL_PATTERNS[0][1])
assert all(wn // (2 * d) == _DIL_W for wn, d in DIL_PATTERNS)
_DIL_QROWS_MAX = 128
_PERM_ROWS = 256
_V7X_VMEM_BYTES = 64 * 1024 * 1024
_VMEM_LIMIT = _V7X_VMEM_BYTES - 3 * 1024 * 1024
_NT = (((1,), (1,)), ((), ()))


class _Cfg(NamedTuple):
    d_model: int
    d_ff: int
    depth: int
    groups: tuple
    n_mem: int
    tm: int
    tm_out: int
    tm_cross: int
    tm_ffn: int
    tf: int
    lru_chunk: int
    lru_rows: int
    swa_rows: int
    dil_rows: int
    swa_batch: int
    dil_batch: int


_CFG = _Cfg(d_model=D_MODEL, d_ff=D_FF, depth=DEPTH, groups=((BATCH, SEQ), (DEC_BATCH, DEC_SEQ)),
            n_mem=N_MEM, tm=512, tm_out=1024, tm_cross=1024, tm_ffn=1024, tf=1024, lru_chunk=2048, lru_rows=256,
            swa_rows=1024,
            dil_rows=1024, swa_batch=2, dil_batch=4)


def _sequences(cfg):
    out, start = [], 0
    for n, length in cfg.groups:
        for _ in range(n):
            out.append((start, length))
            start += length
    return out, start


def _chunk_flags(cfg, rows):
    seqs, total = _sequences(cfg)
    starts = {s for s, _ in seqs}
    ends = {s + l for s, l in seqs}
    for s, l in seqs:
        assert l % rows == 0, (l, rows)
    n = total // rows
    flags = np.zeros((n,), np.int32)
    for c in range(n):
        flags[c] = (1 if c * rows in starts else 0) | (2 if (c + 1) * rows in ends else 0)
    return jnp.asarray(flags)


def _params(semantics):
    return pltpu.CompilerParams(dimension_semantics=semantics, vmem_limit_bytes=_VMEM_LIMIT)


def _rms(x, g):
    ms = jnp.mean(x * x, axis=-1, keepdims=True)
    return x * lax.rsqrt(ms + EPS) * g


def _row_blocks(n):
    return [slice(r, min(r + _NORM_ROWS, n)) for r in range(0, n, _NORM_ROWS)]


def _resident(shape, layer=None):
    if layer is None:
        return pl.BlockSpec(shape, lambda *_: (0,) * len(shape), pipeline_mode=pl.Buffered(1))
    return pl.BlockSpec((None,) + tuple(shape), lambda *_: (layer,) + (0,) * len(shape),
                        pipeline_mode=pl.Buffered(1))


def _inproj_plan():
    lru_w = 2 * LRU_WIDTH
    dil_w = 3 * DIL_WIDTH
    segs = [
        (0, lru_w, 0, None),
        (lru_w, lru_w + DIL_WIDTH, 1, _QK_SCALE),
        (lru_w + DIL_WIDTH, lru_w + dil_w, 1, None),
        (lru_w + dil_w, lru_w + dil_w + SWA_WIDTH, 2, _QK_SCALE),
        (lru_w + dil_w + SWA_WIDTH, IN_WIDTH, 2, None),
    ]
    base = {0: 0, 1: lru_w, 2: lru_w + dil_w}
    plan = []
    for c0, c1, oi, scale in segs:
        c = c0
        while c < c1:
            n = min(512, c1 - c)
            plan.append((c, c + n, oi, c - base[oi], scale))
            c += n
    return tuple(plan)


def _inproj_kernel(x_ref, g_ref, w_ref, lru_ref, qkvb_ref, qkvc_ref, xn_scr, *, plan):
    for rows in _row_blocks(x_ref.shape[0]):
        xn_scr[rows, :] = _rms(x_ref[rows, :], g_ref[...]).astype(BF16)
    outs = (lru_ref, qkvb_ref, qkvc_ref)
    for c0, c1, oi, o0, scale in plan:
        acc = jnp.dot(xn_scr[...], w_ref[:, c0:c1], preferred_element_type=F32)
        if scale is not None:
            acc = acc * scale
        outs[oi][:, o0:o0 + (c1 - c0)] = acc.astype(outs[oi].dtype)


def _inproj(cfg, x, g, w, layer):
    t, d = x.shape
    tm = cfg.tm
    widths = (2 * LRU_WIDTH, 3 * DIL_WIDTH, SWA_WIDTH + 2 * SWA_KV_WIDTH)
    return pl.pallas_call(
        functools.partial(_inproj_kernel, plan=_inproj_plan()),
        grid=(t // tm,),
        in_specs=[
            pl.BlockSpec((tm, d), lambda i: (i, 0)),
            _resident((1, d)),
            _resident((d, IN_WIDTH), layer),
        ],
        out_specs=[pl.BlockSpec((tm, wd), lambda i: (i, 0)) for wd in widths],
        out_shape=[
            jax.ShapeDtypeStruct((t, widths[0]), F32),
            jax.ShapeDtypeStruct((t, widths[1]), BF16),
            jax.ShapeDtypeStruct((t, widths[2]), BF16),
        ],
        scratch_shapes=[pltpu.VMEM((tm, d), BF16)],
        compiler_params=_params(("parallel",)),
        name="mixer_inproj",
    )(x, g, w)


def _lru_fill_halo(first, last, xa_ref, xp_ref, xn_ref, xext, lc):
    xext[_HALO_ROWS:_HALO_ROWS + lc, :] = xa_ref[...]

    @pl.when(first)
    def _():
        xext[0:_HALO_ROWS, :] = jnp.zeros((_HALO_ROWS, LRU_WIDTH), F32)

    @pl.when(jnp.logical_not(first))
    def _():
        xext[0:_HALO_ROWS, :] = xp_ref[...]

    @pl.when(last)
    def _():
        xext[_HALO_ROWS + lc:, :] = jnp.zeros((_HALO_ROWS, LRU_WIDTH), F32)

    @pl.when(jnp.logical_not(last))
    def _():
        xext[_HALO_ROWS + lc:, :] = xn_ref[...]


def _lru_conv(xext, cw_ref, cb_ref, xc_ref, lc, rb):
    cb = cb_ref[...]
    taps = [cw_ref[j:j + 1, :] for j in range(CONV_WIDTH)]
    for blk in range(lc // rb):
        r0 = blk * rb
        xc_ref[r0:r0 + rb, :] = cb + sum(
            taps[j] * xext[r0 + _HALO_ROWS - CONV_LEFT + j:r0 + _HALO_ROWS - CONV_LEFT + j + rb, :]
            for j in range(CONV_WIDTH))


def _lru_gates(xc_ref, wg_ref, ba_ref, bx_ref, lam_ref, a_scr, u_scr, lc, rb):
    lam = lam_ref[...]
    neg = -lam
    softplus = jnp.maximum(neg, 0.0) + jnp.log1p(jnp.exp(-jnp.abs(neg)))
    for blk in range(lc // rb):
        r0 = blk * rb
        xc = xc_ref[r0:r0 + rb, :]
        xcb = xc.astype(BF16)
        for n in range(LRU_BLOCKS):
            cs = slice(n * LRU_BLOCK_WIDTH, (n + 1) * LRU_BLOCK_WIDTH)
            g = jnp.dot(xcb[:, cs], wg_ref[n], preferred_element_type=F32)
            r = jax.nn.sigmoid(g[:, :LRU_BLOCK_WIDTH] + ba_ref[:, cs])
            ig = jax.nn.sigmoid(g[:, LRU_BLOCK_WIDTH:] + bx_ref[:, cs])
            neg_log_a = (LRU_C * r) * softplus[:, cs]
            a = jnp.exp(-neg_log_a)
            one_minus_a2 = jnp.tanh(neg_log_a) * (1.0 + a * a)
            root = jnp.where(one_minus_a2 > 0.0, one_minus_a2 * lax.rsqrt(one_minus_a2), 0.0)
            u = root * (ig * xc[:, cs])
            groups = slice(r0 // _SCAN_GROUP, (r0 + rb) // _SCAN_GROUP)
            a_scr[groups, :, cs] = a.reshape(rb // _SCAN_GROUP, _SCAN_GROUP, LRU_BLOCK_WIDTH)
            u_scr[groups, :, cs] = u.reshape(rb // _SCAN_GROUP, _SCAN_GROUP, LRU_BLOCK_WIDTH)


def _lru_scan(reset, a_scr, u_scr, h_dst, carry, lc, reverse):
    @pl.when(reset)
    def _():
        carry[...] = jnp.zeros((1, LRU_WIDTH), F32)

    def group(i, h):
        g = lc // _SCAN_GROUP - 1 - i if reverse else i
        order = [_SCAN_GROUP - 1 - j if reverse else j for j in range(_SCAN_GROUP)]
        p = a_scr[g, order[0]:order[0] + 1, :]
        q = u_scr[g, order[0]:order[0] + 1, :]
        outs = [p * h + q]
        for j in order[1:]:
            a = a_scr[g, j:j + 1, :]
            q = a * q + u_scr[g, j:j + 1, :]
            p = a * p
            outs.append(p * h + q)
        for j, o in zip(order, outs):
            h_dst[g, j:j + 1, :] = o
        return outs[-1]

    carry[...] = lax.fori_loop(0, lc // _SCAN_GROUP, group, carry[...], unroll=2)


def _lru_fwd_kernel(flags_ref, xa_ref, xp_ref, xn_ref, cw_ref, cb_ref, wg_ref, ba_ref, bx_ref, lam_ref,
                    hf_ref, xc_ref, xext, a_scr, u_scr, carry, *, lc, rb):
    fl = flags_ref[pl.program_id(0)]
    first = (fl & 1) != 0
    last = (fl & 2) != 0
    _lru_fill_halo(first, last, xa_ref, xp_ref, xn_ref, xext, lc)
    _lru_conv(xext, cw_ref, cb_ref, xc_ref, lc, rb)
    _lru_gates(xc_ref, wg_ref, ba_ref, bx_ref, lam_ref, a_scr, u_scr, lc, rb)
    _lru_scan(first, a_scr, u_scr, hf_ref, carry, lc, reverse=False)


def _lru_bwd_kernel(flags_ref, xc_ref, gate_ref, hf_ref, wg_ref, ba_ref, bx_ref, lam_ref, gn_ref, y_ref,
                    a_scr, u_scr, h_scr, carry, *, lc, rb, nchunks):
    fl = flags_ref[nchunks - 1 - pl.program_id(0)]
    last = (fl & 2) != 0
    _lru_gates(xc_ref, wg_ref, ba_ref, bx_ref, lam_ref, a_scr, u_scr, lc, rb)
    _lru_scan(last, a_scr, u_scr, h_scr, carry, lc, reverse=True)
    for blk in range(lc // rb):
        rows = slice(blk * rb, (blk + 1) * rb)
        groups = slice(blk * rb // _SCAN_GROUP, (blk + 1) * rb // _SCAN_GROUP)
        h = (hf_ref[groups] + h_scr[groups]).reshape(rb, LRU_WIDTH)
        y = h * jax.nn.gelu(gate_ref[rows, :])
        y_ref[rows, :] = _rms(y, gn_ref[...]).astype(BF16)


def _lru(cfg, lru_in, cw, cb, wg, ba, bx, lam, gn):
    t = lru_in.shape[0]
    lc, rb = cfg.lru_chunk, cfg.lru_rows
    nchunks = t // lc
    hb = lc // _HALO_ROWS
    nhalo = t // _HALO_ROWS
    flags = _chunk_flags(cfg, lc)
    row = lambda: _resident((1, LRU_WIDTH))

    def specs(chunk_of):
        return [
            pl.BlockSpec((lc, LRU_WIDTH), lambda i, f: (chunk_of(i), 0)),
            pl.BlockSpec((_HALO_ROWS, LRU_WIDTH), lambda i, f: (jnp.maximum(chunk_of(i) * hb - 1, 0), 0)),
            pl.BlockSpec((_HALO_ROWS, LRU_WIDTH), lambda i, f: (jnp.minimum((chunk_of(i) + 1) * hb, nhalo - 1), 0)),
        ]

    conv_weights = [_resident((CONV_WIDTH, LRU_WIDTH)), row()]

    def weights(d):
        return [
            pl.BlockSpec((None, LRU_BLOCKS, LRU_BLOCK_WIDTH, 2 * LRU_BLOCK_WIDTH), lambda i, f: (d, 0, 0, 0)),
            pl.BlockSpec((None, 1, LRU_WIDTH), lambda i, f: (d, 0, 0)),
            pl.BlockSpec((None, 1, LRU_WIDTH), lambda i, f: (d, 0, 0)),
            pl.BlockSpec((None, 1, LRU_WIDTH), lambda i, f: (d, 0, 0)),
        ]

    grouped = (lc // _SCAN_GROUP, _SCAN_GROUP, LRU_WIDTH)
    scratch = [pltpu.VMEM(grouped, F32), pltpu.VMEM(grouped, F32)]
    carry = [pltpu.VMEM((1, LRU_WIDTH), F32)]

    fwd = lambda i: i
    hf, xc = pl.pallas_call(
        functools.partial(_lru_fwd_kernel, lc=lc, rb=rb),
        grid_spec=pltpu.PrefetchScalarGridSpec(
            num_scalar_prefetch=1, grid=(nchunks,),
            in_specs=specs(fwd) + conv_weights + weights(0),
            out_specs=[pl.BlockSpec(grouped, lambda i, f: (i, 0, 0)),
                       pl.BlockSpec((lc, LRU_WIDTH), lambda i, f: (i, 0))],
            scratch_shapes=[pltpu.VMEM((lc + 2 * _HALO_ROWS, LRU_WIDTH), F32)] + scratch + carry),
        out_shape=[jax.ShapeDtypeStruct((t // _SCAN_GROUP, _SCAN_GROUP, LRU_WIDTH), F32),
                   jax.ShapeDtypeStruct((t, LRU_WIDTH), F32)],
        compiler_params=_params(("arbitrary",)),
        name="lru_forward",
    )(flags, lru_in, lru_in, lru_in, cw, cb, wg, ba, bx, lam)

    bwd = lambda i: nchunks - 1 - i
    return pl.pallas_call(
        functools.partial(_lru_bwd_kernel, lc=lc, rb=rb, nchunks=nchunks),
        grid_spec=pltpu.PrefetchScalarGridSpec(
            num_scalar_prefetch=1, grid=(nchunks,),
            in_specs=[
                pl.BlockSpec((lc, LRU_WIDTH), lambda i, f: (bwd(i), 0)),
                pl.BlockSpec((lc, LRU_WIDTH), lambda i, f: (bwd(i), 1)),
                pl.BlockSpec(grouped, lambda i, f: (bwd(i), 0, 0)),
            ] + weights(1) + [row()],
            out_specs=pl.BlockSpec((lc, LRU_WIDTH), lambda i, f: (bwd(i), 0)),
            scratch_shapes=scratch + [pltpu.VMEM(grouped, F32)] + carry),
        out_shape=jax.ShapeDtypeStruct((t, LRU_WIDTH), BF16),
        compiler_params=_params(("arbitrary",)),
        name="lru_backward",
    )(flags, xc, lru_in, hf, wg, ba, bx, lam, gn)


def _alibi_slopes(n):
    return [2.0 ** (-8.0 * (i + 1) / n) for i in range(n)]


def _band_bias(wq, halo, dist_scale, hq, hkv):
    rep = hq // hkv
    slopes = _alibi_slopes(hq)
    qi = np.arange(wq)[:, None]
    kj = np.arange(wq + 2 * halo)[None, :]
    rel = np.abs(kj - halo - qi)
    out = np.empty((hkv, rep * wq, wq + 2 * halo), np.float32)
    for g in range(hkv):
        for r in range(rep):
            out[g, r * wq:(r + 1) * wq] = np.where(rel <= halo, -slopes[g * rep + r] * dist_scale * rel, _NEG)
    return out


def _edge_penalties(fl, wq, halo):
    col = lax.broadcasted_iota(jnp.int32, (1, wq + 2 * halo), 1)
    pen_first = jnp.where(col < halo, jnp.where((fl & 1) != 0, _NEG, 0.0), 0.0)
    pen_last = jnp.where(col >= wq + halo, jnp.where((fl & 2) != 0, _NEG, 0.0), 0.0)
    return pen_first, pen_last


def _attend(qs, ks, vs, biases):
    scores = [lax.dot_general(q, k, _NT, preferred_element_type=F32) for q, k in zip(qs, ks)]
    ms, ps = [], []
    for s, b in zip(scores, biases):
        if b is not None:
            s = s + b
        m = jnp.max(s, axis=-1, keepdims=True)
        ms.append(m)
        ps.append(jnp.exp(s - m).astype(BF16))
    ls, accs = [], []
    for p, v in zip(ps, vs):
        v1 = jnp.concatenate([v, jnp.ones(v.shape, v.dtype)], axis=1)
        out = jnp.dot(p, v1, preferred_element_type=F32)
        accs.append(out[:, :HEAD_DIM])
        ls.append(out[:, HEAD_DIM:])
    return ms, ls, accs


def _swa_kernel(flags_ref, q_ref, kc_ref, kp_ref, kn_ref, vc_ref, vp_ref, vn_ref, bias_ref, sink_ref, gn_ref,
                y_out, kbuf, vbuf, *, w, nsub, nb):
    rep = SWA_HEADS // SWA_KV_HEADS
    rows_total = nsub * w
    pen_first, pen_last = _edge_penalties(flags_ref[pl.program_id(0)], w, w)
    kbuf[0:w, :] = kp_ref[...]
    kbuf[w:w + rows_total, :] = kc_ref[...]
    kbuf[w + rows_total:, :] = kn_ref[...]
    vbuf[0:w, :] = vp_ref[...]
    vbuf[w:w + rows_total, :] = vc_ref[...]
    vbuf[w + rows_total:, :] = vn_ref[...]

    for j0 in range(0, nsub, nb):
        blocks = list(range(j0, min(j0 + nb, nsub)))
        qs, ks, vs, bs = [], [], [], []
        for j in blocks:
            rows = slice(j * w, (j + 1) * w)
            for g in range(SWA_KV_HEADS):
                gs = slice(g * HEAD_DIM, (g + 1) * HEAD_DIM)
                qs.append(jnp.concatenate(
                    [q_ref[rows, (g * rep + r) * HEAD_DIM:(g * rep + r + 1) * HEAD_DIM] for r in range(rep)], axis=0))
                ks.append(kbuf[j * w:(j + 3) * w, gs])
                vs.append(vbuf[j * w:(j + 3) * w, gs])
                b = bias_ref[g]
                if j == 0:
                    b = b + pen_first
                if j == nsub - 1:
                    b = b + pen_last
                bs.append(b)
        ms, ls, accs = _attend(qs, ks, vs, bs)
        for bi, j in enumerate(blocks):
            heads_out = []
            for g in range(SWA_KV_HEADS):
                idx = bi * SWA_KV_HEADS + g
                for r in range(rep):
                    h = g * rep + r
                    part = slice(r * w, (r + 1) * w)
                    m, l, acc = ms[idx][part], ls[idx][part], accs[idx][part]
                    factor = jax.nn.sigmoid(m + jnp.log(l) - sink_ref[:, h:h + 1])
                    heads_out.append((acc / l) * factor)
            y = jnp.concatenate(heads_out, axis=1)
            y_out[j * w:(j + 1) * w, :] = _rms(y, gn_ref[...]).astype(BF16)


def _swa(cfg, qkvc, gn, sink):
    t, c = qkvc.shape
    w = SWA_WINDOW
    rows = cfg.swa_rows
    nsub = rows // w
    nchunks = t // rows
    nblk = t // w
    qw, kvw = SWA_WIDTH, SWA_KV_WIDTH
    rep = SWA_HEADS // SWA_KV_HEADS
    assert qw % kvw == 0 and c == qw + 2 * kvw
    kcol, vcol = qw // kvw, qw // kvw + 1
    prev = lambda col: (lambda i, f: (jnp.maximum(i * nsub - 1, 0), col))
    nxt = lambda col: (lambda i, f: (jnp.minimum((i + 1) * nsub, nblk - 1), col))
    return pl.pallas_call(
        functools.partial(_swa_kernel, w=w, nsub=nsub, nb=cfg.swa_batch),
        grid_spec=pltpu.PrefetchScalarGridSpec(
            num_scalar_prefetch=1, grid=(nchunks,),
            in_specs=[
                pl.BlockSpec((rows, qw), lambda i, f: (i, 0)),
                pl.BlockSpec((rows, kvw), lambda i, f: (i, kcol)),
                pl.BlockSpec((w, kvw), prev(kcol)),
                pl.BlockSpec((w, kvw), nxt(kcol)),
                pl.BlockSpec((rows, kvw), lambda i, f: (i, vcol)),
                pl.BlockSpec((w, kvw), prev(vcol)),
                pl.BlockSpec((w, kvw), nxt(vcol)),
                _resident((SWA_KV_HEADS, rep * w, 3 * w)),
                _resident((1, 128)),
                _resident((1, qw)),
            ],
            out_specs=pl.BlockSpec((rows, qw), lambda i, f: (i, 0)),
            scratch_shapes=[pltpu.VMEM((rows + 2 * w, kvw), BF16), pltpu.VMEM((rows + 2 * w, kvw), BF16)]),
        out_shape=jax.ShapeDtypeStruct((t, qw), BF16),
        compiler_params=_params(("parallel",)),
        name="windowed_gqa_sink",
    )(_chunk_flags(cfg, rows), qkvc, qkvc, qkvc, qkvc, qkvc, qkvc, qkvc,
      jnp.asarray(_band_bias(w, w, 1, SWA_HEADS, SWA_KV_HEADS)), sink, gn)


def _perm_matrix(d):
    n = _PERM_ROWS
    per = n // d
    p = np.zeros((n, n), np.float32)
    for r in range(d):
        for m in range(per):
            p[r * per + m, d * m + r] = 1.0
    return p


def _dilated_qrows(c, d):
    return min(_DIL_QROWS_MAX, c // d)


def _dilated_kernel(flags_ref, q_ref, kp_ref, kc_ref, kn_ref, vp_ref, vc_ref, vn_ref, bias0, bias1, bias2,
                    perm_ref, gn_ref, y_ref, qd, kd, vd, bv0, bv1, bv2, acc_nat, m_nat, l_nat, *, c, batch_rows):
    w = _DIL_W
    nh = DIL_HEADS
    fl = flags_ref[pl.program_id(0)]
    bias_refs = (bias0, bias1, bias2)
    biasv = (bv0, bv1, bv2)
    for p, (_, d) in enumerate(DIL_PATTERNS):
        wq = _dilated_qrows(c, d)
        pen_first, pen_last = _edge_penalties(fl, wq, w)
        for h in range(nh):
            b = bias_refs[p][h]
            if c // (d * wq) == 1:
                biasv[p][0, h] = b + pen_first + pen_last
            else:
                biasv[p][0, h] = b
                biasv[p][1, h] = b + pen_first
                biasv[p][2, h] = b + pen_last

    heads = [slice(h * HEAD_DIM, (h + 1) * HEAD_DIM) for h in range(nh)]

    def run_pattern(p, d, qsrc, ksrc, vsrc, qstride, kstride, first_pattern, last_pattern):
        wq = _dilated_qrows(c, d)
        win = wq + 2 * w
        nblk = c // (d * wq)
        nb = max(1, batch_rows // wq)
        assert (c // wq) % nb == 0
        assert d == 1 or not last_pattern

        def body(it, carry):
            qs, ks, vs, bs, where = [], [], [], [], []
            for b in range(nb):
                sb = it * nb + b
                r = sb // nblk
                s = sb % nblk
                var = 0 if nblk == 1 else jnp.where(s == 0, 1, 0) + jnp.where(s == nblk - 1, 2, 0)
                q0 = pl.multiple_of(r * qstride + s * wq, w)
                k0 = pl.multiple_of(r * kstride + s * wq, w)
                where.append(d * wq * s + r)
                for h in range(nh):
                    qs.append(qsrc[pl.ds(q0, wq), heads[h]])
                    ks.append(ksrc[pl.ds(k0, win), heads[h]])
                    vs.append(vsrc[pl.ds(k0, win), heads[h]])
                    bs.append(biasv[p][var, h])
            ms, ls, accs = _attend(qs, ks, vs, bs)
            for b in range(nb):
                nat = pl.ds(where[b], wq, stride=d) if d > 1 else pl.ds(pl.multiple_of(where[b], w), wq)
                outs = []
                for h in range(nh):
                    m, l, acc = ms[b * nh + h], ls[b * nh + h], accs[b * nh + h]
                    if not first_pattern:
                        m_p = m_nat[h, nat, :]
                        m_n = jnp.maximum(m_p, m)
                        alpha = jnp.exp(m_p - m_n)
                        beta = jnp.exp(m - m_n)
                        l = alpha * l_nat[h, nat, :] + beta * l
                        acc = alpha * acc_nat[h, nat, :] + beta * acc
                        m = m_n
                    if last_pattern:
                        outs.append(acc / l)
                    else:
                        acc_nat[h, nat, :] = acc
                        m_nat[h, nat, :] = jnp.broadcast_to(m, (wq, HEAD_DIM))
                        l_nat[h, nat, :] = jnp.broadcast_to(l, (wq, HEAD_DIM))
                if last_pattern:
                    y_ref[nat, :] = _rms(jnp.concatenate(outs, axis=1), gn_ref[...]).astype(BF16)
            return carry

        lax.fori_loop(0, c // (wq * nb), body, 0)

    def deinterleave(p, d):
        per = _PERM_ROWS // d
        halo_groups = (d * w) // _PERM_ROWS
        chunk_groups = c // _PERM_ROWS
        kstride = (chunk_groups + 2 * halo_groups) * per
        perm = perm_ref[p - 1]

        def move(src, row0, dst, g, stride):
            res = jnp.dot(perm, src[row0:row0 + _PERM_ROWS, :], preferred_element_type=F32).astype(BF16)
            for r in range(d):
                dst[r * stride + g * per:r * stride + (g + 1) * per, :] = res[r * per:(r + 1) * per]

        for g in range(chunk_groups):
            move(q_ref, g * _PERM_ROWS, qd, g, c // d)
        for prev_ref, cur_ref, next_ref, dst in ((kp_ref, kc_ref, kn_ref, kd), (vp_ref, vc_ref, vn_ref, vd)):
            srcs = ([(prev_ref, c - (halo_groups - g) * _PERM_ROWS) for g in range(halo_groups)]
                    + [(cur_ref, g * _PERM_ROWS) for g in range(chunk_groups)]
                    + [(next_ref, g * _PERM_ROWS) for g in range(halo_groups)])
            for g, (src, row0) in enumerate(srcs):
                move(src, row0, dst, g, kstride)
        return c // d, kstride

    order = sorted(range(len(DIL_PATTERNS)), key=lambda p: -DIL_PATTERNS[p][1])
    for idx, p in enumerate(order):
        d = DIL_PATTERNS[p][1]
        first, last = idx == 0, idx == len(order) - 1
        if d == 1:
            for prev_ref, cur_ref, next_ref, dst in ((kp_ref, kc_ref, kn_ref, kd), (vp_ref, vc_ref, vn_ref, vd)):
                dst[0:w, :] = prev_ref[c - w:c, :]
                dst[w:w + c, :] = cur_ref[...]
                dst[w + c:2 * w + c, :] = next_ref[0:w, :]
            run_pattern(p, d, q_ref, kd, vd, 0, 0, first, last)
        else:
            qstride, kstride = deinterleave(p, d)
            run_pattern(p, d, qd, kd, vd, qstride, kstride, first, last)


def _dilated(cfg, qkvb, gn):
    t = qkvb.shape[0]
    c = cfg.dil_rows
    w = _DIL_W
    n = t // c
    dmax = max(d for _, d in DIL_PATTERNS)
    assert DIL_PATTERNS[0][1] == 1 and c % (dmax * w) == 0 and c >= dmax * w and c % _PERM_ROWS == 0
    assert all((d * w) % _PERM_ROWS == 0 for _, d in DIL_PATTERNS[1:])
    assert len(DIL_PATTERNS) == 3
    biases = [_band_bias(_dilated_qrows(c, d), w, d, DIL_HEADS, DIL_HEADS) for _, d in DIL_PATTERNS]
    variants = [1 if c // (d * _dilated_qrows(c, d)) == 1 else 3 for _, d in DIL_PATTERNS]
    perm = np.stack([_perm_matrix(d) for _, d in DIL_PATTERNS[1:]])
    blk = (c, DIL_WIDTH)
    prev = lambda col: (lambda i, f: (jnp.maximum(i - 1, 0), col))
    cur = lambda col: (lambda i, f: (i, col))
    nxt = lambda col: (lambda i, f: (jnp.minimum(i + 1, n - 1), col))
    return pl.pallas_call(
        functools.partial(_dilated_kernel, c=c, batch_rows=cfg.dil_batch * w),
        grid_spec=pltpu.PrefetchScalarGridSpec(
            num_scalar_prefetch=1, grid=(n,),
            in_specs=[
                pl.BlockSpec(blk, cur(0)),
                pl.BlockSpec(blk, prev(1)), pl.BlockSpec(blk, cur(1)), pl.BlockSpec(blk, nxt(1)),
                pl.BlockSpec(blk, prev(2)), pl.BlockSpec(blk, cur(2)), pl.BlockSpec(blk, nxt(2)),
                _resident(biases[0].shape), _resident(biases[1].shape), _resident(biases[2].shape),
                _resident(perm.shape), _resident((1, DIL_WIDTH)),
            ],
            out_specs=pl.BlockSpec(blk, cur(0)),
            scratch_shapes=[
                pltpu.VMEM((c, DIL_WIDTH), BF16),
                pltpu.VMEM((3 * c, DIL_WIDTH), BF16), pltpu.VMEM((3 * c, DIL_WIDTH), BF16),
                pltpu.VMEM((variants[0],) + biases[0].shape, F32),
                pltpu.VMEM((variants[1],) + biases[1].shape, F32),
                pltpu.VMEM((variants[2],) + biases[2].shape, F32),
                pltpu.VMEM((DIL_HEADS, c, HEAD_DIM), F32),
                pltpu.VMEM((DIL_HEADS, c, HEAD_DIM), F32),
                pltpu.VMEM((DIL_HEADS, c, HEAD_DIM), F32),
            ]),
        out_shape=jax.ShapeDtypeStruct((t, DIL_WIDTH), BF16),
        compiler_params=_params(("parallel",)),
        name="dilated_attention",
    )(_chunk_flags(cfg, c), qkvb, qkvb, qkvb, qkvb, qkvb, qkvb, qkvb,
      jnp.asarray(biases[0]), jnp.asarray(biases[1]), jnp.asarray(biases[2]), jnp.asarray(perm, dtype=BF16), gn)


def _outproj_kernel(ya_ref, yb_ref, yc_ref, x_ref, w_ref, g_ref, o_ref, *, nchunk):
    d = o_ref.shape[1]
    b0, b1 = LRU_WIDTH, LRU_WIDTH + DIL_WIDTH
    for c in range(0, d, nchunk):
        cs = slice(c, c + nchunk)
        acc = jnp.dot(ya_ref[...], w_ref[0:b0, cs], preferred_element_type=F32)
        acc += jnp.dot(yb_ref[...], w_ref[b0:b1, cs], preferred_element_type=F32)
        acc += jnp.dot(yc_ref[...], w_ref[b1:, cs], preferred_element_type=F32)
        o_ref[:, cs] = acc
    for rows in _row_blocks(o_ref.shape[0]):
        o_ref[rows, :] = x_ref[rows, :] + _rms(o_ref[rows, :], g_ref[...])


def _outproj(cfg, ya, yb, yc, x, w, g, layer):
    t, d = x.shape
    tm = cfg.tm_out
    return pl.pallas_call(
        functools.partial(_outproj_kernel, nchunk=min(512, d)),
        grid=(t // tm,),
        in_specs=[
            pl.BlockSpec((tm, LRU_WIDTH), lambda i: (i, 0)),
            pl.BlockSpec((tm, DIL_WIDTH), lambda i: (i, 0)),
            pl.BlockSpec((tm, SWA_WIDTH), lambda i: (i, 0)),
            pl.BlockSpec((tm, d), lambda i: (i, 0)),
            _resident((MIX_WIDTH, d), layer),
            _resident((1, d)),
        ],
        out_specs=pl.BlockSpec((tm, d), lambda i: (i, 0)),
        out_shape=jax.ShapeDtypeStruct((t, d), F32),
        compiler_params=_params(("parallel",)),
        name="mixer_outproj",
    )(ya, yb, yc, x, w, g)


def _memkv_kernel(mem_ref, g_ref, w_ref, k_ref, v_ref):
    mn = _rms(mem_ref[...], g_ref[...]).astype(BF16)
    kv = jnp.dot(mn, w_ref[...], preferred_element_type=F32)
    k_ref[...] = kv[:, :MEM_WIDTH].astype(BF16)
    v_ref[...] = kv[:, MEM_WIDTH:].astype(BF16)


def _memkv(cfg, mem, g, wkv, layer):
    rows, d = mem.shape
    nm = cfg.n_mem
    shape = jax.ShapeDtypeStruct((rows, MEM_WIDTH), BF16)
    return pl.pallas_call(
        _memkv_kernel,
        grid=(rows // nm,),
        in_specs=[pl.BlockSpec((nm, d), lambda i: (i, 0)), _resident((1, d)),
                  _resident((d, 2 * MEM_WIDTH), layer)],
        out_specs=[pl.BlockSpec((nm, MEM_WIDTH), lambda i: (i, 0))] * 2,
        out_shape=[shape, shape],
        compiler_params=_params(("parallel",)),
        name="memory_kv",
    )(mem, g, wkv)


def _cross_kernel(seq_ref, x_ref, gpre_ref, wq_ref, k_ref, v_ref, wo_ref, gpost_ref, o_ref, xn_scr, o_scr, *,
                  nchunk):
    del seq_ref
    tm, d = x_ref.shape
    for rows in _row_blocks(tm):
        xn_scr[rows, :] = _rms(x_ref[rows, :], gpre_ref[...]).astype(BF16)
    q = (jnp.dot(xn_scr[...], wq_ref[...], preferred_element_type=F32) * _QK_SCALE).astype(BF16)
    heads = [slice(h * HEAD_DIM, (h + 1) * HEAD_DIM) for h in range(MEM_HEADS)]
    _, ls, accs = _attend([q[:, hs] for hs in heads], [k_ref[:, hs] for hs in heads],
                          [v_ref[:, hs] for hs in heads], [None] * MEM_HEADS)
    for hs, l, acc in zip(heads, ls, accs):
        o_scr[:, hs] = (acc / l).astype(BF16)
    for c in range(0, d, nchunk):
        cs = slice(c, min(c + nchunk, d))
        o_ref[:, cs] = jnp.dot(o_scr[...], wo_ref[:, cs], preferred_element_type=F32)
    for rows in _row_blocks(tm):
        o_ref[rows, :] = x_ref[rows, :] + _rms(o_ref[rows, :], gpost_ref[...])


def _cross(cfg, x, gpre, wq, kmem, vmem, wo, gpost, layer):
    t, d = x.shape
    tm, nm = cfg.tm_cross, cfg.n_mem
    seqs, _ = _sequences(cfg)
    seq_of_tile = []
    for si, (_, length) in enumerate(seqs):
        assert length % tm == 0
        seq_of_tile += [si] * (length // tm)
    seq_of_tile = jnp.asarray(np.asarray(seq_of_tile, np.int32))
    return pl.pallas_call(
        functools.partial(_cross_kernel, nchunk=512),
        grid_spec=pltpu.PrefetchScalarGridSpec(
            num_scalar_prefetch=1, grid=(t // tm,),
            in_specs=[
                pl.BlockSpec((tm, d), lambda i, s: (i, 0)),
                _resident((1, d)),
                _resident((d, MEM_WIDTH), layer),
                pl.BlockSpec((nm, MEM_WIDTH), lambda i, s: (s[i], 0)),
                pl.BlockSpec((nm, MEM_WIDTH), lambda i, s: (s[i], 0)),
                _resident((MEM_WIDTH, d), layer),
                _resident((1, d)),
            ],
            out_specs=pl.BlockSpec((tm, d), lambda i, s: (i, 0)),
            scratch_shapes=[pltpu.VMEM((tm, d), BF16), pltpu.VMEM((tm, MEM_WIDTH), BF16)]),
        out_shape=jax.ShapeDtypeStruct((t, d), F32),
        compiler_params=_params(("parallel",)),
        name="memory_cross_attention",
    )(seq_of_tile, x, gpre, wq, kmem, vmem, wo, gpost)


def _ffn_kernel(x_ref, gpre_ref, w1_ref, w2_ref, gpost_ref, o_ref, xn_scr, h_scr, *, nchunk):
    f = pl.program_id(1)
    tf = w1_ref.shape[1]
    d = w2_ref.shape[1]

    tm = x_ref.shape[0]
    row_blocks = [slice(r, min(r + _NORM_ROWS, tm)) for r in range(0, tm, _NORM_ROWS)]

    @pl.when(f == 0)
    def _():
        for rows in row_blocks:
            xn_scr[rows, :] = _rms(x_ref[rows, :], gpre_ref[...]).astype(BF16)
            o_ref[rows, :] = jnp.zeros((rows.stop - rows.start, d), F32)

    for c in range(0, tf, nchunk):
        cs = slice(c, min(c + nchunk, tf))
        h = jnp.dot(xn_scr[...], w1_ref[:, cs], preferred_element_type=F32)
        h_scr[:, cs] = jnp.square(jnp.maximum(h, 0.0)).astype(BF16)
    for c in range(0, d, nchunk):
        cs = slice(c, min(c + nchunk, d))
        o_ref[:, cs] += jnp.dot(h_scr[...], w2_ref[:, cs], preferred_element_type=F32)

    @pl.when(f == pl.num_programs(1) - 1)
    def _():
        for rows in row_blocks:
            o_ref[rows, :] = x_ref[rows, :] + _rms(o_ref[rows, :], gpost_ref[...])


def _ffn(cfg, x, gpre, w1, w2, gpost, layer):
    t, d = x.shape
    tm, tf = cfg.tm_ffn, cfg.tf
    dff = w1.shape[2]
    return pl.pallas_call(
        functools.partial(_ffn_kernel, nchunk=512),
        grid=(t // tm, dff // tf),
        in_specs=[
            pl.BlockSpec((tm, d), lambda i, f: (i, 0)),
            _resident((1, d)),
            pl.BlockSpec((None, d, tf), lambda i, f: (layer, 0, f)),
            pl.BlockSpec((None, tf, d), lambda i, f: (layer, f, 0)),
            _resident((1, d)),
        ],
        out_specs=pl.BlockSpec((tm, d), lambda i, f: (i, 0)),
        out_shape=jax.ShapeDtypeStruct((t, d), F32),
        scratch_shapes=[pltpu.VMEM((tm, d), BF16), pltpu.VMEM((tm, tf), BF16)],
        compiler_params=_params(("parallel", "arbitrary")),
        name="squared_relu_mlp",
    )(x, gpre, w1, w2, gpost)


def _forward(cfg, x, mem, p):
    row = lambda a: a.reshape(1, -1).astype(F32)
    for l in range(cfg.depth):
        lru_in, qkvb, qkvc = _inproj(cfg, x, row(p["mix_norm_pre"][l]), p["w_in"], l)
        gn = p["group_norm"][l]
        wg = jnp.concatenate([p["lru_wa"][l], p["lru_wx"][l]], axis=-1).astype(BF16)
        ya = _lru(cfg, lru_in, p["conv_w"][l], row(p["conv_b"][l]), wg,
                  p["lru_ba"][l][:, None, :], p["lru_bx"][l][:, None, :], p["lru_lam"][l][:, None, :],
                  row(gn[:LRU_WIDTH]))
        yb = _dilated(cfg, qkvb, row(gn[LRU_WIDTH:LRU_WIDTH + DIL_WIDTH]))
        sink = jnp.zeros((1, 128), F32).at[0, :SWA_HEADS].set(p["swa_sink"][l].astype(F32))
        yc = _swa(cfg, qkvc, row(gn[LRU_WIDTH + DIL_WIDTH:]), sink)
        x = _outproj(cfg, ya, yb, yc, x, p["w_out"], row(p["mix_norm_post"][l]), l)
        kmem, vmem = _memkv(cfg, mem, row(p["mem_kv_norm"][l]), p["w_mkv"], l)
        x = _cross(cfg, x, row(p["mem_norm_pre"][l]), p["w_mq"], kmem, vmem, p["w_mo"],
                   row(p["mem_norm_post"][l]), l)
        x = _ffn(cfg, x, row(p["ffn_norm_pre"][l]), p["w_ff1"], p["w_ff2"], row(p["ffn_norm_post"][l]), l)
    return x


def _run(cfg, x_prompt, x_sample, mem_prompt, mem_sample, mix_norm_pre, mix_norm_post, w_in, conv_w, conv_b,
         lru_wa, lru_ba, lru_wx, lru_bx, lru_lam, swa_sink, group_norm, w_out, mem_norm_pre, mem_norm_post,
         mem_kv_norm, w_mq, w_mk, w_mv, w_mo, ffn_norm_pre, ffn_norm_post, w_ff1, w_ff2):
    d = cfg.d_model
    p = dict(
        mix_norm_pre=mix_norm_pre, mix_norm_post=mix_norm_post, w_in=w_in.astype(BF16), conv_w=conv_w,
        conv_b=conv_b, lru_wa=lru_wa, lru_ba=lru_ba, lru_wx=lru_wx, lru_bx=lru_bx, lru_lam=lru_lam,
        swa_sink=swa_sink, group_norm=group_norm, w_out=w_out.astype(BF16), mem_norm_pre=mem_norm_pre,
        mem_norm_post=mem_norm_post, mem_kv_norm=mem_kv_norm, w_mq=w_mq.astype(BF16),
        w_mkv=jnp.concatenate([w_mk, w_mv], axis=-1).astype(BF16), w_mo=w_mo.astype(BF16),
        ffn_norm_pre=ffn_norm_pre, ffn_norm_post=ffn_norm_post, w_ff1=w_ff1.astype(BF16),
        w_ff2=w_ff2.astype(BF16))
    outs = []
    for group, x, mem in zip(cfg.groups, (x_prompt, x_sample), (mem_prompt, mem_sample)):
        sub = cfg._replace(groups=(group,))
        outs.append(_forward(sub, x.reshape(-1, d), mem.reshape(-1, d), p).reshape(x.shape))
    return tuple(outs)


def kernel(x_prompt, x_sample, mem_prompt, mem_sample, mix_norm_pre, mix_norm_post, w_in, conv_w, conv_b, lru_wa,
           lru_ba, lru_wx, lru_bx, lru_lam, swa_sink, group_norm, w_out, mem_norm_pre, mem_norm_post, mem_kv_norm,
           w_mq, w_mk, w_mv, w_mo, ffn_norm_pre, ffn_norm_post, w_ff1, w_ff2):
    return _run(_CFG, x_prompt, x_sample, mem_prompt, mem_sample, mix_norm_pre, mix_norm_post, w_in, conv_w,
                conv_b, lru_wa, lru_ba, lru_wx, lru_bx, lru_lam, swa_sink, group_norm, w_out, mem_norm_pre,
                mem_norm_post, mem_kv_norm, w_mq, w_mk, w_mv, w_mo, ffn_norm_pre, ffn_norm_post, w_ff1, w_ff2)
```

```python
import functools
from typing import NamedTuple

import numpy as np
import jax
import jax.numpy as jnp
from jax import lax
from jax.experimental import pallas as pl
from jax.experimental.pallas import tpu as pltpu

F32 = jnp.float32
BF16 = jnp.bfloat16

D_MODEL = 2048
BATCH = 8
SEQ = 4096
DEPTH = 4
DEC_BATCH = 1
DEC_SEQ = 16384
HEAD_DIM = 128
LRU_WIDTH = 512
LRU_BLOCKS = 4
LRU_BLOCK_WIDTH = LRU_WIDTH // LRU_BLOCKS
CONV_WIDTH = 4
CONV_LEFT = 2
LRU_C = 8.0
DIL_HEADS = 6
DIL_PATTERNS = ((128, 1), (512, 4), (2048, 16))
SWA_HEADS = 6
SWA_KV_HEADS = 2
SWA_WINDOW = 128
DIL_WIDTH = DIL_HEADS * HEAD_DIM
SWA_WIDTH = SWA_HEADS * HEAD_DIM
SWA_KV_WIDTH = SWA_KV_HEADS * HEAD_DIM
MIX_WIDTH = LRU_WIDTH + DIL_WIDTH + SWA_WIDTH
IN_WIDTH = 2 * LRU_WIDTH + 3 * DIL_WIDTH + SWA_WIDTH + 2 * SWA_KV_WIDTH
N_MEM = 256
MEM_HEADS = 4
MEM_WIDTH = MEM_HEADS * HEAD_DIM
D_FF = 4 * D_MODEL
EPS = 1e-6

_NEG = -1e30
_QK_SCALE = HEAD_DIM ** -0.5
_HALO_ROWS = 8
_NORM_ROWS = 128
_SCAN_GROUP = 8
_DIL_W = DIL_PATTERNS[0][0] // (2 * DIL_PATTERNS[0][1])
assert all(wn // (2 * d) == _DIL_W for wn, d in DIL_PATTERNS)
_DIL_QROWS_MAX = 128
_PERM_ROWS = 256
_V7X_VMEM_BYTES = 64 * 1024 * 1024
_VMEM_LIMIT = _V7X_VMEM_BYTES - 3 * 1024 * 1024
_NT = (((1,), (1,)), ((), ()))


class _Cfg(NamedTuple):
    d_model: int
    d_ff: int
    depth: int
    groups: tuple
    n_mem: int
    tm: int
    tm_out: int
    tm_cross: int
    tm_ffn: int
    tf: int
    lru_chunk: int
    lru_rows: int
    swa_rows: int
    dil_rows: int
    swa_batch: int
    dil_batch: int


_CFG = _Cfg(d_model=D_MODEL, d_ff=D_FF, depth=DEPTH, groups=((BATCH, SEQ), (DEC_BATCH, DEC_SEQ)),
            n_mem=N_MEM, tm=512, tm_out=1024, tm_cross=1024, tm_ffn=1024, tf=1024, lru_chunk=2048, lru_rows=256,
            swa_rows=1024,
            dil_rows=1024, swa_batch=2, dil_batch=4)


def _sequences(cfg):
    out, start = [], 0
    for n, length in cfg.groups:
        for _ in range(n):
            out.append((start, length))
            start += length
    return out, start


def _chunk_flags(cfg, rows):
    seqs, total = _sequences(cfg)
    starts = {s for s, _ in seqs}
    ends = {s + l for s, l in seqs}
    for s, l in seqs:
        assert l % rows == 0, (l, rows)
    n = total // rows
    flags = np.zeros((n,), np.int32)
    for c in range(n):
        flags[c] = (1 if c * rows in starts else 0) | (2 if (c + 1) * rows in ends else 0)
    return jnp.asarray(flags)


def _params(semantics):
    return pltpu.CompilerParams(dimension_semantics=semantics, vmem_limit_bytes=_VMEM_LIMIT)


def _rms(x, g):
    ms = jnp.mean(x * x, axis=-1, keepdims=True)
    return x * lax.rsqrt(ms + EPS) * g


def _row_blocks(n):
    return [slice(r, min(r + _NORM_ROWS, n)) for r in range(0, n, _NORM_ROWS)]


def _resident(shape, layer=None):
    if layer is None:
        return pl.BlockSpec(shape, lambda *_: (0,) * len(shape), pipeline_mode=pl.Buffered(1))
    return pl.BlockSpec((None,) + tuple(shape), lambda *_: (layer,) + (0,) * len(shape),
                        pipeline_mode=pl.Buffered(1))


def _inproj_plan():
    lru_w = 2 * LRU_WIDTH
    dil_w = 3 * DIL_WIDTH
    segs = [
        (0, lru_w, 0, None),
        (lru_w, lru_w + DIL_WIDTH, 1, _QK_SCALE),
        (lru_w + DIL_WIDTH, lru_w + dil_w, 1, None),
        (lru_w + dil_w, lru_w + dil_w + SWA_WIDTH, 2, _QK_SCALE),
        (lru_w + dil_w + SWA_WIDTH, IN_WIDTH, 2, None),
    ]
    base = {0: 0, 1: lru_w, 2: lru_w + dil_w}
    plan = []
    for c0, c1, oi, scale in segs:
        c = c0
        while c < c1:
            n = min(512, c1 - c)
            plan.append((c, c + n, oi, c - base[oi], scale))
            c += n
    return tuple(plan)


def _inproj_kernel(x_ref, g_ref, w_ref, lru_ref, qkvb_ref, qkvc_ref, xn_scr, *, plan):
    for rows in _row_blocks(x_ref.shape[0]):
        xn_scr[rows, :] = _rms(x_ref[rows, :], g_ref[...]).astype(BF16)
    outs = (lru_ref, qkvb_ref, qkvc_ref)
    for c0, c1, oi, o0, scale in plan:
        acc = jnp.dot(xn_scr[...], w_ref[:, c0:c1], preferred_element_type=F32)
        if scale is not None:
            acc = acc * scale
        outs[oi][:, o0:o0 + (c1 - c0)] = acc.astype(outs[oi].dtype)


def _inproj(cfg, x, g, w, layer):
    t, d = x.shape
    tm = cfg.tm
    widths = (2 * LRU_WIDTH, 3 * DIL_WIDTH, SWA_WIDTH + 2 * SWA_KV_WIDTH)
    return pl.pallas_call(
        functools.partial(_inproj_kernel, plan=_inproj_plan()),
        grid=(t // tm,),
        in_specs=[
            pl.BlockSpec((tm, d), lambda i: (i, 0)),
            _resident((1, d)),
            _resident((d, IN_WIDTH), layer),
        ],
        out_specs=[pl.BlockSpec((tm, wd), lambda i: (i, 0)) for wd in widths],
        out_shape=[
            jax.ShapeDtypeStruct((t, widths[0]), F32),
            jax.ShapeDtypeStruct((t, widths[1]), BF16),
            jax.ShapeDtypeStruct((t, widths[2]), BF16),
        ],
        scratch_shapes=[pltpu.VMEM((tm, d), BF16)],
        compiler_params=_params(("parallel",)),
        name="mixer_inproj",
    )(x, g, w)


def _lru_fill_halo(first, last, xa_ref, xp_ref, xn_ref, xext, lc):
    xext[_HALO_ROWS:_HALO_ROWS + lc, :] = xa_ref[...]

    @pl.when(first)
    def _():
        xext[0:_HALO_ROWS, :] = jnp.zeros((_HALO_ROWS, LRU_WIDTH), F32)

    @pl.when(jnp.logical_not(first))
    def _():
        xext[0:_HALO_ROWS, :] = xp_ref[...]

    @pl.when(last)
    def _():
        xext[_HALO_ROWS + lc:, :] = jnp.zeros((_HALO_ROWS, LRU_WIDTH), F32)

    @pl.when(jnp.logical_not(last))
    def _():
        xext[_HALO_ROWS + lc:, :] = xn_ref[...]


def _lru_conv(xext, cw_ref, cb_ref, xc_ref, lc, rb):
    cb = cb_ref[...]
    taps = [cw_ref[j:j + 1, :] for j in range(CONV_WIDTH)]
    for blk in range(lc // rb):
        r0 = blk * rb
        xc_ref[r0:r0 + rb, :] = cb + sum(
            taps[j] * xext[r0 + _HALO_ROWS - CONV_LEFT + j:r0 + _HALO_ROWS - CONV_LEFT + j + rb, :]
            for j in range(CONV_WIDTH))


def _lru_gates(xc_ref, wg_ref, ba_ref, bx_ref, lam_ref, a_scr, u_scr, lc, rb):
    lam = lam_ref[...]
    neg = -lam
    softplus = jnp.maximum(neg, 0.0) + jnp.log1p(jnp.exp(-jnp.abs(neg)))
    for blk in range(lc // rb):
        r0 = blk * rb
        xc = xc_ref[r0:r0 + rb, :]
        xcb = xc.astype(BF16)
        for n in range(LRU_BLOCKS):
            cs = slice(n * LRU_BLOCK_WIDTH, (n + 1) * LRU_BLOCK_WIDTH)
            g = jnp.dot(xcb[:, cs], wg_ref[n], preferred_element_type=F32)
            r = jax.nn.sigmoid(g[:, :LRU_BLOCK_WIDTH] + ba_ref[:, cs])
            ig = jax.nn.sigmoid(g[:, LRU_BLOCK_WIDTH:] + bx_ref[:, cs])
            neg_log_a = (LRU_C * r) * softplus[:, cs]
            a = jnp.exp(-neg_log_a)
            one_minus_a2 = jnp.tanh(neg_log_a) * (1.0 + a * a)
            root = jnp.where(one_minus_a2 > 0.0, one_minus_a2 * lax.rsqrt(one_minus_a2), 0.0)
            u = root * (ig * xc[:, cs])
            groups = slice(r0 // _SCAN_GROUP, (r0 + rb) // _SCAN_GROUP)
            a_scr[groups, :, cs] = a.reshape(rb // _SCAN_GROUP, _SCAN_GROUP, LRU_BLOCK_WIDTH)
            u_scr[groups, :, cs] = u.reshape(rb // _SCAN_GROUP, _SCAN_GROUP, LRU_BLOCK_WIDTH)


def _lru_scan(reset, a_scr, u_scr, h_dst, carry, lc, reverse):
    @pl.when(reset)
    def _():
        carry[...] = jnp.zeros((1, LRU_WIDTH), F32)

    def group(i, h):
        g = lc // _SCAN_GROUP - 1 - i if reverse else i
        order = [_SCAN_GROUP - 1 - j if reverse else j for j in range(_SCAN_GROUP)]
        p = a_scr[g, order[0]:order[0] + 1, :]
        q = u_scr[g, order[0]:order[0] + 1, :]
        outs = [p * h + q]
        for j in order[1:]:
            a = a_scr[g, j:j + 1, :]
            q = a * q + u_scr[g, j:j + 1, :]
            p = a * p
            outs.append(p * h + q)
        for j, o in zip(order, outs):
            h_dst[g, j:j + 1, :] = o
        return outs[-1]

    carry[...] = lax.fori_loop(0, lc // _SCAN_GROUP, group, carry[...], unroll=2)


def _lru_fwd_kernel(flags_ref, xa_ref, xp_ref, xn_ref, cw_ref, cb_ref, wg_ref, ba_ref, bx_ref, lam_ref,
                    hf_ref, xc_ref, xext, a_scr, u_scr, carry, *, lc, rb):
    fl = flags_ref[pl.program_id(0)]
    first = (fl & 1) != 0
    last = (fl & 2) != 0
    _lru_fill_halo(first, last, xa_ref, xp_ref, xn_ref, xext, lc)
    _lru_conv(xext, cw_ref, cb_ref, xc_ref, lc, rb)
    _lru_gates(xc_ref, wg_ref, ba_ref, bx_ref, lam_ref, a_scr, u_scr, lc, rb)
    _lru_scan(first, a_scr, u_scr, hf_ref, carry, lc, reverse=False)


def _lru_bwd_kernel(flags_ref, xc_ref, gate_ref, hf_ref, wg_ref, ba_ref, bx_ref, lam_ref, gn_ref, y_ref,
                    a_scr, u_scr, h_scr, carry, *, lc, rb, nchunks):
    fl = flags_ref[nchunks - 1 - pl.program_id(0)]
    last = (fl & 2) != 0
    _lru_gates(xc_ref, wg_ref, ba_ref, bx_ref, lam_ref, a_scr, u_scr, lc, rb)
    _lru_scan(last, a_scr, u_scr, h_scr, carry, lc, reverse=True)
    for blk in range(lc // rb):
        rows = slice(blk * rb, (blk + 1) * rb)
        groups = slice(blk * rb // _SCAN_GROUP, (blk + 1) * rb // _SCAN_GROUP)
        h = (hf_ref[groups] + h_scr[groups]).reshape(rb, LRU_WIDTH)
        y = h * jax.nn.gelu(gate_ref[rows, :])
        y_ref[rows, :] = _rms(y, gn_ref[...]).astype(BF16)


def _lru(cfg, lru_in, cw, cb, wg, ba, bx, lam, gn):
    t = lru_in.shape[0]
    lc, rb = cfg.lru_chunk, cfg.lru_rows
    nchunks = t // lc
    hb = lc // _HALO_ROWS
    nhalo = t // _HALO_ROWS
    flags = _chunk_flags(cfg, lc)
    row = lambda: _resident((1, LRU_WIDTH))

    def specs(chunk_of):
        return [
            pl.BlockSpec((lc, LRU_WIDTH), lambda i, f: (chunk_of(i), 0)),
            pl.BlockSpec((_HALO_ROWS, LRU_WIDTH), lambda i, f: (jnp.maximum(chunk_of(i) * hb - 1, 0), 0)),
            pl.BlockSpec((_HALO_ROWS, LRU_WIDTH), lambda i, f: (jnp.minimum((chunk_of(i) + 1) * hb, nhalo - 1), 0)),
        ]

    conv_weights = [_resident((CONV_WIDTH, LRU_WIDTH)), row()]

    def weights(d):
        return [
            pl.BlockSpec((None, LRU_BLOCKS, LRU_BLOCK_WIDTH, 2 * LRU_BLOCK_WIDTH), lambda i, f: (d, 0, 0, 0)),
            pl.BlockSpec((None, 1, LRU_WIDTH), lambda i, f: (d, 0, 0)),
            pl.BlockSpec((None, 1, LRU_WIDTH), lambda i, f: (d, 0, 0)),
            pl.BlockSpec((None, 1, LRU_WIDTH), lambda i, f: (d, 0, 0)),
        ]

    grouped = (lc // _SCAN_GROUP, _SCAN_GROUP, LRU_WIDTH)
    scratch = [pltpu.VMEM(grouped, F32), pltpu.VMEM(grouped, F32)]
    carry = [pltpu.VMEM((1, LRU_WIDTH), F32)]

    fwd = lambda i: i
    hf, xc = pl.pallas_call(
        functools.partial(_lru_fwd_kernel, lc=lc, rb=rb),
        grid_spec=pltpu.PrefetchScalarGridSpec(
            num_scalar_prefetch=1, grid=(nchunks,),
            in_specs=specs(fwd) + conv_weights + weights(0),
            out_specs=[pl.BlockSpec(grouped, lambda i, f: (i, 0, 0)),
                       pl.BlockSpec((lc, LRU_WIDTH), lambda i, f: (i, 0))],
            scratch_shapes=[pltpu.VMEM((lc + 2 * _HALO_ROWS, LRU_WIDTH), F32)] + scratch + carry),
        out_shape=[jax.ShapeDtypeStruct((t // _SCAN_GROUP, _SCAN_GROUP, LRU_WIDTH), F32),
                   jax.ShapeDtypeStruct((t, LRU_WIDTH), F32)],
        compiler_params=_params(("arbitrary",)),
        name="lru_forward",
    )(flags, lru_in, lru_in, lru_in, cw, cb, wg, ba, bx, lam)

    bwd = lambda i: nchunks - 1 - i
    return pl.pallas_call(
        functools.partial(_lru_bwd_kernel, lc=lc, rb=rb, nchunks=nchunks),
        grid_spec=pltpu.PrefetchScalarGridSpec(
            num_scalar_prefetch=1, grid=(nchunks,),
            in_specs=[
                pl.BlockSpec((lc, LRU_WIDTH), lambda i, f: (bwd(i), 0)),
                pl.BlockSpec((lc, LRU_WIDTH), lambda i, f: (bwd(i), 1)),
                pl.BlockSpec(grouped, lambda i, f: (bwd(i), 0, 0)),
            ] + weights(1) + [row()],
            out_specs=pl.BlockSpec((lc, LRU_WIDTH), lambda i, f: (bwd(i), 0)),
            scratch_shapes=scratch + [pltpu.VMEM(grouped, F32)] + carry),
        out_shape=jax.ShapeDtypeStruct((t, LRU_WIDTH), BF16),
        compiler_params=_params(("arbitrary",)),
        name="lru_backward",
    )(flags, xc, lru_in, hf, wg, ba, bx, lam, gn)


def _alibi_slopes(n):
    return [2.0 ** (-8.0 * (i + 1) / n) for i in range(n)]


def _band_bias(wq, halo, dist_scale, hq, hkv):
    rep = hq // hkv
    slopes = _alibi_slopes(hq)
    qi = np.arange(wq)[:, None]
    kj = np.arange(wq + 2 * halo)[None, :]
    rel = np.abs(kj - halo - qi)
    out = np.empty((hkv, rep * wq, wq + 2 * halo), np.float32)
    for g in range(hkv):
        for r in range(rep):
            out[g, r * wq:(r + 1) * wq] = np.where(rel <= halo, -slopes[g * rep + r] * dist_scale * rel, _NEG)
    return out


def _edge_penalties(fl, wq, halo):
    col = lax.broadcasted_iota(jnp.int32, (1, wq + 2 * halo), 1)
    pen_first = jnp.where(col < halo, jnp.where((fl & 1) != 0, _NEG, 0.0), 0.0)
    pen_last = jnp.where(col >= wq + halo, jnp.where((fl & 2) != 0, _NEG, 0.0), 0.0)
    return pen_first, pen_last


def _attend(qs, ks, vs, biases):
    scores = [lax.dot_general(q, k, _NT, preferred_element_type=F32) for q, k in zip(qs, ks)]
    ms, ps = [], []
    for s, b in zip(scores, biases):
        if b is not None:
            s = s + b
        m = jnp.max(s, axis=-1, keepdims=True)
        ms.append(m)
        ps.append(jnp.exp(s - m).astype(BF16))
    ls, accs = [], []
    for p, v in zip(ps, vs):
        v1 = jnp.concatenate([v, jnp.ones(v.shape, v.dtype)], axis=1)
        out = jnp.dot(p, v1, preferred_element_type=F32)
        accs.append(out[:, :HEAD_DIM])
        ls.append(out[:, HEAD_DIM:])
    return ms, ls, accs


def _swa_kernel(flags_ref, q_ref, kc_ref, kp_ref, kn_ref, vc_ref, vp_ref, vn_ref, bias_ref, sink_ref, gn_ref,
                y_out, kbuf, vbuf, *, w, nsub, nb):
    rep = SWA_HEADS // SWA_KV_HEADS
    rows_total = nsub * w
    pen_first, pen_last = _edge_penalties(flags_ref[pl.program_id(0)], w, w)
    kbuf[0:w, :] = kp_ref[...]
    kbuf[w:w + rows_total, :] = kc_ref[...]
    kbuf[w + rows_total:, :] = kn_ref[...]
    vbuf[0:w, :] = vp_ref[...]
    vbuf[w:w + rows_total, :] = vc_ref[...]
    vbuf[w + rows_total:, :] = vn_ref[...]

    for j0 in range(0, nsub, nb):
        blocks = list(range(j0, min(j0 + nb, nsub)))
        qs, ks, vs, bs = [], [], [], []
        for j in blocks:
            rows = slice(j * w, (j + 1) * w)
            for g in range(SWA_KV_HEADS):
                gs = slice(g * HEAD_DIM, (g + 1) * HEAD_DIM)
                qs.append(jnp.concatenate(
                    [q_ref[rows, (g * rep + r) * HEAD_DIM:(g * rep + r + 1) * HEAD_DIM] for r in range(rep)], axis=0))
                ks.append(kbuf[j * w:(j + 3) * w, gs])
                vs.append(vbuf[j * w:(j + 3) * w, gs])
                b = bias_ref[g]
                if j == 0:
                    b = b + pen_first
                if j == nsub - 1:
                    b = b + pen_last
                bs.append(b)
        ms, ls, accs = _attend(qs, ks, vs, bs)
        for bi, j in enumerate(blocks):
            heads_out = []
            for g in range(SWA_KV_HEADS):
                idx = bi * SWA_KV_HEADS + g
                for r in range(rep):
                    h = g * rep + r
                    part = slice(r * w, (r + 1) * w)
                    m, l, acc = ms[idx][part], ls[idx][part], accs[idx][part]
                    factor = jax.nn.sigmoid(m + jnp.log(l) - sink_ref[:, h:h + 1])
                    heads_out.append((acc / l) * factor)
            y = jnp.concatenate(heads_out, axis=1)
            y_out[j * w:(j + 1) * w, :] = _rms(y, gn_ref[...]).astype(BF16)


def _swa(cfg, qkvc, gn, sink):
    t, c = qkvc.shape
    w = SWA_WINDOW
    rows = cfg.swa_rows
    nsub = rows // w
    nchunks = t // rows
    nblk = t // w
    qw, kvw = SWA_WIDTH, SWA_KV_WIDTH
    rep = SWA_HEADS // SWA_KV_HEADS
    assert qw % kvw == 0 and c == qw + 2 * kvw
    kcol, vcol = qw // kvw, qw // kvw + 1
    prev = lambda col: (lambda i, f: (jnp.maximum(i * nsub - 1, 0), col))
    nxt = lambda col: (lambda i, f: (jnp.minimum((i + 1) * nsub, nblk - 1), col))
    return pl.pallas_call(
        functools.partial(_swa_kernel, w=w, nsub=nsub, nb=cfg.swa_batch),
        grid_spec=pltpu.PrefetchScalarGridSpec(
            num_scalar_prefetch=1, grid=(nchunks,),
            in_specs=[
                pl.BlockSpec((rows, qw), lambda i, f: (i, 0)),
                pl.BlockSpec((rows, kvw), lambda i, f: (i, kcol)),
                pl.BlockSpec((w, kvw), prev(kcol)),
                pl.BlockSpec((w, kvw), nxt(kcol)),
                pl.BlockSpec((rows, kvw), lambda i, f: (i, vcol)),
                pl.BlockSpec((w, kvw), prev(vcol)),
                pl.BlockSpec((w, kvw), nxt(vcol)),
                _resident((SWA_KV_HEADS, rep * w, 3 * w)),
                _resident((1, 128)),
                _resident((1, qw)),
            ],
            out_specs=pl.BlockSpec((rows, qw), lambda i, f: (i, 0)),
            scratch_shapes=[pltpu.VMEM((rows + 2 * w, kvw), BF16), pltpu.VMEM((rows + 2 * w, kvw), BF16)]),
        out_shape=jax.ShapeDtypeStruct((t, qw), BF16),
        compiler_params=_params(("parallel",)),
        name="windowed_gqa_sink",
    )(_chunk_flags(cfg, rows), qkvc, qkvc, qkvc, qkvc, qkvc, qkvc, qkvc,
      jnp.asarray(_band_bias(w, w, 1, SWA_HEADS, SWA_KV_HEADS)), sink, gn)


def _perm_matrix(d):
    n = _PERM_ROWS
    per = n // d
    p = np.zeros((n, n), np.float32)
    for r in range(d):
        for m in range(per):
            p[r * per + m, d * m + r] = 1.0
    return p


def _dilated_qrows(c, d):
    return min(_DIL_QROWS_MAX, c // d)


def _dilated_kernel(flags_ref, q_ref, kp_ref, kc_ref, kn_ref, vp_ref, vc_ref, vn_ref, bias0, bias1, bias2,
                    perm_ref, gn_ref, y_ref, qd, kd, vd, ring_k, ring_v, bv0, bv1, bv2, acc_nat, m_nat, l_nat, *,
                    c, batch_rows, tail):
    w = _DIL_W
    nh = DIL_HEADS
    step = pl.program_id(0)
    fl = flags_ref[step]
    slot_prev, slot_cur, slot_next = lax.rem(step + 2, 3), lax.rem(step, 3), lax.rem(step + 1, 3)
    bias_refs = (bias0, bias1, bias2)
    biasv = (bv0, bv1, bv2)
    for p, (_, d) in enumerate(DIL_PATTERNS):
        wq = _dilated_qrows(c, d)
        pen_first, pen_last = _edge_penalties(fl, wq, w)
        for h in range(nh):
            b = bias_refs[p][h]
            if c // (d * wq) == 1:
                biasv[p][0, h] = b + pen_first + pen_last
            else:
                biasv[p][0, h] = b
                biasv[p][1, h] = b + pen_first
                biasv[p][2, h] = b + pen_last

    heads = [slice(h * HEAD_DIM, (h + 1) * HEAD_DIM) for h in range(nh)]

    def run_pattern(p, d, qsrc, ksrc, vsrc, qstride, kstride, first_pattern, last_pattern):
        wq = _dilated_qrows(c, d)
        win = wq + 2 * w
        nblk = c // (d * wq)
        nb = max(1, batch_rows // wq)
        assert (c // wq) % nb == 0
        assert d == 1 or not last_pattern

        def body(it, carry):
            qs, ks, vs, bs, where = [], [], [], [], []
            for b in range(nb):
                sb = it * nb + b
                r = sb // nblk
                s = sb % nblk
                var = 0 if nblk == 1 else jnp.where(s == 0, 1, 0) + jnp.where(s == nblk - 1, 2, 0)
                q0 = pl.multiple_of(r * qstride + s * wq, w)
                k0 = pl.multiple_of(r * kstride + s * wq, w)
                where.append(d * wq * s + r)
                for h in range(nh):
                    qs.append(qsrc[pl.ds(q0, wq), heads[h]])
                    if ksrc is None:
                        ks.append(jnp.concatenate([ring_k[sl, pl.ds(q0, wq), heads[h]]
                                                   for sl in (slot_prev, slot_cur, slot_next)], axis=0))
                        vs.append(jnp.concatenate([ring_v[sl, pl.ds(q0, wq), heads[h]]
                                                   for sl in (slot_prev, slot_cur, slot_next)], axis=0))
                    else:
                        ks.append(ksrc[pl.ds(k0, win), heads[h]])
                        vs.append(vsrc[pl.ds(k0, win), heads[h]])
                    bs.append(biasv[p][var, h])
            ms, ls, accs = _attend(qs, ks, vs, bs)
            for b in range(nb):
                nat = pl.ds(where[b], wq, stride=d) if d > 1 else pl.ds(pl.multiple_of(where[b], w), wq)
                outs = []
                for h in range(nh):
                    m, l, acc = ms[b * nh + h], ls[b * nh + h], accs[b * nh + h]
                    if not first_pattern:
                        m_p = m_nat[h, nat, :]
                        m_n = jnp.maximum(m_p, m)
                        alpha = jnp.exp(m_p - m_n)
                        beta = jnp.exp(m - m_n)
                        l = alpha * l_nat[h, nat, :] + beta * l
                        acc = alpha * acc_nat[h, nat, :] + beta * acc
                        m = m_n
                    if last_pattern:
                        outs.append(acc / l)
                    else:
                        acc_nat[h, nat, :] = acc
                        m_nat[h, nat, :] = jnp.broadcast_to(m, (wq, HEAD_DIM))
                        l_nat[h, nat, :] = jnp.broadcast_to(l, (wq, HEAD_DIM))
                if last_pattern:
                    y_ref[nat, :] = _rms(jnp.concatenate(outs, axis=1), gn_ref[...]).astype(BF16)
            return carry

        lax.fori_loop(0, c // (wq * nb), body, 0)

    def deinterleave(p, d):
        per = _PERM_ROWS // d
        halo_groups = (d * w) // _PERM_ROWS
        chunk_groups = c // _PERM_ROWS
        kstride = (chunk_groups + 2 * halo_groups) * per
        perm = perm_ref[p - 1]

        def move(src, row0, dst, g, stride):
            res = jnp.dot(perm, src[row0:row0 + _PERM_ROWS, :], preferred_element_type=F32).astype(BF16)
            for r in range(d):
                dst[r * stride + g * per:r * stride + (g + 1) * per, :] = res[r * per:(r + 1) * per]

        for g in range(chunk_groups):
            move(q_ref, g * _PERM_ROWS, qd, g, c // d)
        if d * w == c:
            def fill(slot, k_src, v_src):
                for src, ring in ((k_src, ring_k), (v_src, ring_v)):
                    for g in range(chunk_groups):
                        move(src, g * _PERM_ROWS, ring.at[slot], g, c // d)

            @pl.when(step == 0)
            def _():
                ring_k[2] = jnp.zeros((c, DIL_WIDTH), BF16)
                ring_v[2] = jnp.zeros((c, DIL_WIDTH), BF16)
                fill(0, kc_ref, vc_ref)

            fill(slot_next, kn_ref, vn_ref)
            return c // d, None
        for prev_ref, cur_ref, next_ref, dst in ((kp_ref, kc_ref, kn_ref, kd), (vp_ref, vc_ref, vn_ref, vd)):
            srcs = ([(prev_ref, tail - (halo_groups - g) * _PERM_ROWS) for g in range(halo_groups)]
                    + [(cur_ref, g * _PERM_ROWS) for g in range(chunk_groups)]
                    + [(next_ref, g * _PERM_ROWS) for g in range(halo_groups)])
            for g, (src, row0) in enumerate(srcs):
                move(src, row0, dst, g, kstride)
        return c // d, kstride

    order = sorted(range(len(DIL_PATTERNS)), key=lambda p: -DIL_PATTERNS[p][1])
    for idx, p in enumerate(order):
        d = DIL_PATTERNS[p][1]
        first, last = idx == 0, idx == len(order) - 1
        if d == 1:
            for prev_ref, cur_ref, next_ref, dst in ((kp_ref, kc_ref, kn_ref, kd), (vp_ref, vc_ref, vn_ref, vd)):
                dst[0:w, :] = prev_ref[tail - w:tail, :]
                dst[w:w + c, :] = cur_ref[...]
                dst[w + c:2 * w + c, :] = next_ref[0:w, :]
            run_pattern(p, d, q_ref, kd, vd, 0, 0, first, last)
        else:
            qstride, kstride = deinterleave(p, d)
            if kstride is None:
                run_pattern(p, d, qd, None, None, qstride, 0, first, last)
            else:
                run_pattern(p, d, qd, kd, vd, qstride, kstride, first, last)


def _dilated(cfg, qkvb, gn):
    t = qkvb.shape[0]
    c = cfg.dil_rows
    w = _DIL_W
    n = t // c
    dmax = max(d for _, d in DIL_PATTERNS)
    assert DIL_PATTERNS[0][1] == 1 and c % (dmax * w) == 0 and c >= dmax * w and c % _PERM_ROWS == 0
    assert all((d * w) % _PERM_ROWS == 0 for _, d in DIL_PATTERNS[1:])
    assert len(DIL_PATTERNS) == 3
    biases = [_band_bias(_dilated_qrows(c, d), w, d, DIL_HEADS, DIL_HEADS) for _, d in DIL_PATTERNS]
    variants = [1 if c // (d * _dilated_qrows(c, d)) == 1 else 3 for _, d in DIL_PATTERNS]
    perm = np.stack([_perm_matrix(d) for _, d in DIL_PATTERNS[1:]])
    blk = (c, DIL_WIDTH)
    assert dmax * w == c
    local = [d for _, d in DIL_PATTERNS if d * w < c]
    tail = max(d * w for d in local)
    assert c % tail == 0 and tail % _PERM_ROWS == 0
    class_rows = max((c // _PERM_ROWS + 2 * (d * w // _PERM_ROWS)) * _PERM_ROWS if d > 1 else c + 2 * w for d in local)
    prev = lambda col: (lambda i, f: (jnp.maximum(i * (c // tail) - 1, 0), col))
    cur = lambda col: (lambda i, f: (i, col))
    nxt = lambda col: (lambda i, f: (jnp.minimum(i + 1, n - 1), col))
    tail_blk = (tail, DIL_WIDTH)
    return pl.pallas_call(
        functools.partial(_dilated_kernel, c=c, batch_rows=cfg.dil_batch * w, tail=tail),
        grid_spec=pltpu.PrefetchScalarGridSpec(
            num_scalar_prefetch=1, grid=(n,),
            in_specs=[
                pl.BlockSpec(blk, cur(0)),
                pl.BlockSpec(tail_blk, prev(1)), pl.BlockSpec(blk, cur(1)), pl.BlockSpec(blk, nxt(1)),
                pl.BlockSpec(tail_blk, prev(2)), pl.BlockSpec(blk, cur(2)), pl.BlockSpec(blk, nxt(2)),
                _resident(biases[0].shape), _resident(biases[1].shape), _resident(biases[2].shape),
                _resident(perm.shape), _resident((1, DIL_WIDTH)),
            ],
            out_specs=pl.BlockSpec(blk, cur(0)),
            scratch_shapes=[
                pltpu.VMEM((c, DIL_WIDTH), BF16),
                pltpu.VMEM((class_rows, DIL_WIDTH), BF16), pltpu.VMEM((class_rows, DIL_WIDTH), BF16),
                pltpu.VMEM((3, c, DIL_WIDTH), BF16), pltpu.VMEM((3, c, DIL_WIDTH), BF16),
                pltpu.VMEM((variants[0],) + biases[0].shape, F32),
                pltpu.VMEM((variants[1],) + biases[1].shape, F32),
                pltpu.VMEM((variants[2],) + biases[2].shape, F32),
                pltpu.VMEM((DIL_HEADS, c, HEAD_DIM), F32),
                pltpu.VMEM((DIL_HEADS, c, HEAD_DIM), F32),
                pltpu.VMEM((DIL_HEADS, c, HEAD_DIM), F32),
            ]),
        out_shape=jax.ShapeDtypeStruct((t, DIL_WIDTH), BF16),
        compiler_params=_params(("arbitrary",)),
        name="dilated_attention",
    )(_chunk_flags(cfg, c), qkvb, qkvb, qkvb, qkvb, qkvb, qkvb, qkvb,
      jnp.asarray(biases[0]), jnp.asarray(biases[1]), jnp.asarray(biases[2]), jnp.asarray(perm, dtype=BF16), gn)


def _outproj_kernel(ya_ref, yb_ref, yc_ref, x_ref, w_ref, g_ref, o_ref, *, nchunk):
    d = o_ref.shape[1]
    b0, b1 = LRU_WIDTH, LRU_WIDTH + DIL_WIDTH
    for c in range(0, d, nchunk):
        cs = slice(c, c + nchunk)
        acc = jnp.dot(ya_ref[...], w_ref[0:b0, cs], preferred_element_type=F32)
        acc += jnp.dot(yb_ref[...], w_ref[b0:b1, cs], preferred_element_type=F32)
        acc += jnp.dot(yc_ref[...], w_ref[b1:, cs], preferred_element_type=F32)
        o_ref[:, cs] = acc
    for rows in _row_blocks(o_ref.shape[0]):
        o_ref[rows, :] = x_ref[rows, :] + _rms(o_ref[rows, :], g_ref[...])


def _outproj(cfg, ya, yb, yc, x, w, g, layer):
    t, d = x.shape
    tm = cfg.tm_out
    return pl.pallas_call(
        functools.partial(_outproj_kernel, nchunk=min(512, d)),
        grid=(t // tm,),
        in_specs=[
            pl.BlockSpec((tm, LRU_WIDTH), lambda i: (i, 0)),
            pl.BlockSpec((tm, DIL_WIDTH), lambda i: (i, 0)),
            pl.BlockSpec((tm, SWA_WIDTH), lambda i: (i, 0)),
            pl.BlockSpec((tm, d), lambda i: (i, 0)),
            _resident((MIX_WIDTH, d), layer),
            _resident((1, d)),
        ],
        out_specs=pl.BlockSpec((tm, d), lambda i: (i, 0)),
        out_shape=jax.ShapeDtypeStruct((t, d), F32),
        compiler_params=_params(("parallel",)),
        name="mixer_outproj",
    )(ya, yb, yc, x, w, g)


def _memkv_kernel(mem_ref, g_ref, w_ref, k_ref, v_ref):
    mn = _rms(mem_ref[...], g_ref[...]).astype(BF16)
    kv = jnp.dot(mn, w_ref[...], preferred_element_type=F32)
    k_ref[...] = kv[:, :MEM_WIDTH].astype(BF16)
    v_ref[...] = kv[:, MEM_WIDTH:].astype(BF16)


def _memkv(cfg, mem, g, wkv, layer):
    rows, d = mem.shape
    nm = cfg.n_mem
    shape = jax.ShapeDtypeStruct((rows, MEM_WIDTH), BF16)
    return pl.pallas_call(
        _memkv_kernel,
        grid=(rows // nm,),
        in_specs=[pl.BlockSpec((nm, d), lambda i: (i, 0)), _resident((1, d)),
                  _resident((d, 2 * MEM_WIDTH), layer)],
        out_specs=[pl.BlockSpec((nm, MEM_WIDTH), lambda i: (i, 0))] * 2,
        out_shape=[shape, shape],
        compiler_params=_params(("parallel",)),
        name="memory_kv",
    )(mem, g, wkv)


def _cross_kernel(seq_ref, x_ref, gpre_ref, wq_ref, k_ref, v_ref, wo_ref, gpost_ref, o_ref, xn_scr, o_scr, *,
                  nchunk):
    del seq_ref
    tm, d = x_ref.shape
    for rows in _row_blocks(tm):
        xn_scr[rows, :] = _rms(x_ref[rows, :], gpre_ref[...]).astype(BF16)
    q = (jnp.dot(xn_scr[...], wq_ref[...], preferred_element_type=F32) * _QK_SCALE).astype(BF16)
    heads = [slice(h * HEAD_DIM, (h + 1) * HEAD_DIM) for h in range(MEM_HEADS)]
    _, ls, accs = _attend([q[:, hs] for hs in heads], [k_ref[:, hs] for hs in heads],
                          [v_ref[:, hs] for hs in heads], [None] * MEM_HEADS)
    for hs, l, acc in zip(heads, ls, accs):
        o_scr[:, hs] = (acc / l).astype(BF16)
    for c in range(0, d, nchunk):
        cs = slice(c, min(c + nchunk, d))
        o_ref[:, cs] = jnp.dot(o_scr[...], wo_ref[:, cs], preferred_element_type=F32)
    for rows in _row_blocks(tm):
        o_ref[rows, :] = x_ref[rows, :] + _rms(o_ref[rows, :], gpost_ref[...])


def _cross(cfg, x, gpre, wq, kmem, vmem, wo, gpost, layer):
    t, d = x.shape
    tm, nm = cfg.tm_cross, cfg.n_mem
    seqs, _ = _sequences(cfg)
    seq_of_tile = []
    for si, (_, length) in enumerate(seqs):
        assert length % tm == 0
        seq_of_tile += [si] * (length // tm)
    seq_of_tile = jnp.asarray(np.asarray(seq_of_tile, np.int32))
    return pl.pallas_call(
        functools.partial(_cross_kernel, nchunk=512),
        grid_spec=pltpu.PrefetchScalarGridSpec(
            num_scalar_prefetch=1, grid=(t // tm,),
            in_specs=[
                pl.BlockSpec((tm, d), lambda i, s: (i, 0)),
                _resident((1, d)),
                _resident((d, MEM_WIDTH), layer),
                pl.BlockSpec((nm, MEM_WIDTH), lambda i, s: (s[i], 0)),
                pl.BlockSpec((nm, MEM_WIDTH), lambda i, s: (s[i], 0)),
                _resident((MEM_WIDTH, d), layer),
                _resident((1, d)),
            ],
            out_specs=pl.BlockSpec((tm, d), lambda i, s: (i, 0)),
            scratch_shapes=[pltpu.VMEM((tm, d), BF16), pltpu.VMEM((tm, MEM_WIDTH), BF16)]),
        out_shape=jax.ShapeDtypeStruct((t, d), F32),
        compiler_params=_params(("parallel",)),
        name="memory_cross_attention",
    )(seq_of_tile, x, gpre, wq, kmem, vmem, wo, gpost)


def _ffn_kernel(x_ref, gpre_ref, w1_ref, w2_ref, gpost_ref, o_ref, xn_scr, h_scr, *, nchunk):
    f = pl.program_id(1)
    tf = w1_ref.shape[1]
    d = w2_ref.shape[1]

    tm = x_ref.shape[0]
    row_blocks = [slice(r, min(r + _NORM_ROWS, tm)) for r in range(0, tm, _NORM_ROWS)]

    @pl.when(f == 0)
    def _():
        for rows in row_blocks:
            xn_scr[rows, :] = _rms(x_ref[rows, :], gpre_ref[...]).astype(BF16)
            o_ref[rows, :] = jnp.zeros((rows.stop - rows.start, d), F32)

    for c in range(0, tf, nchunk):
        cs = slice(c, min(c + nchunk, tf))
        h = jnp.dot(xn_scr[...], w1_ref[:, cs], preferred_element_type=F32)
        h_scr[:, cs] = jnp.square(jnp.maximum(h, 0.0)).astype(BF16)
    for c in range(0, d, nchunk):
        cs = slice(c, min(c + nchunk, d))
        o_ref[:, cs] += jnp.dot(h_scr[...], w2_ref[:, cs], preferred_element_type=F32)

    @pl.when(f == pl.num_programs(1) - 1)
    def _():
        for rows in row_blocks:
            o_ref[rows, :] = x_ref[rows, :] + _rms(o_ref[rows, :], gpost_ref[...])


def _ffn(cfg, x, gpre, w1, w2, gpost, layer):
    t, d = x.shape
    tm, tf = cfg.tm_ffn, cfg.tf
    dff = w1.shape[2]
    return pl.pallas_call(
        functools.partial(_ffn_kernel, nchunk=512),
        grid=(t // tm, dff // tf),
        in_specs=[
            pl.BlockSpec((tm, d), lambda i, f: (i, 0)),
            _resident((1, d)),
            pl.BlockSpec((None, d, tf), lambda i, f: (layer, 0, f)),
            pl.BlockSpec((None, tf, d), lambda i, f: (layer, f, 0)),
            _resident((1, d)),
        ],
        out_specs=pl.BlockSpec((tm, d), lambda i, f: (i, 0)),
        out_shape=jax.ShapeDtypeStruct((t, d), F32),
        scratch_shapes=[pltpu.VMEM((tm, d), BF16), pltpu.VMEM((tm, tf), BF16)],
        compiler_params=_params(("parallel", "arbitrary")),
        name="squared_relu_mlp",
    )(x, gpre, w1, w2, gpost)


def _forward(cfg, x, mem, p):
    row = lambda a: a.reshape(1, -1).astype(F32)
    for l in range(cfg.depth):
        lru_in, qkvb, qkvc = _inproj(cfg, x, row(p["mix_norm_pre"][l]), p["w_in"], l)
        gn = p["group_norm"][l]
        wg = jnp.concatenate([p["lru_wa"][l], p["lru_wx"][l]], axis=-1).astype(BF16)
        ya = _lru(cfg, lru_in, p["conv_w"][l], row(p["conv_b"][l]), wg,
                  p["lru_ba"][l][:, None, :], p["lru_bx"][l][:, None, :], p["lru_lam"][l][:, None, :],
                  row(gn[:LRU_WIDTH]))
        yb = _dilated(cfg, qkvb, row(gn[LRU_WIDTH:LRU_WIDTH + DIL_WIDTH]))
        sink = jnp.zeros((1, 128), F32).at[0, :SWA_HEADS].set(p["swa_sink"][l].astype(F32))
        yc = _swa(cfg, qkvc, row(gn[LRU_WIDTH + DIL_WIDTH:]), sink)
        x = _outproj(cfg, ya, yb, yc, x, p["w_out"], row(p["mix_norm_post"][l]), l)
        kmem, vmem = _memkv(cfg, mem, row(p["mem_kv_norm"][l]), p["w_mkv"], l)
        x = _cross(cfg, x, row(p["mem_norm_pre"][l]), p["w_mq"], kmem, vmem, p["w_mo"],
                   row(p["mem_norm_post"][l]), l)
        x = _ffn(cfg, x, row(p["ffn_norm_pre"][l]), p["w_ff1"], p["w_ff2"], row(p["ffn_norm_post"][l]), l)
    return x


def _run(cfg, x_prompt, x_sample, mem_prompt, mem_sample, mix_norm_pre, mix_norm_post, w_in, conv_w, conv_b,
         lru_wa, lru_ba, lru_wx, lru_bx, lru_lam, swa_sink, group_norm, w_out, mem_norm_pre, mem_norm_post,
         mem_kv_norm, w_mq, w_mk, w_mv, w_mo, ffn_norm_pre, ffn_norm_post, w_ff1, w_ff2):
    d = cfg.d_model
    p = dict(
        mix_norm_pre=mix_norm_pre, mix_norm_post=mix_norm_post, w_in=w_in.astype(BF16), conv_w=conv_w,
        conv_b=conv_b, lru_wa=lru_wa, lru_ba=lru_ba, lru_wx=lru_wx, lru_bx=lru_bx, lru_lam=lru_lam,
        swa_sink=swa_sink, group_norm=group_norm, w_out=w_out.astype(BF16), mem_norm_pre=mem_norm_pre,
        mem_norm_post=mem_norm_post, mem_kv_norm=mem_kv_norm, w_mq=w_mq.astype(BF16),
        w_mkv=jnp.concatenate([w_mk, w_mv], axis=-1).astype(BF16), w_mo=w_mo.astype(BF16),
        ffn_norm_pre=ffn_norm_pre, ffn_norm_post=ffn_norm_post, w_ff1=w_ff1.astype(BF16),
        w_ff2=w_ff2.astype(BF16))
    outs = []
    for group, x, mem in zip(cfg.groups, (x_prompt, x_sample), (mem_prompt, mem_sample)):
        sub = cfg._replace(groups=(group,))
        outs.append(_forward(sub, x.reshape(-1, d), mem.reshape(-1, d), p).reshape(x.shape))
    return tuple(outs)


def kernel(x_prompt, x_sample, mem_prompt, mem_sample, mix_norm_pre, mix_norm_post, w_in, conv_w, conv_b, lru_wa,
           lru_ba, lru_wx, lru_bx, lru_lam, swa_sink, group_norm, w_out, mem_norm_pre, mem_norm_post, mem_kv_norm,
           w_mq, w_mk, w_mv, w_mo, ffn_norm_pre, ffn_norm_post, w_ff1, w_ff2):
    return _run(_CFG, x_prompt, x_sample, mem_prompt, mem_sample, mix_norm_pre, mix_norm_post, w_in, conv_w,
                conv_b, lru_wa, lru_ba, lru_wx, lru_bx, lru_lam, swa_sink, group_norm, w_out, mem_norm_pre,
                mem_norm_post, mem_kv_norm, w_mq, w_mk, w_mv, w_mo, ffn_norm_pre, ffn_norm_post, w_ff1, w_ff2)
```

```python
import functools
from typing import NamedTuple

import numpy as np
import jax
import jax.numpy as jnp
from jax import lax
from jax.experimental import pallas as pl
from jax.experimental.pallas import tpu as pltpu

F32 = jnp.float32
BF16 = jnp.bfloat16

D_MODEL = 2048
BATCH = 8
SEQ = 4096
DEPTH = 4
DEC_BATCH = 1
DEC_SEQ = 16384
HEAD_DIM = 128
LRU_WIDTH = 512
LRU_BLOCKS = 4
LRU_BLOCK_WIDTH = LRU_WIDTH // LRU_BLOCKS
CONV_WIDTH = 4
CONV_LEFT = 2
LRU_C = 8.0
DIL_HEADS = 6
DIL_PATTERNS = ((128, 1), (512, 4), (2048, 16))
SWA_HEADS = 6
SWA_KV_HEADS = 2
SWA_WINDOW = 128
DIL_WIDTH = DIL_HEADS * HEAD_DIM
SWA_WIDTH = SWA_HEADS * HEAD_DIM
SWA_KV_WIDTH = SWA_KV_HEADS * HEAD_DIM
MIX_WIDTH = LRU_WIDTH + DIL_WIDTH + SWA_WIDTH
IN_WIDTH = 2 * LRU_WIDTH + 3 * DIL_WIDTH + SWA_WIDTH + 2 * SWA_KV_WIDTH
N_MEM = 256
MEM_HEADS = 4
MEM_WIDTH = MEM_HEADS * HEAD_DIM
D_FF = 4 * D_MODEL
EPS = 1e-6

_NEG = -1e30
_QK_SCALE = HEAD_DIM ** -0.5
_HALO_ROWS = 8
_NORM_ROWS = 128
_SCAN_GROUP = 8
_DIL_W = DIL_PATTERNS[0][0] // (2 * DIL_PATTERNS[0][1])
assert all(wn // (2 * d) == _DIL_W for wn, d in DIL_PATTERNS)
_DIL_QROWS_MAX = 128
_PERM_ROWS = 256
_V7X_VMEM_BYTES = 64 * 1024 * 1024
_VMEM_LIMIT = _V7X_VMEM_BYTES - 3 * 1024 * 1024
_NT = (((1,), (1,)), ((), ()))


class _Cfg(NamedTuple):
    d_model: int
    d_ff: int
    depth: int
    groups: tuple
    n_mem: int
    tm: int
    tm_out: int
    tm_cross: int
    tm_ffn: int
    tf: int
    lru_chunk: int
    lru_rows: int
    swa_rows: int
    dil_rows: int
    swa_batch: int
    dil_batch: int


_CFG = _Cfg(d_model=D_MODEL, d_ff=D_FF, depth=DEPTH, groups=((BATCH, SEQ), (DEC_BATCH, DEC_SEQ)),
            n_mem=N_MEM, tm=512, tm_out=1024, tm_cross=1024, tm_ffn=1024, tf=1024, lru_chunk=2048, lru_rows=256,
            swa_rows=1024,
            dil_rows=1024, swa_batch=2, dil_batch=4)


def _sequences(cfg):
    out, start = [], 0
    for n, length in cfg.groups:
        for _ in range(n):
            out.append((start, length))
            start += length
    return out, start


def _chunk_flags(cfg, rows):
    seqs, total = _sequences(cfg)
    starts = {s for s, _ in seqs}
    ends = {s + l for s, l in seqs}
    for s, l in seqs:
        assert l % rows == 0, (l, rows)
    n = total // rows
    flags = np.zeros((n,), np.int32)
    for c in range(n):
        flags[c] = (1 if c * rows in starts else 0) | (2 if (c + 1) * rows in ends else 0)
    return jnp.asarray(flags)


def _params(semantics):
    return pltpu.CompilerParams(dimension_semantics=semantics, vmem_limit_bytes=_VMEM_LIMIT)


def _rms(x, g):
    ms = jnp.mean(x * x, axis=-1, keepdims=True)
    return x * lax.rsqrt(ms + EPS) * g


def _row_blocks(n):
    return [slice(r, min(r + _NORM_ROWS, n)) for r in range(0, n, _NORM_ROWS)]


def _resident(shape, layer=None):
    if layer is None:
        return pl.BlockSpec(shape, lambda *_: (0,) * len(shape), pipeline_mode=pl.Buffered(1))
    return pl.BlockSpec((None,) + tuple(shape), lambda *_: (layer,) + (0,) * len(shape),
                        pipeline_mode=pl.Buffered(1))


def _inproj_plan():
    lru_w = 2 * LRU_WIDTH
    dil_w = 3 * DIL_WIDTH
    segs = [
        (0, lru_w, 0, None),
        (lru_w, lru_w + DIL_WIDTH, 1, _QK_SCALE),
        (lru_w + DIL_WIDTH, lru_w + dil_w, 1, None),
        (lru_w + dil_w, lru_w + dil_w + SWA_WIDTH, 2, _QK_SCALE),
        (lru_w + dil_w + SWA_WIDTH, IN_WIDTH, 2, None),
    ]
    base = {0: 0, 1: lru_w, 2: lru_w + dil_w}
    plan = []
    for c0, c1, oi, scale in segs:
        c = c0
        while c < c1:
            n = min(512, c1 - c)
            plan.append((c, c + n, oi, c - base[oi], scale))
            c += n
    return tuple(plan)


def _inproj_kernel(x_ref, g_ref, w_ref, lru_ref, qkvb_ref, qkvc_ref, xn_scr, *, plan):
    for rows in _row_blocks(x_ref.shape[0]):
        xn_scr[rows, :] = _rms(x_ref[rows, :], g_ref[...]).astype(BF16)
    outs = (lru_ref, qkvb_ref, qkvc_ref)
    for c0, c1, oi, o0, scale in plan:
        acc = jnp.dot(xn_scr[...], w_ref[:, c0:c1], preferred_element_type=F32)
        if scale is not None:
            acc = acc * scale
        outs[oi][:, o0:o0 + (c1 - c0)] = acc.astype(outs[oi].dtype)


def _inproj(cfg, x, g, w, layer):
    t, d = x.shape
    tm = cfg.tm
    widths = (2 * LRU_WIDTH, 3 * DIL_WIDTH, SWA_WIDTH + 2 * SWA_KV_WIDTH)
    return pl.pallas_call(
        functools.partial(_inproj_kernel, plan=_inproj_plan()),
        grid=(t // tm,),
        in_specs=[
            pl.BlockSpec((tm, d), lambda i: (i, 0)),
            _resident((1, d)),
            _resident((d, IN_WIDTH), layer),
        ],
        out_specs=[pl.BlockSpec((tm, wd), lambda i: (i, 0)) for wd in widths],
        out_shape=[
            jax.ShapeDtypeStruct((t, widths[0]), F32),
            jax.ShapeDtypeStruct((t, widths[1]), BF16),
            jax.ShapeDtypeStruct((t, widths[2]), BF16),
        ],
        scratch_shapes=[pltpu.VMEM((tm, d), BF16)],
        compiler_params=_params(("parallel",)),
        name="mixer_inproj",
    )(x, g, w)


def _lru_fill_halo(first, last, xa_ref, xp_ref, xn_ref, xext, lc):
    xext[_HALO_ROWS:_HALO_ROWS + lc, :] = xa_ref[...]

    @pl.when(first)
    def _():
        xext[0:_HALO_ROWS, :] = jnp.zeros((_HALO_ROWS, LRU_WIDTH), F32)

    @pl.when(jnp.logical_not(first))
    def _():
        xext[0:_HALO_ROWS, :] = xp_ref[...]

    @pl.when(last)
    def _():
        xext[_HALO_ROWS + lc:, :] = jnp.zeros((_HALO_ROWS, LRU_WIDTH), F32)

    @pl.when(jnp.logical_not(last))
    def _():
        xext[_HALO_ROWS + lc:, :] = xn_ref[...]


def _lru_conv(xext, cw_ref, cb_ref, xc_ref, lc, rb):
    cb = cb_ref[...]
    taps = [cw_ref[j:j + 1, :] for j in range(CONV_WIDTH)]
    for blk in range(lc // rb):
        r0 = blk * rb
        xc_ref[r0:r0 + rb, :] = cb + sum(
            taps[j] * xext[r0 + _HALO_ROWS - CONV_LEFT + j:r0 + _HALO_ROWS - CONV_LEFT + j + rb, :]
            for j in range(CONV_WIDTH))


def _lru_gates(xc_ref, wg_ref, ba_ref, bx_ref, lam_ref, a_scr, u_scr, lc, rb):
    lam = lam_ref[...]
    neg = -lam
    softplus = jnp.maximum(neg, 0.0) + jnp.log1p(jnp.exp(-jnp.abs(neg)))
    for blk in range(lc // rb):
        r0 = blk * rb
        xc = xc_ref[r0:r0 + rb, :]
        xcb = xc.astype(BF16)
        for n in range(LRU_BLOCKS):
            cs = slice(n * LRU_BLOCK_WIDTH, (n + 1) * LRU_BLOCK_WIDTH)
            g = jnp.dot(xcb[:, cs], wg_ref[n], preferred_element_type=F32)
            r = jax.nn.sigmoid(g[:, :LRU_BLOCK_WIDTH] + ba_ref[:, cs])
            ig = jax.nn.sigmoid(g[:, LRU_BLOCK_WIDTH:] + bx_ref[:, cs])
            neg_log_a = (LRU_C * r) * softplus[:, cs]
            a = jnp.exp(-neg_log_a)
            one_minus_a2 = jnp.tanh(neg_log_a) * (1.0 + a * a)
            root = jnp.where(one_minus_a2 > 0.0, one_minus_a2 * lax.rsqrt(one_minus_a2), 0.0)
            u = root * (ig * xc[:, cs])
            groups = slice(r0 // _SCAN_GROUP, (r0 + rb) // _SCAN_GROUP)
            a_scr[groups, :, cs] = a.reshape(rb // _SCAN_GROUP, _SCAN_GROUP, LRU_BLOCK_WIDTH)
            u_scr[groups, :, cs] = u.reshape(rb // _SCAN_GROUP, _SCAN_GROUP, LRU_BLOCK_WIDTH)


def _lru_scan(reset, a_scr, u_scr, h_dst, carry, lc, reverse):
    @pl.when(reset)
    def _():
        carry[...] = jnp.zeros((1, LRU_WIDTH), F32)

    def group(i, h):
        g = lc // _SCAN_GROUP - 1 - i if reverse else i
        order = [_SCAN_GROUP - 1 - j if reverse else j for j in range(_SCAN_GROUP)]
        p = a_scr[g, order[0]:order[0] + 1, :]
        q = u_scr[g, order[0]:order[0] + 1, :]
        outs = [p * h + q]
        for j in order[1:]:
            a = a_scr[g, j:j + 1, :]
            q = a * q + u_scr[g, j:j + 1, :]
            p = a * p
            outs.append(p * h + q)
        for j, o in zip(order, outs):
            h_dst[g, j:j + 1, :] = o
        return outs[-1]

    carry[...] = lax.fori_loop(0, lc // _SCAN_GROUP, group, carry[...], unroll=4)


def _lru_fwd_kernel(flags_ref, xa_ref, xp_ref, xn_ref, cw_ref, cb_ref, wg_ref, ba_ref, bx_ref, lam_ref,
                    hf_ref, xc_ref, xext, a_scr, u_scr, carry, *, lc, rb):
    fl = flags_ref[pl.program_id(0)]
    first = (fl & 1) != 0
    last = (fl & 2) != 0
    _lru_fill_halo(first, last, xa_ref, xp_ref, xn_ref, xext, lc)
    _lru_conv(xext, cw_ref, cb_ref, xc_ref, lc, rb)
    _lru_gates(xc_ref, wg_ref, ba_ref, bx_ref, lam_ref, a_scr, u_scr, lc, rb)
    _lru_scan(first, a_scr, u_scr, hf_ref, carry, lc, reverse=False)


def _lru_bwd_kernel(flags_ref, xc_ref, gate_ref, hf_ref, wg_ref, ba_ref, bx_ref, lam_ref, gn_ref, y_ref,
                    a_scr, u_scr, h_scr, carry, *, lc, rb, nchunks):
    fl = flags_ref[nchunks - 1 - pl.program_id(0)]
    last = (fl & 2) != 0
    _lru_gates(xc_ref, wg_ref, ba_ref, bx_ref, lam_ref, a_scr, u_scr, lc, rb)
    _lru_scan(last, a_scr, u_scr, h_scr, carry, lc, reverse=True)
    for blk in range(lc // rb):
        rows = slice(blk * rb, (blk + 1) * rb)
        groups = slice(blk * rb // _SCAN_GROUP, (blk + 1) * rb // _SCAN_GROUP)
        h = (hf_ref[groups] + h_scr[groups]).reshape(rb, LRU_WIDTH)
        y = h * jax.nn.gelu(gate_ref[rows, :])
        y_ref[rows, :] = _rms(y, gn_ref[...]).astype(BF16)


def _lru(cfg, lru_in, cw, cb, wg, ba, bx, lam, gn):
    t = lru_in.shape[0]
    lc, rb = cfg.lru_chunk, cfg.lru_rows
    nchunks = t // lc
    hb = lc // _HALO_ROWS
    nhalo = t // _HALO_ROWS
    flags = _chunk_flags(cfg, lc)
    row = lambda: _resident((1, LRU_WIDTH))

    def specs(chunk_of):
        return [
            pl.BlockSpec((lc, LRU_WIDTH), lambda i, f: (chunk_of(i), 0)),
            pl.BlockSpec((_HALO_ROWS, LRU_WIDTH), lambda i, f: (jnp.maximum(chunk_of(i) * hb - 1, 0), 0)),
            pl.BlockSpec((_HALO_ROWS, LRU_WIDTH), lambda i, f: (jnp.minimum((chunk_of(i) + 1) * hb, nhalo - 1), 0)),
        ]

    conv_weights = [_resident((CONV_WIDTH, LRU_WIDTH)), row()]

    def weights(d):
        return [
            pl.BlockSpec((None, LRU_BLOCKS, LRU_BLOCK_WIDTH, 2 * LRU_BLOCK_WIDTH), lambda i, f: (d, 0, 0, 0)),
            pl.BlockSpec((None, 1, LRU_WIDTH), lambda i, f: (d, 0, 0)),
            pl.BlockSpec((None, 1, LRU_WIDTH), lambda i, f: (d, 0, 0)),
            pl.BlockSpec((None, 1, LRU_WIDTH), lambda i, f: (d, 0, 0)),
        ]

    grouped = (lc // _SCAN_GROUP, _SCAN_GROUP, LRU_WIDTH)
    scratch = [pltpu.VMEM(grouped, F32), pltpu.VMEM(grouped, F32)]
    carry = [pltpu.VMEM((1, LRU_WIDTH), F32)]

    fwd = lambda i: i
    hf, xc = pl.pallas_call(
        functools.partial(_lru_fwd_kernel, lc=lc, rb=rb),
        grid_spec=pltpu.PrefetchScalarGridSpec(
            num_scalar_prefetch=1, grid=(nchunks,),
            in_specs=specs(fwd) + conv_weights + weights(0),
            out_specs=[pl.BlockSpec(grouped, lambda i, f: (i, 0, 0)),
                       pl.BlockSpec((lc, LRU_WIDTH), lambda i, f: (i, 0))],
            scratch_shapes=[pltpu.VMEM((lc + 2 * _HALO_ROWS, LRU_WIDTH), F32)] + scratch + carry),
        out_shape=[jax.ShapeDtypeStruct((t // _SCAN_GROUP, _SCAN_GROUP, LRU_WIDTH), F32),
                   jax.ShapeDtypeStruct((t, LRU_WIDTH), F32)],
        compiler_params=_params(("arbitrary",)),
        name="lru_forward",
    )(flags, lru_in, lru_in, lru_in, cw, cb, wg, ba, bx, lam)

    bwd = lambda i: nchunks - 1 - i
    return pl.pallas_call(
        functools.partial(_lru_bwd_kernel, lc=lc, rb=rb, nchunks=nchunks),
        grid_spec=pltpu.PrefetchScalarGridSpec(
            num_scalar_prefetch=1, grid=(nchunks,),
            in_specs=[
                pl.BlockSpec((lc, LRU_WIDTH), lambda i, f: (bwd(i), 0)),
                pl.BlockSpec((lc, LRU_WIDTH), lambda i, f: (bwd(i), 1)),
                pl.BlockSpec(grouped, lambda i, f: (bwd(i), 0, 0)),
            ] + weights(1) + [row()],
            out_specs=pl.BlockSpec((lc, LRU_WIDTH), lambda i, f: (bwd(i), 0)),
            scratch_shapes=scratch + [pltpu.VMEM(grouped, F32)] + carry),
        out_shape=jax.ShapeDtypeStruct((t, LRU_WIDTH), BF16),
        compiler_params=_params(("arbitrary",)),
        name="lru_backward",
    )(flags, xc, lru_in, hf, wg, ba, bx, lam, gn)


def _alibi_slopes(n):
    return [2.0 ** (-8.0 * (i + 1) / n) for i in range(n)]


def _band_bias(wq, halo, dist_scale, hq, hkv):
    rep = hq // hkv
    slopes = _alibi_slopes(hq)
    qi = np.arange(wq)[:, None]
    kj = np.arange(wq + 2 * halo)[None, :]
    rel = np.abs(kj - halo - qi)
    out = np.empty((hkv, rep * wq, wq + 2 * halo), np.float32)
    for g in range(hkv):
        for r in range(rep):
            out[g, r * wq:(r + 1) * wq] = np.where(rel <= halo, -slopes[g * rep + r] * dist_scale * rel, _NEG)
    return out


def _band_bias_variants(wq, halo, dist_scale, heads):
    base = _band_bias(wq, halo, dist_scale, heads, heads)
    col = np.arange(wq + 2 * halo)
    out = np.stack([base] * 4)
    out[1::2, :, :, col < halo] = _NEG
    out[2:, :, :, col >= wq + halo] = _NEG
    return out


def _edge_penalties(fl, wq, halo):
    col = lax.broadcasted_iota(jnp.int32, (1, wq + 2 * halo), 1)
    pen_first = jnp.where(col < halo, jnp.where((fl & 1) != 0, _NEG, 0.0), 0.0)
    pen_last = jnp.where(col >= wq + halo, jnp.where((fl & 2) != 0, _NEG, 0.0), 0.0)
    return pen_first, pen_last


def _attend(qs, ks, vs, biases):
    scores = [lax.dot_general(q, k, _NT, preferred_element_type=F32) for q, k in zip(qs, ks)]
    ms, ps = [], []
    for s, b in zip(scores, biases):
        if b is not None:
            s = s + b
        m = jnp.max(s, axis=-1, keepdims=True)
        ms.append(m)
        ps.append(jnp.exp(s - m).astype(BF16))
    ls, accs = [], []
    for p, v in zip(ps, vs):
        v1 = jnp.concatenate([v, jnp.ones(v.shape, v.dtype)], axis=1)
        out = jnp.dot(p, v1, preferred_element_type=F32)
        accs.append(out[:, :HEAD_DIM])
        ls.append(out[:, HEAD_DIM:])
    return ms, ls, accs


def _swa_kernel(flags_ref, q_ref, kc_ref, kp_ref, kn_ref, vc_ref, vp_ref, vn_ref, bias_ref, sink_ref, gn_ref,
                y_out, kbuf, vbuf, *, w, nsub, nb):
    rep = SWA_HEADS // SWA_KV_HEADS
    rows_total = nsub * w
    pen_first, pen_last = _edge_penalties(flags_ref[pl.program_id(0)], w, w)
    kbuf[0:w, :] = kp_ref[...]
    kbuf[w:w + rows_total, :] = kc_ref[...]
    kbuf[w + rows_total:, :] = kn_ref[...]
    vbuf[0:w, :] = vp_ref[...]
    vbuf[w:w + rows_total, :] = vc_ref[...]
    vbuf[w + rows_total:, :] = vn_ref[...]

    for j0 in range(0, nsub, nb):
        blocks = list(range(j0, min(j0 + nb, nsub)))
        qs, ks, vs, bs = [], [], [], []
        for j in blocks:
            rows = slice(j * w, (j + 1) * w)
            for g in range(SWA_KV_HEADS):
                gs = slice(g * HEAD_DIM, (g + 1) * HEAD_DIM)
                qs.append(jnp.concatenate(
                    [q_ref[rows, (g * rep + r) * HEAD_DIM:(g * rep + r + 1) * HEAD_DIM] for r in range(rep)], axis=0))
                ks.append(kbuf[j * w:(j + 3) * w, gs])
                vs.append(vbuf[j * w:(j + 3) * w, gs])
                b = bias_ref[g]
                if j == 0:
                    b = b + pen_first
                if j == nsub - 1:
                    b = b + pen_last
                bs.append(b)
        ms, ls, accs = _attend(qs, ks, vs, bs)
        for bi, j in enumerate(blocks):
            heads_out = []
            for g in range(SWA_KV_HEADS):
                idx = bi * SWA_KV_HEADS + g
                for r in range(rep):
                    h = g * rep + r
                    part = slice(r * w, (r + 1) * w)
                    m, l, acc = ms[idx][part], ls[idx][part], accs[idx][part]
                    factor = jax.nn.sigmoid(m + jnp.log(l) - sink_ref[:, h:h + 1])
                    heads_out.append((acc / l) * factor)
            y = jnp.concatenate(heads_out, axis=1)
            y_out[j * w:(j + 1) * w, :] = _rms(y, gn_ref[...]).astype(BF16)


def _swa(cfg, qkvc, gn, sink):
    t, c = qkvc.shape
    w = SWA_WINDOW
    rows = cfg.swa_rows
    nsub = rows // w
    nchunks = t // rows
    nblk = t // w
    qw, kvw = SWA_WIDTH, SWA_KV_WIDTH
    rep = SWA_HEADS // SWA_KV_HEADS
    assert qw % kvw == 0 and c == qw + 2 * kvw
    kcol, vcol = qw // kvw, qw // kvw + 1
    prev = lambda col: (lambda i, f: (jnp.maximum(i * nsub - 1, 0), col))
    nxt = lambda col: (lambda i, f: (jnp.minimum((i + 1) * nsub, nblk - 1), col))
    return pl.pallas_call(
        functools.partial(_swa_kernel, w=w, nsub=nsub, nb=cfg.swa_batch),
        grid_spec=pltpu.PrefetchScalarGridSpec(
            num_scalar_prefetch=1, grid=(nchunks,),
            in_specs=[
                pl.BlockSpec((rows, qw), lambda i, f: (i, 0)),
                pl.BlockSpec((rows, kvw), lambda i, f: (i, kcol)),
                pl.BlockSpec((w, kvw), prev(kcol)),
                pl.BlockSpec((w, kvw), nxt(kcol)),
                pl.BlockSpec((rows, kvw), lambda i, f: (i, vcol)),
                pl.BlockSpec((w, kvw), prev(vcol)),
                pl.BlockSpec((w, kvw), nxt(vcol)),
                _resident((SWA_KV_HEADS, rep * w, 3 * w)),
                _resident((1, 128)),
                _resident((1, qw)),
            ],
            out_specs=pl.BlockSpec((rows, qw), lambda i, f: (i, 0)),
            scratch_shapes=[pltpu.VMEM((rows + 2 * w, kvw), BF16), pltpu.VMEM((rows + 2 * w, kvw), BF16)]),
        out_shape=jax.ShapeDtypeStruct((t, qw), BF16),
        compiler_params=_params(("parallel",)),
        name="windowed_gqa_sink",
    )(_chunk_flags(cfg, rows), qkvc, qkvc, qkvc, qkvc, qkvc, qkvc, qkvc,
      jnp.asarray(_band_bias(w, w, 1, SWA_HEADS, SWA_KV_HEADS)), sink, gn)


def _perm_matrix(d):
    n = _PERM_ROWS
    per = n // d
    p = np.zeros((n, n), np.float32)
    for r in range(d):
        for m in range(per):
            p[r * per + m, d * m + r] = 1.0
    return p


def _dilated_qrows(c, d):
    return min(_DIL_QROWS_MAX, c // d)


def _dilated_kernel(flags_ref, q_ref, kp_ref, kc_ref, kn_ref, vp_ref, vc_ref, vn_ref, bias0, bias1, bias2,
                    perm_ref, gn_ref, y_ref, qd, kd, vd, ring_k, ring_v, acc_nat, m_nat, l_nat, *,
                    c, batch_rows, tail):
    w = _DIL_W
    nh = DIL_HEADS
    step = pl.program_id(0)
    fl = flags_ref[step]
    slot_prev, slot_cur, slot_next = lax.rem(step + 2, 3), lax.rem(step, 3), lax.rem(step + 1, 3)
    bias_refs = (bias0, bias1, bias2)
    starts_sequence = fl & 1
    ends_sequence = (fl >> 1) & 1
    heads = [slice(h * HEAD_DIM, (h + 1) * HEAD_DIM) for h in range(nh)]

    def run_pattern(p, d, qsrc, ksrc, vsrc, qstride, kstride, first_pattern, last_pattern):
        wq = _dilated_qrows(c, d)
        win = wq + 2 * w
        nblk = c // (d * wq)
        nb = max(1, batch_rows // wq)
        assert (c // wq) % nb == 0
        assert d == 1 or not last_pattern

        def body(it, carry):
            qs, ks, vs, bs, where = [], [], [], [], []
            for b in range(nb):
                sb = it * nb + b
                r = sb // nblk
                s = sb % nblk
                var = jnp.where(s == 0, starts_sequence, 0) + 2 * jnp.where(s == nblk - 1, ends_sequence, 0)
                q0 = pl.multiple_of(r * qstride + s * wq, w)
                k0 = pl.multiple_of(r * kstride + s * wq, w)
                where.append(d * wq * s + r)
                for h in range(nh):
                    qs.append(qsrc[pl.ds(q0, wq), heads[h]])
                    if ksrc is None:
                        ks.append(jnp.concatenate([ring_k[sl, pl.ds(q0, wq), heads[h]]
                                                   for sl in (slot_prev, slot_cur, slot_next)], axis=0))
                        vs.append(jnp.concatenate([ring_v[sl, pl.ds(q0, wq), heads[h]]
                                                   for sl in (slot_prev, slot_cur, slot_next)], axis=0))
                    else:
                        ks.append(ksrc[pl.ds(k0, win), heads[h]])
                        vs.append(vsrc[pl.ds(k0, win), heads[h]])
                    bs.append(bias_refs[p][var, h])
            ms, ls, accs = _attend(qs, ks, vs, bs)
            for b in range(nb):
                nat = pl.ds(where[b], wq, stride=d) if d > 1 else pl.ds(pl.multiple_of(where[b], w), wq)
                outs = []
                for h in range(nh):
                    m, l, acc = ms[b * nh + h], ls[b * nh + h], accs[b * nh + h]
                    if not first_pattern:
                        m_p = m_nat[h, nat, :]
                        m_n = jnp.maximum(m_p, m)
                        alpha = jnp.exp(m_p - m_n)
                        beta = jnp.exp(m - m_n)
                        l = alpha * l_nat[h, nat, :] + beta * l
                        acc = alpha * acc_nat[h, nat, :] + beta * acc
                        m = m_n
                    if last_pattern:
                        outs.append(acc / l)
                    else:
                        acc_nat[h, nat, :] = acc
                        m_nat[h, nat, :] = jnp.broadcast_to(m, (wq, HEAD_DIM))
                        l_nat[h, nat, :] = jnp.broadcast_to(l, (wq, HEAD_DIM))
                if last_pattern:
                    y_ref[nat, :] = _rms(jnp.concatenate(outs, axis=1), gn_ref[...]).astype(BF16)
            return carry

        lax.fori_loop(0, c // (wq * nb), body, 0)

    def deinterleave(p, d):
        per = _PERM_ROWS // d
        halo_groups = (d * w) // _PERM_ROWS
        chunk_groups = c // _PERM_ROWS
        kstride = (chunk_groups + 2 * halo_groups) * per
        perm = perm_ref[p - 1]

        def move(src, row0, dst, g, stride):
            res = jnp.dot(perm, src[row0:row0 + _PERM_ROWS, :], preferred_element_type=F32).astype(BF16)
            for r in range(d):
                dst[r * stride + g * per:r * stride + (g + 1) * per, :] = res[r * per:(r + 1) * per]

        for g in range(chunk_groups):
            move(q_ref, g * _PERM_ROWS, qd, g, c // d)
        if d * w == c:
            def fill(slot, k_src, v_src):
                for src, ring in ((k_src, ring_k), (v_src, ring_v)):
                    for g in range(chunk_groups):
                        move(src, g * _PERM_ROWS, ring.at[slot], g, c // d)

            @pl.when(step == 0)
            def _():
                ring_k[2] = jnp.zeros((c, DIL_WIDTH), BF16)
                ring_v[2] = jnp.zeros((c, DIL_WIDTH), BF16)
                fill(0, kc_ref, vc_ref)

            fill(slot_next, kn_ref, vn_ref)
            return c // d, None
        for prev_ref, cur_ref, next_ref, dst in ((kp_ref, kc_ref, kn_ref, kd), (vp_ref, vc_ref, vn_ref, vd)):
            srcs = ([(prev_ref, tail - (halo_groups - g) * _PERM_ROWS) for g in range(halo_groups)]
                    + [(cur_ref, g * _PERM_ROWS) for g in range(chunk_groups)]
                    + [(next_ref, g * _PERM_ROWS) for g in range(halo_groups)])
            for g, (src, row0) in enumerate(srcs):
                move(src, row0, dst, g, kstride)
        return c // d, kstride

    order = sorted(range(len(DIL_PATTERNS)), key=lambda p: -DIL_PATTERNS[p][1])
    for idx, p in enumerate(order):
        d = DIL_PATTERNS[p][1]
        first, last = idx == 0, idx == len(order) - 1
        if d == 1:
            for prev_ref, cur_ref, next_ref, dst in ((kp_ref, kc_ref, kn_ref, kd), (vp_ref, vc_ref, vn_ref, vd)):
                dst[0:w, :] = prev_ref[tail - w:tail, :]
                dst[w:w + c, :] = cur_ref[...]
                dst[w + c:2 * w + c, :] = next_ref[0:w, :]
            run_pattern(p, d, q_ref, kd, vd, 0, 0, first, last)
        else:
            qstride, kstride = deinterleave(p, d)
            if kstride is None:
                run_pattern(p, d, qd, None, None, qstride, 0, first, last)
            else:
                run_pattern(p, d, qd, kd, vd, qstride, kstride, first, last)


def _dilated(cfg, qkvb, gn):
    t = qkvb.shape[0]
    c = cfg.dil_rows
    w = _DIL_W
    n = t // c
    dmax = max(d for _, d in DIL_PATTERNS)
    assert DIL_PATTERNS[0][1] == 1 and c % (dmax * w) == 0 and c >= dmax * w and c % _PERM_ROWS == 0
    assert all((d * w) % _PERM_ROWS == 0 for _, d in DIL_PATTERNS[1:])
    assert len(DIL_PATTERNS) == 3
    biases = [_band_bias_variants(_dilated_qrows(c, d), w, d, DIL_HEADS) for _, d in DIL_PATTERNS]
    perm = np.stack([_perm_matrix(d) for _, d in DIL_PATTERNS[1:]])
    blk = (c, DIL_WIDTH)
    assert dmax * w == c
    local = [d for _, d in DIL_PATTERNS if d * w < c]
    tail = max(d * w for d in local)
    assert c % tail == 0 and tail % _PERM_ROWS == 0
    class_rows = max((c // _PERM_ROWS + 2 * (d * w // _PERM_ROWS)) * _PERM_ROWS if d > 1 else c + 2 * w for d in local)
    prev = lambda col: (lambda i, f: (jnp.maximum(i * (c // tail) - 1, 0), col))
    cur = lambda col: (lambda i, f: (i, col))
    nxt = lambda col: (lambda i, f: (jnp.minimum(i + 1, n - 1), col))
    tail_blk = (tail, DIL_WIDTH)
    return pl.pallas_call(
        functools.partial(_dilated_kernel, c=c, batch_rows=cfg.dil_batch * w, tail=tail),
        grid_spec=pltpu.PrefetchScalarGridSpec(
            num_scalar_prefetch=1, grid=(n,),
            in_specs=[
                pl.BlockSpec(blk, cur(0)),
                pl.BlockSpec(tail_blk, prev(1)), pl.BlockSpec(blk, cur(1)), pl.BlockSpec(blk, nxt(1)),
                pl.BlockSpec(tail_blk, prev(2)), pl.BlockSpec(blk, cur(2)), pl.BlockSpec(blk, nxt(2)),
                _resident(biases[0].shape), _resident(biases[1].shape), _resident(biases[2].shape),
                _resident(perm.shape), _resident((1, DIL_WIDTH)),
            ],
            out_specs=pl.BlockSpec(blk, cur(0)),
            scratch_shapes=[
                pltpu.VMEM((c, DIL_WIDTH), BF16),
                pltpu.VMEM((class_rows, DIL_WIDTH), BF16), pltpu.VMEM((class_rows, DIL_WIDTH), BF16),
                pltpu.VMEM((3, c, DIL_WIDTH), BF16), pltpu.VMEM((3, c, DIL_WIDTH), BF16),
                pltpu.VMEM((DIL_HEADS, c, HEAD_DIM), F32),
                pltpu.VMEM((DIL_HEADS, c, HEAD_DIM), F32),
                pltpu.VMEM((DIL_HEADS, c, HEAD_DIM), F32),
            ]),
        out_shape=jax.ShapeDtypeStruct((t, DIL_WIDTH), BF16),
        compiler_params=_params(("arbitrary",)),
        name="dilated_attention",
    )(_chunk_flags(cfg, c), qkvb, qkvb, qkvb, qkvb, qkvb, qkvb, qkvb,
      jnp.asarray(biases[0]), jnp.asarray(biases[1]), jnp.asarray(biases[2]), jnp.asarray(perm, dtype=BF16), gn)


def _outproj_kernel(ya_ref, yb_ref, yc_ref, x_ref, w_ref, g_ref, o_ref, *, nchunk):
    d = o_ref.shape[1]
    b0, b1 = LRU_WIDTH, LRU_WIDTH + DIL_WIDTH
    for c in range(0, d, nchunk):
        cs = slice(c, c + nchunk)
        acc = jnp.dot(ya_ref[...], w_ref[0:b0, cs], preferred_element_type=F32)
        acc += jnp.dot(yb_ref[...], w_ref[b0:b1, cs], preferred_element_type=F32)
        acc += jnp.dot(yc_ref[...], w_ref[b1:, cs], preferred_element_type=F32)
        o_ref[:, cs] = acc
    for rows in _row_blocks(o_ref.shape[0]):
        o_ref[rows, :] = x_ref[rows, :] + _rms(o_ref[rows, :], g_ref[...])


def _outproj(cfg, ya, yb, yc, x, w, g, layer):
    t, d = x.shape
    tm = cfg.tm_out
    return pl.pallas_call(
        functools.partial(_outproj_kernel, nchunk=min(512, d)),
        grid=(t // tm,),
        in_specs=[
            pl.BlockSpec((tm, LRU_WIDTH), lambda i: (i, 0)),
            pl.BlockSpec((tm, DIL_WIDTH), lambda i: (i, 0)),
            pl.BlockSpec((tm, SWA_WIDTH), lambda i: (i, 0)),
            pl.BlockSpec((tm, d), lambda i: (i, 0)),
            _resident((MIX_WIDTH, d), layer),
            _resident((1, d)),
        ],
        out_specs=pl.BlockSpec((tm, d), lambda i: (i, 0)),
        out_shape=jax.ShapeDtypeStruct((t, d), F32),
        compiler_params=_params(("parallel",)),
        name="mixer_outproj",
    )(ya, yb, yc, x, w, g)


def _memkv_kernel(mem_ref, g_ref, w_ref, k_ref, v_ref):
    mn = _rms(mem_ref[...], g_ref[...]).astype(BF16)
    kv = jnp.dot(mn, w_ref[...], preferred_element_type=F32)
    k_ref[...] = kv[:, :MEM_WIDTH].astype(BF16)
    v_ref[...] = kv[:, MEM_WIDTH:].astype(BF16)


def _memkv(cfg, mem, g, wkv, layer):
    rows, d = mem.shape
    nm = cfg.n_mem
    shape = jax.ShapeDtypeStruct((rows, MEM_WIDTH), BF16)
    return pl.pallas_call(
        _memkv_kernel,
        grid=(rows // nm,),
        in_specs=[pl.BlockSpec((nm, d), lambda i: (i, 0)), _resident((1, d)),
                  _resident((d, 2 * MEM_WIDTH), layer)],
        out_specs=[pl.BlockSpec((nm, MEM_WIDTH), lambda i: (i, 0))] * 2,
        out_shape=[shape, shape],
        compiler_params=_params(("parallel",)),
        name="memory_kv",
    )(mem, g, wkv)


def _cross_kernel(seq_ref, x_ref, gpre_ref, wq_ref, k_ref, v_ref, wo_ref, gpost_ref, o_ref, xn_scr, o_scr, *,
                  nchunk):
    del seq_ref
    tm, d = x_ref.shape
    for rows in _row_blocks(tm):
        xn_scr[rows, :] = _rms(x_ref[rows, :], gpre_ref[...]).astype(BF16)
    q = (jnp.dot(xn_scr[...], wq_ref[...], preferred_element_type=F32) * _QK_SCALE).astype(BF16)
    heads = [slice(h * HEAD_DIM, (h + 1) * HEAD_DIM) for h in range(MEM_HEADS)]
    _, ls, accs = _attend([q[:, hs] for hs in heads], [k_ref[:, hs] for hs in heads],
                          [v_ref[:, hs] for hs in heads], [None] * MEM_HEADS)
    for hs, l, acc in zip(heads, ls, accs):
        o_scr[:, hs] = (acc / l).astype(BF16)
    for c in range(0, d, nchunk):
        cs = slice(c, min(c + nchunk, d))
        o_ref[:, cs] = jnp.dot(o_scr[...], wo_ref[:, cs], preferred_element_type=F32)
    for rows in _row_blocks(tm):
        o_ref[rows, :] = x_ref[rows, :] + _rms(o_ref[rows, :], gpost_ref[...])


def _cross(cfg, x, gpre, wq, kmem, vmem, wo, gpost, layer):
    t, d = x.shape
    tm, nm = cfg.tm_cross, cfg.n_mem
    seqs, _ = _sequences(cfg)
    seq_of_tile = []
    for si, (_, length) in enumerate(seqs):
        assert length % tm == 0
        seq_of_tile += [si] * (length // tm)
    seq_of_tile = jnp.asarray(np.asarray(seq_of_tile, np.int32))
    return pl.pallas_call(
        functools.partial(_cross_kernel, nchunk=512),
        grid_spec=pltpu.PrefetchScalarGridSpec(
            num_scalar_prefetch=1, grid=(t // tm,),
            in_specs=[
                pl.BlockSpec((tm, d), lambda i, s: (i, 0)),
                _resident((1, d)),
                _resident((d, MEM_WIDTH), layer),
                pl.BlockSpec((nm, MEM_WIDTH), lambda i, s: (s[i], 0)),
                pl.BlockSpec((nm, MEM_WIDTH), lambda i, s: (s[i], 0)),
                _resident((MEM_WIDTH, d), layer),
                _resident((1, d)),
            ],
            out_specs=pl.BlockSpec((tm, d), lambda i, s: (i, 0)),
            scratch_shapes=[pltpu.VMEM((tm, d), BF16), pltpu.VMEM((tm, MEM_WIDTH), BF16)]),
        out_shape=jax.ShapeDtypeStruct((t, d), F32),
        compiler_params=_params(("parallel",)),
        name="memory_cross_attention",
    )(seq_of_tile, x, gpre, wq, kmem, vmem, wo, gpost)


def _ffn_kernel(x_ref, gpre_ref, w1_ref, w2_ref, gpost_ref, o_ref, xn_scr, h_scr, *, nchunk):
    f = pl.program_id(1)
    tf = w1_ref.shape[1]
    d = w2_ref.shape[1]

    tm = x_ref.shape[0]
    row_blocks = [slice(r, min(r + _NORM_ROWS, tm)) for r in range(0, tm, _NORM_ROWS)]

    @pl.when(f == 0)
    def _():
        for rows in row_blocks:
            xn_scr[rows, :] = _rms(x_ref[rows, :], gpre_ref[...]).astype(BF16)
            o_ref[rows, :] = jnp.zeros((rows.stop - rows.start, d), F32)

    for c in range(0, tf, nchunk):
        cs = slice(c, min(c + nchunk, tf))
        h = jnp.dot(xn_scr[...], w1_ref[:, cs], preferred_element_type=F32)
        h_scr[:, cs] = jnp.square(jnp.maximum(h, 0.0)).astype(BF16)
    for c in range(0, d, nchunk):
        cs = slice(c, min(c + nchunk, d))
        o_ref[:, cs] += jnp.dot(h_scr[...], w2_ref[:, cs], preferred_element_type=F32)

    @pl.when(f == pl.num_programs(1) - 1)
    def _():
        for rows in row_blocks:
            o_ref[rows, :] = x_ref[rows, :] + _rms(o_ref[rows, :], gpost_ref[...])


def _ffn(cfg, x, gpre, w1, w2, gpost, layer):
    t, d = x.shape
    tm, tf = cfg.tm_ffn, cfg.tf
    dff = w1.shape[2]
    return pl.pallas_call(
        functools.partial(_ffn_kernel, nchunk=512),
        grid=(t // tm, dff // tf),
        in_specs=[
            pl.BlockSpec((tm, d), lambda i, f: (i, 0)),
            _resident((1, d)),
            pl.BlockSpec((None, d, tf), lambda i, f: (layer, 0, f)),
            pl.BlockSpec((None, tf, d), lambda i, f: (layer, f, 0)),
            _resident((1, d)),
        ],
        out_specs=pl.BlockSpec((tm, d), lambda i, f: (i, 0)),
        out_shape=jax.ShapeDtypeStruct((t, d), F32),
        scratch_shapes=[pltpu.VMEM((tm, d), BF16), pltpu.VMEM((tm, tf), BF16)],
        compiler_params=_params(("parallel", "arbitrary")),
        name="squared_relu_mlp",
    )(x, gpre, w1, w2, gpost)


def _forward(cfg, x, mem, p):
    row = lambda a: a.reshape(1, -1).astype(F32)
    for l in range(cfg.depth):
        lru_in, qkvb, qkvc = _inproj(cfg, x, row(p["mix_norm_pre"][l]), p["w_in"], l)
        gn = p["group_norm"][l]
        wg = jnp.concatenate([p["lru_wa"][l], p["lru_wx"][l]], axis=-1).astype(BF16)
        ya = _lru(cfg, lru_in, p["conv_w"][l], row(p["conv_b"][l]), wg,
                  p["lru_ba"][l][:, None, :], p["lru_bx"][l][:, None, :], p["lru_lam"][l][:, None, :],
                  row(gn[:LRU_WIDTH]))
        yb = _dilated(cfg, qkvb, row(gn[LRU_WIDTH:LRU_WIDTH + DIL_WIDTH]))
        sink = jnp.zeros((1, 128), F32).at[0, :SWA_HEADS].set(p["swa_sink"][l].astype(F32))
        yc = _swa(cfg, qkvc, row(gn[LRU_WIDTH + DIL_WIDTH:]), sink)
        x = _outproj(cfg, ya, yb, yc, x, p["w_out"], row(p["mix_norm_post"][l]), l)
        kmem, vmem = _memkv(cfg, mem, row(p["mem_kv_norm"][l]), p["w_mkv"], l)
        x = _cross(cfg, x, row(p["mem_norm_pre"][l]), p["w_mq"], kmem, vmem, p["w_mo"],
                   row(p["mem_norm_post"][l]), l)
        x = _ffn(cfg, x, row(p["ffn_norm_pre"][l]), p["w_ff1"], p["w_ff2"], row(p["ffn_norm_post"][l]), l)
    return x


def _run(cfg, x_prompt, x_sample, mem_prompt, mem_sample, mix_norm_pre, mix_norm_post, w_in, conv_w, conv_b,
         lru_wa, lru_ba, lru_wx, lru_bx, lru_lam, swa_sink, group_norm, w_out, mem_norm_pre, mem_norm_post,
         mem_kv_norm, w_mq, w_mk, w_mv, w_mo, ffn_norm_pre, ffn_norm_post, w_ff1, w_ff2):
    d = cfg.d_model
    p = dict(
        mix_norm_pre=mix_norm_pre, mix_norm_post=mix_norm_post, w_in=w_in.astype(BF16), conv_w=conv_w,
        conv_b=conv_b, lru_wa=lru_wa, lru_ba=lru_ba, lru_wx=lru_wx, lru_bx=lru_bx, lru_lam=lru_lam,
        swa_sink=swa_sink, group_norm=group_norm, w_out=w_out.astype(BF16), mem_norm_pre=mem_norm_pre,
        mem_norm_post=mem_norm_post, mem_kv_norm=mem_kv_norm, w_mq=w_mq.astype(BF16),
        w_mkv=jnp.concatenate([w_mk, w_mv], axis=-1).astype(BF16), w_mo=w_mo.astype(BF16),
        ffn_norm_pre=ffn_norm_pre, ffn_norm_post=ffn_norm_post, w_ff1=w_ff1.astype(BF16),
        w_ff2=w_ff2.astype(BF16))
    outs = []
    for group, x, mem in zip(cfg.groups, (x_prompt, x_sample), (mem_prompt, mem_sample)):
        sub = cfg._replace(groups=(group,))
        outs.append(_forward(sub, x.reshape(-1, d), mem.reshape(-1, d), p).reshape(x.shape))
    return tuple(outs)


def kernel(x_prompt, x_sample, mem_prompt, mem_sample, mix_norm_pre, mix_norm_post, w_in, conv_w, conv_b, lru_wa,
           lru_ba, lru_wx, lru_bx, lru_lam, swa_sink, group_norm, w_out, mem_norm_pre, mem_norm_post, mem_kv_norm,
           w_mq, w_mk, w_mv, w_mo, ffn_norm_pre, ffn_norm_post, w_ff1, w_ff2):
    return _run(_CFG, x_prompt, x_sample, mem_prompt, mem_sample, mix_norm_pre, mix_norm_post, w_in, conv_w,
                conv_b, lru_wa, lru_ba, lru_wx, lru_bx, lru_lam, swa_sink, group_norm, w_out, mem_norm_pre,
                mem_norm_post, mem_kv_norm, w_mq, w_mk, w_mv, w_mo, ffn_norm_pre, ffn_norm_post, w_ff1, w_ff2)
```

```python
import functools
from typing import NamedTuple

import numpy as np
import jax
import jax.numpy as jnp
from jax import lax
from jax.experimental import pallas as pl
from jax.experimental.pallas import tpu as pltpu

F32 = jnp.float32
BF16 = jnp.bfloat16

D_MODEL = 2048
BATCH = 8
SEQ = 4096
DEPTH = 4
DEC_BATCH = 1
DEC_SEQ = 16384
HEAD_DIM = 128
LRU_WIDTH = 512
LRU_BLOCKS = 4
LRU_BLOCK_WIDTH = LRU_WIDTH // LRU_BLOCKS
CONV_WIDTH = 4
CONV_LEFT = 2
LRU_C = 8.0
DIL_HEADS = 6
DIL_PATTERNS = ((128, 1), (512, 4), (2048, 16))
SWA_HEADS = 6
SWA_KV_HEADS = 2
SWA_WINDOW = 128
DIL_WIDTH = DIL_HEADS * HEAD_DIM
SWA_WIDTH = SWA_HEADS * HEAD_DIM
SWA_KV_WIDTH = SWA_KV_HEADS * HEAD_DIM
MIX_WIDTH = LRU_WIDTH + DIL_WIDTH + SWA_WIDTH
IN_WIDTH = 2 * LRU_WIDTH + 3 * DIL_WIDTH + SWA_WIDTH + 2 * SWA_KV_WIDTH
N_MEM = 256
MEM_HEADS = 4
MEM_WIDTH = MEM_HEADS * HEAD_DIM
D_FF = 4 * D_MODEL
EPS = 1e-6

_NEG = -1e30
_QK_SCALE = HEAD_DIM ** -0.5
_HALO_ROWS = 8
_NORM_ROWS = 128
_SCAN_GROUP = 8
_DIL_W = DIL_PATTERNS[0][0] // (2 * DIL_PATTERNS[0][1])
assert all(wn // (2 * d) == _DIL_W for wn, d in DIL_PATTERNS)
_DIL_QROWS_MAX = 128
_PERM_ROWS = 256
_V7X_VMEM_BYTES = 64 * 1024 * 1024
_VMEM_LIMIT = _V7X_VMEM_BYTES - 3 * 1024 * 1024
_NT = (((1,), (1,)), ((), ()))


class _Cfg(NamedTuple):
    d_model: int
    d_ff: int
    depth: int
    groups: tuple
    n_mem: int
    tm: int
    tm_out: int
    tm_cross: int
    tm_ffn: int
    tf: int
    lru_chunk: int
    lru_rows: int
    swa_rows: int
    dil_rows: int
    swa_batch: int
    dil_batch: int


_CFG = _Cfg(d_model=D_MODEL, d_ff=D_FF, depth=DEPTH, groups=((BATCH, SEQ), (DEC_BATCH, DEC_SEQ)),
            n_mem=N_MEM, tm=512, tm_out=1024, tm_cross=1024, tm_ffn=1024, tf=1024, lru_chunk=2048, lru_rows=256,
            swa_rows=1024,
            dil_rows=1024, swa_batch=2, dil_batch=2)


def _sequences(cfg):
    out, start = [], 0
    for n, length in cfg.groups:
        for _ in range(n):
            out.append((start, length))
            start += length
    return out, start


def _chunk_flags(cfg, rows):
    seqs, total = _sequences(cfg)
    starts = {s for s, _ in seqs}
    ends = {s + l for s, l in seqs}
    for s, l in seqs:
        assert l % rows == 0, (l, rows)
    n = total // rows
    flags = np.zeros((n,), np.int32)
    for c in range(n):
        flags[c] = (1 if c * rows in starts else 0) | (2 if (c + 1) * rows in ends else 0)
    return jnp.asarray(flags)


def _params(semantics):
    return pltpu.CompilerParams(dimension_semantics=semantics, vmem_limit_bytes=_VMEM_LIMIT)


def _rms(x, g):
    ms = jnp.mean(x * x, axis=-1, keepdims=True)
    return x * lax.rsqrt(ms + EPS) * g


def _row_blocks(n):
    return [slice(r, min(r + _NORM_ROWS, n)) for r in range(0, n, _NORM_ROWS)]


def _resident(shape, layer=None):
    if layer is None:
        return pl.BlockSpec(shape, lambda *_: (0,) * len(shape), pipeline_mode=pl.Buffered(1))
    return pl.BlockSpec((None,) + tuple(shape), lambda *_: (layer,) + (0,) * len(shape),
                        pipeline_mode=pl.Buffered(1))


def _inproj_plan():
    lru_w = 2 * LRU_WIDTH
    dil_w = 3 * DIL_WIDTH
    segs = [
        (0, lru_w, 0, None),
        (lru_w, lru_w + DIL_WIDTH, 1, _QK_SCALE),
        (lru_w + DIL_WIDTH, lru_w + dil_w, 1, None),
        (lru_w + dil_w, lru_w + dil_w + SWA_WIDTH, 2, _QK_SCALE),
        (lru_w + dil_w + SWA_WIDTH, IN_WIDTH, 2, None),
    ]
    base = {0: 0, 1: lru_w, 2: lru_w + dil_w}
    plan = []
    for c0, c1, oi, scale in segs:
        c = c0
        while c < c1:
            n = min(512, c1 - c)
            plan.append((c, c + n, oi, c - base[oi], scale))
            c += n
    return tuple(plan)


def _inproj_kernel(x_ref, g_ref, w_ref, lru_ref, qkvb_ref, qkvc_ref, xn_scr, *, plan):
    for rows in _row_blocks(x_ref.shape[0]):
        xn_scr[rows, :] = _rms(x_ref[rows, :], g_ref[...]).astype(BF16)
    outs = (lru_ref, qkvb_ref, qkvc_ref)
    for c0, c1, oi, o0, scale in plan:
        acc = jnp.dot(xn_scr[...], w_ref[:, c0:c1], preferred_element_type=F32)
        if scale is not None:
            acc = acc * scale
        outs[oi][:, o0:o0 + (c1 - c0)] = acc.astype(outs[oi].dtype)


def _inproj(cfg, x, g, w, layer):
    t, d = x.shape
    tm = cfg.tm
    widths = (2 * LRU_WIDTH, 3 * DIL_WIDTH, SWA_WIDTH + 2 * SWA_KV_WIDTH)
    return pl.pallas_call(
        functools.partial(_inproj_kernel, plan=_inproj_plan()),
        grid=(t // tm,),
        in_specs=[
            pl.BlockSpec((tm, d), lambda i: (i, 0)),
            _resident((1, d)),
            _resident((d, IN_WIDTH), layer),
        ],
        out_specs=[pl.BlockSpec((tm, wd), lambda i: (i, 0)) for wd in widths],
        out_shape=[
            jax.ShapeDtypeStruct((t, widths[0]), F32),
            jax.ShapeDtypeStruct((t, widths[1]), BF16),
            jax.ShapeDtypeStruct((t, widths[2]), BF16),
        ],
        scratch_shapes=[pltpu.VMEM((tm, d), BF16)],
        compiler_params=_params(("parallel",)),
        name="mixer_inproj",
    )(x, g, w)


def _lru_fill_halo(first, last, xa_ref, xp_ref, xn_ref, xext, lc):
    xext[_HALO_ROWS:_HALO_ROWS + lc, :] = xa_ref[...]

    @pl.when(first)
    def _():
        xext[0:_HALO_ROWS, :] = jnp.zeros((_HALO_ROWS, LRU_WIDTH), F32)

    @pl.when(jnp.logical_not(first))
    def _():
        xext[0:_HALO_ROWS, :] = xp_ref[...]

    @pl.when(last)
    def _():
        xext[_HALO_ROWS + lc:, :] = jnp.zeros((_HALO_ROWS, LRU_WIDTH), F32)

    @pl.when(jnp.logical_not(last))
    def _():
        xext[_HALO_ROWS + lc:, :] = xn_ref[...]


def _lru_conv(xext, cw_ref, cb_ref, xc_ref, lc, rb):
    cb = cb_ref[...]
    taps = [cw_ref[j:j + 1, :] for j in range(CONV_WIDTH)]
    for blk in range(lc // rb):
        r0 = blk * rb
        xc_ref[r0:r0 + rb, :] = cb + sum(
            taps[j] * xext[r0 + _HALO_ROWS - CONV_LEFT + j:r0 + _HALO_ROWS - CONV_LEFT + j + rb, :]
            for j in range(CONV_WIDTH))


def _lru_gates(xc_ref, wg_ref, ba_ref, bx_ref, lam_ref, a_scr, u_scr, lc, rb):
    lam = lam_ref[...]
    neg = -lam
    softplus = jnp.maximum(neg, 0.0) + jnp.log1p(jnp.exp(-jnp.abs(neg)))
    for blk in range(lc // rb):
        r0 = blk * rb
        xc = xc_ref[r0:r0 + rb, :]
        xcb = xc.astype(BF16)
        for n in range(LRU_BLOCKS):
            cs = slice(n * LRU_BLOCK_WIDTH, (n + 1) * LRU_BLOCK_WIDTH)
            g = jnp.dot(xcb[:, cs], wg_ref[n], preferred_element_type=F32)
            r = jax.nn.sigmoid(g[:, :LRU_BLOCK_WIDTH] + ba_ref[:, cs])
            ig = jax.nn.sigmoid(g[:, LRU_BLOCK_WIDTH:] + bx_ref[:, cs])
            neg_log_a = (LRU_C * r) * softplus[:, cs]
            a = jnp.exp(-neg_log_a)
            one_minus_a2 = jnp.tanh(neg_log_a) * (1.0 + a * a)
            root = jnp.where(one_minus_a2 > 0.0, one_minus_a2 * lax.rsqrt(one_minus_a2), 0.0)
            u = root * (ig * xc[:, cs])
            groups = slice(r0 // _SCAN_GROUP, (r0 + rb) // _SCAN_GROUP)
            a_scr[groups, :, cs] = a.reshape(rb // _SCAN_GROUP, _SCAN_GROUP, LRU_BLOCK_WIDTH)
            u_scr[groups, :, cs] = u.reshape(rb // _SCAN_GROUP, _SCAN_GROUP, LRU_BLOCK_WIDTH)


def _lru_scan(reset, a_scr, u_scr, h_dst, carry, lc, reverse):
    @pl.when(reset)
    def _():
        carry[...] = jnp.zeros((1, LRU_WIDTH), F32)

    def group(i, h):
        g = lc // _SCAN_GROUP - 1 - i if reverse else i
        order = [_SCAN_GROUP - 1 - j if reverse else j for j in range(_SCAN_GROUP)]
        p = a_scr[g, order[0]:order[0] + 1, :]
        q = u_scr[g, order[0]:order[0] + 1, :]
        outs = [p * h + q]
        for j in order[1:]:
            a = a_scr[g, j:j + 1, :]
            q = a * q + u_scr[g, j:j + 1, :]
            p = a * p
            outs.append(p * h + q)
        for j, o in zip(order, outs):
            h_dst[g, j:j + 1, :] = o
        return outs[-1]

    carry[...] = lax.fori_loop(0, lc // _SCAN_GROUP, group, carry[...], unroll=4)


def _lru_fwd_kernel(flags_ref, xa_ref, xp_ref, xn_ref, cw_ref, cb_ref, wg_ref, ba_ref, bx_ref, lam_ref,
                    hf_ref, xc_ref, xext, a_scr, u_scr, carry, *, lc, rb):
    fl = flags_ref[pl.program_id(0)]
    first = (fl & 1) != 0
    last = (fl & 2) != 0
    _lru_fill_halo(first, last, xa_ref, xp_ref, xn_ref, xext, lc)
    _lru_conv(xext, cw_ref, cb_ref, xc_ref, lc, rb)
    _lru_gates(xc_ref, wg_ref, ba_ref, bx_ref, lam_ref, a_scr, u_scr, lc, rb)
    _lru_scan(first, a_scr, u_scr, hf_ref, carry, lc, reverse=False)


def _lru_bwd_kernel(flags_ref, xc_ref, gate_ref, hf_ref, wg_ref, ba_ref, bx_ref, lam_ref, gn_ref, y_ref,
                    a_scr, u_scr, h_scr, carry, *, lc, rb, nchunks):
    fl = flags_ref[nchunks - 1 - pl.program_id(0)]
    last = (fl & 2) != 0
    _lru_gates(xc_ref, wg_ref, ba_ref, bx_ref, lam_ref, a_scr, u_scr, lc, rb)
    _lru_scan(last, a_scr, u_scr, h_scr, carry, lc, reverse=True)
    for blk in range(lc // rb):
        rows = slice(blk * rb, (blk + 1) * rb)
        groups = slice(blk * rb // _SCAN_GROUP, (blk + 1) * rb // _SCAN_GROUP)
        h = (hf_ref[groups] + h_scr[groups]).reshape(rb, LRU_WIDTH)
        y = h * jax.nn.gelu(gate_ref[rows, :])
        y_ref[rows, :] = _rms(y, gn_ref[...]).astype(BF16)


def _lru(cfg, lru_in, cw, cb, wg, ba, bx, lam, gn):
    t = lru_in.shape[0]
    lc, rb = cfg.lru_chunk, cfg.lru_rows
    nchunks = t // lc
    hb = lc // _HALO_ROWS
    nhalo = t // _HALO_ROWS
    flags = _chunk_flags(cfg, lc)
    row = lambda: _resident((1, LRU_WIDTH))

    def specs(chunk_of):
        return [
            pl.BlockSpec((lc, LRU_WIDTH), lambda i, f: (chunk_of(i), 0)),
            pl.BlockSpec((_HALO_ROWS, LRU_WIDTH), lambda i, f: (jnp.maximum(chunk_of(i) * hb - 1, 0), 0)),
            pl.BlockSpec((_HALO_ROWS, LRU_WIDTH), lambda i, f: (jnp.minimum((chunk_of(i) + 1) * hb, nhalo - 1), 0)),
        ]

    conv_weights = [_resident((CONV_WIDTH, LRU_WIDTH)), row()]

    def weights(d):
        return [
            pl.BlockSpec((None, LRU_BLOCKS, LRU_BLOCK_WIDTH, 2 * LRU_BLOCK_WIDTH), lambda i, f: (d, 0, 0, 0)),
            pl.BlockSpec((None, 1, LRU_WIDTH), lambda i, f: (d, 0, 0)),
            pl.BlockSpec((None, 1, LRU_WIDTH), lambda i, f: (d, 0, 0)),
            pl.BlockSpec((None, 1, LRU_WIDTH), lambda i, f: (d, 0, 0)),
        ]

    grouped = (lc // _SCAN_GROUP, _SCAN_GROUP, LRU_WIDTH)
    scratch = [pltpu.VMEM(grouped, F32), pltpu.VMEM(grouped, F32)]
    carry = [pltpu.VMEM((1, LRU_WIDTH), F32)]

    fwd = lambda i: i
    hf, xc = pl.pallas_call(
        functools.partial(_lru_fwd_kernel, lc=lc, rb=rb),
        grid_spec=pltpu.PrefetchScalarGridSpec(
            num_scalar_prefetch=1, grid=(nchunks,),
            in_specs=specs(fwd) + conv_weights + weights(0),
            out_specs=[pl.BlockSpec(grouped, lambda i, f: (i, 0, 0)),
                       pl.BlockSpec((lc, LRU_WIDTH), lambda i, f: (i, 0))],
            scratch_shapes=[pltpu.VMEM((lc + 2 * _HALO_ROWS, LRU_WIDTH), F32)] + scratch + carry),
        out_shape=[jax.ShapeDtypeStruct((t // _SCAN_GROUP, _SCAN_GROUP, LRU_WIDTH), F32),
                   jax.ShapeDtypeStruct((t, LRU_WIDTH), F32)],
        compiler_params=_params(("arbitrary",)),
        name="lru_forward",
    )(flags, lru_in, lru_in, lru_in, cw, cb, wg, ba, bx, lam)

    bwd = lambda i: nchunks - 1 - i
    return pl.pallas_call(
        functools.partial(_lru_bwd_kernel, lc=lc, rb=rb, nchunks=nchunks),
        grid_spec=pltpu.PrefetchScalarGridSpec(
            num_scalar_prefetch=1, grid=(nchunks,),
            in_specs=[
                pl.BlockSpec((lc, LRU_WIDTH), lambda i, f: (bwd(i), 0)),
                pl.BlockSpec((lc, LRU_WIDTH), lambda i, f: (bwd(i), 1)),
                pl.BlockSpec(grouped, lambda i, f: (bwd(i), 0, 0)),
            ] + weights(1) + [row()],
            out_specs=pl.BlockSpec((lc, LRU_WIDTH), lambda i, f: (bwd(i), 0)),
            scratch_shapes=scratch + [pltpu.VMEM(grouped, F32)] + carry),
        out_shape=jax.ShapeDtypeStruct((t, LRU_WIDTH), BF16),
        compiler_params=_params(("arbitrary",)),
        name="lru_backward",
    )(flags, xc, lru_in, hf, wg, ba, bx, lam, gn)


def _alibi_slopes(n):
    return [2.0 ** (-8.0 * (i + 1) / n) for i in range(n)]


def _band_bias(wq, halo, dist_scale, hq, hkv):
    rep = hq // hkv
    slopes = _alibi_slopes(hq)
    qi = np.arange(wq)[:, None]
    kj = np.arange(wq + 2 * halo)[None, :]
    rel = np.abs(kj - halo - qi)
    out = np.empty((hkv, rep * wq, wq + 2 * halo), np.float32)
    for g in range(hkv):
        for r in range(rep):
            out[g, r * wq:(r + 1) * wq] = np.where(rel <= halo, -slopes[g * rep + r] * dist_scale * rel, _NEG)
    return out


def _band_bias_variants(wq, halo, dist_scale, heads):
    base = _band_bias(wq, halo, dist_scale, heads, heads)
    col = np.arange(wq + 2 * halo)
    out = np.stack([base] * 4)
    out[1::2, :, :, col < halo] = _NEG
    out[2:, :, :, col >= wq + halo] = _NEG
    return out


def _edge_penalties(fl, wq, halo):
    col = lax.broadcasted_iota(jnp.int32, (1, wq + 2 * halo), 1)
    pen_first = jnp.where(col < halo, jnp.where((fl & 1) != 0, _NEG, 0.0), 0.0)
    pen_last = jnp.where(col >= wq + halo, jnp.where((fl & 2) != 0, _NEG, 0.0), 0.0)
    return pen_first, pen_last


def _attend(qs, ks, vs, biases):
    scores = [lax.dot_general(q, k, _NT, preferred_element_type=F32) for q, k in zip(qs, ks)]
    ms, ps = [], []
    for s, b in zip(scores, biases):
        if b is not None:
            s = s + b
        m = jnp.max(s, axis=-1, keepdims=True)
        ms.append(m)
        ps.append(jnp.exp(s - m).astype(BF16))
    ls, accs = [], []
    for p, v in zip(ps, vs):
        v1 = jnp.concatenate([v, jnp.ones(v.shape, v.dtype)], axis=1)
        out = jnp.dot(p, v1, preferred_element_type=F32)
        accs.append(out[:, :HEAD_DIM])
        ls.append(out[:, HEAD_DIM:])
    return ms, ls, accs


def _swa_kernel(flags_ref, q_ref, kc_ref, kp_ref, kn_ref, vc_ref, vp_ref, vn_ref, bias_ref, sink_ref, gn_ref,
                y_out, kbuf, vbuf, *, w, nsub, nb):
    rep = SWA_HEADS // SWA_KV_HEADS
    rows_total = nsub * w
    pen_first, pen_last = _edge_penalties(flags_ref[pl.program_id(0)], w, w)
    kbuf[0:w, :] = kp_ref[...]
    kbuf[w:w + rows_total, :] = kc_ref[...]
    kbuf[w + rows_total:, :] = kn_ref[...]
    vbuf[0:w, :] = vp_ref[...]
    vbuf[w:w + rows_total, :] = vc_ref[...]
    vbuf[w + rows_total:, :] = vn_ref[...]

    for j0 in range(0, nsub, nb):
        blocks = list(range(j0, min(j0 + nb, nsub)))
        qs, ks, vs, bs = [], [], [], []
        for j in blocks:
            rows = slice(j * w, (j + 1) * w)
            for g in range(SWA_KV_HEADS):
                gs = slice(g * HEAD_DIM, (g + 1) * HEAD_DIM)
                qs.append(jnp.concatenate(
                    [q_ref[rows, (g * rep + r) * HEAD_DIM:(g * rep + r + 1) * HEAD_DIM] for r in range(rep)], axis=0))
                ks.append(kbuf[j * w:(j + 3) * w, gs])
                vs.append(vbuf[j * w:(j + 3) * w, gs])
                b = bias_ref[g]
                if j == 0:
                    b = b + pen_first
                if j == nsub - 1:
                    b = b + pen_last
                bs.append(b)
        ms, ls, accs = _attend(qs, ks, vs, bs)
        for bi, j in enumerate(blocks):
            heads_out = []
            for g in range(SWA_KV_HEADS):
                idx = bi * SWA_KV_HEADS + g
                for r in range(rep):
                    h = g * rep + r
                    part = slice(r * w, (r + 1) * w)
                    m, l, acc = ms[idx][part], ls[idx][part], accs[idx][part]
                    factor = jax.nn.sigmoid(m + jnp.log(l) - sink_ref[:, h:h + 1])
                    heads_out.append((acc / l) * factor)
            y = jnp.concatenate(heads_out, axis=1)
            y_out[j * w:(j + 1) * w, :] = _rms(y, gn_ref[...]).astype(BF16)


def _swa(cfg, qkvc, gn, sink):
    t, c = qkvc.shape
    w = SWA_WINDOW
    rows = cfg.swa_rows
    nsub = rows // w
    nchunks = t // rows
    nblk = t // w
    qw, kvw = SWA_WIDTH, SWA_KV_WIDTH
    rep = SWA_HEADS // SWA_KV_HEADS
    assert qw % kvw == 0 and c == qw + 2 * kvw
    kcol, vcol = qw // kvw, qw // kvw + 1
    prev = lambda col: (lambda i, f: (jnp.maximum(i * nsub - 1, 0), col))
    nxt = lambda col: (lambda i, f: (jnp.minimum((i + 1) * nsub, nblk - 1), col))
    return pl.pallas_call(
        functools.partial(_swa_kernel, w=w, nsub=nsub, nb=cfg.swa_batch),
        grid_spec=pltpu.PrefetchScalarGridSpec(
            num_scalar_prefetch=1, grid=(nchunks,),
            in_specs=[
                pl.BlockSpec((rows, qw), lambda i, f: (i, 0)),
                pl.BlockSpec((rows, kvw), lambda i, f: (i, kcol)),
                pl.BlockSpec((w, kvw), prev(kcol)),
                pl.BlockSpec((w, kvw), nxt(kcol)),
                pl.BlockSpec((rows, kvw), lambda i, f: (i, vcol)),
                pl.BlockSpec((w, kvw), prev(vcol)),
                pl.BlockSpec((w, kvw), nxt(vcol)),
                _resident((SWA_KV_HEADS, rep * w, 3 * w)),
                _resident((1, 128)),
                _resident((1, qw)),
            ],
            out_specs=pl.BlockSpec((rows, qw), lambda i, f: (i, 0)),
            scratch_shapes=[pltpu.VMEM((rows + 2 * w, kvw), BF16), pltpu.VMEM((rows + 2 * w, kvw), BF16)]),
        out_shape=jax.ShapeDtypeStruct((t, qw), BF16),
        compiler_params=_params(("parallel",)),
        name="windowed_gqa_sink",
    )(_chunk_flags(cfg, rows), qkvc, qkvc, qkvc, qkvc, qkvc, qkvc, qkvc,
      jnp.asarray(_band_bias(w, w, 1, SWA_HEADS, SWA_KV_HEADS)), sink, gn)


def _perm_matrix(d):
    n = _PERM_ROWS
    per = n // d
    p = np.zeros((n, n), np.float32)
    for r in range(d):
        for m in range(per):
            p[r * per + m, d * m + r] = 1.0
    return p


def _dilated_qrows(c, d):
    return min(_DIL_QROWS_MAX, c // d)


def _dilated_kernel(flags_ref, q_ref, kp_ref, kc_ref, kn_ref, vp_ref, vc_ref, vn_ref, bias0, bias1, bias2,
                    perm_ref, gn_ref, y_ref, qd, kd, vd, ring_k, ring_v, acc_nat, m_nat, l_nat, *,
                    c, batch_rows, tail):
    w = _DIL_W
    nh = DIL_HEADS
    step = pl.program_id(0)
    fl = flags_ref[step]
    slot_prev, slot_cur, slot_next = lax.rem(step + 2, 3), lax.rem(step, 3), lax.rem(step + 1, 3)
    bias_refs = (bias0, bias1, bias2)
    starts_sequence = fl & 1
    ends_sequence = (fl >> 1) & 1
    heads = [slice(h * HEAD_DIM, (h + 1) * HEAD_DIM) for h in range(nh)]

    def run_pattern(p, d, qsrc, ksrc, vsrc, qstride, kstride, first_pattern, last_pattern):
        wq = _dilated_qrows(c, d)
        win = wq + 2 * w
        nblk = c // (d * wq)
        nb = max(1, batch_rows // wq)
        assert (c // wq) % nb == 0
        assert d == 1 or not last_pattern

        def body(it, carry):
            qs, ks, vs, bs, where = [], [], [], [], []
            for b in range(nb):
                sb = it * nb + b
                r = sb // nblk
                s = sb % nblk
                var = jnp.where(s == 0, starts_sequence, 0) + 2 * jnp.where(s == nblk - 1, ends_sequence, 0)
                q0 = pl.multiple_of(r * qstride + s * wq, w)
                k0 = pl.multiple_of(r * kstride + s * wq, w)
                where.append(d * wq * s + r)
                for h in range(nh):
                    qs.append(qsrc[pl.ds(q0, wq), heads[h]])
                    if ksrc is None:
                        ks.append(jnp.concatenate([ring_k[sl, pl.ds(q0, wq), heads[h]]
                                                   for sl in (slot_prev, slot_cur, slot_next)], axis=0))
                        vs.append(jnp.concatenate([ring_v[sl, pl.ds(q0, wq), heads[h]]
                                                   for sl in (slot_prev, slot_cur, slot_next)], axis=0))
                    else:
                        ks.append(ksrc[pl.ds(k0, win), heads[h]])
                        vs.append(vsrc[pl.ds(k0, win), heads[h]])
                    bs.append(bias_refs[p][var, h])
            ms, ls, accs = _attend(qs, ks, vs, bs)
            for b in range(nb):
                nat = pl.ds(where[b], wq, stride=d) if d > 1 else pl.ds(pl.multiple_of(where[b], w), wq)
                outs = []
                for h in range(nh):
                    m, l, acc = ms[b * nh + h], ls[b * nh + h], accs[b * nh + h]
                    if not first_pattern:
                        m_p = m_nat[h, nat, :]
                        m_n = jnp.maximum(m_p, m)
                        alpha = jnp.exp(m_p - m_n)
                        beta = jnp.exp(m - m_n)
                        l = alpha * l_nat[h, nat, :] + beta * l
                        acc = alpha * acc_nat[h, nat, :] + beta * acc
                        m = m_n
                    if last_pattern:
                        outs.append(acc / l)
                    else:
                        acc_nat[h, nat, :] = acc
                        m_nat[h, nat, :] = jnp.broadcast_to(m, (wq, HEAD_DIM))
                        l_nat[h, nat, :] = jnp.broadcast_to(l, (wq, HEAD_DIM))
                if last_pattern:
                    y_ref[nat, :] = _rms(jnp.concatenate(outs, axis=1), gn_ref[...]).astype(BF16)
            return carry

        lax.fori_loop(0, c // (wq * nb), body, 0)

    def deinterleave(p, d):
        per = _PERM_ROWS // d
        halo_groups = (d * w) // _PERM_ROWS
        chunk_groups = c // _PERM_ROWS
        kstride = (chunk_groups + 2 * halo_groups) * per
        perm = perm_ref[p - 1]

        def move(src, row0, dst, g, stride):
            res = jnp.dot(perm, src[row0:row0 + _PERM_ROWS, :], preferred_element_type=F32).astype(BF16)
            for r in range(d):
                dst[r * stride + g * per:r * stride + (g + 1) * per, :] = res[r * per:(r + 1) * per]

        for g in range(chunk_groups):
            move(q_ref, g * _PERM_ROWS, qd, g, c // d)
        if d * w == c:
            def fill(slot, k_src, v_src):
                for src, ring in ((k_src, ring_k), (v_src, ring_v)):
                    for g in range(chunk_groups):
                        move(src, g * _PERM_ROWS, ring.at[slot], g, c // d)

            @pl.when(step == 0)
            def _():
                ring_k[2] = jnp.zeros((c, DIL_WIDTH), BF16)
                ring_v[2] = jnp.zeros((c, DIL_WIDTH), BF16)
                fill(0, kc_ref, vc_ref)

            fill(slot_next, kn_ref, vn_ref)
            return c // d, None
        for prev_ref, cur_ref, next_ref, dst in ((kp_ref, kc_ref, kn_ref, kd), (vp_ref, vc_ref, vn_ref, vd)):
            srcs = ([(prev_ref, tail - (halo_groups - g) * _PERM_ROWS) for g in range(halo_groups)]
                    + [(cur_ref, g * _PERM_ROWS) for g in range(chunk_groups)]
                    + [(next_ref, g * _PERM_ROWS) for g in range(halo_groups)])
            for g, (src, row0) in enumerate(srcs):
                move(src, row0, dst, g, kstride)
        return c // d, kstride

    order = sorted(range(len(DIL_PATTERNS)), key=lambda p: -DIL_PATTERNS[p][1])
    for idx, p in enumerate(order):
        d = DIL_PATTERNS[p][1]
        first, last = idx == 0, idx == len(order) - 1
        if d == 1:
            for prev_ref, cur_ref, next_ref, dst in ((kp_ref, kc_ref, kn_ref, kd), (vp_ref, vc_ref, vn_ref, vd)):
                dst[0:w, :] = prev_ref[tail - w:tail, :]
                dst[w:w + c, :] = cur_ref[...]
                dst[w + c:2 * w + c, :] = next_ref[0:w, :]
            run_pattern(p, d, q_ref, kd, vd, 0, 0, first, last)
        else:
            qstride, kstride = deinterleave(p, d)
            if kstride is None:
                run_pattern(p, d, qd, None, None, qstride, 0, first, last)
            else:
                run_pattern(p, d, qd, kd, vd, qstride, kstride, first, last)


def _dilated(cfg, qkvb, gn):
    t = qkvb.shape[0]
    c = cfg.dil_rows
    w = _DIL_W
    n = t // c
    dmax = max(d for _, d in DIL_PATTERNS)
    assert DIL_PATTERNS[0][1] == 1 and c % (dmax * w) == 0 and c >= dmax * w and c % _PERM_ROWS == 0
    assert all((d * w) % _PERM_ROWS == 0 for _, d in DIL_PATTERNS[1:])
    assert len(DIL_PATTERNS) == 3
    biases = [_band_bias_variants(_dilated_qrows(c, d), w, d, DIL_HEADS) for _, d in DIL_PATTERNS]
    perm = np.stack([_perm_matrix(d) for _, d in DIL_PATTERNS[1:]])
    blk = (c, DIL_WIDTH)
    assert dmax * w == c
    local = [d for _, d in DIL_PATTERNS if d * w < c]
    tail = max(d * w for d in local)
    assert c % tail == 0 and tail % _PERM_ROWS == 0
    class_rows = max((c // _PERM_ROWS + 2 * (d * w // _PERM_ROWS)) * _PERM_ROWS if d > 1 else c + 2 * w for d in local)
    prev = lambda col: (lambda i, f: (jnp.maximum(i * (c // tail) - 1, 0), col))
    cur = lambda col: (lambda i, f: (i, col))
    nxt = lambda col: (lambda i, f: (jnp.minimum(i + 1, n - 1), col))
    tail_blk = (tail, DIL_WIDTH)
    return pl.pallas_call(
        functools.partial(_dilated_kernel, c=c, batch_rows=cfg.dil_batch * w, tail=tail),
        grid_spec=pltpu.PrefetchScalarGridSpec(
            num_scalar_prefetch=1, grid=(n,),
            in_specs=[
                pl.BlockSpec(blk, cur(0)),
                pl.BlockSpec(tail_blk, prev(1)), pl.BlockSpec(blk, cur(1)), pl.BlockSpec(blk, nxt(1)),
                pl.BlockSpec(tail_blk, prev(2)), pl.BlockSpec(blk, cur(2)), pl.BlockSpec(blk, nxt(2)),
                _resident(biases[0].shape), _resident(biases[1].shape), _resident(biases[2].shape),
                _resident(perm.shape), _resident((1, DIL_WIDTH)),
            ],
            out_specs=pl.BlockSpec(blk, cur(0)),
            scratch_shapes=[
                pltpu.VMEM((c, DIL_WIDTH), BF16),
                pltpu.VMEM((class_rows, DIL_WIDTH), BF16), pltpu.VMEM((class_rows, DIL_WIDTH), BF16),
                pltpu.VMEM((3, c, DIL_WIDTH), BF16), pltpu.VMEM((3, c, DIL_WIDTH), BF16),
                pltpu.VMEM((DIL_HEADS, c, HEAD_DIM), F32),
                pltpu.VMEM((DIL_HEADS, c, HEAD_DIM), F32),
                pltpu.VMEM((DIL_HEADS, c, HEAD_DIM), F32),
            ]),
        out_shape=jax.ShapeDtypeStruct((t, DIL_WIDTH), BF16),
        compiler_params=_params(("arbitrary",)),
        name="dilated_attention",
    )(_chunk_flags(cfg, c), qkvb, qkvb, qkvb, qkvb, qkvb, qkvb, qkvb,
      jnp.asarray(biases[0]), jnp.asarray(biases[1]), jnp.asarray(biases[2]), jnp.asarray(perm, dtype=BF16), gn)


def _outproj_kernel(ya_ref, yb_ref, yc_ref, x_ref, w_ref, g_ref, o_ref, *, nchunk):
    d = o_ref.shape[1]
    b0, b1 = LRU_WIDTH, LRU_WIDTH + DIL_WIDTH
    for c in range(0, d, nchunk):
        cs = slice(c, c + nchunk)
        acc = jnp.dot(ya_ref[...], w_ref[0:b0, cs], preferred_element_type=F32)
        acc += jnp.dot(yb_ref[...], w_ref[b0:b1, cs], preferred_element_type=F32)
        acc += jnp.dot(yc_ref[...], w_ref[b1:, cs], preferred_element_type=F32)
        o_ref[:, cs] = acc
    for rows in _row_blocks(o_ref.shape[0]):
        o_ref[rows, :] = x_ref[rows, :] + _rms(o_ref[rows, :], g_ref[...])


def _outproj(cfg, ya, yb, yc, x, w, g, layer):
    t, d = x.shape
    tm = cfg.tm_out
    return pl.pallas_call(
        functools.partial(_outproj_kernel, nchunk=min(512, d)),
        grid=(t // tm,),
        in_specs=[
            pl.BlockSpec((tm, LRU_WIDTH), lambda i: (i, 0)),
            pl.BlockSpec((tm, DIL_WIDTH), lambda i: (i, 0)),
            pl.BlockSpec((tm, SWA_WIDTH), lambda i: (i, 0)),
            pl.BlockSpec((tm, d), lambda i: (i, 0)),
            _resident((MIX_WIDTH, d), layer),
            _resident((1, d)),
        ],
        out_specs=pl.BlockSpec((tm, d), lambda i: (i, 0)),
        out_shape=jax.ShapeDtypeStruct((t, d), F32),
        compiler_params=_params(("parallel",)),
        name="mixer_outproj",
    )(ya, yb, yc, x, w, g)


def _memkv_kernel(mem_ref, g_ref, w_ref, k_ref, v_ref):
    mn = _rms(mem_ref[...], g_ref[...]).astype(BF16)
    kv = jnp.dot(mn, w_ref[...], preferred_element_type=F32)
    k_ref[...] = kv[:, :MEM_WIDTH].astype(BF16)
    v_ref[...] = kv[:, MEM_WIDTH:].astype(BF16)


def _memkv(cfg, mem, g, wkv, layer):
    rows, d = mem.shape
    nm = cfg.n_mem
    shape = jax.ShapeDtypeStruct((rows, MEM_WIDTH), BF16)
    return pl.pallas_call(
        _memkv_kernel,
        grid=(rows // nm,),
        in_specs=[pl.BlockSpec((nm, d), lambda i: (i, 0)), _resident((1, d)),
                  _resident((d, 2 * MEM_WIDTH), layer)],
        out_specs=[pl.BlockSpec((nm, MEM_WIDTH), lambda i: (i, 0))] * 2,
        out_shape=[shape, shape],
        compiler_params=_params(("parallel",)),
        name="memory_kv",
    )(mem, g, wkv)


def _cross_kernel(seq_ref, x_ref, gpre_ref, wq_ref, k_ref, v_ref, wo_ref, gpost_ref, o_ref, xn_scr, o_scr, *,
                  nchunk):
    del seq_ref
    tm, d = x_ref.shape
    for rows in _row_blocks(tm):
        xn_scr[rows, :] = _rms(x_ref[rows, :], gpre_ref[...]).astype(BF16)
    q = (jnp.dot(xn_scr[...], wq_ref[...], preferred_element_type=F32) * _QK_SCALE).astype(BF16)
    heads = [slice(h * HEAD_DIM, (h + 1) * HEAD_DIM) for h in range(MEM_HEADS)]
    _, ls, accs = _attend([q[:, hs] for hs in heads], [k_ref[:, hs] for hs in heads],
                          [v_ref[:, hs] for hs in heads], [None] * MEM_HEADS)
    for hs, l, acc in zip(heads, ls, accs):
        o_scr[:, hs] = (acc / l).astype(BF16)
    for c in range(0, d, nchunk):
        cs = slice(c, min(c + nchunk, d))
        o_ref[:, cs] = jnp.dot(o_scr[...], wo_ref[:, cs], preferred_element_type=F32)
    for rows in _row_blocks(tm):
        o_ref[rows, :] = x_ref[rows, :] + _rms(o_ref[rows, :], gpost_ref[...])


def _cross(cfg, x, gpre, wq, kmem, vmem, wo, gpost, layer):
    t, d = x.shape
    tm, nm = cfg.tm_cross, cfg.n_mem
    seqs, _ = _sequences(cfg)
    seq_of_tile = []
    for si, (_, length) in enumerate(seqs):
        assert length % tm == 0
        seq_of_tile += [si] * (length // tm)
    seq_of_tile = jnp.asarray(np.asarray(seq_of_tile, np.int32))
    return pl.pallas_call(
        functools.partial(_cross_kernel, nchunk=512),
        grid_spec=pltpu.PrefetchScalarGridSpec(
            num_scalar_prefetch=1, grid=(t // tm,),
            in_specs=[
                pl.BlockSpec((tm, d), lambda i, s: (i, 0)),
                _resident((1, d)),
                _resident((d, MEM_WIDTH), layer),
                pl.BlockSpec((nm, MEM_WIDTH), lambda i, s: (s[i], 0)),
                pl.BlockSpec((nm, MEM_WIDTH), lambda i, s: (s[i], 0)),
                _resident((MEM_WIDTH, d), layer),
                _resident((1, d)),
            ],
            out_specs=pl.BlockSpec((tm, d), lambda i, s: (i, 0)),
            scratch_shapes=[pltpu.VMEM((tm, d), BF16), pltpu.VMEM((tm, MEM_WIDTH), BF16)]),
        out_shape=jax.ShapeDtypeStruct((t, d), F32),
        compiler_params=_params(("parallel",)),
        name="memory_cross_attention",
    )(seq_of_tile, x, gpre, wq, kmem, vmem, wo, gpost)


def _ffn_kernel(x_ref, gpre_ref, w1_ref, w2_ref, gpost_ref, o_ref, xn_scr, h_scr, *, nchunk):
    f = pl.program_id(1)
    tf = w1_ref.shape[1]
    d = w2_ref.shape[1]

    tm = x_ref.shape[0]
    row_blocks = [slice(r, min(r + _NORM_ROWS, tm)) for r in range(0, tm, _NORM_ROWS)]

    @pl.when(f == 0)
    def _():
        for rows in row_blocks:
            xn_scr[rows, :] = _rms(x_ref[rows, :], gpre_ref[...]).astype(BF16)
            o_ref[rows, :] = jnp.zeros((rows.stop - rows.start, d), F32)

    for c in range(0, tf, nchunk):
        cs = slice(c, min(c + nchunk, tf))
        h = jnp.dot(xn_scr[...], w1_ref[:, cs], preferred_element_type=F32)
        h_scr[:, cs] = jnp.square(jnp.maximum(h, 0.0)).astype(BF16)
    for c in range(0, d, nchunk):
        cs = slice(c, min(c + nchunk, d))
        o_ref[:, cs] += jnp.dot(h_scr[...], w2_ref[:, cs], preferred_element_type=F32)

    @pl.when(f == pl.num_programs(1) - 1)
    def _():
        for rows in row_blocks:
            o_ref[rows, :] = x_ref[rows, :] + _rms(o_ref[rows, :], gpost_ref[...])


def _ffn(cfg, x, gpre, w1, w2, gpost, layer):
    t, d = x.shape
    tm, tf = cfg.tm_ffn, cfg.tf
    dff = w1.shape[2]
    return pl.pallas_call(
        functools.partial(_ffn_kernel, nchunk=512),
        grid=(t // tm, dff // tf),
        in_specs=[
            pl.BlockSpec((tm, d), lambda i, f: (i, 0)),
            _resident((1, d)),
            pl.BlockSpec((None, d, tf), lambda i, f: (layer, 0, f)),
            pl.BlockSpec((None, tf, d), lambda i, f: (layer, f, 0)),
            _resident((1, d)),
        ],
        out_specs=pl.BlockSpec((tm, d), lambda i, f: (i, 0)),
        out_shape=jax.ShapeDtypeStruct((t, d), F32),
        scratch_shapes=[pltpu.VMEM((tm, d), BF16), pltpu.VMEM((tm, tf), BF16)],
        compiler_params=_params(("parallel", "arbitrary")),
        name="squared_relu_mlp",
    )(x, gpre, w1, w2, gpost)


def _forward(cfg, x, mem, p):
    row = lambda a: a.reshape(1, -1).astype(F32)
    for l in range(cfg.depth):
        lru_in, qkvb, qkvc = _inproj(cfg, x, row(p["mix_norm_pre"][l]), p["w_in"], l)
        gn = p["group_norm"][l]
        wg = jnp.concatenate([p["lru_wa"][l], p["lru_wx"][l]], axis=-1).astype(BF16)
        ya = _lru(cfg, lru_in, p["conv_w"][l], row(p["conv_b"][l]), wg,
                  p["lru_ba"][l][:, None, :], p["lru_bx"][l][:, None, :], p["lru_lam"][l][:, None, :],
                  row(gn[:LRU_WIDTH]))
        yb = _dilated(cfg, qkvb, row(gn[LRU_WIDTH:LRU_WIDTH + DIL_WIDTH]))
        sink = jnp.zeros((1, 128), F32).at[0, :SWA_HEADS].set(p["swa_sink"][l].astype(F32))
        yc = _swa(cfg, qkvc, row(gn[LRU_WIDTH + DIL_WIDTH:]), sink)
        x = _outproj(cfg, ya, yb, yc, x, p["w_out"], row(p["mix_norm_post"][l]), l)
        kmem, vmem = _memkv(cfg, mem, row(p["mem_kv_norm"][l]), p["w_mkv"], l)
        x = _cross(cfg, x, row(p["mem_norm_pre"][l]), p["w_mq"], kmem, vmem, p["w_mo"],
                   row(p["mem_norm_post"][l]), l)
        x = _ffn(cfg, x, row(p["ffn_norm_pre"][l]), p["w_ff1"], p["w_ff2"], row(p["ffn_norm_post"][l]), l)
    return x


def _run(cfg, x_prompt, x_sample, mem_prompt, mem_sample, mix_norm_pre, mix_norm_post, w_in, conv_w, conv_b,
         lru_wa, lru_ba, lru_wx, lru_bx, lru_lam, swa_sink, group_norm, w_out, mem_norm_pre, mem_norm_post,
         mem_kv_norm, w_mq, w_mk, w_mv, w_mo, ffn_norm_pre, ffn_norm_post, w_ff1, w_ff2):
    d = cfg.d_model
    p = dict(
        mix_norm_pre=mix_norm_pre, mix_norm_post=mix_norm_post, w_in=w_in.astype(BF16), conv_w=conv_w,
        conv_b=conv_b, lru_wa=lru_wa, lru_ba=lru_ba, lru_wx=lru_wx, lru_bx=lru_bx, lru_lam=lru_lam,
        swa_sink=swa_sink, group_norm=group_norm, w_out=w_out.astype(BF16), mem_norm_pre=mem_norm_pre,
        mem_norm_post=mem_norm_post, mem_kv_norm=mem_kv_norm, w_mq=w_mq.astype(BF16),
        w_mkv=jnp.concatenate([w_mk, w_mv], axis=-1).astype(BF16), w_mo=w_mo.astype(BF16),
        ffn_norm_pre=ffn_norm_pre, ffn_norm_post=ffn_norm_post, w_ff1=w_ff1.astype(BF16),
        w_ff2=w_ff2.astype(BF16))
    outs = []
    for group, x, mem in zip(cfg.groups, (x_prompt, x_sample), (mem_prompt, mem_sample)):
        sub = cfg._replace(groups=(group,))
        outs.append(_forward(sub, x.reshape(-1, d), mem.reshape(-1, d), p).reshape(x.shape))
    return tuple(outs)


def kernel(x_prompt, x_sample, mem_prompt, mem_sample, mix_norm_pre, mix_norm_post, w_in, conv_w, conv_b, lru_wa,
           lru_ba, lru_wx, lru_bx, lru_lam, swa_sink, group_norm, w_out, mem_norm_pre, mem_norm_post, mem_kv_norm,
           w_mq, w_mk, w_mv, w_mo, ffn_norm_pre, ffn_norm_post, w_ff1, w_ff2):
    return _run(_CFG, x_prompt, x_sample, mem_prompt, mem_sample, mix_norm_pre, mix_norm_post, w_in, conv_w,
                conv_b, lru_wa, lru_ba, lru_wx, lru_bx, lru_lam, swa_sink, group_norm, w_out, mem_norm_pre,
                mem_norm_post, mem_kv_norm, w_mq, w_mk, w_mv, w_mo, ffn_norm_pre, ffn_norm_post, w_ff1, w_ff2)
```

```python
import functools
from typing import NamedTuple

import numpy as np
import jax
import jax.numpy as jnp
from jax import lax
from jax.experimental import pallas as pl
from jax.experimental.pallas import tpu as pltpu

F32 = jnp.float32
BF16 = jnp.bfloat16

D_MODEL = 2048
BATCH = 8
SEQ = 4096
DEPTH = 4
DEC_BATCH = 1
DEC_SEQ = 16384
HEAD_DIM = 128
LRU_WIDTH = 512
LRU_BLOCKS = 4
LRU_BLOCK_WIDTH = LRU_WIDTH // LRU_BLOCKS
CONV_WIDTH = 4
CONV_LEFT = 2
LRU_C = 8.0
DIL_HEADS = 6
DIL_PATTERNS = ((128, 1), (512, 4), (2048, 16))
SWA_HEADS = 6
SWA_KV_HEADS = 2
SWA_WINDOW = 128
DIL_WIDTH = DIL_HEADS * HEAD_DIM
SWA_WIDTH = SWA_HEADS * HEAD_DIM
SWA_KV_WIDTH = SWA_KV_HEADS * HEAD_DIM
MIX_WIDTH = LRU_WIDTH + DIL_WIDTH + SWA_WIDTH
IN_WIDTH = 2 * LRU_WIDTH + 3 * DIL_WIDTH + SWA_WIDTH + 2 * SWA_KV_WIDTH
N_MEM = 256
MEM_HEADS = 4
MEM_WIDTH = MEM_HEADS * HEAD_DIM
D_FF = 4 * D_MODEL
EPS = 1e-6

_NEG = -1e30
_QK_SCALE = HEAD_DIM ** -0.5
_HALO_ROWS = 8
_LANES = 128
_COL_CHUNK = 512
_NORM_ROWS = 128
_SCAN_GROUP = 8
_DIL_W = DIL_PATTERNS[0][0] // (2 * DIL_PATTERNS[0][1])
assert all(wn // (2 * d) == _DIL_W for wn, d in DIL_PATTERNS)
_DIL_QROWS_MAX = 128
_PERM_ROWS = 256
_V7X_VMEM_BYTES = 64 * 1024 * 1024
_VMEM_LIMIT = _V7X_VMEM_BYTES - 3 * 1024 * 1024
_NT = (((1,), (1,)), ((), ()))


class _Cfg(NamedTuple):
    d_model: int
    d_ff: int
    depth: int
    groups: tuple
    n_mem: int
    tm: int
    tm_out: int
    tm_cross: int
    tm_ffn: int
    tf: int
    lru_chunk: int
    lru_rows: int
    swa_rows: int
    dil_rows: int
    swa_batch: int
    dil_batch: int


_CFG = _Cfg(d_model=D_MODEL, d_ff=D_FF, depth=DEPTH, groups=((BATCH, SEQ), (DEC_BATCH, DEC_SEQ)),
            n_mem=N_MEM, tm=512, tm_out=1024, tm_cross=1024, tm_ffn=1024, tf=1024, lru_chunk=2048, lru_rows=256,
            swa_rows=1024,
            dil_rows=1024, swa_batch=2, dil_batch=4)


def _sequences(cfg):
    out, start = [], 0
    for n, length in cfg.groups:
        for _ in range(n):
            out.append((start, length))
            start += length
    return out, start


def _chunk_flags(cfg, rows):
    seqs, total = _sequences(cfg)
    starts = {s for s, _ in seqs}
    ends = {s + l for s, l in seqs}
    for s, l in seqs:
        assert l % rows == 0, (l, rows)
    n = total // rows
    flags = np.zeros((n,), np.int32)
    for c in range(n):
        flags[c] = (1 if c * rows in starts else 0) | (2 if (c + 1) * rows in ends else 0)
    return jnp.asarray(flags)


def _params(semantics):
    return pltpu.CompilerParams(dimension_semantics=semantics, vmem_limit_bytes=_VMEM_LIMIT)


def _rms(x, g):
    ms = jnp.mean(x * x, axis=-1, keepdims=True)
    return x * lax.rsqrt(ms + EPS) * g


def _row_blocks(n):
    return [slice(r, min(r + _NORM_ROWS, n)) for r in range(0, n, _NORM_ROWS)]


def _resident(shape, layer=None):
    if layer is None:
        return pl.BlockSpec(shape, lambda *_: (0,) * len(shape), pipeline_mode=pl.Buffered(1))
    return pl.BlockSpec((None,) + tuple(shape), lambda *_: (layer,) + (0,) * len(shape),
                        pipeline_mode=pl.Buffered(1))


def _inproj_plan():
    lru_w = 2 * LRU_WIDTH
    dil_w = 3 * DIL_WIDTH
    segs = [
        (0, lru_w, 0, None),
        (lru_w, lru_w + DIL_WIDTH, 1, _QK_SCALE),
        (lru_w + DIL_WIDTH, lru_w + dil_w, 1, None),
        (lru_w + dil_w, lru_w + dil_w + SWA_WIDTH, 2, _QK_SCALE),
        (lru_w + dil_w + SWA_WIDTH, IN_WIDTH, 2, None),
    ]
    base = {0: 0, 1: lru_w, 2: lru_w + dil_w}
    plan = []
    for c0, c1, oi, scale in segs:
        c = c0
        while c < c1:
            n = min(_COL_CHUNK, c1 - c)
            plan.append((c, c + n, oi, c - base[oi], scale))
            c += n
    return tuple(plan)


def _inproj_kernel(x_ref, g_ref, w_ref, lru_ref, qkvb_ref, qkvc_ref, xn_scr, *, plan):
    for rows in _row_blocks(x_ref.shape[0]):
        xn_scr[rows, :] = _rms(x_ref[rows, :], g_ref[...]).astype(BF16)
    outs = (lru_ref, qkvb_ref, qkvc_ref)
    for c0, c1, oi, o0, scale in plan:
        acc = jnp.dot(xn_scr[...], w_ref[:, c0:c1], preferred_element_type=F32)
        if scale is not None:
            acc = acc * scale
        outs[oi][:, o0:o0 + (c1 - c0)] = acc.astype(outs[oi].dtype)


def _inproj(cfg, x, g, w, layer):
    t, d = x.shape
    tm = cfg.tm
    widths = (2 * LRU_WIDTH, 3 * DIL_WIDTH, SWA_WIDTH + 2 * SWA_KV_WIDTH)
    return pl.pallas_call(
        functools.partial(_inproj_kernel, plan=_inproj_plan()),
        grid=(t // tm,),
        in_specs=[
            pl.BlockSpec((tm, d), lambda i: (i, 0)),
            _resident((1, d)),
            _resident((d, IN_WIDTH), layer),
        ],
        out_specs=[pl.BlockSpec((tm, wd), lambda i: (i, 0)) for wd in widths],
        out_shape=[
            jax.ShapeDtypeStruct((t, widths[0]), F32),
            jax.ShapeDtypeStruct((t, widths[1]), BF16),
            jax.ShapeDtypeStruct((t, widths[2]), BF16),
        ],
        scratch_shapes=[pltpu.VMEM((tm, d), BF16)],
        compiler_params=_params(("parallel",)),
        name="mixer_inproj",
    )(x, g, w)


def _lru_fill_halo(first, last, xa_ref, xp_ref, xn_ref, xext, lc):
    xext[_HALO_ROWS:_HALO_ROWS + lc, :] = xa_ref[...]

    @pl.when(first)
    def _():
        xext[0:_HALO_ROWS, :] = jnp.zeros((_HALO_ROWS, LRU_WIDTH), F32)

    @pl.when(jnp.logical_not(first))
    def _():
        xext[0:_HALO_ROWS, :] = xp_ref[...]

    @pl.when(last)
    def _():
        xext[_HALO_ROWS + lc:, :] = jnp.zeros((_HALO_ROWS, LRU_WIDTH), F32)

    @pl.when(jnp.logical_not(last))
    def _():
        xext[_HALO_ROWS + lc:, :] = xn_ref[...]


def _lru_conv(xext, cw_ref, cb_ref, xc_ref, lc, rb):
    cb = cb_ref[...]
    taps = [cw_ref[j:j + 1, :] for j in range(CONV_WIDTH)]
    for blk in range(lc // rb):
        r0 = blk * rb
        xc_ref[r0:r0 + rb, :] = cb + sum(
            taps[j] * xext[r0 + _HALO_ROWS - CONV_LEFT + j:r0 + _HALO_ROWS - CONV_LEFT + j + rb, :]
            for j in range(CONV_WIDTH))


def _lru_gates(xc_ref, wg_ref, ba_ref, bx_ref, lam_ref, a_scr, u_scr, lc, rb):
    lam = lam_ref[...]
    neg = -lam
    softplus = jnp.maximum(neg, 0.0) + jnp.log1p(jnp.exp(-jnp.abs(neg)))
    for blk in range(lc // rb):
        r0 = blk * rb
        xc = xc_ref[r0:r0 + rb, :]
        xcb = xc.astype(BF16)
        for n in range(LRU_BLOCKS):
            cs = slice(n * LRU_BLOCK_WIDTH, (n + 1) * LRU_BLOCK_WIDTH)
            g = jnp.dot(xcb[:, cs], wg_ref[n], preferred_element_type=F32)
            r = jax.nn.sigmoid(g[:, :LRU_BLOCK_WIDTH] + ba_ref[:, cs])
            ig = jax.nn.sigmoid(g[:, LRU_BLOCK_WIDTH:] + bx_ref[:, cs])
            neg_log_a = (LRU_C * r) * softplus[:, cs]
            a = jnp.exp(-neg_log_a)
            one_minus_a2 = jnp.tanh(neg_log_a) * (1.0 + a * a)
            root = jnp.where(one_minus_a2 > 0.0, one_minus_a2 * lax.rsqrt(one_minus_a2), 0.0)
            u = root * (ig * xc[:, cs])
            groups = slice(r0 // _SCAN_GROUP, (r0 + rb) // _SCAN_GROUP)
            a_scr[groups, :, cs] = a.reshape(rb // _SCAN_GROUP, _SCAN_GROUP, LRU_BLOCK_WIDTH)
            u_scr[groups, :, cs] = u.reshape(rb // _SCAN_GROUP, _SCAN_GROUP, LRU_BLOCK_WIDTH)


def _lru_scan(reset, a_scr, u_scr, h_dst, carry, lc, reverse):
    @pl.when(reset)
    def _():
        carry[...] = jnp.zeros((1, LRU_WIDTH), F32)

    def group(i, h):
        g = lc // _SCAN_GROUP - 1 - i if reverse else i
        order = [_SCAN_GROUP - 1 - j if reverse else j for j in range(_SCAN_GROUP)]
        p = a_scr[g, order[0]:order[0] + 1, :]
        q = u_scr[g, order[0]:order[0] + 1, :]
        outs = [p * h + q]
        for j in order[1:]:
            a = a_scr[g, j:j + 1, :]
            q = a * q + u_scr[g, j:j + 1, :]
            p = a * p
            outs.append(p * h + q)
        for j, o in zip(order, outs):
            h_dst[g, j:j + 1, :] = o
        return outs[-1]

    carry[...] = lax.fori_loop(0, lc // _SCAN_GROUP, group, carry[...], unroll=8)


def _lru_fwd_kernel(flags_ref, xa_ref, xp_ref, xn_ref, cw_ref, cb_ref, wg_ref, ba_ref, bx_ref, lam_ref,
                    hf_ref, xc_ref, xext, a_scr, u_scr, carry, *, lc, rb):
    fl = flags_ref[pl.program_id(0)]
    first = (fl & 1) != 0
    last = (fl & 2) != 0
    _lru_fill_halo(first, last, xa_ref, xp_ref, xn_ref, xext, lc)
    _lru_conv(xext, cw_ref, cb_ref, xc_ref, lc, rb)
    _lru_gates(xc_ref, wg_ref, ba_ref, bx_ref, lam_ref, a_scr, u_scr, lc, rb)
    _lru_scan(first, a_scr, u_scr, hf_ref, carry, lc, reverse=False)


def _lru_bwd_kernel(flags_ref, xc_ref, gate_ref, hf_ref, wg_ref, ba_ref, bx_ref, lam_ref, gn_ref, y_ref,
                    a_scr, u_scr, h_scr, carry, *, lc, rb, nchunks):
    fl = flags_ref[nchunks - 1 - pl.program_id(0)]
    last = (fl & 2) != 0
    _lru_gates(xc_ref, wg_ref, ba_ref, bx_ref, lam_ref, a_scr, u_scr, lc, rb)
    _lru_scan(last, a_scr, u_scr, h_scr, carry, lc, reverse=True)
    for blk in range(lc // rb):
        rows = slice(blk * rb, (blk + 1) * rb)
        groups = slice(blk * rb // _SCAN_GROUP, (blk + 1) * rb // _SCAN_GROUP)
        h = (hf_ref[groups] + h_scr[groups]).reshape(rb, LRU_WIDTH)
        y = h * jax.nn.gelu(gate_ref[rows, :])
        y_ref[rows, :] = _rms(y, gn_ref[...]).astype(BF16)


def _lru(cfg, lru_in, cw, cb, wg, ba, bx, lam, gn):
    t = lru_in.shape[0]
    lc, rb = cfg.lru_chunk, cfg.lru_rows
    nchunks = t // lc
    hb = lc // _HALO_ROWS
    nhalo = t // _HALO_ROWS
    flags = _chunk_flags(cfg, lc)
    row = lambda: _resident((1, LRU_WIDTH))

    def specs(chunk_of):
        return [
            pl.BlockSpec((lc, LRU_WIDTH), lambda i, f: (chunk_of(i), 0)),
            pl.BlockSpec((_HALO_ROWS, LRU_WIDTH), lambda i, f: (jnp.maximum(chunk_of(i) * hb - 1, 0), 0)),
            pl.BlockSpec((_HALO_ROWS, LRU_WIDTH), lambda i, f: (jnp.minimum((chunk_of(i) + 1) * hb, nhalo - 1), 0)),
        ]

    conv_weights = [_resident((CONV_WIDTH, LRU_WIDTH)), row()]

    def weights(d):
        return [
            pl.BlockSpec((None, LRU_BLOCKS, LRU_BLOCK_WIDTH, 2 * LRU_BLOCK_WIDTH), lambda i, f: (d, 0, 0, 0)),
            pl.BlockSpec((None, 1, LRU_WIDTH), lambda i, f: (d, 0, 0)),
            pl.BlockSpec((None, 1, LRU_WIDTH), lambda i, f: (d, 0, 0)),
            pl.BlockSpec((None, 1, LRU_WIDTH), lambda i, f: (d, 0, 0)),
        ]

    grouped = (lc // _SCAN_GROUP, _SCAN_GROUP, LRU_WIDTH)
    scratch = [pltpu.VMEM(grouped, F32), pltpu.VMEM(grouped, F32)]
    carry = [pltpu.VMEM((1, LRU_WIDTH), F32)]

    fwd = lambda i: i
    hf, xc = pl.pallas_call(
        functools.partial(_lru_fwd_kernel, lc=lc, rb=rb),
        grid_spec=pltpu.PrefetchScalarGridSpec(
            num_scalar_prefetch=1, grid=(nchunks,),
            in_specs=specs(fwd) + conv_weights + weights(0),
            out_specs=[pl.BlockSpec(grouped, lambda i, f: (i, 0, 0)),
                       pl.BlockSpec((lc, LRU_WIDTH), lambda i, f: (i, 0))],
            scratch_shapes=[pltpu.VMEM((lc + 2 * _HALO_ROWS, LRU_WIDTH), F32)] + scratch + carry),
        out_shape=[jax.ShapeDtypeStruct((t // _SCAN_GROUP, _SCAN_GROUP, LRU_WIDTH), F32),
                   jax.ShapeDtypeStruct((t, LRU_WIDTH), F32)],
        compiler_params=_params(("arbitrary",)),
        name="lru_forward",
    )(flags, lru_in, lru_in, lru_in, cw, cb, wg, ba, bx, lam)

    bwd = lambda i: nchunks - 1 - i
    return pl.pallas_call(
        functools.partial(_lru_bwd_kernel, lc=lc, rb=rb, nchunks=nchunks),
        grid_spec=pltpu.PrefetchScalarGridSpec(
            num_scalar_prefetch=1, grid=(nchunks,),
            in_specs=[
                pl.BlockSpec((lc, LRU_WIDTH), lambda i, f: (bwd(i), 0)),
                pl.BlockSpec((lc, LRU_WIDTH), lambda i, f: (bwd(i), 1)),
                pl.BlockSpec(grouped, lambda i, f: (bwd(i), 0, 0)),
            ] + weights(1) + [row()],
            out_specs=pl.BlockSpec((lc, LRU_WIDTH), lambda i, f: (bwd(i), 0)),
            scratch_shapes=scratch + [pltpu.VMEM(grouped, F32)] + carry),
        out_shape=jax.ShapeDtypeStruct((t, LRU_WIDTH), BF16),
        compiler_params=_params(("arbitrary",)),
        name="lru_backward",
    )(flags, xc, lru_in, hf, wg, ba, bx, lam, gn)


def _alibi_slopes(n):
    return [2.0 ** (-8.0 * (i + 1) / n) for i in range(n)]


def _band_bias(wq, halo, dist_scale, hq, hkv):
    rep = hq // hkv
    slopes = _alibi_slopes(hq)
    qi = np.arange(wq)[:, None]
    kj = np.arange(wq + 2 * halo)[None, :]
    rel = np.abs(kj - halo - qi)
    out = np.empty((hkv, rep * wq, wq + 2 * halo), np.float32)
    for g in range(hkv):
        for r in range(rep):
            out[g, r * wq:(r + 1) * wq] = np.where(rel <= halo, -slopes[g * rep + r] * dist_scale * rel, _NEG)
    return out


def _band_bias_variants(wq, halo, dist_scale, heads):
    base = _band_bias(wq, halo, dist_scale, heads, heads)
    col = np.arange(wq + 2 * halo)
    out = np.stack([base] * 4)
    out[1::2, :, :, col < halo] = _NEG
    out[2:, :, :, col >= wq + halo] = _NEG
    return out


def _edge_penalties(fl, wq, halo):
    col = lax.broadcasted_iota(jnp.int32, (1, wq + 2 * halo), 1)
    pen_first = jnp.where(col < halo, jnp.where((fl & 1) != 0, _NEG, 0.0), 0.0)
    pen_last = jnp.where(col >= wq + halo, jnp.where((fl & 2) != 0, _NEG, 0.0), 0.0)
    return pen_first, pen_last


def _attend(qs, ks, vs, biases):
    scores = [lax.dot_general(q, k, _NT, preferred_element_type=F32) for q, k in zip(qs, ks)]
    ms, ps = [], []
    for s, b in zip(scores, biases):
        if b is not None:
            s = s + b
        m = jnp.max(s, axis=-1, keepdims=True)
        ms.append(m)
        ps.append(jnp.exp(s - m).astype(BF16))
    ls, accs = [], []
    for p, v in zip(ps, vs):
        v1 = jnp.concatenate([v, jnp.ones(v.shape, v.dtype)], axis=1)
        out = jnp.dot(p, v1, preferred_element_type=F32)
        accs.append(out[:, :HEAD_DIM])
        ls.append(out[:, HEAD_DIM:])
    return ms, ls, accs


def _swa_kernel(flags_ref, q_ref, kc_ref, kp_ref, kn_ref, vc_ref, vp_ref, vn_ref, bias_ref, sink_ref, gn_ref,
                y_out, kbuf, vbuf, *, w, nsub, nb):
    rep = SWA_HEADS // SWA_KV_HEADS
    rows_total = nsub * w
    pen_first, pen_last = _edge_penalties(flags_ref[pl.program_id(0)], w, w)
    kbuf[0:w, :] = kp_ref[...]
    kbuf[w:w + rows_total, :] = kc_ref[...]
    kbuf[w + rows_total:, :] = kn_ref[...]
    vbuf[0:w, :] = vp_ref[...]
    vbuf[w:w + rows_total, :] = vc_ref[...]
    vbuf[w + rows_total:, :] = vn_ref[...]

    for j0 in range(0, nsub, nb):
        blocks = list(range(j0, min(j0 + nb, nsub)))
        qs, ks, vs, bs = [], [], [], []
        for j in blocks:
            rows = slice(j * w, (j + 1) * w)
            for g in range(SWA_KV_HEADS):
                gs = slice(g * HEAD_DIM, (g + 1) * HEAD_DIM)
                qs.append(jnp.concatenate(
                    [q_ref[rows, (g * rep + r) * HEAD_DIM:(g * rep + r + 1) * HEAD_DIM] for r in range(rep)], axis=0))
                ks.append(kbuf[j * w:(j + 3) * w, gs])
                vs.append(vbuf[j * w:(j + 3) * w, gs])
                b = bias_ref[g]
                if j == 0:
                    b = b + pen_first
                if j == nsub - 1:
                    b = b + pen_last
                bs.append(b)
        ms, ls, accs = _attend(qs, ks, vs, bs)
        for bi, j in enumerate(blocks):
            heads_out = []
            for g in range(SWA_KV_HEADS):
                idx = bi * SWA_KV_HEADS + g
                for r in range(rep):
                    h = g * rep + r
                    part = slice(r * w, (r + 1) * w)
                    m, l, acc = ms[idx][part], ls[idx][part], accs[idx][part]
                    factor = jax.nn.sigmoid(m + jnp.log(l) - sink_ref[:, h:h + 1])
                    heads_out.append((acc / l) * factor)
            y = jnp.concatenate(heads_out, axis=1)
            y_out[j * w:(j + 1) * w, :] = _rms(y, gn_ref[...]).astype(BF16)


def _swa(cfg, qkvc, gn, sink):
    t, c = qkvc.shape
    w = SWA_WINDOW
    rows = cfg.swa_rows
    nsub = rows // w
    nchunks = t // rows
    nblk = t // w
    qw, kvw = SWA_WIDTH, SWA_KV_WIDTH
    rep = SWA_HEADS // SWA_KV_HEADS
    assert qw % kvw == 0 and c == qw + 2 * kvw
    kcol, vcol = qw // kvw, qw // kvw + 1
    prev = lambda col: (lambda i, f: (jnp.maximum(i * nsub - 1, 0), col))
    nxt = lambda col: (lambda i, f: (jnp.minimum((i + 1) * nsub, nblk - 1), col))
    return pl.pallas_call(
        functools.partial(_swa_kernel, w=w, nsub=nsub, nb=cfg.swa_batch),
        grid_spec=pltpu.PrefetchScalarGridSpec(
            num_scalar_prefetch=1, grid=(nchunks,),
            in_specs=[
                pl.BlockSpec((rows, qw), lambda i, f: (i, 0)),
                pl.BlockSpec((rows, kvw), lambda i, f: (i, kcol)),
                pl.BlockSpec((w, kvw), prev(kcol)),
                pl.BlockSpec((w, kvw), nxt(kcol)),
                pl.BlockSpec((rows, kvw), lambda i, f: (i, vcol)),
                pl.BlockSpec((w, kvw), prev(vcol)),
                pl.BlockSpec((w, kvw), nxt(vcol)),
                _resident((SWA_KV_HEADS, rep * w, 3 * w)),
                _resident((1, _LANES)),
                _resident((1, qw)),
            ],
            out_specs=pl.BlockSpec((rows, qw), lambda i, f: (i, 0)),
            scratch_shapes=[pltpu.VMEM((rows + 2 * w, kvw), BF16), pltpu.VMEM((rows + 2 * w, kvw), BF16)]),
        out_shape=jax.ShapeDtypeStruct((t, qw), BF16),
        compiler_params=_params(("parallel",)),
        name="windowed_gqa_sink",
    )(_chunk_flags(cfg, rows), qkvc, qkvc, qkvc, qkvc, qkvc, qkvc, qkvc,
      jnp.asarray(_band_bias(w, w, 1, SWA_HEADS, SWA_KV_HEADS)), sink, gn)


def _perm_matrix(d):
    n = _PERM_ROWS
    per = n // d
    p = np.zeros((n, n), np.float32)
    for r in range(d):
        for m in range(per):
            p[r * per + m, d * m + r] = 1.0
    return p


def _dilated_qrows(c, d):
    return min(_DIL_QROWS_MAX, c // d)


def _dilated_kernel(flags_ref, q_ref, kp_ref, kc_ref, kn_ref, vp_ref, vc_ref, vn_ref, bias0, bias1, bias2,
                    perm_ref, gn_ref, y_ref, qd, kd, vd, ring_k, ring_v, acc_nat, m_nat, l_nat, *,
                    c, batch_rows, tail):
    w = _DIL_W
    nh = DIL_HEADS
    step = pl.program_id(0)
    fl = flags_ref[step]
    slot_prev, slot_cur, slot_next = lax.rem(step + 2, 3), lax.rem(step, 3), lax.rem(step + 1, 3)
    bias_refs = (bias0, bias1, bias2)
    starts_sequence = fl & 1
    ends_sequence = (fl >> 1) & 1
    heads = [slice(h * HEAD_DIM, (h + 1) * HEAD_DIM) for h in range(nh)]

    def run_pattern(p, d, qsrc, ksrc, vsrc, qstride, kstride, first_pattern, last_pattern):
        wq = _dilated_qrows(c, d)
        win = wq + 2 * w
        nblk = c // (d * wq)
        nb = max(1, batch_rows // wq)
        assert (c // wq) % nb == 0
        assert d == 1 or not last_pattern

        def body(it, carry):
            qs, ks, vs, bs, where = [], [], [], [], []
            for b in range(nb):
                sb = it * nb + b
                r = sb // nblk
                s = sb % nblk
                var = jnp.where(s == 0, starts_sequence, 0) + 2 * jnp.where(s == nblk - 1, ends_sequence, 0)
                q0 = pl.multiple_of(r * qstride + s * wq, w)
                k0 = pl.multiple_of(r * kstride + s * wq, w)
                where.append(d * wq * s + r)
                for h in range(nh):
                    qs.append(qsrc[pl.ds(q0, wq), heads[h]])
                    if ksrc is None:
                        ks.append(jnp.concatenate([ring_k[sl, pl.ds(q0, wq), heads[h]]
                                                   for sl in (slot_prev, slot_cur, slot_next)], axis=0))
                        vs.append(jnp.concatenate([ring_v[sl, pl.ds(q0, wq), heads[h]]
                                                   for sl in (slot_prev, slot_cur, slot_next)], axis=0))
                    else:
                        ks.append(ksrc[pl.ds(k0, win), heads[h]])
                        vs.append(vsrc[pl.ds(k0, win), heads[h]])
                    bs.append(bias_refs[p][var, h])
            ms, ls, accs = _attend(qs, ks, vs, bs)
            for b in range(nb):
                nat = pl.ds(where[b], wq, stride=d) if d > 1 else pl.ds(pl.multiple_of(where[b], w), wq)
                outs = []
                for h in range(nh):
                    m, l, acc = ms[b * nh + h], ls[b * nh + h], accs[b * nh + h]
                    if not first_pattern:
                        m_p = m_nat[h, nat, :]
                        m_n = jnp.maximum(m_p, m)
                        alpha = jnp.exp(m_p - m_n)
                        beta = jnp.exp(m - m_n)
                        l = alpha * l_nat[h, nat, :] + beta * l
                        acc = alpha * acc_nat[h, nat, :] + beta * acc
                        m = m_n
                    if last_pattern:
                        outs.append(acc / l)
                    else:
                        acc_nat[h, nat, :] = acc
                        m_nat[h, nat, :] = jnp.broadcast_to(m, (wq, HEAD_DIM))
                        l_nat[h, nat, :] = jnp.broadcast_to(l, (wq, HEAD_DIM))
                if last_pattern:
                    y_ref[nat, :] = _rms(jnp.concatenate(outs, axis=1), gn_ref[...]).astype(BF16)
            return carry

        lax.fori_loop(0, c // (wq * nb), body, 0)

    def deinterleave(p, d):
        per = _PERM_ROWS // d
        halo_groups = (d * w) // _PERM_ROWS
        chunk_groups = c // _PERM_ROWS
        kstride = (chunk_groups + 2 * halo_groups) * per
        perm = perm_ref[p - 1]

        def move(src, row0, dst, g, stride):
            res = jnp.dot(perm, src[row0:row0 + _PERM_ROWS, :], preferred_element_type=F32).astype(BF16)
            for r in range(d):
                dst[r * stride + g * per:r * stride + (g + 1) * per, :] = res[r * per:(r + 1) * per]

        for g in range(chunk_groups):
            move(q_ref, g * _PERM_ROWS, qd, g, c // d)
        if d * w == c:
            def fill(slot, k_src, v_src):
                for src, ring in ((k_src, ring_k), (v_src, ring_v)):
                    for g in range(chunk_groups):
                        move(src, g * _PERM_ROWS, ring.at[slot], g, c // d)

            @pl.when(step == 0)
            def _():
                ring_k[2] = jnp.zeros((c, DIL_WIDTH), BF16)
                ring_v[2] = jnp.zeros((c, DIL_WIDTH), BF16)
                fill(0, kc_ref, vc_ref)

            fill(slot_next, kn_ref, vn_ref)
            return c // d, None
        for prev_ref, cur_ref, next_ref, dst in ((kp_ref, kc_ref, kn_ref, kd), (vp_ref, vc_ref, vn_ref, vd)):
            srcs = ([(prev_ref, tail - (halo_groups - g) * _PERM_ROWS) for g in range(halo_groups)]
                    + [(cur_ref, g * _PERM_ROWS) for g in range(chunk_groups)]
                    + [(next_ref, g * _PERM_ROWS) for g in range(halo_groups)])
            for g, (src, row0) in enumerate(srcs):
                move(src, row0, dst, g, kstride)
        return c // d, kstride

    order = sorted(range(len(DIL_PATTERNS)), key=lambda p: -DIL_PATTERNS[p][1])
    for idx, p in enumerate(order):
        d = DIL_PATTERNS[p][1]
        first, last = idx == 0, idx == len(order) - 1
        if d == 1:
            for prev_ref, cur_ref, next_ref, dst in ((kp_ref, kc_ref, kn_ref, kd), (vp_ref, vc_ref, vn_ref, vd)):
                dst[0:w, :] = prev_ref[tail - w:tail, :]
                dst[w:w + c, :] = cur_ref[...]
                dst[w + c:2 * w + c, :] = next_ref[0:w, :]
            run_pattern(p, d, q_ref, kd, vd, 0, 0, first, last)
        else:
            qstride, kstride = deinterleave(p, d)
            if kstride is None:
                run_pattern(p, d, qd, None, None, qstride, 0, first, last)
            else:
                run_pattern(p, d, qd, kd, vd, qstride, kstride, first, last)


def _dilated(cfg, qkvb, gn):
    t = qkvb.shape[0]
    c = cfg.dil_rows
    w = _DIL_W
    n = t // c
    dmax = max(d for _, d in DIL_PATTERNS)
    assert DIL_PATTERNS[0][1] == 1 and c % (dmax * w) == 0 and c >= dmax * w and c % _PERM_ROWS == 0
    assert all((d * w) % _PERM_ROWS == 0 for _, d in DIL_PATTERNS[1:])
    assert len(DIL_PATTERNS) == 3
    biases = [_band_bias_variants(_dilated_qrows(c, d), w, d, DIL_HEADS) for _, d in DIL_PATTERNS]
    perm = np.stack([_perm_matrix(d) for _, d in DIL_PATTERNS[1:]])
    blk = (c, DIL_WIDTH)
    assert dmax * w == c
    local = [d for _, d in DIL_PATTERNS if d * w < c]
    tail = max(d * w for d in local)
    assert c % tail == 0 and tail % _PERM_ROWS == 0
    class_rows = max((c // _PERM_ROWS + 2 * (d * w // _PERM_ROWS)) * _PERM_ROWS if d > 1 else c + 2 * w for d in local)
    prev = lambda col: (lambda i, f: (jnp.maximum(i * (c // tail) - 1, 0), col))
    cur = lambda col: (lambda i, f: (i, col))
    nxt = lambda col: (lambda i, f: (jnp.minimum(i + 1, n - 1), col))
    tail_blk = (tail, DIL_WIDTH)
    return pl.pallas_call(
        functools.partial(_dilated_kernel, c=c, batch_rows=cfg.dil_batch * w, tail=tail),
        grid_spec=pltpu.PrefetchScalarGridSpec(
            num_scalar_prefetch=1, grid=(n,),
            in_specs=[
                pl.BlockSpec(blk, cur(0)),
                pl.BlockSpec(tail_blk, prev(1)), pl.BlockSpec(blk, cur(1)), pl.BlockSpec(blk, nxt(1)),
                pl.BlockSpec(tail_blk, prev(2)), pl.BlockSpec(blk, cur(2)), pl.BlockSpec(blk, nxt(2)),
                _resident(biases[0].shape), _resident(biases[1].shape), _resident(biases[2].shape),
                _resident(perm.shape), _resident((1, DIL_WIDTH)),
            ],
            out_specs=pl.BlockSpec(blk, cur(0)),
            scratch_shapes=[
                pltpu.VMEM((c, DIL_WIDTH), BF16),
                pltpu.VMEM((class_rows, DIL_WIDTH), BF16), pltpu.VMEM((class_rows, DIL_WIDTH), BF16),
                pltpu.VMEM((3, c, DIL_WIDTH), BF16), pltpu.VMEM((3, c, DIL_WIDTH), BF16),
                pltpu.VMEM((DIL_HEADS, c, HEAD_DIM), F32),
                pltpu.VMEM((DIL_HEADS, c, HEAD_DIM), F32),
                pltpu.VMEM((DIL_HEADS, c, HEAD_DIM), F32),
            ]),
        out_shape=jax.ShapeDtypeStruct((t, DIL_WIDTH), BF16),
        compiler_params=_params(("arbitrary",)),
        name="dilated_attention",
    )(_chunk_flags(cfg, c), qkvb, qkvb, qkvb, qkvb, qkvb, qkvb, qkvb,
      jnp.asarray(biases[0]), jnp.asarray(biases[1]), jnp.asarray(biases[2]), jnp.asarray(perm, dtype=BF16), gn)


def _outproj_kernel(ya_ref, yb_ref, yc_ref, x_ref, w_ref, g_ref, o_ref, *, nchunk):
    d = o_ref.shape[1]
    b0, b1 = LRU_WIDTH, LRU_WIDTH + DIL_WIDTH
    for c in range(0, d, nchunk):
        cs = slice(c, c + nchunk)
        acc = jnp.dot(ya_ref[...], w_ref[0:b0, cs], preferred_element_type=F32)
        acc += jnp.dot(yb_ref[...], w_ref[b0:b1, cs], preferred_element_type=F32)
        acc += jnp.dot(yc_ref[...], w_ref[b1:, cs], preferred_element_type=F32)
        o_ref[:, cs] = acc
    for rows in _row_blocks(o_ref.shape[0]):
        o_ref[rows, :] = x_ref[rows, :] + _rms(o_ref[rows, :], g_ref[...])


def _outproj(cfg, ya, yb, yc, x, w, g, layer):
    t, d = x.shape
    tm = cfg.tm_out
    return pl.pallas_call(
        functools.partial(_outproj_kernel, nchunk=min(_COL_CHUNK, d)),
        grid=(t // tm,),
        in_specs=[
            pl.BlockSpec((tm, LRU_WIDTH), lambda i: (i, 0)),
            pl.BlockSpec((tm, DIL_WIDTH), lambda i: (i, 0)),
            pl.BlockSpec((tm, SWA_WIDTH), lambda i: (i, 0)),
            pl.BlockSpec((tm, d), lambda i: (i, 0)),
            _resident((MIX_WIDTH, d), layer),
            _resident((1, d)),
        ],
        out_specs=pl.BlockSpec((tm, d), lambda i: (i, 0)),
        out_shape=jax.ShapeDtypeStruct((t, d), F32),
        compiler_params=_params(("parallel",)),
        name="mixer_outproj",
    )(ya, yb, yc, x, w, g)


def _memkv_kernel(mem_ref, g_ref, w_ref, k_ref, v_ref):
    mn = _rms(mem_ref[...], g_ref[...]).astype(BF16)
    kv = jnp.dot(mn, w_ref[...], preferred_element_type=F32)
    k_ref[...] = kv[:, :MEM_WIDTH].astype(BF16)
    v_ref[...] = kv[:, MEM_WIDTH:].astype(BF16)


def _memkv(cfg, mem, g, wkv, layer):
    rows, d = mem.shape
    nm = cfg.n_mem
    shape = jax.ShapeDtypeStruct((rows, MEM_WIDTH), BF16)
    return pl.pallas_call(
        _memkv_kernel,
        grid=(rows // nm,),
        in_specs=[pl.BlockSpec((nm, d), lambda i: (i, 0)), _resident((1, d)),
                  _resident((d, 2 * MEM_WIDTH), layer)],
        out_specs=[pl.BlockSpec((nm, MEM_WIDTH), lambda i: (i, 0))] * 2,
        out_shape=[shape, shape],
        compiler_params=_params(("parallel",)),
        name="memory_kv",
    )(mem, g, wkv)


def _cross_kernel(seq_ref, x_ref, gpre_ref, wq_ref, k_ref, v_ref, wo_ref, gpost_ref, o_ref, xn_scr, o_scr, *,
                  nchunk):
    del seq_ref
    tm, d = x_ref.shape
    for rows in _row_blocks(tm):
        xn_scr[rows, :] = _rms(x_ref[rows, :], gpre_ref[...]).astype(BF16)
    q = (jnp.dot(xn_scr[...], wq_ref[...], preferred_element_type=F32) * _QK_SCALE).astype(BF16)
    heads = [slice(h * HEAD_DIM, (h + 1) * HEAD_DIM) for h in range(MEM_HEADS)]
    _, ls, accs = _attend([q[:, hs] for hs in heads], [k_ref[:, hs] for hs in heads],
                          [v_ref[:, hs] for hs in heads], [None] * MEM_HEADS)
    for hs, l, acc in zip(heads, ls, accs):
        o_scr[:, hs] = (acc / l).astype(BF16)
    for c in range(0, d, nchunk):
        cs = slice(c, min(c + nchunk, d))
        o_ref[:, cs] = jnp.dot(o_scr[...], wo_ref[:, cs], preferred_element_type=F32)
    for rows in _row_blocks(tm):
        o_ref[rows, :] = x_ref[rows, :] + _rms(o_ref[rows, :], gpost_ref[...])


def _cross(cfg, x, gpre, wq, kmem, vmem, wo, gpost, layer):
    t, d = x.shape
    tm, nm = cfg.tm_cross, cfg.n_mem
    seqs, _ = _sequences(cfg)
    seq_of_tile = []
    for si, (_, length) in enumerate(seqs):
        assert length % tm == 0
        seq_of_tile += [si] * (length // tm)
    seq_of_tile = jnp.asarray(np.asarray(seq_of_tile, np.int32))
    return pl.pallas_call(
        functools.partial(_cross_kernel, nchunk=_COL_CHUNK),
        grid_spec=pltpu.PrefetchScalarGridSpec(
            num_scalar_prefetch=1, grid=(t // tm,),
            in_specs=[
                pl.BlockSpec((tm, d), lambda i, s: (i, 0)),
                _resident((1, d)),
                _resident((d, MEM_WIDTH), layer),
                pl.BlockSpec((nm, MEM_WIDTH), lambda i, s: (s[i], 0)),
                pl.BlockSpec((nm, MEM_WIDTH), lambda i, s: (s[i], 0)),
                _resident((MEM_WIDTH, d), layer),
                _resident((1, d)),
            ],
            out_specs=pl.BlockSpec((tm, d), lambda i, s: (i, 0)),
            scratch_shapes=[pltpu.VMEM((tm, d), BF16), pltpu.VMEM((tm, MEM_WIDTH), BF16)]),
        out_shape=jax.ShapeDtypeStruct((t, d), F32),
        compiler_params=_params(("parallel",)),
        name="memory_cross_attention",
    )(seq_of_tile, x, gpre, wq, kmem, vmem, wo, gpost)


def _ffn_kernel(x_ref, gpre_ref, w1_ref, w2_ref, gpost_ref, o_ref, xn_scr, h_scr, *, nchunk):
    f = pl.program_id(1)
    tf = w1_ref.shape[1]
    d = w2_ref.shape[1]

    tm = x_ref.shape[0]
    row_blocks = [slice(r, min(r + _NORM_ROWS, tm)) for r in range(0, tm, _NORM_ROWS)]

    @pl.when(f == 0)
    def _():
        for rows in row_blocks:
            xn_scr[rows, :] = _rms(x_ref[rows, :], gpre_ref[...]).astype(BF16)
            o_ref[rows, :] = jnp.zeros((rows.stop - rows.start, d), F32)

    for c in range(0, tf, nchunk):
        cs = slice(c, min(c + nchunk, tf))
        h = jnp.dot(xn_scr[...], w1_ref[:, cs], preferred_element_type=F32)
        h_scr[:, cs] = jnp.square(jnp.maximum(h, 0.0)).astype(BF16)
    for c in range(0, d, nchunk):
        cs = slice(c, min(c + nchunk, d))
        o_ref[:, cs] += jnp.dot(h_scr[...], w2_ref[:, cs], preferred_element_type=F32)

    @pl.when(f == pl.num_programs(1) - 1)
    def _():
        for rows in row_blocks:
            o_ref[rows, :] = x_ref[rows, :] + _rms(o_ref[rows, :], gpost_ref[...])


def _ffn(cfg, x, gpre, w1, w2, gpost, layer):
    t, d = x.shape
    tm, tf = cfg.tm_ffn, cfg.tf
    dff = w1.shape[2]
    return pl.pallas_call(
        functools.partial(_ffn_kernel, nchunk=_COL_CHUNK),
        grid=(t // tm, dff // tf),
        in_specs=[
            pl.BlockSpec((tm, d), lambda i, f: (i, 0)),
            _resident((1, d)),
            pl.BlockSpec((None, d, tf), lambda i, f: (layer, 0, f)),
            pl.BlockSpec((None, tf, d), lambda i, f: (layer, f, 0)),
            _resident((1, d)),
        ],
        out_specs=pl.BlockSpec((tm, d), lambda i, f: (i, 0)),
        out_shape=jax.ShapeDtypeStruct((t, d), F32),
        scratch_shapes=[pltpu.VMEM((tm, d), BF16), pltpu.VMEM((tm, tf), BF16)],
        compiler_params=_params(("parallel", "arbitrary")),
        name="squared_relu_mlp",
    )(x, gpre, w1, w2, gpost)


def _forward(cfg, x, mem, p):
    row = lambda a: a.reshape(1, -1).astype(F32)
    for l in range(cfg.depth):
        lru_in, qkvb, qkvc = _inproj(cfg, x, row(p["mix_norm_pre"][l]), p["w_in"], l)
        gn = p["group_norm"][l]
        wg = jnp.concatenate([p["lru_wa"][l], p["lru_wx"][l]], axis=-1).astype(BF16)
        ya = _lru(cfg, lru_in, p["conv_w"][l], row(p["conv_b"][l]), wg,
                  p["lru_ba"][l][:, None, :], p["lru_bx"][l][:, None, :], p["lru_lam"][l][:, None, :],
                  row(gn[:LRU_WIDTH]))
        yb = _dilated(cfg, qkvb, row(gn[LRU_WIDTH:LRU_WIDTH + DIL_WIDTH]))
        sink = jnp.zeros((1, _LANES), F32).at[0, :SWA_HEADS].set(p["swa_sink"][l].astype(F32))
        yc = _swa(cfg, qkvc, row(gn[LRU_WIDTH + DIL_WIDTH:]), sink)
        x = _outproj(cfg, ya, yb, yc, x, p["w_out"], row(p["mix_norm_post"][l]), l)
        kmem, vmem = _memkv(cfg, mem, row(p["mem_kv_norm"][l]), p["w_mkv"], l)
        x = _cross(cfg, x, row(p["mem_norm_pre"][l]), p["w_mq"], kmem, vmem, p["w_mo"],
                   row(p["mem_norm_post"][l]), l)
        x = _ffn(cfg, x, row(p["ffn_norm_pre"][l]), p["w_ff1"], p["w_ff2"], row(p["ffn_norm_post"][l]), l)
    return x


def _run(cfg, x_prompt, x_sample, mem_prompt, mem_sample, mix_norm_pre, mix_norm_post, w_in, conv_w, conv_b,
         lru_wa, lru_ba, lru_wx, lru_bx, lru_lam, swa_sink, group_norm, w_out, mem_norm_pre, mem_norm_post,
         mem_kv_norm, w_mq, w_mk, w_mv, w_mo, ffn_norm_pre, ffn_norm_post, w_ff1, w_ff2):
    d = cfg.d_model
    p = dict(
        mix_norm_pre=mix_norm_pre, mix_norm_post=mix_norm_post, w_in=w_in.astype(BF16), conv_w=conv_w,
        conv_b=conv_b, lru_wa=lru_wa, lru_ba=lru_ba, lru_wx=lru_wx, lru_bx=lru_bx, lru_lam=lru_lam,
        swa_sink=swa_sink, group_norm=group_norm, w_out=w_out.astype(BF16), mem_norm_pre=mem_norm_pre,
        mem_norm_post=mem_norm_post, mem_kv_norm=mem_kv_norm, w_mq=w_mq.astype(BF16),
        w_mkv=jnp.concatenate([w_mk, w_mv], axis=-1).astype(BF16), w_mo=w_mo.astype(BF16),
        ffn_norm_pre=ffn_norm_pre, ffn_norm_post=ffn_norm_post, w_ff1=w_ff1.astype(BF16),
        w_ff2=w_ff2.astype(BF16))
    outs = []
    for group, x, mem in zip(cfg.groups, (x_prompt, x_sample), (mem_prompt, mem_sample)):
        sub = cfg._replace(groups=(group,))
        outs.append(_forward(sub, x.reshape(-1, d), mem.reshape(-1, d), p).reshape(x.shape))
    return tuple(outs)


def kernel(x_prompt, x_sample, mem_prompt, mem_sample, mix_norm_pre, mix_norm_post, w_in, conv_w, conv_b, lru_wa,
           lru_ba, lru_wx, lru_bx, lru_lam, swa_sink, group_norm, w_out, mem_norm_pre, mem_norm_post, mem_kv_norm,
           w_mq, w_mk, w_mv, w_mo, ffn_norm_pre, ffn_norm_post, w_ff1, w_ff2):
    return _run(_CFG, x_prompt, x_sample, mem_prompt, mem_sample, mix_norm_pre, mix_norm_post, w_in, conv_w,
                conv_b, lru_wa, lru_ba, lru_wx, lru_bx, lru_lam, swa_sink, group_norm, w_out, mem_norm_pre,
                mem_norm_post, mem_kv_norm, w_mq, w_mk, w_mv, w_mo, ffn_norm_pre, ffn_norm_post, w_ff1, w_ff2)
```

```python
import functools
from typing import NamedTuple

import numpy as np
import jax
import jax.numpy as jnp
from jax import lax
from jax.experimental import pallas as pl
from jax.experimental.pallas import tpu as pltpu

F32 = jnp.float32
BF16 = jnp.bfloat16

D_MODEL = 2048
BATCH = 8
SEQ = 4096
DEPTH = 4
DEC_BATCH = 1
DEC_SEQ = 16384
HEAD_DIM = 128
LRU_WIDTH = 512
LRU_BLOCKS = 4
LRU_BLOCK_WIDTH = LRU_WIDTH // LRU_BLOCKS
CONV_WIDTH = 4
CONV_LEFT = 2
LRU_C = 8.0
DIL_HEADS = 6
DIL_PATTERNS = ((128, 1), (512, 4), (2048, 16))
SWA_HEADS = 6
SWA_KV_HEADS = 2
SWA_WINDOW = 128
DIL_WIDTH = DIL_HEADS * HEAD_DIM
SWA_WIDTH = SWA_HEADS * HEAD_DIM
SWA_KV_WIDTH = SWA_KV_HEADS * HEAD_DIM
MIX_WIDTH = LRU_WIDTH + DIL_WIDTH + SWA_WIDTH
IN_WIDTH = 2 * LRU_WIDTH + 3 * DIL_WIDTH + SWA_WIDTH + 2 * SWA_KV_WIDTH
N_MEM = 256
MEM_HEADS = 4
MEM_WIDTH = MEM_HEADS * HEAD_DIM
D_FF = 4 * D_MODEL
EPS = 1e-6

_NEG = -1e30
_QK_SCALE = HEAD_DIM ** -0.5
_HALO_ROWS = 8
_LANES = 128
_COL_CHUNK = 512
_NORM_ROWS = 128
_SCAN_GROUP = 8
_DIL_W = DIL_PATTERNS[0][0] // (2 * DIL_PATTERNS[0][1])
assert all(wn // (2 * d) == _DIL_W for wn, d in DIL_PATTERNS)
_DIL_QROWS_MAX = 128
_PERM_ROWS = 256
_V7X_VMEM_BYTES = 64 * 1024 * 1024
_VMEM_LIMIT = _V7X_VMEM_BYTES - 3 * 1024 * 1024
_NT = (((1,), (1,)), ((), ()))


class _Cfg(NamedTuple):
    d_model: int
    d_ff: int
    depth: int
    groups: tuple
    n_mem: int
    tm: int
    tm_out: int
    tm_cross: int
    tm_ffn: int
    tf: int
    lru_chunk: int
    lru_rows: int
    swa_rows: int
    dil_rows: int
    swa_batch: int
    dil_batch: int


_CFG = _Cfg(d_model=D_MODEL, d_ff=D_FF, depth=DEPTH, groups=((BATCH, SEQ), (DEC_BATCH, DEC_SEQ)),
            n_mem=N_MEM, tm=512, tm_out=1024, tm_cross=1024, tm_ffn=1024, tf=1024, lru_chunk=2048, lru_rows=256,
            swa_rows=2048,
            dil_rows=1024, swa_batch=2, dil_batch=4)


def _sequences(cfg):
    out, start = [], 0
    for n, length in cfg.groups:
        for _ in range(n):
            out.append((start, length))
            start += length
    return out, start


def _chunk_flags(cfg, rows):
    seqs, total = _sequences(cfg)
    starts = {s for s, _ in seqs}
    ends = {s + l for s, l in seqs}
    for s, l in seqs:
        assert l % rows == 0, (l, rows)
    n = total // rows
    flags = np.zeros((n,), np.int32)
    for c in range(n):
        flags[c] = (1 if c * rows in starts else 0) | (2 if (c + 1) * rows in ends else 0)
    return jnp.asarray(flags)


def _params(semantics):
    return pltpu.CompilerParams(dimension_semantics=semantics, vmem_limit_bytes=_VMEM_LIMIT)


def _rms(x, g):
    ms = jnp.mean(x * x, axis=-1, keepdims=True)
    return x * lax.rsqrt(ms + EPS) * g


def _row_blocks(n):
    return [slice(r, min(r + _NORM_ROWS, n)) for r in range(0, n, _NORM_ROWS)]


def _resident(shape, layer=None):
    if layer is None:
        return pl.BlockSpec(shape, lambda *_: (0,) * len(shape), pipeline_mode=pl.Buffered(1))
    return pl.BlockSpec((None,) + tuple(shape), lambda *_: (layer,) + (0,) * len(shape),
                        pipeline_mode=pl.Buffered(1))


def _inproj_plan():
    lru_w = 2 * LRU_WIDTH
    dil_w = 3 * DIL_WIDTH
    segs = [
        (0, lru_w, 0, None),
        (lru_w, lru_w + DIL_WIDTH, 1, _QK_SCALE),
        (lru_w + DIL_WIDTH, lru_w + dil_w, 1, None),
        (lru_w + dil_w, lru_w + dil_w + SWA_WIDTH, 2, _QK_SCALE),
        (lru_w + dil_w + SWA_WIDTH, IN_WIDTH, 2, None),
    ]
    base = {0: 0, 1: lru_w, 2: lru_w + dil_w}
    plan = []
    for c0, c1, oi, scale in segs:
        c = c0
        while c < c1:
            n = min(_COL_CHUNK, c1 - c)
            plan.append((c, c + n, oi, c - base[oi], scale))
            c += n
    return tuple(plan)


def _inproj_kernel(x_ref, g_ref, w_ref, lru_ref, qkvb_ref, qkvc_ref, xn_scr, *, plan):
    for rows in _row_blocks(x_ref.shape[0]):
        xn_scr[rows, :] = _rms(x_ref[rows, :], g_ref[...]).astype(BF16)
    outs = (lru_ref, qkvb_ref, qkvc_ref)
    for c0, c1, oi, o0, scale in plan:
        acc = jnp.dot(xn_scr[...], w_ref[:, c0:c1], preferred_element_type=F32)
        if scale is not None:
            acc = acc * scale
        outs[oi][:, o0:o0 + (c1 - c0)] = acc.astype(outs[oi].dtype)


def _inproj(cfg, x, g, w, layer):
    t, d = x.shape
    tm = cfg.tm
    widths = (2 * LRU_WIDTH, 3 * DIL_WIDTH, SWA_WIDTH + 2 * SWA_KV_WIDTH)
    return pl.pallas_call(
        functools.partial(_inproj_kernel, plan=_inproj_plan()),
        grid=(t // tm,),
        in_specs=[
            pl.BlockSpec((tm, d), lambda i: (i, 0)),
            _resident((1, d)),
            _resident((d, IN_WIDTH), layer),
        ],
        out_specs=[pl.BlockSpec((tm, wd), lambda i: (i, 0)) for wd in widths],
        out_shape=[
            jax.ShapeDtypeStruct((t, widths[0]), F32),
            jax.ShapeDtypeStruct((t, widths[1]), BF16),
            jax.ShapeDtypeStruct((t, widths[2]), BF16),
        ],
        scratch_shapes=[pltpu.VMEM((tm, d), BF16)],
        compiler_params=_params(("parallel",)),
        name="mixer_inproj",
    )(x, g, w)


def _lru_fill_halo(first, last, xa_ref, xp_ref, xn_ref, xext, lc):
    xext[_HALO_ROWS:_HALO_ROWS + lc, :] = xa_ref[...]

    @pl.when(first)
    def _():
        xext[0:_HALO_ROWS, :] = jnp.zeros((_HALO_ROWS, LRU_WIDTH), F32)

    @pl.when(jnp.logical_not(first))
    def _():
        xext[0:_HALO_ROWS, :] = xp_ref[...]

    @pl.when(last)
    def _():
        xext[_HALO_ROWS + lc:, :] = jnp.zeros((_HALO_ROWS, LRU_WIDTH), F32)

    @pl.when(jnp.logical_not(last))
    def _():
        xext[_HALO_ROWS + lc:, :] = xn_ref[...]


def _lru_conv(xext, cw_ref, cb_ref, xc_ref, lc, rb):
    cb = cb_ref[...]
    taps = [cw_ref[j:j + 1, :] for j in range(CONV_WIDTH)]
    for blk in range(lc // rb):
        r0 = blk * rb
        xc_ref[r0:r0 + rb, :] = cb + sum(
            taps[j] * xext[r0 + _HALO_ROWS - CONV_LEFT + j:r0 + _HALO_ROWS - CONV_LEFT + j + rb, :]
            for j in range(CONV_WIDTH))


def _lru_gates(xc_ref, wg_ref, ba_ref, bx_ref, lam_ref, a_scr, u_scr, lc, rb):
    lam = lam_ref[...]
    neg = -lam
    softplus = jnp.maximum(neg, 0.0) + jnp.log1p(jnp.exp(-jnp.abs(neg)))
    for blk in range(lc // rb):
        r0 = blk * rb
        xc = xc_ref[r0:r0 + rb, :]
        xcb = xc.astype(BF16)
        for n in range(LRU_BLOCKS):
            cs = slice(n * LRU_BLOCK_WIDTH, (n + 1) * LRU_BLOCK_WIDTH)
            g = jnp.dot(xcb[:, cs], wg_ref[n], preferred_element_type=F32)
            r = jax.nn.sigmoid(g[:, :LRU_BLOCK_WIDTH] + ba_ref[:, cs])
            ig = jax.nn.sigmoid(g[:, LRU_BLOCK_WIDTH:] + bx_ref[:, cs])
            neg_log_a = (LRU_C * r) * softplus[:, cs]
            a = jnp.exp(-neg_log_a)
            one_minus_a2 = jnp.tanh(neg_log_a) * (1.0 + a * a)
            root = jnp.where(one_minus_a2 > 0.0, one_minus_a2 * lax.rsqrt(one_minus_a2), 0.0)
            u = root * (ig * xc[:, cs])
            groups = slice(r0 // _SCAN_GROUP, (r0 + rb) // _SCAN_GROUP)
            a_scr[groups, :, cs] = a.reshape(rb // _SCAN_GROUP, _SCAN_GROUP, LRU_BLOCK_WIDTH)
            u_scr[groups, :, cs] = u.reshape(rb // _SCAN_GROUP, _SCAN_GROUP, LRU_BLOCK_WIDTH)


def _lru_scan(reset, a_scr, u_scr, h_dst, carry, lc, reverse):
    @pl.when(reset)
    def _():
        carry[...] = jnp.zeros((1, LRU_WIDTH), F32)

    def group(i, h):
        g = lc // _SCAN_GROUP - 1 - i if reverse else i
        order = [_SCAN_GROUP - 1 - j if reverse else j for j in range(_SCAN_GROUP)]
        p = a_scr[g, order[0]:order[0] + 1, :]
        q = u_scr[g, order[0]:order[0] + 1, :]
        outs = [p * h + q]
        for j in order[1:]:
            a = a_scr[g, j:j + 1, :]
            q = a * q + u_scr[g, j:j + 1, :]
            p = a * p
            outs.append(p * h + q)
        for j, o in zip(order, outs):
            h_dst[g, j:j + 1, :] = o
        return outs[-1]

    carry[...] = lax.fori_loop(0, lc // _SCAN_GROUP, group, carry[...], unroll=8)


def _lru_fwd_kernel(flags_ref, xa_ref, xp_ref, xn_ref, cw_ref, cb_ref, wg_ref, ba_ref, bx_ref, lam_ref,
                    hf_ref, xc_ref, xext, a_scr, u_scr, carry, *, lc, rb):
    fl = flags_ref[pl.program_id(0)]
    first = (fl & 1) != 0
    last = (fl & 2) != 0
    _lru_fill_halo(first, last, xa_ref, xp_ref, xn_ref, xext, lc)
    _lru_conv(xext, cw_ref, cb_ref, xc_ref, lc, rb)
    _lru_gates(xc_ref, wg_ref, ba_ref, bx_ref, lam_ref, a_scr, u_scr, lc, rb)
    _lru_scan(first, a_scr, u_scr, hf_ref, carry, lc, reverse=False)


def _lru_bwd_kernel(flags_ref, xc_ref, gate_ref, hf_ref, wg_ref, ba_ref, bx_ref, lam_ref, gn_ref, y_ref,
                    a_scr, u_scr, h_scr, carry, *, lc, rb, nchunks):
    fl = flags_ref[nchunks - 1 - pl.program_id(0)]
    last = (fl & 2) != 0
    _lru_gates(xc_ref, wg_ref, ba_ref, bx_ref, lam_ref, a_scr, u_scr, lc, rb)
    _lru_scan(last, a_scr, u_scr, h_scr, carry, lc, reverse=True)
    for blk in range(lc // rb):
        rows = slice(blk * rb, (blk + 1) * rb)
        groups = slice(blk * rb // _SCAN_GROUP, (blk + 1) * rb // _SCAN_GROUP)
        h = (hf_ref[groups] + h_scr[groups]).reshape(rb, LRU_WIDTH)
        y = h * jax.nn.gelu(gate_ref[rows, :])
        y_ref[rows, :] = _rms(y, gn_ref[...]).astype(BF16)


def _lru(cfg, lru_in, cw, cb, wg, ba, bx, lam, gn):
    t = lru_in.shape[0]
    lc, rb = cfg.lru_chunk, cfg.lru_rows
    nchunks = t // lc
    hb = lc // _HALO_ROWS
    nhalo = t // _HALO_ROWS
    flags = _chunk_flags(cfg, lc)
    row = lambda: _resident((1, LRU_WIDTH))

    def specs(chunk_of):
        return [
            pl.BlockSpec((lc, LRU_WIDTH), lambda i, f: (chunk_of(i), 0)),
            pl.BlockSpec((_HALO_ROWS, LRU_WIDTH), lambda i, f: (jnp.maximum(chunk_of(i) * hb - 1, 0), 0)),
            pl.BlockSpec((_HALO_ROWS, LRU_WIDTH), lambda i, f: (jnp.minimum((chunk_of(i) + 1) * hb, nhalo - 1), 0)),
        ]

    conv_weights = [_resident((CONV_WIDTH, LRU_WIDTH)), row()]

    def weights(d):
        return [
            pl.BlockSpec((None, LRU_BLOCKS, LRU_BLOCK_WIDTH, 2 * LRU_BLOCK_WIDTH), lambda i, f: (d, 0, 0, 0)),
            pl.BlockSpec((None, 1, LRU_WIDTH), lambda i, f: (d, 0, 0)),
            pl.BlockSpec((None, 1, LRU_WIDTH), lambda i, f: (d, 0, 0)),
            pl.BlockSpec((None, 1, LRU_WIDTH), lambda i, f: (d, 0, 0)),
        ]

    grouped = (lc // _SCAN_GROUP, _SCAN_GROUP, LRU_WIDTH)
    scratch = [pltpu.VMEM(grouped, F32), pltpu.VMEM(grouped, F32)]
    carry = [pltpu.VMEM((1, LRU_WIDTH), F32)]

    fwd = lambda i: i
    hf, xc = pl.pallas_call(
        functools.partial(_lru_fwd_kernel, lc=lc, rb=rb),
        grid_spec=pltpu.PrefetchScalarGridSpec(
            num_scalar_prefetch=1, grid=(nchunks,),
            in_specs=specs(fwd) + conv_weights + weights(0),
            out_specs=[pl.BlockSpec(grouped, lambda i, f: (i, 0, 0)),
                       pl.BlockSpec((lc, LRU_WIDTH), lambda i, f: (i, 0))],
            scratch_shapes=[pltpu.VMEM((lc + 2 * _HALO_ROWS, LRU_WIDTH), F32)] + scratch + carry),
        out_shape=[jax.ShapeDtypeStruct((t // _SCAN_GROUP, _SCAN_GROUP, LRU_WIDTH), F32),
                   jax.ShapeDtypeStruct((t, LRU_WIDTH), F32)],
        compiler_params=_params(("arbitrary",)),
        name="lru_forward",
    )(flags, lru_in, lru_in, lru_in, cw, cb, wg, ba, bx, lam)

    bwd = lambda i: nchunks - 1 - i
    return pl.pallas_call(
        functools.partial(_lru_bwd_kernel, lc=lc, rb=rb, nchunks=nchunks),
        grid_spec=pltpu.PrefetchScalarGridSpec(
            num_scalar_prefetch=1, grid=(nchunks,),
            in_specs=[
                pl.BlockSpec((lc, LRU_WIDTH), lambda i, f: (bwd(i), 0)),
                pl.BlockSpec((lc, LRU_WIDTH), lambda i, f: (bwd(i), 1)),
                pl.BlockSpec(grouped, lambda i, f: (bwd(i), 0, 0)),
            ] + weights(1) + [row()],
            out_specs=pl.BlockSpec((lc, LRU_WIDTH), lambda i, f: (bwd(i), 0)),
            scratch_shapes=scratch + [pltpu.VMEM(grouped, F32)] + carry),
        out_shape=jax.ShapeDtypeStruct((t, LRU_WIDTH), BF16),
        compiler_params=_params(("arbitrary",)),
        name="lru_backward",
    )(flags, xc, lru_in, hf, wg, ba, bx, lam, gn)


def _alibi_slopes(n):
    return [2.0 ** (-8.0 * (i + 1) / n) for i in range(n)]


def _band_bias(wq, halo, dist_scale, hq, hkv):
    rep = hq // hkv
    slopes = _alibi_slopes(hq)
    qi = np.arange(wq)[:, None]
    kj = np.arange(wq + 2 * halo)[None, :]
    rel = np.abs(kj - halo - qi)
    out = np.empty((hkv, rep * wq, wq + 2 * halo), np.float32)
    for g in range(hkv):
        for r in range(rep):
            out[g, r * wq:(r + 1) * wq] = np.where(rel <= halo, -slopes[g * rep + r] * dist_scale * rel, _NEG)
    return out


def _band_bias_variants(wq, halo, dist_scale, heads):
    base = _band_bias(wq, halo, dist_scale, heads, heads)
    col = np.arange(wq + 2 * halo)
    out = np.stack([base] * 4)
    out[1::2, :, :, col < halo] = _NEG
    out[2:, :, :, col >= wq + halo] = _NEG
    return out


def _edge_penalties(fl, wq, halo):
    col = lax.broadcasted_iota(jnp.int32, (1, wq + 2 * halo), 1)
    pen_first = jnp.where(col < halo, jnp.where((fl & 1) != 0, _NEG, 0.0), 0.0)
    pen_last = jnp.where(col >= wq + halo, jnp.where((fl & 2) != 0, _NEG, 0.0), 0.0)
    return pen_first, pen_last


def _attend(qs, ks, vs, biases):
    scores = [lax.dot_general(q, k, _NT, preferred_element_type=F32) for q, k in zip(qs, ks)]
    ms, ps = [], []
    for s, b in zip(scores, biases):
        if b is not None:
            s = s + b
        m = jnp.max(s, axis=-1, keepdims=True)
        ms.append(m)
        ps.append(jnp.exp(s - m).astype(BF16))
    ls, accs = [], []
    for p, v in zip(ps, vs):
        v1 = jnp.concatenate([v, jnp.ones(v.shape, v.dtype)], axis=1)
        out = jnp.dot(p, v1, preferred_element_type=F32)
        accs.append(out[:, :HEAD_DIM])
        ls.append(out[:, HEAD_DIM:])
    return ms, ls, accs


def _swa_kernel(flags_ref, q_ref, kc_ref, kp_ref, kn_ref, vc_ref, vp_ref, vn_ref, bias_ref, sink_ref, gn_ref,
                y_out, kbuf, vbuf, *, w, nsub, nb):
    rep = SWA_HEADS // SWA_KV_HEADS
    rows_total = nsub * w
    pen_first, pen_last = _edge_penalties(flags_ref[pl.program_id(0)], w, w)
    kbuf[0:w, :] = kp_ref[...]
    kbuf[w:w + rows_total, :] = kc_ref[...]
    kbuf[w + rows_total:, :] = kn_ref[...]
    vbuf[0:w, :] = vp_ref[...]
    vbuf[w:w + rows_total, :] = vc_ref[...]
    vbuf[w + rows_total:, :] = vn_ref[...]

    for j0 in range(0, nsub, nb):
        blocks = list(range(j0, min(j0 + nb, nsub)))
        qs, ks, vs, bs = [], [], [], []
        for j in blocks:
            rows = slice(j * w, (j + 1) * w)
            for g in range(SWA_KV_HEADS):
                gs = slice(g * HEAD_DIM, (g + 1) * HEAD_DIM)
                qs.append(jnp.concatenate(
                    [q_ref[rows, (g * rep + r) * HEAD_DIM:(g * rep + r + 1) * HEAD_DIM] for r in range(rep)], axis=0))
                ks.append(kbuf[j * w:(j + 3) * w, gs])
                vs.append(vbuf[j * w:(j + 3) * w, gs])
                b = bias_ref[g]
                if j == 0:
                    b = b + pen_first
                if j == nsub - 1:
                    b = b + pen_last
                bs.append(b)
        ms, ls, accs = _attend(qs, ks, vs, bs)
        for bi, j in enumerate(blocks):
            heads_out = []
            for g in range(SWA_KV_HEADS):
                idx = bi * SWA_KV_HEADS + g
                for r in range(rep):
                    h = g * rep + r
                    part = slice(r * w, (r + 1) * w)
                    m, l, acc = ms[idx][part], ls[idx][part], accs[idx][part]
                    factor = jax.nn.sigmoid(m + jnp.log(l) - sink_ref[:, h:h + 1])
                    heads_out.append((acc / l) * factor)
            y = jnp.concatenate(heads_out, axis=1)
            y_out[j * w:(j + 1) * w, :] = _rms(y, gn_ref[...]).astype(BF16)


def _swa(cfg, qkvc, gn, sink):
    t, c = qkvc.shape
    w = SWA_WINDOW
    rows = cfg.swa_rows
    nsub = rows // w
    nchunks = t // rows
    nblk = t // w
    qw, kvw = SWA_WIDTH, SWA_KV_WIDTH
    rep = SWA_HEADS // SWA_KV_HEADS
    assert qw % kvw == 0 and c == qw + 2 * kvw
    kcol, vcol = qw // kvw, qw // kvw + 1
    prev = lambda col: (lambda i, f: (jnp.maximum(i * nsub - 1, 0), col))
    nxt = lambda col: (lambda i, f: (jnp.minimum((i + 1) * nsub, nblk - 1), col))
    return pl.pallas_call(
        functools.partial(_swa_kernel, w=w, nsub=nsub, nb=cfg.swa_batch),
        grid_spec=pltpu.PrefetchScalarGridSpec(
            num_scalar_prefetch=1, grid=(nchunks,),
            in_specs=[
                pl.BlockSpec((rows, qw), lambda i, f: (i, 0)),
                pl.BlockSpec((rows, kvw), lambda i, f: (i, kcol)),
                pl.BlockSpec((w, kvw), prev(kcol)),
                pl.BlockSpec((w, kvw), nxt(kcol)),
                pl.BlockSpec((rows, kvw), lambda i, f: (i, vcol)),
                pl.BlockSpec((w, kvw), prev(vcol)),
                pl.BlockSpec((w, kvw), nxt(vcol)),
                _resident((SWA_KV_HEADS, rep * w, 3 * w)),
                _resident((1, _LANES)),
                _resident((1, qw)),
            ],
            out_specs=pl.BlockSpec((rows, qw), lambda i, f: (i, 0)),
            scratch_shapes=[pltpu.VMEM((rows + 2 * w, kvw), BF16), pltpu.VMEM((rows + 2 * w, kvw), BF16)]),
        out_shape=jax.ShapeDtypeStruct((t, qw), BF16),
        compiler_params=_params(("parallel",)),
        name="windowed_gqa_sink",
    )(_chunk_flags(cfg, rows), qkvc, qkvc, qkvc, qkvc, qkvc, qkvc, qkvc,
      jnp.asarray(_band_bias(w, w, 1, SWA_HEADS, SWA_KV_HEADS)), sink, gn)


def _perm_matrix(d):
    n = _PERM_ROWS
    per = n // d
    p = np.zeros((n, n), np.float32)
    for r in range(d):
        for m in range(per):
            p[r * per + m, d * m + r] = 1.0
    return p


def _dilated_qrows(c, d):
    return min(_DIL_QROWS_MAX, c // d)


def _dilated_kernel(flags_ref, q_ref, kp_ref, kc_ref, kn_ref, vp_ref, vc_ref, vn_ref, bias0, bias1, bias2,
                    perm_ref, gn_ref, y_ref, qd, kd, vd, ring_k, ring_v, acc_nat, m_nat, l_nat, *,
                    c, batch_rows, tail):
    w = _DIL_W
    nh = DIL_HEADS
    step = pl.program_id(0)
    fl = flags_ref[step]
    slot_prev, slot_cur, slot_next = lax.rem(step + 2, 3), lax.rem(step, 3), lax.rem(step + 1, 3)
    bias_refs = (bias0, bias1, bias2)
    starts_sequence = fl & 1
    ends_sequence = (fl >> 1) & 1
    heads = [slice(h * HEAD_DIM, (h + 1) * HEAD_DIM) for h in range(nh)]

    def run_pattern(p, d, qsrc, ksrc, vsrc, qstride, kstride, first_pattern, last_pattern):
        wq = _dilated_qrows(c, d)
        win = wq + 2 * w
        nblk = c // (d * wq)
        nb = max(1, batch_rows // wq)
        assert (c // wq) % nb == 0
        assert d == 1 or not last_pattern

        def body(it, carry):
            qs, ks, vs, bs, where = [], [], [], [], []
            for b in range(nb):
                sb = it * nb + b
                r = sb // nblk
                s = sb % nblk
                var = jnp.where(s == 0, starts_sequence, 0) + 2 * jnp.where(s == nblk - 1, ends_sequence, 0)
                q0 = pl.multiple_of(r * qstride + s * wq, w)
                k0 = pl.multiple_of(r * kstride + s * wq, w)
                where.append(d * wq * s + r)
                for h in range(nh):
                    qs.append(qsrc[pl.ds(q0, wq), heads[h]])
                    if ksrc is None:
                        ks.append(jnp.concatenate([ring_k[sl, pl.ds(q0, wq), heads[h]]
                                                   for sl in (slot_prev, slot_cur, slot_next)], axis=0))
                        vs.append(jnp.concatenate([ring_v[sl, pl.ds(q0, wq), heads[h]]
                                                   for sl in (slot_prev, slot_cur, slot_next)], axis=0))
                    else:
                        ks.append(ksrc[pl.ds(k0, win), heads[h]])
                        vs.append(vsrc[pl.ds(k0, win), heads[h]])
                    bs.append(bias_refs[p][var, h])
            ms, ls, accs = _attend(qs, ks, vs, bs)
            for b in range(nb):
                nat = pl.ds(where[b], wq, stride=d) if d > 1 else pl.ds(pl.multiple_of(where[b], w), wq)
                outs = []
                for h in range(nh):
                    m, l, acc = ms[b * nh + h], ls[b * nh + h], accs[b * nh + h]
                    if not first_pattern:
                        m_p = m_nat[h, nat, :]
                        m_n = jnp.maximum(m_p, m)
                        alpha = jnp.exp(m_p - m_n)
                        beta = jnp.exp(m - m_n)
                        l = alpha * l_nat[h, nat, :] + beta * l
                        acc = alpha * acc_nat[h, nat, :] + beta * acc
                        m = m_n
                    if last_pattern:
                        outs.append(acc / l)
                    else:
                        acc_nat[h, nat, :] = acc
                        m_nat[h, nat, :] = jnp.broadcast_to(m, (wq, HEAD_DIM))
                        l_nat[h, nat, :] = jnp.broadcast_to(l, (wq, HEAD_DIM))
                if last_pattern:
                    y_ref[nat, :] = _rms(jnp.concatenate(outs, axis=1), gn_ref[...]).astype(BF16)
            return carry

        lax.fori_loop(0, c // (wq * nb), body, 0)

    def deinterleave(p, d):
        per = _PERM_ROWS // d
        halo_groups = (d * w) // _PERM_ROWS
        chunk_groups = c // _PERM_ROWS
        kstride = (chunk_groups + 2 * halo_groups) * per
        perm = perm_ref[p - 1]

        def move(src, row0, dst, g, stride):
            res = jnp.dot(perm, src[row0:row0 + _PERM_ROWS, :], preferred_element_type=F32).astype(BF16)
            for r in range(d):
                dst[r * stride + g * per:r * stride + (g + 1) * per, :] = res[r * per:(r + 1) * per]

        for g in range(chunk_groups):
            move(q_ref, g * _PERM_ROWS, qd, g, c // d)
        if d * w == c:
            def fill(slot, k_src, v_src):
                for src, ring in ((k_src, ring_k), (v_src, ring_v)):
                    for g in range(chunk_groups):
                        move(src, g * _PERM_ROWS, ring.at[slot], g, c // d)

            @pl.when(step == 0)
            def _():
                ring_k[2] = jnp.zeros((c, DIL_WIDTH), BF16)
                ring_v[2] = jnp.zeros((c, DIL_WIDTH), BF16)
                fill(0, kc_ref, vc_ref)

            fill(slot_next, kn_ref, vn_ref)
            return c // d, None
        for prev_ref, cur_ref, next_ref, dst in ((kp_ref, kc_ref, kn_ref, kd), (vp_ref, vc_ref, vn_ref, vd)):
            srcs = ([(prev_ref, tail - (halo_groups - g) * _PERM_ROWS) for g in range(halo_groups)]
                    + [(cur_ref, g * _PERM_ROWS) for g in range(chunk_groups)]
                    + [(next_ref, g * _PERM_ROWS) for g in range(halo_groups)])
            for g, (src, row0) in enumerate(srcs):
                move(src, row0, dst, g, kstride)
        return c // d, kstride

    order = sorted(range(len(DIL_PATTERNS)), key=lambda p: -DIL_PATTERNS[p][1])
    for idx, p in enumerate(order):
        d = DIL_PATTERNS[p][1]
        first, last = idx == 0, idx == len(order) - 1
        if d == 1:
            for prev_ref, cur_ref, next_ref, dst in ((kp_ref, kc_ref, kn_ref, kd), (vp_ref, vc_ref, vn_ref, vd)):
                dst[0:w, :] = prev_ref[tail - w:tail, :]
                dst[w:w + c, :] = cur_ref[...]
                dst[w + c:2 * w + c, :] = next_ref[0:w, :]
            run_pattern(p, d, q_ref, kd, vd, 0, 0, first, last)
        else:
            qstride, kstride = deinterleave(p, d)
            if kstride is None:
                run_pattern(p, d, qd, None, None, qstride, 0, first, last)
            else:
                run_pattern(p, d, qd, kd, vd, qstride, kstride, first, last)


def _dilated(cfg, qkvb, gn):
    t = qkvb.shape[0]
    c = cfg.dil_rows
    w = _DIL_W
    n = t // c
    dmax = max(d for _, d in DIL_PATTERNS)
    assert DIL_PATTERNS[0][1] == 1 and c % (dmax * w) == 0 and c >= dmax * w and c % _PERM_ROWS == 0
    assert all((d * w) % _PERM_ROWS == 0 for _, d in DIL_PATTERNS[1:])
    assert len(DIL_PATTERNS) == 3
    biases = [_band_bias_variants(_dilated_qrows(c, d), w, d, DIL_HEADS) for _, d in DIL_PATTERNS]
    perm = np.stack([_perm_matrix(d) for _, d in DIL_PATTERNS[1:]])
    blk = (c, DIL_WIDTH)
    assert dmax * w == c
    local = [d for _, d in DIL_PATTERNS if d * w < c]
    tail = max(d * w for d in local)
    assert c % tail == 0 and tail % _PERM_ROWS == 0
    class_rows = max((c // _PERM_ROWS + 2 * (d * w // _PERM_ROWS)) * _PERM_ROWS if d > 1 else c + 2 * w for d in local)
    prev = lambda col: (lambda i, f: (jnp.maximum(i * (c // tail) - 1, 0), col))
    cur = lambda col: (lambda i, f: (i, col))
    nxt = lambda col: (lambda i, f: (jnp.minimum(i + 1, n - 1), col))
    tail_blk = (tail, DIL_WIDTH)
    return pl.pallas_call(
        functools.partial(_dilated_kernel, c=c, batch_rows=cfg.dil_batch * w, tail=tail),
        grid_spec=pltpu.PrefetchScalarGridSpec(
            num_scalar_prefetch=1, grid=(n,),
            in_specs=[
                pl.BlockSpec(blk, cur(0)),
                pl.BlockSpec(tail_blk, prev(1)), pl.BlockSpec(blk, cur(1)), pl.BlockSpec(blk, nxt(1)),
                pl.BlockSpec(tail_blk, prev(2)), pl.BlockSpec(blk, cur(2)), pl.BlockSpec(blk, nxt(2)),
                _resident(biases[0].shape), _resident(biases[1].shape), _resident(biases[2].shape),
                _resident(perm.shape), _resident((1, DIL_WIDTH)),
            ],
            out_specs=pl.BlockSpec(blk, cur(0)),
            scratch_shapes=[
                pltpu.VMEM((c, DIL_WIDTH), BF16),
                pltpu.VMEM((class_rows, DIL_WIDTH), BF16), pltpu.VMEM((class_rows, DIL_WIDTH), BF16),
                pltpu.VMEM((3, c, DIL_WIDTH), BF16), pltpu.VMEM((3, c, DIL_WIDTH), BF16),
                pltpu.VMEM((DIL_HEADS, c, HEAD_DIM), F32),
                pltpu.VMEM((DIL_HEADS, c, HEAD_DIM), F32),
                pltpu.VMEM((DIL_HEADS, c, HEAD_DIM), F32),
            ]),
        out_shape=jax.ShapeDtypeStruct((t, DIL_WIDTH), BF16),
        compiler_params=_params(("arbitrary",)),
        name="dilated_attention",
    )(_chunk_flags(cfg, c), qkvb, qkvb, qkvb, qkvb, qkvb, qkvb, qkvb,
      jnp.asarray(biases[0]), jnp.asarray(biases[1]), jnp.asarray(biases[2]), jnp.asarray(perm, dtype=BF16), gn)


def _outproj_kernel(ya_ref, yb_ref, yc_ref, x_ref, w_ref, g_ref, o_ref, *, nchunk):
    d = o_ref.shape[1]
    b0, b1 = LRU_WIDTH, LRU_WIDTH + DIL_WIDTH
    for c in range(0, d, nchunk):
        cs = slice(c, c + nchunk)
        acc = jnp.dot(ya_ref[...], w_ref[0:b0, cs], preferred_element_type=F32)
        acc += jnp.dot(yb_ref[...], w_ref[b0:b1, cs], preferred_element_type=F32)
        acc += jnp.dot(yc_ref[...], w_ref[b1:, cs], preferred_element_type=F32)
        o_ref[:, cs] = acc
    for rows in _row_blocks(o_ref.shape[0]):
        o_ref[rows, :] = x_ref[rows, :] + _rms(o_ref[rows, :], g_ref[...])


def _outproj(cfg, ya, yb, yc, x, w, g, layer):
    t, d = x.shape
    tm = cfg.tm_out
    return pl.pallas_call(
        functools.partial(_outproj_kernel, nchunk=min(_COL_CHUNK, d)),
        grid=(t // tm,),
        in_specs=[
            pl.BlockSpec((tm, LRU_WIDTH), lambda i: (i, 0)),
            pl.BlockSpec((tm, DIL_WIDTH), lambda i: (i, 0)),
            pl.BlockSpec((tm, SWA_WIDTH), lambda i: (i, 0)),
            pl.BlockSpec((tm, d), lambda i: (i, 0)),
            _resident((MIX_WIDTH, d), layer),
            _resident((1, d)),
        ],
        out_specs=pl.BlockSpec((tm, d), lambda i: (i, 0)),
        out_shape=jax.ShapeDtypeStruct((t, d), F32),
        compiler_params=_params(("parallel",)),
        name="mixer_outproj",
    )(ya, yb, yc, x, w, g)


def _memkv_kernel(mem_ref, g_ref, w_ref, k_ref, v_ref):
    mn = _rms(mem_ref[...], g_ref[...]).astype(BF16)
    kv = jnp.dot(mn, w_ref[...], preferred_element_type=F32)
    k_ref[...] = kv[:, :MEM_WIDTH].astype(BF16)
    v_ref[...] = kv[:, MEM_WIDTH:].astype(BF16)


def _memkv(cfg, mem, g, wkv, layer):
    rows, d = mem.shape
    nm = cfg.n_mem
    shape = jax.ShapeDtypeStruct((rows, MEM_WIDTH), BF16)
    return pl.pallas_call(
        _memkv_kernel,
        grid=(rows // nm,),
        in_specs=[pl.BlockSpec((nm, d), lambda i: (i, 0)), _resident((1, d)),
                  _resident((d, 2 * MEM_WIDTH), layer)],
        out_specs=[pl.BlockSpec((nm, MEM_WIDTH), lambda i: (i, 0))] * 2,
        out_shape=[shape, shape],
        compiler_params=_params(("parallel",)),
        name="memory_kv",
    )(mem, g, wkv)


def _cross_kernel(seq_ref, x_ref, gpre_ref, wq_ref, k_ref, v_ref, wo_ref, gpost_ref, o_ref, xn_scr, o_scr, *,
                  nchunk):
    del seq_ref
    tm, d = x_ref.shape
    for rows in _row_blocks(tm):
        xn_scr[rows, :] = _rms(x_ref[rows, :], gpre_ref[...]).astype(BF16)
    q = (jnp.dot(xn_scr[...], wq_ref[...], preferred_element_type=F32) * _QK_SCALE).astype(BF16)
    heads = [slice(h * HEAD_DIM, (h + 1) * HEAD_DIM) for h in range(MEM_HEADS)]
    _, ls, accs = _attend([q[:, hs] for hs in heads], [k_ref[:, hs] for hs in heads],
                          [v_ref[:, hs] for hs in heads], [None] * MEM_HEADS)
    for hs, l, acc in zip(heads, ls, accs):
        o_scr[:, hs] = (acc / l).astype(BF16)
    for c in range(0, d, nchunk):
        cs = slice(c, min(c + nchunk, d))
        o_ref[:, cs] = jnp.dot(o_scr[...], wo_ref[:, cs], preferred_element_type=F32)
    for rows in _row_blocks(tm):
        o_ref[rows, :] = x_ref[rows, :] + _rms(o_ref[rows, :], gpost_ref[...])


def _cross(cfg, x, gpre, wq, kmem, vmem, wo, gpost, layer):
    t, d = x.shape
    tm, nm = cfg.tm_cross, cfg.n_mem
    seqs, _ = _sequences(cfg)
    seq_of_tile = []
    for si, (_, length) in enumerate(seqs):
        assert length % tm == 0
        seq_of_tile += [si] * (length // tm)
    seq_of_tile = jnp.asarray(np.asarray(seq_of_tile, np.int32))
    return pl.pallas_call(
        functools.partial(_cross_kernel, nchunk=_COL_CHUNK),
        grid_spec=pltpu.PrefetchScalarGridSpec(
            num_scalar_prefetch=1, grid=(t // tm,),
            in_specs=[
                pl.BlockSpec((tm, d), lambda i, s: (i, 0)),
                _resident((1, d)),
                _resident((d, MEM_WIDTH), layer),
                pl.BlockSpec((nm, MEM_WIDTH), lambda i, s: (s[i], 0)),
                pl.BlockSpec((nm, MEM_WIDTH), lambda i, s: (s[i], 0)),
                _resident((MEM_WIDTH, d), layer),
                _resident((1, d)),
            ],
            out_specs=pl.BlockSpec((tm, d), lambda i, s: (i, 0)),
            scratch_shapes=[pltpu.VMEM((tm, d), BF16), pltpu.VMEM((tm, MEM_WIDTH), BF16)]),
        out_shape=jax.ShapeDtypeStruct((t, d), F32),
        compiler_params=_params(("parallel",)),
        name="memory_cross_attention",
    )(seq_of_tile, x, gpre, wq, kmem, vmem, wo, gpost)


def _ffn_kernel(x_ref, gpre_ref, w1_ref, w2_ref, gpost_ref, o_ref, xn_scr, h_scr, *, nchunk):
    f = pl.program_id(1)
    tf = w1_ref.shape[1]
    d = w2_ref.shape[1]

    tm = x_ref.shape[0]
    row_blocks = [slice(r, min(r + _NORM_ROWS, tm)) for r in range(0, tm, _NORM_ROWS)]

    @pl.when(f == 0)
    def _():
        for rows in row_blocks:
            xn_scr[rows, :] = _rms(x_ref[rows, :], gpre_ref[...]).astype(BF16)
            o_ref[rows, :] = jnp.zeros((rows.stop - rows.start, d), F32)

    for c in range(0, tf, nchunk):
        cs = slice(c, min(c + nchunk, tf))
        h = jnp.dot(xn_scr[...], w1_ref[:, cs], preferred_element_type=F32)
        h_scr[:, cs] = jnp.square(jnp.maximum(h, 0.0)).astype(BF16)
    for c in range(0, d, nchunk):
        cs = slice(c, min(c + nchunk, d))
        o_ref[:, cs] += jnp.dot(h_scr[...], w2_ref[:, cs], preferred_element_type=F32)

    @pl.when(f == pl.num_programs(1) - 1)
    def _():
        for rows in row_blocks:
            o_ref[rows, :] = x_ref[rows, :] + _rms(o_ref[rows, :], gpost_ref[...])


def _ffn(cfg, x, gpre, w1, w2, gpost, layer):
    t, d = x.shape
    tm, tf = cfg.tm_ffn, cfg.tf
    dff = w1.shape[2]
    return pl.pallas_call(
        functools.partial(_ffn_kernel, nchunk=_COL_CHUNK),
        grid=(t // tm, dff // tf),
        in_specs=[
            pl.BlockSpec((tm, d), lambda i, f: (i, 0)),
            _resident((1, d)),
            pl.BlockSpec((None, d, tf), lambda i, f: (layer, 0, f)),
            pl.BlockSpec((None, tf, d), lambda i, f: (layer, f, 0)),
            _resident((1, d)),
        ],
        out_specs=pl.BlockSpec((tm, d), lambda i, f: (i, 0)),
        out_shape=jax.ShapeDtypeStruct((t, d), F32),
        scratch_shapes=[pltpu.VMEM((tm, d), BF16), pltpu.VMEM((tm, tf), BF16)],
        compiler_params=_params(("parallel", "arbitrary")),
        name="squared_relu_mlp",
    )(x, gpre, w1, w2, gpost)


def _forward(cfg, x, mem, p):
    row = lambda a: a.reshape(1, -1).astype(F32)
    for l in range(cfg.depth):
        lru_in, qkvb, qkvc = _inproj(cfg, x, row(p["mix_norm_pre"][l]), p["w_in"], l)
        gn = p["group_norm"][l]
        wg = jnp.concatenate([p["lru_wa"][l], p["lru_wx"][l]], axis=-1).astype(BF16)
        ya = _lru(cfg, lru_in, p["conv_w"][l], row(p["conv_b"][l]), wg,
                  p["lru_ba"][l][:, None, :], p["lru_bx"][l][:, None, :], p["lru_lam"][l][:, None, :],
                  row(gn[:LRU_WIDTH]))
        yb = _dilated(cfg, qkvb, row(gn[LRU_WIDTH:LRU_WIDTH + DIL_WIDTH]))
        sink = jnp.zeros((1, _LANES), F32).at[0, :SWA_HEADS].set(p["swa_sink"][l].astype(F32))
        yc = _swa(cfg, qkvc, row(gn[LRU_WIDTH + DIL_WIDTH:]), sink)
        x = _outproj(cfg, ya, yb, yc, x, p["w_out"], row(p["mix_norm_post"][l]), l)
        kmem, vmem = _memkv(cfg, mem, row(p["mem_kv_norm"][l]), p["w_mkv"], l)
        x = _cross(cfg, x, row(p["mem_norm_pre"][l]), p["w_mq"], kmem, vmem, p["w_mo"],
                   row(p["mem_norm_post"][l]), l)
        x = _ffn(cfg, x, row(p["ffn_norm_pre"][l]), p["w_ff1"], p["w_ff2"], row(p["ffn_norm_post"][l]), l)
    return x


def _run(cfg, x_prompt, x_sample, mem_prompt, mem_sample, mix_norm_pre, mix_norm_post, w_in, conv_w, conv_b,
         lru_wa, lru_ba, lru_wx, lru_bx, lru_lam, swa_sink, group_norm, w_out, mem_norm_pre, mem_norm_post,
         mem_kv_norm, w_mq, w_mk, w_mv, w_mo, ffn_norm_pre, ffn_norm_post, w_ff1, w_ff2):
    d = cfg.d_model
    p = dict(
        mix_norm_pre=mix_norm_pre, mix_norm_post=mix_norm_post, w_in=w_in.astype(BF16), conv_w=conv_w,
        conv_b=conv_b, lru_wa=lru_wa, lru_ba=lru_ba, lru_wx=lru_wx, lru_bx=lru_bx, lru_lam=lru_lam,
        swa_sink=swa_sink, group_norm=group_norm, w_out=w_out.astype(BF16), mem_norm_pre=mem_norm_pre,
        mem_norm_post=mem_norm_post, mem_kv_norm=mem_kv_norm, w_mq=w_mq.astype(BF16),
        w_mkv=jnp.concatenate([w_mk, w_mv], axis=-1).astype(BF16), w_mo=w_mo.astype(BF16),
        ffn_norm_pre=ffn_norm_pre, ffn_norm_post=ffn_norm_post, w_ff1=w_ff1.astype(BF16),
        w_ff2=w_ff2.astype(BF16))
    outs = []
    for group, x, mem in zip(cfg.groups, (x_prompt, x_sample), (mem_prompt, mem_sample)):
        sub = cfg._replace(groups=(group,))
        outs.append(_forward(sub, x.reshape(-1, d), mem.reshape(-1, d), p).reshape(x.shape))
    return tuple(outs)


def kernel(x_prompt, x_sample, mem_prompt, mem_sample, mix_norm_pre, mix_norm_post, w_in, conv_w, conv_b, lru_wa,
           lru_ba, lru_wx, lru_bx, lru_lam, swa_sink, group_norm, w_out, mem_norm_pre, mem_norm_post, mem_kv_norm,
           w_mq, w_mk, w_mv, w_mo, ffn_norm_pre, ffn_norm_post, w_ff1, w_ff2):
    return _run(_CFG, x_prompt, x_sample, mem_prompt, mem_sample, mix_norm_pre, mix_norm_post, w_in, conv_w,
                conv_b, lru_wa, lru_ba, lru_wx, lru_bx, lru_lam, swa_sink, group_norm, w_out, mem_norm_pre,
                mem_norm_post, mem_kv_norm, w_mq, w_mk, w_mv, w_mo, ffn_norm_pre, ffn_norm_post, w_ff1, w_ff2)
```

```python
import functools
from typing import NamedTuple

import numpy as np
import jax
import jax.numpy as jnp
from jax import lax
from jax.experimental import pallas as pl
from jax.experimental.pallas import tpu as pltpu

F32 = jnp.float32
BF16 = jnp.bfloat16

D_MODEL = 2048
BATCH = 8
SEQ = 4096
DEPTH = 4
DEC_BATCH = 1
DEC_SEQ = 16384
HEAD_DIM = 128
LRU_WIDTH = 512
LRU_BLOCKS = 4
LRU_BLOCK_WIDTH = LRU_WIDTH // LRU_BLOCKS
CONV_WIDTH = 4
CONV_LEFT = 2
LRU_C = 8.0
DIL_HEADS = 6
DIL_PATTERNS = ((128, 1), (512, 4), (2048, 16))
SWA_HEADS = 6
SWA_KV_HEADS = 2
SWA_WINDOW = 128
DIL_WIDTH = DIL_HEADS * HEAD_DIM
SWA_WIDTH = SWA_HEADS * HEAD_DIM
SWA_KV_WIDTH = SWA_KV_HEADS * HEAD_DIM
MIX_WIDTH = LRU_WIDTH + DIL_WIDTH + SWA_WIDTH
IN_WIDTH = 2 * LRU_WIDTH + 3 * DIL_WIDTH + SWA_WIDTH + 2 * SWA_KV_WIDTH
N_MEM = 256
MEM_HEADS = 4
MEM_WIDTH = MEM_HEADS * HEAD_DIM
D_FF = 4 * D_MODEL
EPS = 1e-6

_NEG = -1e30
_QK_SCALE = HEAD_DIM ** -0.5
_HALO_ROWS = 8
_LANES = 128
_COL_CHUNK = 512
_NORM_ROWS = 128
_SCAN_GROUP = 8
_DIL_W = DIL_PATTERNS[0][0] // (2 * DIL_PATTERNS[0][1])
assert all(wn // (2 * d) == _DIL_W for wn, d in DIL_PATTERNS)
_DIL_QROWS_MAX = 128
_PERM_ROWS = 256
_V7X_VMEM_BYTES = 64 * 1024 * 1024
_VMEM_LIMIT = _V7X_VMEM_BYTES - 3 * 1024 * 1024
_NT = (((1,), (1,)), ((), ()))


class _Cfg(NamedTuple):
    d_model: int
    d_ff: int
    depth: int
    groups: tuple
    n_mem: int
    tm: int
    tm_out: int
    tm_cross: int
    tm_ffn: int
    tf: int
    lru_chunk: int
    lru_rows: int
    swa_rows: int
    dil_rows: int
    swa_batch: int
    dil_batch: int


_CFG = _Cfg(d_model=D_MODEL, d_ff=D_FF, depth=DEPTH, groups=((BATCH, SEQ), (DEC_BATCH, DEC_SEQ)),
            n_mem=N_MEM, tm=512, tm_out=1024, tm_cross=1024, tm_ffn=1024, tf=1024, lru_chunk=2048, lru_rows=256,
            swa_rows=2048,
            dil_rows=1024, swa_batch=2, dil_batch=4)


def _sequences(cfg):
    out, start = [], 0
    for n, length in cfg.groups:
        for _ in range(n):
            out.append((start, length))
            start += length
    return out, start


def _chunk_flags(cfg, rows):
    seqs, total = _sequences(cfg)
    starts = {s for s, _ in seqs}
    ends = {s + l for s, l in seqs}
    for s, l in seqs:
        assert l % rows == 0, (l, rows)
    n = total // rows
    flags = np.zeros((n,), np.int32)
    for c in range(n):
        flags[c] = (1 if c * rows in starts else 0) | (2 if (c + 1) * rows in ends else 0)
    return jnp.asarray(flags)


def _params(semantics):
    return pltpu.CompilerParams(dimension_semantics=semantics, vmem_limit_bytes=_VMEM_LIMIT)


def _rms(x, g):
    ms = jnp.mean(x * x, axis=-1, keepdims=True)
    return x * lax.rsqrt(ms + EPS) * g


def _row_blocks(n):
    return [slice(r, min(r + _NORM_ROWS, n)) for r in range(0, n, _NORM_ROWS)]


def _resident(shape, layer=None):
    if layer is None:
        return pl.BlockSpec(shape, lambda *_: (0,) * len(shape), pipeline_mode=pl.Buffered(1))
    return pl.BlockSpec((None,) + tuple(shape), lambda *_: (layer,) + (0,) * len(shape),
                        pipeline_mode=pl.Buffered(1))


def _inproj_plan():
    lru_w = 2 * LRU_WIDTH
    dil_w = 3 * DIL_WIDTH
    segs = [
        (0, lru_w, 0, None),
        (lru_w, lru_w + DIL_WIDTH, 1, _QK_SCALE),
        (lru_w + DIL_WIDTH, lru_w + dil_w, 1, None),
        (lru_w + dil_w, lru_w + dil_w + SWA_WIDTH, 2, _QK_SCALE),
        (lru_w + dil_w + SWA_WIDTH, IN_WIDTH, 2, None),
    ]
    base = {0: 0, 1: lru_w, 2: lru_w + dil_w}
    plan = []
    for c0, c1, oi, scale in segs:
        c = c0
        while c < c1:
            n = min(_COL_CHUNK, c1 - c)
            plan.append((c, c + n, oi, c - base[oi], scale))
            c += n
    return tuple(plan)


def _inproj_kernel(x_ref, g_ref, w_ref, lru_ref, qkvb_ref, qkvc_ref, xn_scr, *, plan):
    for rows in _row_blocks(x_ref.shape[0]):
        xn_scr[rows, :] = _rms(x_ref[rows, :], g_ref[...]).astype(BF16)
    outs = (lru_ref, qkvb_ref, qkvc_ref)
    for c0, c1, oi, o0, scale in plan:
        acc = jnp.dot(xn_scr[...], w_ref[:, c0:c1], preferred_element_type=F32)
        if scale is not None:
            acc = acc * scale
        outs[oi][:, o0:o0 + (c1 - c0)] = acc.astype(outs[oi].dtype)


def _inproj(cfg, x, g, w, layer):
    t, d = x.shape
    tm = cfg.tm
    widths = (2 * LRU_WIDTH, 3 * DIL_WIDTH, SWA_WIDTH + 2 * SWA_KV_WIDTH)
    return pl.pallas_call(
        functools.partial(_inproj_kernel, plan=_inproj_plan()),
        grid=(t // tm,),
        in_specs=[
            pl.BlockSpec((tm, d), lambda i: (i, 0)),
            _resident((1, d)),
            _resident((d, IN_WIDTH), layer),
        ],
        out_specs=[pl.BlockSpec((tm, wd), lambda i: (i, 0)) for wd in widths],
        out_shape=[
            jax.ShapeDtypeStruct((t, widths[0]), F32),
            jax.ShapeDtypeStruct((t, widths[1]), BF16),
            jax.ShapeDtypeStruct((t, widths[2]), BF16),
        ],
        scratch_shapes=[pltpu.VMEM((tm, d), BF16)],
        compiler_params=_params(("parallel",)),
        name="mixer_inproj",
    )(x, g, w)


def _lru_fill_halo(first, last, xa_ref, xp_ref, xn_ref, xext, lc):
    xext[_HALO_ROWS:_HALO_ROWS + lc, :] = xa_ref[...]

    @pl.when(first)
    def _():
        xext[0:_HALO_ROWS, :] = jnp.zeros((_HALO_ROWS, LRU_WIDTH), F32)

    @pl.when(jnp.logical_not(first))
    def _():
        xext[0:_HALO_ROWS, :] = xp_ref[...]

    @pl.when(last)
    def _():
        xext[_HALO_ROWS + lc:, :] = jnp.zeros((_HALO_ROWS, LRU_WIDTH), F32)

    @pl.when(jnp.logical_not(last))
    def _():
        xext[_HALO_ROWS + lc:, :] = xn_ref[...]


def _lru_conv(xext, cw_ref, cb_ref, xc_ref, lc, rb):
    cb = cb_ref[...]
    taps = [cw_ref[j:j + 1, :] for j in range(CONV_WIDTH)]
    for blk in range(lc // rb):
        r0 = blk * rb
        xc_ref[r0:r0 + rb, :] = cb + sum(
            taps[j] * xext[r0 + _HALO_ROWS - CONV_LEFT + j:r0 + _HALO_ROWS - CONV_LEFT + j + rb, :]
            for j in range(CONV_WIDTH))


def _lru_gates(xc_ref, wg_ref, ba_ref, bx_ref, lam_ref, a_scr, u_scr, lc, rb):
    lam = lam_ref[...]
    neg = -lam
    softplus = jnp.maximum(neg, 0.0) + jnp.log1p(jnp.exp(-jnp.abs(neg)))
    for blk in range(lc // rb):
        r0 = blk * rb
        xc = xc_ref[r0:r0 + rb, :]
        xcb = xc.astype(BF16)
        for n in range(LRU_BLOCKS):
            cs = slice(n * LRU_BLOCK_WIDTH, (n + 1) * LRU_BLOCK_WIDTH)
            g = jnp.dot(xcb[:, cs], wg_ref[n], preferred_element_type=F32)
            r = jax.nn.sigmoid(g[:, :LRU_BLOCK_WIDTH] + ba_ref[:, cs])
            ig = jax.nn.sigmoid(g[:, LRU_BLOCK_WIDTH:] + bx_ref[:, cs])
            neg_log_a = (LRU_C * r) * softplus[:, cs]
            a = jnp.exp(-neg_log_a)
            one_minus_a2 = jnp.tanh(neg_log_a) * (1.0 + a * a)
            root = jnp.where(one_minus_a2 > 0.0, one_minus_a2 * lax.rsqrt(one_minus_a2), 0.0)
            u = root * (ig * xc[:, cs])
            groups = slice(r0 // _SCAN_GROUP, (r0 + rb) // _SCAN_GROUP)
            a_scr[groups, :, cs] = a.reshape(rb // _SCAN_GROUP, _SCAN_GROUP, LRU_BLOCK_WIDTH)
            u_scr[groups, :, cs] = u.reshape(rb // _SCAN_GROUP, _SCAN_GROUP, LRU_BLOCK_WIDTH)


def _lru_scan(reset, a_scr, u_scr, h_dst, carry, lc, reverse):
    @pl.when(reset)
    def _():
        carry[...] = jnp.zeros((1, LRU_WIDTH), F32)

    def group(i, h):
        g = lc // _SCAN_GROUP - 1 - i if reverse else i
        order = [_SCAN_GROUP - 1 - j if reverse else j for j in range(_SCAN_GROUP)]
        p = a_scr[g, order[0]:order[0] + 1, :]
        q = u_scr[g, order[0]:order[0] + 1, :]
        outs = [p * h + q]
        for j in order[1:]:
            a = a_scr[g, j:j + 1, :]
            q = a * q + u_scr[g, j:j + 1, :]
            p = a * p
            outs.append(p * h + q)
        for j, o in zip(order, outs):
            h_dst[g, j:j + 1, :] = o
        return outs[-1]

    carry[...] = lax.fori_loop(0, lc // _SCAN_GROUP, group, carry[...], unroll=8)


def _lru_fwd_kernel(flags_ref, xa_ref, xp_ref, xn_ref, cw_ref, cb_ref, wg_ref, ba_ref, bx_ref, lam_ref,
                    hf_ref, xc_ref, xext, a_scr, u_scr, carry, *, lc, rb):
    fl = flags_ref[pl.program_id(0)]
    first = (fl & 1) != 0
    last = (fl & 2) != 0
    _lru_fill_halo(first, last, xa_ref, xp_ref, xn_ref, xext, lc)
    _lru_conv(xext, cw_ref, cb_ref, xc_ref, lc, rb)
    _lru_gates(xc_ref, wg_ref, ba_ref, bx_ref, lam_ref, a_scr, u_scr, lc, rb)
    _lru_scan(first, a_scr, u_scr, hf_ref, carry, lc, reverse=False)


def _lru_bwd_kernel(flags_ref, xc_ref, gate_ref, hf_ref, wg_ref, ba_ref, bx_ref, lam_ref, gn_ref, y_ref,
                    a_scr, u_scr, h_scr, carry, *, lc, rb, nchunks):
    fl = flags_ref[nchunks - 1 - pl.program_id(0)]
    last = (fl & 2) != 0
    _lru_gates(xc_ref, wg_ref, ba_ref, bx_ref, lam_ref, a_scr, u_scr, lc, rb)
    _lru_scan(last, a_scr, u_scr, h_scr, carry, lc, reverse=True)
    for blk in range(lc // rb):
        rows = slice(blk * rb, (blk + 1) * rb)
        groups = slice(blk * rb // _SCAN_GROUP, (blk + 1) * rb // _SCAN_GROUP)
        h = (hf_ref[groups] + h_scr[groups]).reshape(rb, LRU_WIDTH)
        y = h * jax.nn.gelu(gate_ref[rows, :])
        y_ref[rows, :] = _rms(y, gn_ref[...]).astype(BF16)


def _lru(cfg, lru_in, cw, cb, wg, ba, bx, lam, gn):
    t = lru_in.shape[0]
    lc, rb = cfg.lru_chunk, cfg.lru_rows
    nchunks = t // lc
    hb = lc // _HALO_ROWS
    nhalo = t // _HALO_ROWS
    flags = _chunk_flags(cfg, lc)
    row = lambda: _resident((1, LRU_WIDTH))

    def specs(chunk_of):
        return [
            pl.BlockSpec((lc, LRU_WIDTH), lambda i, f: (chunk_of(i), 0)),
            pl.BlockSpec((_HALO_ROWS, LRU_WIDTH), lambda i, f: (jnp.maximum(chunk_of(i) * hb - 1, 0), 0)),
            pl.BlockSpec((_HALO_ROWS, LRU_WIDTH), lambda i, f: (jnp.minimum((chunk_of(i) + 1) * hb, nhalo - 1), 0)),
        ]

    conv_weights = [_resident((CONV_WIDTH, LRU_WIDTH)), row()]

    def weights(d):
        return [
            pl.BlockSpec((None, LRU_BLOCKS, LRU_BLOCK_WIDTH, 2 * LRU_BLOCK_WIDTH), lambda i, f: (d, 0, 0, 0)),
            pl.BlockSpec((None, 1, LRU_WIDTH), lambda i, f: (d, 0, 0)),
            pl.BlockSpec((None, 1, LRU_WIDTH), lambda i, f: (d, 0, 0)),
            pl.BlockSpec((None, 1, LRU_WIDTH), lambda i, f: (d, 0, 0)),
        ]

    grouped = (lc // _SCAN_GROUP, _SCAN_GROUP, LRU_WIDTH)
    scratch = [pltpu.VMEM(grouped, F32), pltpu.VMEM(grouped, F32)]
    carry = [pltpu.VMEM((1, LRU_WIDTH), F32)]

    fwd = lambda i: i
    hf, xc = pl.pallas_call(
        functools.partial(_lru_fwd_kernel, lc=lc, rb=rb),
        grid_spec=pltpu.PrefetchScalarGridSpec(
            num_scalar_prefetch=1, grid=(nchunks,),
            in_specs=specs(fwd) + conv_weights + weights(0),
            out_specs=[pl.BlockSpec(grouped, lambda i, f: (i, 0, 0)),
                       pl.BlockSpec((lc, LRU_WIDTH), lambda i, f: (i, 0))],
            scratch_shapes=[pltpu.VMEM((lc + 2 * _HALO_ROWS, LRU_WIDTH), F32)] + scratch + carry),
        out_shape=[jax.ShapeDtypeStruct((t // _SCAN_GROUP, _SCAN_GROUP, LRU_WIDTH), F32),
                   jax.ShapeDtypeStruct((t, LRU_WIDTH), F32)],
        compiler_params=_params(("arbitrary",)),
        name="lru_forward",
    )(flags, lru_in, lru_in, lru_in, cw, cb, wg, ba, bx, lam)

    bwd = lambda i: nchunks - 1 - i
    return pl.pallas_call(
        functools.partial(_lru_bwd_kernel, lc=lc, rb=rb, nchunks=nchunks),
        grid_spec=pltpu.PrefetchScalarGridSpec(
            num_scalar_prefetch=1, grid=(nchunks,),
            in_specs=[
                pl.BlockSpec((lc, LRU_WIDTH), lambda i, f: (bwd(i), 0)),
                pl.BlockSpec((lc, LRU_WIDTH), lambda i, f: (bwd(i), 1)),
                pl.BlockSpec(grouped, lambda i, f: (bwd(i), 0, 0)),
            ] + weights(1) + [row()],
            out_specs=pl.BlockSpec((lc, LRU_WIDTH), lambda i, f: (bwd(i), 0)),
            scratch_shapes=scratch + [pltpu.VMEM(grouped, F32)] + carry),
        out_shape=jax.ShapeDtypeStruct((t, LRU_WIDTH), BF16),
        compiler_params=_params(("arbitrary",)),
        name="lru_backward",
    )(flags, xc, lru_in, hf, wg, ba, bx, lam, gn)


def _alibi_slopes(n):
    return [2.0 ** (-8.0 * (i + 1) / n) for i in range(n)]


def _band_bias(wq, halo, dist_scale, hq, hkv):
    rep = hq // hkv
    slopes = _alibi_slopes(hq)
    qi = np.arange(wq)[:, None]
    kj = np.arange(wq + 2 * halo)[None, :]
    rel = np.abs(kj - halo - qi)
    out = np.empty((hkv, rep * wq, wq + 2 * halo), np.float32)
    for g in range(hkv):
        for r in range(rep):
            out[g, r * wq:(r + 1) * wq] = np.where(rel <= halo, -slopes[g * rep + r] * dist_scale * rel, _NEG)
    return out


def _band_bias_variants(wq, halo, dist_scale, heads):
    base = _band_bias(wq, halo, dist_scale, heads, heads)
    col = np.arange(wq + 2 * halo)
    out = np.stack([base] * 4)
    out[1::2, :, :, col < halo] = _NEG
    out[2:, :, :, col >= wq + halo] = _NEG
    return out


def _edge_penalties(fl, wq, halo):
    col = lax.broadcasted_iota(jnp.int32, (1, wq + 2 * halo), 1)
    pen_first = jnp.where(col < halo, jnp.where((fl & 1) != 0, _NEG, 0.0), 0.0)
    pen_last = jnp.where(col >= wq + halo, jnp.where((fl & 2) != 0, _NEG, 0.0), 0.0)
    return pen_first, pen_last


def _attend(qs, ks, vs, biases):
    scores = [lax.dot_general(q, k, _NT, preferred_element_type=F32) for q, k in zip(qs, ks)]
    ms, ps = [], []
    for s, b in zip(scores, biases):
        if b is not None:
            s = s + b
        m = jnp.max(s, axis=-1, keepdims=True)
        ms.append(m)
        ps.append(jnp.exp(s - m).astype(BF16))
    ls, accs = [], []
    for p, v in zip(ps, vs):
        v1 = jnp.concatenate([v, jnp.ones(v.shape, v.dtype)], axis=1)
        out = jnp.dot(p, v1, preferred_element_type=F32)
        accs.append(out[:, :HEAD_DIM])
        ls.append(out[:, HEAD_DIM:])
    return ms, ls, accs


def _swa_kernel(flags_ref, q_ref, kc_ref, kp_ref, kn_ref, vc_ref, vp_ref, vn_ref, bias_ref, sink_ref, gn_ref,
                y_out, kbuf, vbuf, *, w, nsub, nb):
    rep = SWA_HEADS // SWA_KV_HEADS
    rows_total = nsub * w
    pen_first, pen_last = _edge_penalties(flags_ref[pl.program_id(0)], w, w)
    kbuf[0:w, :] = kp_ref[...]
    kbuf[w:w + rows_total, :] = kc_ref[...]
    kbuf[w + rows_total:, :] = kn_ref[...]
    vbuf[0:w, :] = vp_ref[...]
    vbuf[w:w + rows_total, :] = vc_ref[...]
    vbuf[w + rows_total:, :] = vn_ref[...]

    for j0 in range(0, nsub, nb):
        blocks = list(range(j0, min(j0 + nb, nsub)))
        qs, ks, vs, bs = [], [], [], []
        for j in blocks:
            rows = slice(j * w, (j + 1) * w)
            for g in range(SWA_KV_HEADS):
                gs = slice(g * HEAD_DIM, (g + 1) * HEAD_DIM)
                qs.append(jnp.concatenate(
                    [q_ref[rows, (g * rep + r) * HEAD_DIM:(g * rep + r + 1) * HEAD_DIM] for r in range(rep)], axis=0))
                ks.append(kbuf[j * w:(j + 3) * w, gs])
                vs.append(vbuf[j * w:(j + 3) * w, gs])
                b = bias_ref[g]
                if j == 0:
                    b = b + pen_first
                if j == nsub - 1:
                    b = b + pen_last
                bs.append(b)
        ms, ls, accs = _attend(qs, ks, vs, bs)
        for bi, j in enumerate(blocks):
            heads_out = []
            for g in range(SWA_KV_HEADS):
                idx = bi * SWA_KV_HEADS + g
                for r in range(rep):
                    h = g * rep + r
                    part = slice(r * w, (r + 1) * w)
                    m, l, acc = ms[idx][part], ls[idx][part], accs[idx][part]
                    factor = jax.nn.sigmoid(m + jnp.log(l) - sink_ref[:, h:h + 1])
                    heads_out.append((acc / l) * factor)
            y = jnp.concatenate(heads_out, axis=1)
            y_out[j * w:(j + 1) * w, :] = _rms(y, gn_ref[...]).astype(BF16)


def _swa(cfg, qkvc, gn, sink):
    t, c = qkvc.shape
    w = SWA_WINDOW
    rows = cfg.swa_rows
    nsub = rows // w
    nchunks = t // rows
    nblk = t // w
    qw, kvw = SWA_WIDTH, SWA_KV_WIDTH
    rep = SWA_HEADS // SWA_KV_HEADS
    assert qw % kvw == 0 and c == qw + 2 * kvw
    kcol, vcol = qw // kvw, qw // kvw + 1
    prev = lambda col: (lambda i, f: (jnp.maximum(i * nsub - 1, 0), col))
    nxt = lambda col: (lambda i, f: (jnp.minimum((i + 1) * nsub, nblk - 1), col))
    return pl.pallas_call(
        functools.partial(_swa_kernel, w=w, nsub=nsub, nb=cfg.swa_batch),
        grid_spec=pltpu.PrefetchScalarGridSpec(
            num_scalar_prefetch=1, grid=(nchunks,),
            in_specs=[
                pl.BlockSpec((rows, qw), lambda i, f: (i, 0)),
                pl.BlockSpec((rows, kvw), lambda i, f: (i, kcol)),
                pl.BlockSpec((w, kvw), prev(kcol)),
                pl.BlockSpec((w, kvw), nxt(kcol)),
                pl.BlockSpec((rows, kvw), lambda i, f: (i, vcol)),
                pl.BlockSpec((w, kvw), prev(vcol)),
                pl.BlockSpec((w, kvw), nxt(vcol)),
                _resident((SWA_KV_HEADS, rep * w, 3 * w)),
                _resident((1, _LANES)),
                _resident((1, qw)),
            ],
            out_specs=pl.BlockSpec((rows, qw), lambda i, f: (i, 0)),
            scratch_shapes=[pltpu.VMEM((rows + 2 * w, kvw), BF16), pltpu.VMEM((rows + 2 * w, kvw), BF16)]),
        out_shape=jax.ShapeDtypeStruct((t, qw), BF16),
        compiler_params=_params(("parallel",)),
        name="windowed_gqa_sink",
    )(_chunk_flags(cfg, rows), qkvc, qkvc, qkvc, qkvc, qkvc, qkvc, qkvc,
      jnp.asarray(_band_bias(w, w, 1, SWA_HEADS, SWA_KV_HEADS)), sink, gn)


def _perm_matrix(d):
    n = _PERM_ROWS
    per = n // d
    p = np.zeros((n, n), np.float32)
    for r in range(d):
        for m in range(per):
            p[r * per + m, d * m + r] = 1.0
    return p


def _dilated_qrows(c, d):
    return min(_DIL_QROWS_MAX, c // d)


def _dilated_kernel(flags_ref, q_ref, kp_ref, kc_ref, kn_ref, vp_ref, vc_ref, vn_ref, bias0, bias1, bias2,
                    perm_ref, gn_ref, y_ref, qd, kd, vd, ring_k, ring_v, acc_nat, m_nat, l_nat, *,
                    c, batch_rows, tail):
    w = _DIL_W
    nh = DIL_HEADS
    step = pl.program_id(0)
    fl = flags_ref[step]
    slot_prev, slot_cur, slot_next = lax.rem(step + 2, 3), lax.rem(step, 3), lax.rem(step + 1, 3)
    bias_refs = (bias0, bias1, bias2)
    starts_sequence = fl & 1
    ends_sequence = (fl >> 1) & 1
    heads = [slice(h * HEAD_DIM, (h + 1) * HEAD_DIM) for h in range(nh)]

    def run_pattern(p, d, qsrc, ksrc, vsrc, qstride, kstride, first_pattern, last_pattern):
        wq = _dilated_qrows(c, d)
        win = wq + 2 * w
        nblk = c // (d * wq)
        nb = max(1, batch_rows // wq)
        assert (c // wq) % nb == 0
        assert d == 1 or not last_pattern

        def body(it, carry):
            qs, ks, vs, bs, where = [], [], [], [], []
            for b in range(nb):
                sb = it * nb + b
                r = sb // nblk
                s = sb % nblk
                var = jnp.where(s == 0, starts_sequence, 0) + 2 * jnp.where(s == nblk - 1, ends_sequence, 0)
                q0 = pl.multiple_of(r * qstride + s * wq, w)
                k0 = pl.multiple_of(r * kstride + s * wq, w)
                where.append(d * wq * s + r)
                for h in range(nh):
                    qs.append(qsrc[pl.ds(q0, wq), heads[h]])
                    if ksrc is None:
                        ks.append(jnp.concatenate([ring_k[sl, pl.ds(q0, wq), heads[h]]
                                                   for sl in (slot_prev, slot_cur, slot_next)], axis=0))
                        vs.append(jnp.concatenate([ring_v[sl, pl.ds(q0, wq), heads[h]]
                                                   for sl in (slot_prev, slot_cur, slot_next)], axis=0))
                    else:
                        ks.append(ksrc[pl.ds(k0, win), heads[h]])
                        vs.append(vsrc[pl.ds(k0, win), heads[h]])
                    bs.append(bias_refs[p][var, h])
            ms, ls, accs = _attend(qs, ks, vs, bs)
            for b in range(nb):
                nat = pl.ds(where[b], wq, stride=d) if d > 1 else pl.ds(pl.multiple_of(where[b], w), wq)
                outs = []
                for h in range(nh):
                    m, l, acc = ms[b * nh + h], ls[b * nh + h], accs[b * nh + h]
                    if not first_pattern:
                        m_p = m_nat[h, nat, :]
                        m_n = jnp.maximum(m_p, m)
                        alpha = jnp.exp(m_p - m_n)
                        beta = jnp.exp(m - m_n)
                        l = alpha * l_nat[h, nat, :] + beta * l
                        acc = alpha * acc_nat[h, nat, :] + beta * acc
                        m = m_n
                    if last_pattern:
                        outs.append(acc / l)
                    else:
                        acc_nat[h, nat, :] = acc
                        m_nat[h, nat, :] = jnp.broadcast_to(m, (wq, HEAD_DIM))
                        l_nat[h, nat, :] = jnp.broadcast_to(l, (wq, HEAD_DIM))
                if last_pattern:
                    y_ref[nat, :] = _rms(jnp.concatenate(outs, axis=1), gn_ref[...]).astype(BF16)
            return carry

        lax.fori_loop(0, c // (wq * nb), body, 0)

    def deinterleave(p, d):
        per = _PERM_ROWS // d
        halo_groups = (d * w) // _PERM_ROWS
        chunk_groups = c // _PERM_ROWS
        kstride = (chunk_groups + 2 * halo_groups) * per
        perm = perm_ref[p - 1]

        def move(src, row0, dst, g, stride):
            res = jnp.dot(perm, src[row0:row0 + _PERM_ROWS, :], preferred_element_type=F32).astype(BF16)
            for r in range(d):
                dst[r * stride + g * per:r * stride + (g + 1) * per, :] = res[r * per:(r + 1) * per]

        for g in range(chunk_groups):
            move(q_ref, g * _PERM_ROWS, qd, g, c // d)
        if d * w == c:
            def fill(slot, k_src, v_src):
                for src, ring in ((k_src, ring_k), (v_src, ring_v)):
                    for g in range(chunk_groups):
                        move(src, g * _PERM_ROWS, ring.at[slot], g, c // d)

            @pl.when(step == 0)
            def _():
                ring_k[2] = jnp.zeros((c, DIL_WIDTH), BF16)
                ring_v[2] = jnp.zeros((c, DIL_WIDTH), BF16)
                fill(0, kc_ref, vc_ref)

            fill(slot_next, kn_ref, vn_ref)
            return c // d, None
        for prev_ref, cur_ref, next_ref, dst in ((kp_ref, kc_ref, kn_ref, kd), (vp_ref, vc_ref, vn_ref, vd)):
            srcs = ([(prev_ref, tail - (halo_groups - g) * _PERM_ROWS) for g in range(halo_groups)]
                    + [(cur_ref, g * _PERM_ROWS) for g in range(chunk_groups)]
                    + [(next_ref, g * _PERM_ROWS) for g in range(halo_groups)])
            for g, (src, row0) in enumerate(srcs):
                move(src, row0, dst, g, kstride)
        return c // d, kstride

    order = sorted(range(len(DIL_PATTERNS)), key=lambda p: -DIL_PATTERNS[p][1])
    for idx, p in enumerate(order):
        d = DIL_PATTERNS[p][1]
        first, last = idx == 0, idx == len(order) - 1
        if d == 1:
            for prev_ref, cur_ref, next_ref, dst in ((kp_ref, kc_ref, kn_ref, kd), (vp_ref, vc_ref, vn_ref, vd)):
                dst[0:w, :] = prev_ref[tail - w:tail, :]
                dst[w:w + c, :] = cur_ref[...]
                dst[w + c:2 * w + c, :] = next_ref[0:w, :]
            run_pattern(p, d, q_ref, kd, vd, 0, 0, first, last)
        else:
            qstride, kstride = deinterleave(p, d)
            if kstride is None:
                run_pattern(p, d, qd, None, None, qstride, 0, first, last)
            else:
                run_pattern(p, d, qd, kd, vd, qstride, kstride, first, last)


def _dilated(cfg, qkvb, gn):
    t = qkvb.shape[0]
    c = cfg.dil_rows
    w = _DIL_W
    n = t // c
    dmax = max(d for _, d in DIL_PATTERNS)
    assert DIL_PATTERNS[0][1] == 1 and c % (dmax * w) == 0 and c >= dmax * w and c % _PERM_ROWS == 0
    assert all((d * w) % _PERM_ROWS == 0 for _, d in DIL_PATTERNS[1:])
    assert len(DIL_PATTERNS) == 3
    biases = [_band_bias_variants(_dilated_qrows(c, d), w, d, DIL_HEADS) for _, d in DIL_PATTERNS]
    perm = np.stack([_perm_matrix(d) for _, d in DIL_PATTERNS[1:]])
    blk = (c, DIL_WIDTH)
    assert dmax * w == c
    local = [d for _, d in DIL_PATTERNS if d * w < c]
    tail = max(d * w for d in local)
    assert c % tail == 0 and tail % _PERM_ROWS == 0
    class_rows = max((c // _PERM_ROWS + 2 * (d * w // _PERM_ROWS)) * _PERM_ROWS if d > 1 else c + 2 * w for d in local)
    prev = lambda col: (lambda i, f: (jnp.maximum(i * (c // tail) - 1, 0), col))
    cur = lambda col: (lambda i, f: (i, col))
    nxt = lambda col: (lambda i, f: (jnp.minimum(i + 1, n - 1), col))
    tail_blk = (tail, DIL_WIDTH)
    return pl.pallas_call(
        functools.partial(_dilated_kernel, c=c, batch_rows=cfg.dil_batch * w, tail=tail),
        grid_spec=pltpu.PrefetchScalarGridSpec(
            num_scalar_prefetch=1, grid=(n,),
            in_specs=[
                pl.BlockSpec(blk, cur(0)),
                pl.BlockSpec(tail_blk, prev(1)), pl.BlockSpec(blk, cur(1)), pl.BlockSpec(blk, nxt(1)),
                pl.BlockSpec(tail_blk, prev(2)), pl.BlockSpec(blk, cur(2)), pl.BlockSpec(blk, nxt(2)),
                _resident(biases[0].shape), _resident(biases[1].shape), _resident(biases[2].shape),
                _resident(perm.shape), _resident((1, DIL_WIDTH)),
            ],
            out_specs=pl.BlockSpec(blk, cur(0)),
            scratch_shapes=[
                pltpu.VMEM((c, DIL_WIDTH), BF16),
                pltpu.VMEM((class_rows, DIL_WIDTH), BF16), pltpu.VMEM((class_rows, DIL_WIDTH), BF16),
                pltpu.VMEM((3, c, DIL_WIDTH), BF16), pltpu.VMEM((3, c, DIL_WIDTH), BF16),
                pltpu.VMEM((DIL_HEADS, c, HEAD_DIM), F32),
                pltpu.VMEM((DIL_HEADS, c, HEAD_DIM), F32),
                pltpu.VMEM((DIL_HEADS, c, HEAD_DIM), F32),
            ]),
        out_shape=jax.ShapeDtypeStruct((t, DIL_WIDTH), BF16),
        compiler_params=_params(("arbitrary",)),
        name="dilated_attention",
    )(_chunk_flags(cfg, c), qkvb, qkvb, qkvb, qkvb, qkvb, qkvb, qkvb,
      jnp.asarray(biases[0]), jnp.asarray(biases[1]), jnp.asarray(biases[2]), jnp.asarray(perm, dtype=BF16), gn)


def _outproj_kernel(ya_ref, yb_ref, yc_ref, x_ref, w_ref, g_ref, o_ref, *, nchunk):
    d = o_ref.shape[1]
    b0, b1 = LRU_WIDTH, LRU_WIDTH + DIL_WIDTH
    for c in range(0, d, nchunk):
        cs = slice(c, c + nchunk)
        acc = jnp.dot(ya_ref[...], w_ref[0:b0, cs], preferred_element_type=F32)
        acc += jnp.dot(yb_ref[...], w_ref[b0:b1, cs], preferred_element_type=F32)
        acc += jnp.dot(yc_ref[...], w_ref[b1:, cs], preferred_element_type=F32)
        o_ref[:, cs] = acc
    for rows in _row_blocks(o_ref.shape[0]):
        o_ref[rows, :] = x_ref[rows, :] + _rms(o_ref[rows, :], g_ref[...])


def _outproj(cfg, ya, yb, yc, x, w, g, layer):
    t, d = x.shape
    tm = cfg.tm_out
    return pl.pallas_call(
        functools.partial(_outproj_kernel, nchunk=min(_COL_CHUNK, d)),
        grid=(t // tm,),
        in_specs=[
            pl.BlockSpec((tm, LRU_WIDTH), lambda i: (i, 0)),
            pl.BlockSpec((tm, DIL_WIDTH), lambda i: (i, 0)),
            pl.BlockSpec((tm, SWA_WIDTH), lambda i: (i, 0)),
            pl.BlockSpec((tm, d), lambda i: (i, 0)),
            _resident((MIX_WIDTH, d), layer),
            _resident((1, d)),
        ],
        out_specs=pl.BlockSpec((tm, d), lambda i: (i, 0)),
        out_shape=jax.ShapeDtypeStruct((t, d), F32),
        compiler_params=_params(("parallel",)),
        name="mixer_outproj",
    )(ya, yb, yc, x, w, g)


def _memkv_kernel(mem_ref, g_ref, w_ref, k_ref, v_ref):
    mn = _rms(mem_ref[...], g_ref[...]).astype(BF16)
    kv = jnp.dot(mn, w_ref[...], preferred_element_type=F32)
    k_ref[...] = kv[:, :MEM_WIDTH].astype(BF16)
    v_ref[...] = kv[:, MEM_WIDTH:].astype(BF16)


def _memkv(cfg, mem, g, wkv):
    rows, d = mem.shape
    nm = cfg.n_mem
    shape = jax.ShapeDtypeStruct((cfg.depth, rows, MEM_WIDTH), BF16)
    return pl.pallas_call(
        _memkv_kernel,
        grid=(cfg.depth, rows // nm),
        in_specs=[pl.BlockSpec((nm, d), lambda l, i: (i, 0)),
                  pl.BlockSpec((None, 1, d), lambda l, i: (l, 0, 0)),
                  pl.BlockSpec((None, d, 2 * MEM_WIDTH), lambda l, i: (l, 0, 0))],
        out_specs=[pl.BlockSpec((None, nm, MEM_WIDTH), lambda l, i: (l, i, 0))] * 2,
        out_shape=[shape, shape],
        compiler_params=_params(("parallel", "parallel")),
        name="memory_kv",
    )(mem, g, wkv)


def _cross_kernel(seq_ref, x_ref, gpre_ref, wq_ref, k_ref, v_ref, wo_ref, gpost_ref, o_ref, xn_scr, o_scr, *,
                  nchunk):
    del seq_ref
    tm, d = x_ref.shape
    for rows in _row_blocks(tm):
        xn_scr[rows, :] = _rms(x_ref[rows, :], gpre_ref[...]).astype(BF16)
    q = (jnp.dot(xn_scr[...], wq_ref[...], preferred_element_type=F32) * _QK_SCALE).astype(BF16)
    heads = [slice(h * HEAD_DIM, (h + 1) * HEAD_DIM) for h in range(MEM_HEADS)]
    _, ls, accs = _attend([q[:, hs] for hs in heads], [k_ref[:, hs] for hs in heads],
                          [v_ref[:, hs] for hs in heads], [None] * MEM_HEADS)
    for hs, l, acc in zip(heads, ls, accs):
        o_scr[:, hs] = (acc / l).astype(BF16)
    for c in range(0, d, nchunk):
        cs = slice(c, min(c + nchunk, d))
        o_ref[:, cs] = jnp.dot(o_scr[...], wo_ref[:, cs], preferred_element_type=F32)
    for rows in _row_blocks(tm):
        o_ref[rows, :] = x_ref[rows, :] + _rms(o_ref[rows, :], gpost_ref[...])


def _cross(cfg, x, gpre, wq, kmem, vmem, wo, gpost, layer):
    t, d = x.shape
    tm, nm = cfg.tm_cross, cfg.n_mem
    seqs, _ = _sequences(cfg)
    seq_of_tile = []
    for si, (_, length) in enumerate(seqs):
        assert length % tm == 0
        seq_of_tile += [si] * (length // tm)
    seq_of_tile = jnp.asarray(np.asarray(seq_of_tile, np.int32))
    return pl.pallas_call(
        functools.partial(_cross_kernel, nchunk=_COL_CHUNK),
        grid_spec=pltpu.PrefetchScalarGridSpec(
            num_scalar_prefetch=1, grid=(t // tm,),
            in_specs=[
                pl.BlockSpec((tm, d), lambda i, s: (i, 0)),
                _resident((1, d)),
                _resident((d, MEM_WIDTH), layer),
                pl.BlockSpec((None, nm, MEM_WIDTH), lambda i, s: (layer, s[i], 0)),
                pl.BlockSpec((None, nm, MEM_WIDTH), lambda i, s: (layer, s[i], 0)),
                _resident((MEM_WIDTH, d), layer),
                _resident((1, d)),
            ],
            out_specs=pl.BlockSpec((tm, d), lambda i, s: (i, 0)),
            scratch_shapes=[pltpu.VMEM((tm, d), BF16), pltpu.VMEM((tm, MEM_WIDTH), BF16)]),
        out_shape=jax.ShapeDtypeStruct((t, d), F32),
        compiler_params=_params(("parallel",)),
        name="memory_cross_attention",
    )(seq_of_tile, x, gpre, wq, kmem, vmem, wo, gpost)


def _ffn_kernel(x_ref, gpre_ref, w1_ref, w2_ref, gpost_ref, o_ref, xn_scr, h_scr, *, nchunk):
    f = pl.program_id(1)
    tf = w1_ref.shape[1]
    d = w2_ref.shape[1]

    tm = x_ref.shape[0]
    row_blocks = [slice(r, min(r + _NORM_ROWS, tm)) for r in range(0, tm, _NORM_ROWS)]

    @pl.when(f == 0)
    def _():
        for rows in row_blocks:
            xn_scr[rows, :] = _rms(x_ref[rows, :], gpre_ref[...]).astype(BF16)
            o_ref[rows, :] = jnp.zeros((rows.stop - rows.start, d), F32)

    for c in range(0, tf, nchunk):
        cs = slice(c, min(c + nchunk, tf))
        h = jnp.dot(xn_scr[...], w1_ref[:, cs], preferred_element_type=F32)
        h_scr[:, cs] = jnp.square(jnp.maximum(h, 0.0)).astype(BF16)
    for c in range(0, d, nchunk):
        cs = slice(c, min(c + nchunk, d))
        o_ref[:, cs] += jnp.dot(h_scr[...], w2_ref[:, cs], preferred_element_type=F32)

    @pl.when(f == pl.num_programs(1) - 1)
    def _():
        for rows in row_blocks:
            o_ref[rows, :] = x_ref[rows, :] + _rms(o_ref[rows, :], gpost_ref[...])


def _ffn(cfg, x, gpre, w1, w2, gpost, layer):
    t, d = x.shape
    tm, tf = cfg.tm_ffn, cfg.tf
    dff = w1.shape[2]
    return pl.pallas_call(
        functools.partial(_ffn_kernel, nchunk=_COL_CHUNK),
        grid=(t // tm, dff // tf),
        in_specs=[
            pl.BlockSpec((tm, d), lambda i, f: (i, 0)),
            _resident((1, d)),
            pl.BlockSpec((None, d, tf), lambda i, f: (layer, 0, f)),
            pl.BlockSpec((None, tf, d), lambda i, f: (layer, f, 0)),
            _resident((1, d)),
        ],
        out_specs=pl.BlockSpec((tm, d), lambda i, f: (i, 0)),
        out_shape=jax.ShapeDtypeStruct((t, d), F32),
        scratch_shapes=[pltpu.VMEM((tm, d), BF16), pltpu.VMEM((tm, tf), BF16)],
        compiler_params=_params(("parallel", "arbitrary")),
        name="squared_relu_mlp",
    )(x, gpre, w1, w2, gpost)


def _forward(cfg, x, mem, p):
    row = lambda a: a.reshape(1, -1).astype(F32)
    kmem, vmem = _memkv(cfg, mem, p["mem_kv_norm"].astype(F32)[:, None, :], p["w_mkv"])
    for l in range(cfg.depth):
        lru_in, qkvb, qkvc = _inproj(cfg, x, row(p["mix_norm_pre"][l]), p["w_in"], l)
        gn = p["group_norm"][l]
        wg = jnp.concatenate([p["lru_wa"][l], p["lru_wx"][l]], axis=-1).astype(BF16)
        ya = _lru(cfg, lru_in, p["conv_w"][l], row(p["conv_b"][l]), wg,
                  p["lru_ba"][l][:, None, :], p["lru_bx"][l][:, None, :], p["lru_lam"][l][:, None, :],
                  row(gn[:LRU_WIDTH]))
        yb = _dilated(cfg, qkvb, row(gn[LRU_WIDTH:LRU_WIDTH + DIL_WIDTH]))
        sink = jnp.zeros((1, _LANES), F32).at[0, :SWA_HEADS].set(p["swa_sink"][l].astype(F32))
        yc = _swa(cfg, qkvc, row(gn[LRU_WIDTH + DIL_WIDTH:]), sink)
        x = _outproj(cfg, ya, yb, yc, x, p["w_out"], row(p["mix_norm_post"][l]), l)
        x = _cross(cfg, x, row(p["mem_norm_pre"][l]), p["w_mq"], kmem, vmem, p["w_mo"],
                   row(p["mem_norm_post"][l]), l)
        x = _ffn(cfg, x, row(p["ffn_norm_pre"][l]), p["w_ff1"], p["w_ff2"], row(p["ffn_norm_post"][l]), l)
    return x


def _run(cfg, x_prompt, x_sample, mem_prompt, mem_sample, mix_norm_pre, mix_norm_post, w_in, conv_w, conv_b,
         lru_wa, lru_ba, lru_wx, lru_bx, lru_lam, swa_sink, group_norm, w_out, mem_norm_pre, mem_norm_post,
         mem_kv_norm, w_mq, w_mk, w_mv, w_mo, ffn_norm_pre, ffn_norm_post, w_ff1, w_ff2):
    d = cfg.d_model
    p = dict(
        mix_norm_pre=mix_norm_pre, mix_norm_post=mix_norm_post, w_in=w_in.astype(BF16), conv_w=conv_w,
        conv_b=conv_b, lru_wa=lru_wa, lru_ba=lru_ba, lru_wx=lru_wx, lru_bx=lru_bx, lru_lam=lru_lam,
        swa_sink=swa_sink, group_norm=group_norm, w_out=w_out.astype(BF16), mem_norm_pre=mem_norm_pre,
        mem_norm_post=mem_norm_post, mem_kv_norm=mem_kv_norm, w_mq=w_mq.astype(BF16),
        w_mkv=jnp.concatenate([w_mk, w_mv], axis=-1).astype(BF16), w_mo=w_mo.astype(BF16),
        ffn_norm_pre=ffn_norm_pre, ffn_norm_post=ffn_norm_post, w_ff1=w_ff1.astype(BF16),
        w_ff2=w_ff2.astype(BF16))
    outs = []
    for group, x, mem in zip(cfg.groups, (x_prompt, x_sample), (mem_prompt, mem_sample)):
        sub = cfg._replace(groups=(group,))
        outs.append(_forward(sub, x.reshape(-1, d), mem.reshape(-1, d), p).reshape(x.shape))
    return tuple(outs)


def kernel(x_prompt, x_sample, mem_prompt, mem_sample, mix_norm_pre, mix_norm_post, w_in, conv_w, conv_b, lru_wa,
           lru_ba, lru_wx, lru_bx, lru_lam, swa_sink, group_norm, w_out, mem_norm_pre, mem_norm_post, mem_kv_norm,
           w_mq, w_mk, w_mv, w_mo, ffn_norm_pre, ffn_norm_post, w_ff1, w_ff2):
    return _run(_CFG, x_prompt, x_sample, mem_prompt, mem_sample, mix_norm_pre, mix_norm_post, w_in, conv_w,
                conv_b, lru_wa, lru_ba, lru_wx, lru_bx, lru_lam, swa_sink, group_norm, w_out, mem_norm_pre,
                mem_norm_post, mem_kv_norm, w_mq, w_mk, w_mv, w_mo, ffn_norm_pre, ffn_norm_post, w_ff1, w_ff2)
```
